```python
import jax, jax.numpy as jnp
from jax import lax
import numpy as np

D_MODEL = 1024
BATCH = 32
SEQ = 2048
DEPTH = 1

HG_HEADS = 4
HG_DK = 128
HG_DV = 128
HG_CHUNK = 64
HG_QK_WIDTH = HG_HEADS * HG_DK
HG_V_WIDTH = HG_HEADS * HG_DV
ATT_Q_HEADS = 8
ATT_KV_HEADS = 2
ATT_GROUP = ATT_Q_HEADS // ATT_KV_HEADS
ATT_HEAD_DIM = 64
ATT_Q_WIDTH = ATT_Q_HEADS * ATT_HEAD_DIM
ATT_KV_WIDTH = ATT_KV_HEADS * ATT_HEAD_DIM
WINDOW = 128
ATT_BLOCK = 128
PROJ_SIZES = (HG_QK_WIDTH, HG_QK_WIDTH, HG_V_WIDTH, HG_V_WIDTH,
              ATT_Q_WIDTH, ATT_KV_WIDTH, ATT_KV_WIDTH, D_MODEL, D_MODEL)
PROJ_WIDTH = sum(PROJ_SIZES)
N_GROUPS = 4
EXPERTS_PER_GROUP = 8
N_EXPERTS = N_GROUPS * EXPERTS_PER_GROUP
TOP_K_IN_GROUP = 2
D_EXPERT = 512
MOE_BLOCK = 256
DN_ALPHA = (2.0 * DEPTH) ** 0.25
DN_BETA = (8.0 * DEPTH) ** -0.25
LN_EPS = 1e-5
RMS_EPS = 1e-6
NEG_INF = -1e30

kernel_name = "hybrid_hgrn2_swa_hiermoe_deepnorm"


def layer_norm(x, g, b):
    xf = x.astype(jnp.float32)
    mu = jnp.mean(xf, axis=-1, keepdims=True)
    var = jnp.mean(jnp.square(xf - mu), axis=-1, keepdims=True)
    return ((xf - mu) * lax.rsqrt(var + LN_EPS) * g.astype(jnp.float32) + b.astype(jnp.float32)).astype(x.dtype)


def hgrn2(q, f_logit, inp, g, lb, norm_w):
    B, S, _ = q.shape
    nc = S // HG_CHUNK
    f32 = jnp.float32
    qf = jax.nn.silu(q.astype(f32))
    f = lb + (1.0 - lb) * jax.nn.sigmoid(f_logit.astype(f32))
    logf = jnp.log(f)
    k = 1.0 - f

    def to_chunks(t, d):
        return t.reshape(B, nc, HG_CHUNK, HG_HEADS, d).transpose(1, 0, 3, 2, 4)

    qc = to_chunks(qf, HG_DK)
    kc = to_chunks(k, HG_DK)
    lc = to_chunks(logf, HG_DK)
    vc = to_chunks(inp.astype(f32), HG_DV)
    tri = jnp.tril(jnp.ones((HG_CHUNK, HG_CHUNK), dtype=bool))

    def step(state, xs):
        qb, kb, vb, lb_c = xs
        b = jnp.cumsum(lb_c, axis=2)
        o_inter = jnp.einsum('bhtk,bhkv->bhtv', qb * jnp.exp(b), state)
        diff = b[:, :, :, None, :] - b[:, :, None, :, :]
        decay = jnp.exp(jnp.where(tri[:, :, None], diff, -jnp.inf))
        scores = jnp.einsum('bhtk,bhtsk,bhsk->bhts', qb, decay, kb)
        o_intra = jnp.einsum('bhts,bhsv->bhtv', scores, vb)
        b_last = b[:, :, -1:, :]
        new_state = (jnp.exp(b_last[:, :, 0, :])[..., None] * state
                     + jnp.einsum('bhsk,bhsv->bhkv', kb * jnp.exp(b_last - b), vb))
        return new_state, o_inter + o_intra

    s0 = jnp.zeros((B, HG_HEADS, HG_DK, HG_DV), f32)
    _, o = lax.scan(step, s0, (qc, kc, vc, lc))
    o = o.transpose(1, 0, 3, 2, 4).reshape(B, S, HG_HEADS, HG_DV)
    o = o * lax.rsqrt(jnp.mean(jnp.square(o), axis=-1, keepdims=True) + RMS_EPS) * norm_w.astype(f32)
    o = o * jax.nn.silu(g.astype(f32).reshape(B, S, HG_HEADS, HG_DV))
    return o.reshape(B, S, HG_V_WIDTH).astype(q.dtype)


def sliding_window_attention(q, k, v, sinks):
    B, S, _ = q.shape
    nb = S // ATT_BLOCK
    f32 = jnp.float32
    qb = q.reshape(B, nb, ATT_BLOCK, ATT_KV_HEADS, ATT_GROUP, ATT_HEAD_DIM)
    pad = ((0, 0), (ATT_BLOCK, 0), (0, 0))
    kp = jnp.pad(k, pad).reshape(B, nb + 1, ATT_BLOCK, ATT_KV_HEADS, ATT_HEAD_DIM)
    vp = jnp.pad(v, pad).reshape(B, nb + 1, ATT_BLOCK, ATT_KV_HEADS, ATT_HEAD_DIM)
    kb = jnp.concatenate([kp[:, :-1], kp[:, 1:]], axis=2)
    vb = jnp.concatenate([vp[:, :-1], vp[:, 1:]], axis=2)
    scale = ATT_HEAD_DIM ** -0.5
    scores = jnp.einsum('bnqhgd,bnkhd->bnhgqk', qb, kb).astype(f32) * scale
    dist = (jnp.arange(ATT_BLOCK)[:, None] + ATT_BLOCK - jnp.arange(2 * ATT_BLOCK)[None, :])
    key_pos = jnp.arange(nb)[:, None] * ATT_BLOCK - ATT_BLOCK + jnp.arange(2 * ATT_BLOCK)[None, :]
    valid = ((dist >= 0) & (dist < WINDOW))[None] & (key_pos >= 0)[:, None, :]
    slopes = jnp.exp2(-8.0 * (jnp.arange(ATT_Q_HEADS, dtype=f32) + 1.0) / ATT_Q_HEADS)
    alibi = slopes.reshape(ATT_KV_HEADS, ATT_GROUP, 1, 1) * dist.astype(f32)
    logits = jnp.where(valid[None, :, None, None], scores - alibi, NEG_INF)
    sink_col = jnp.broadcast_to(sinks.astype(f32).reshape(ATT_KV_HEADS, ATT_GROUP, 1, 1),
                                logits.shape[:-1] + (1,))
    probs = jax.nn.softmax(jnp.concatenate([logits, sink_col], axis=-1), axis=-1)[..., :-1]
    out = jnp.einsum('bnhgqk,bnkhd->bnqhgd', probs.astype(v.dtype), vb)
    return out.reshape(B, S, ATT_Q_WIDTH)


def hierarchical_moe(x, w_rg, b_rg, w_re, b_re, w_gate, w_up, w_down):
    B, S, D = x.shape
    T = B * S
    A = T * TOP_K_IN_GROUP
    P = A + N_EXPERTS * MOE_BLOCK
    NB = P // MOE_BLOCK
    f32 = jnp.float32
    xt = x.reshape(T, D)
    glog = (xt @ w_rg + b_rg).astype(f32)
    gprob = jax.nn.softmax(glog, axis=-1)
    gval, gsel = lax.top_k(glog, 1)
    gw = jnp.take_along_axis(gprob, gsel, axis=-1)
    elog = (xt @ w_re + b_re).astype(f32).reshape(T, N_GROUPS, EXPERTS_PER_GROUP)
    elog_sel = jnp.take_along_axis(elog, gsel[:, :, None], axis=1)[:, 0]
    top_v, top_i = lax.top_k(elog_sel, TOP_K_IN_GROUP)
    weights = gw * jax.nn.softmax(top_v, axis=-1)
    expert_id = (gsel * EXPERTS_PER_GROUP + top_i).astype(jnp.int32)

    flat_e = expert_id.reshape(-1)
    flat_tok = jnp.repeat(jnp.arange(T, dtype=jnp.int32), TOP_K_IN_GROUP)
    flat_w = weights.reshape(-1)
    order = jnp.argsort(flat_e)
    se = flat_e[order]
    counts = jnp.bincount(flat_e, length=N_EXPERTS)
    padded = ((counts + MOE_BLOCK - 1) // MOE_BLOCK) * MOE_BLOCK
    start = jnp.cumsum(counts) - counts
    pend = jnp.cumsum(padded)
    pstart = pend - padded
    dest = pstart[se] + (jnp.arange(A) - start[se])
    slot_tok = jnp.zeros((P,), jnp.int32).at[dest].set(flat_tok[order])
    slot_w = jnp.zeros((P,), f32).at[dest].set(flat_w[order])
    block_expert = jnp.clip(jnp.searchsorted(pend, jnp.arange(NB) * MOE_BLOCK, side='right'),
                            0, N_EXPERTS - 1)
    xbuf = xt[slot_tok].reshape(NB, MOE_BLOCK, D)

    def expert_block(args):
        xb, e = args
        h = jax.nn.silu(xb @ w_gate[e]) * (xb @ w_up[e])
        return h @ w_down[e]

    ybuf = lax.map(expert_block, (xbuf, block_expert)).reshape(P, D)
    y = jnp.zeros((T, D), x.dtype).at[slot_tok].add(ybuf * slot_w[:, None].astype(x.dtype))
    return y.reshape(B, S, D)


def setup_inputs(seed: int = 0) -> dict:
    key = jax.random.key(seed)
    ks = jax.random.split(key, 20)
    n = jax.random.normal
    f32 = jnp.float32
    L = DEPTH
    return {
        "x": n(ks[0], (BATCH, SEQ, D_MODEL), f32),
        "lb_logits": 0.1 * n(ks[1], (DEPTH + 1, HG_QK_WIDTH), f32),
        "w_in": n(ks[2], (L, D_MODEL, PROJ_WIDTH), f32) * D_MODEL ** -0.5,
        "hg_norm_w": 1.0 + 0.02 * n(ks[3], (L, HG_DV), f32),
        "sinks": 0.5 * n(ks[4], (L, ATT_Q_HEADS), f32),
        "w_branch_a": n(ks[5], (L, HG_V_WIDTH, D_MODEL), f32) * HG_V_WIDTH ** -0.5,
        "w_branch_b": n(ks[6], (L, ATT_Q_WIDTH, D_MODEL), f32) * ATT_Q_WIDTH ** -0.5,
        "w_out": n(ks[7], (L, D_MODEL, D_MODEL), f32) * (D_MODEL ** -0.5 * DN_BETA),
        "ln1_g": 1.0 + 0.02 * n(ks[8], (L, D_MODEL), f32),
        "ln1_b": 0.02 * n(ks[9], (L, D_MODEL), f32),
        "router_group_w": n(ks[10], (L, D_MODEL, N_GROUPS), f32) * D_MODEL ** -0.5,
        "router_group_b": 0.01 * n(ks[11], (L, N_GROUPS), f32),
        "router_expert_w": n(ks[12], (L, D_MODEL, N_EXPERTS), f32) * D_MODEL ** -0.5,
        "router_expert_b": 0.01 * n(ks[13], (L, N_EXPERTS), f32),
        "w_exp_gate": n(ks[14], (L, N_EXPERTS, D_MODEL, D_EXPERT), f32) * D_MODEL ** -0.5,
        "w_exp_up": n(ks[15], (L, N_EXPERTS, D_MODEL, D_EXPERT), f32) * D_MODEL ** -0.5,
        "w_exp_down": n(ks[16], (L, N_EXPERTS, D_EXPERT, D_MODEL), f32) * (D_EXPERT ** -0.5 * DN_BETA),
        "ln2_g": 1.0 + 0.02 * n(ks[17], (L, D_MODEL), f32),
        "ln2_b": 0.02 * n(ks[18], (L, D_MODEL), f32),
    }


def reference(x, lb_logits, w_in, hg_norm_w, sinks, w_branch_a, w_branch_b, w_out,
              ln1_g, ln1_b, router_group_w, router_group_b, router_expert_w, router_expert_b,
              w_exp_gate, w_exp_up, w_exp_down, ln2_g, ln2_b):
    split_idx = list(np.cumsum(PROJ_SIZES)[:-1])
    lb_all = jnp.cumsum(jax.nn.softmax(lb_logits.astype(jnp.float32), axis=0), axis=0)
    for l in range(DEPTH):
        proj = x @ w_in[l]
        hq, hf, hi, hg, aq, ak, av, ga, gb = jnp.split(proj, split_idx, axis=-1)
        y_a = hgrn2(hq, hf, hi, hg, lb_all[l], hg_norm_w[l]) @ w_branch_a[l]
        y_b = sliding_window_attention(aq, ak, av, sinks[l]) @ w_branch_b[l]
        merged = jax.nn.sigmoid(ga) * y_a + jax.nn.sigmoid(gb) * y_b
        x = layer_norm(DN_ALPHA * x + merged @ w_out[l], ln1_g[l], ln1_b[l])
        moe = hierarchical_moe(x, router_group_w[l], router_group_b[l], router_expert_w[l],
                               router_expert_b[l], w_exp_gate[l], w_exp_up[l], w_exp_down[l])
        x = layer_norm(DN_ALPHA * x + moe, ln2_g[l], ln2_b[l])
    return x
```

```python
import functools

import numpy as np
import jax
import jax.numpy as jnp
from jax import lax
from jax.experimental import pallas as pl
from jax.experimental.pallas import tpu as pltpu

F32 = jnp.float32
BF16 = jnp.bfloat16

D_MODEL = 1024
DEPTH = 1
HG_HEADS = 4
HG_DK = 128
HG_DV = 128
HG_W = HG_HEADS * HG_DK
CHUNK = 64
ATT_Q_HEADS = 8
ATT_KV_HEADS = 2
ATT_GROUP = ATT_Q_HEADS // ATT_KV_HEADS
ATT_HD = 64
ATT_QW = ATT_Q_HEADS * ATT_HD
ATT_KVW = ATT_KV_HEADS * ATT_HD
ATT_BLK = 128
N_GROUPS = 4
EPG = 8
N_EXPERTS = N_GROUPS * EPG
D_EXPERT = 512
MOE_BLOCK = 256
DN_ALPHA = (2.0 * DEPTH) ** 0.25
LN_EPS = 1e-5
RMS_EPS = 1e-6
NEG_INF = -1e30

PROJ_W = 4 * HG_W + ATT_QW + 2 * ATT_KVW + 2 * D_MODEL
PROJ_ROT = 4 * HG_W + ATT_QW + 2 * ATT_KVW
N_LEVELS = 6
N_ARG_GROUPS = N_LEVELS + 2

VMEM_LIMIT = 56 * 1024 * 1024


def _cparams(sem):
    return pltpu.CompilerParams(dimension_semantics=sem, vmem_limit_bytes=VMEM_LIMIT)


def _dot(a, b):
    return jnp.dot(a, b, preferred_element_type=F32)


def _dot_nt(a, b):
    return lax.dot_general(a, b, (((1,), (1,)), ((), ())), preferred_element_type=F32)


def _dot_tn(a, b):
    return lax.dot_general(a, b, (((0,), (0,)), ((), ())), preferred_element_type=F32)


def _layer_norm(z, g, b):
    mu = jnp.mean(z, axis=-1, keepdims=True)
    zc = z - mu
    var = jnp.mean(zc * zc, axis=-1, keepdims=True)
    return zc * lax.rsqrt(var + LN_EPS) * g + b


def _proj_kernel(x_ref, w_ref, o_ref):
    o_ref[...] = _dot(x_ref[...].astype(BF16), w_ref[...])


def _proj(x2, w_bf, tm):
    T = x2.shape[0]
    return pl.pallas_call(
        _proj_kernel,
        grid=(T // tm,),
        in_specs=[pl.BlockSpec((tm, D_MODEL), lambda i: (i, 0)),
                  pl.BlockSpec((D_MODEL, PROJ_W), lambda i: (0, 0))],
        out_specs=pl.BlockSpec((tm, PROJ_W), lambda i: (i, 0)),
        out_shape=jax.ShapeDtypeStruct((T, PROJ_W), F32),
        compiler_params=_cparams(("arbitrary",)),
        name="proj",
    )(x2, w_bf)


def _hgrn_tables():
    C = CHUNK
    w = np.zeros((N_ARG_GROUPS, C, C), np.float32)
    masks = np.zeros((N_LEVELS, C, C), np.float32)
    for lvl in range(N_LEVELS):
        h = 1 << lvl
        for t in range(C):
            base = (t // (2 * h)) * 2 * h
            m = base + h - 1
            if (t // h) % 2 == 1:
                w[lvl, t, m + 1:t + 1] = 1.0
                masks[lvl, t, base:base + h] = 1.0
            else:
                w[lvl, t, t + 1:m + 1] = 1.0
    for t in range(C):
        w[N_LEVELS, t, :t + 1] = 1.0
        w[N_LEVELS + 1, t, t + 1:] = 1.0
    w = w.reshape(N_ARG_GROUPS * C, C)
    right = np.zeros((N_LEVELS, C, 1), np.float32)
    for lvl in range(N_LEVELS):
        h = 1 << lvl
        right[lvl, :, 0] = ((np.arange(C) // h) % 2 == 1)
    return np.concatenate([w, w], axis=1), masks, right


def _hgrn_kernel(q_ref, f_ref, i_ref, g_ref, lb_ref, nw_ref, seg_ref, mask_ref, right_ref, o_ref, state_ref,
                 *, n_chunks):
    @pl.when(pl.program_id(1) == 0)
    def _():
        state_ref[...] = jnp.zeros_like(state_ref)

    lb = lb_ref[...]
    nw = nw_ref[...]
    seg = seg_ref[...]

    def chunk(c, carry):
        rows = pl.ds(pl.multiple_of(c * CHUNK, CHUNK), CHUNK)
        q = q_ref[rows, :]
        qf = q * jax.nn.sigmoid(q)
        f = lb + (1.0 - lb) * jax.nn.sigmoid(f_ref[rows, :])
        logf = jnp.log(f)
        k = 1.0 - f
        v = i_ref[rows, :]
        v_bf = v.astype(BF16)

        l_hi = logf.astype(BF16)
        l_lo = (logf - l_hi.astype(F32)).astype(BF16)
        args = _dot(seg, jnp.concatenate([l_hi, l_lo], axis=0))
        e = jnp.exp(args)

        scores = [jnp.zeros((CHUNK, CHUNK), F32) for _ in range(HG_HEADS)]
        for lvl in range(N_LEVELS):
            e_l = e[lvl * CHUNK:(lvl + 1) * CHUNK, :]
            a = (jnp.where(right_ref[lvl] > 0.5, qf, k) * e_l).astype(BF16)
            m = mask_ref[lvl]
            for hd in range(HG_HEADS):
                a_h = a[:, hd * HG_DK:(hd + 1) * HG_DK]
                scores[hd] = scores[hd] + m * _dot_nt(a_h, a_h)

        e_cum = e[N_LEVELS * CHUNK:(N_LEVELS + 1) * CHUNK, :]
        e_suf = e[(N_LEVELS + 1) * CHUNK:(N_LEVELS + 2) * CHUNK, :]
        q_in = (qf * e_cum).astype(BF16)
        k_out = (k * e_suf).astype(BF16)
        e_last = e_cum[CHUNK - 1:CHUNK, :]
        qk = qf * k

        outs = []
        for hd in range(HG_HEADS):
            cols = slice(hd * HG_DK, (hd + 1) * HG_DK)
            st = state_ref[hd]
            diag = jnp.sum(qk[:, cols], axis=-1, keepdims=True)
            o = (_dot_nt(q_in[:, cols], st.astype(BF16))
                 + _dot(scores[hd].astype(BF16), v_bf[:, cols])
                 + diag * v[:, cols])
            state_ref[hd] = st * e_last[:, cols] + _dot_tn(v_bf[:, cols], k_out[:, cols])
            o = o * lax.rsqrt(jnp.mean(o * o, axis=-1, keepdims=True) + RMS_EPS) * nw
            outs.append(o)
        gate = g_ref[rows, :]
        o_all = jnp.concatenate(outs, axis=1) * (gate * jax.nn.sigmoid(gate))
        o_ref[rows, :] = o_all.astype(o_ref.dtype)
        return carry

    lax.fori_loop(0, n_chunks, chunk, 0)


def _hgrn(proj, lb, norm_w, B, S, ts):
    T = B * S
    ns = S // ts
    seg, masks, right = _hgrn_tables()
    col = lambda cb: pl.BlockSpec((ts, HG_W), lambda b, s: (b * ns + s, cb))
    const2 = lambda shape: pl.BlockSpec(shape, lambda b, s: (0,) * len(shape))
    base = (2 * D_MODEL) // HG_W
    return pl.pallas_call(
        functools.partial(_hgrn_kernel, n_chunks=ts // CHUNK),
        grid=(B, ns),
        in_specs=[col(base), col(base + 1), col(base + 2), col(base + 3),
                  const2((1, HG_W)), const2((1, HG_DV)),
                  const2((N_ARG_GROUPS * CHUNK, 2 * CHUNK)),
                  const2((N_LEVELS, CHUNK, CHUNK)), const2((N_LEVELS, CHUNK, 1))],
        out_specs=pl.BlockSpec((ts, HG_W), lambda b, s: (b * ns + s, 0)),
        out_shape=jax.ShapeDtypeStruct((T, HG_W), BF16),
        scratch_shapes=[pltpu.VMEM((HG_HEADS, HG_DV, HG_DK), F32)],
        compiler_params=_cparams(("arbitrary", "arbitrary")),
        name="hgrn2",
    )(proj, proj, proj, proj, lb, norm_w, jnp.asarray(seg, BF16), jnp.asarray(masks), jnp.asarray(right))


def _attn_kernel(sink_ref, q_ref, kv_ref, kvp_ref, o_ref):
    first = pl.program_id(1) == 0
    r = lax.broadcasted_iota(jnp.int32, (ATT_BLK, 2 * ATT_BLK), 0)
    c = lax.broadcasted_iota(jnp.int32, (ATT_BLK, 2 * ATT_BLK), 1)
    dist = r + ATT_BLK - c
    valid = (dist >= 0) & (dist < ATT_BLK) & ((c >= ATT_BLK) | jnp.logical_not(first))
    dist_f = dist.astype(F32)
    lane = lax.broadcasted_iota(jnp.int32, (2 * ATT_BLK, 2 * ATT_KVW), 1)

    kv = jnp.concatenate([kvp_ref[...], kv_ref[...]], axis=0)
    kv_sw = jnp.concatenate([kv[:, ATT_HD:ATT_KVW], kv[:, :ATT_HD],
                             kv[:, ATT_KVW + ATT_HD:], kv[:, ATT_KVW:ATT_KVW + ATT_HD]], axis=1)
    lo = (lane % ATT_KVW) < ATT_HD

    def variant(h, off, vpart):
        src = kv if h == off else kv_sw
        sel = jnp.where(lo if off == 0 else jnp.logical_not(lo), src, 0.0)
        return sel[:, ATT_KVW:] if vpart else sel[:, :ATT_KVW]

    scale = ATT_HD ** -0.5
    for pair in range(ATT_Q_HEADS // 2):
        qp = q_ref[:, pair * 2 * ATT_HD:(pair + 1) * 2 * ATT_HD].astype(BF16)
        acc = jnp.zeros((ATT_BLK, 2 * ATT_HD), F32)
        for off in range(2):
            j = 2 * pair + off
            h = j // ATT_GROUP
            slope = float(2.0 ** (-8.0 * (j + 1) / ATT_Q_HEADS))
            sink = sink_ref[j]
            s = _dot_nt(qp, variant(h, off, False).astype(BF16)) * scale
            logits = jnp.where(valid, s - slope * dist_f, NEG_INF)
            m = jnp.maximum(jnp.max(logits, axis=-1, keepdims=True), sink)
            p = jnp.exp(logits - m)
            den = jnp.sum(p, axis=-1, keepdims=True) + jnp.exp(sink - m)
            acc = acc + _dot(p.astype(BF16), variant(h, off, True).astype(BF16)) / den
        o_ref[:, pair * 2 * ATT_HD:(pair + 1) * 2 * ATT_HD] = acc.astype(o_ref.dtype)


def _attn(proj, sinks, B, S):
    T = B * S
    nb = S // ATT_BLK
    qcol = (2 * D_MODEL + 4 * HG_W) // ATT_QW
    kvcol = (2 * D_MODEL + 4 * HG_W + ATT_QW) // (2 * ATT_KVW)
    return pl.pallas_call(
        _attn_kernel,
        grid=(B, nb),
        in_specs=[pl.BlockSpec(memory_space=pltpu.SMEM),
                  pl.BlockSpec((ATT_BLK, ATT_QW), lambda b, n: (b * nb + n, qcol)),
                  pl.BlockSpec((ATT_BLK, 2 * ATT_KVW), lambda b, n: (b * nb + n, kvcol)),
                  pl.BlockSpec((ATT_BLK, 2 * ATT_KVW), lambda b, n: (b * nb + jnp.maximum(n - 1, 0), kvcol))],
        out_specs=pl.BlockSpec((ATT_BLK, ATT_QW), lambda b, n: (b * nb + n, 0)),
        out_shape=jax.ShapeDtypeStruct((T, ATT_QW), BF16),
        compiler_params=_cparams(("arbitrary", "arbitrary")),
        name="attn",
    )(sinks, proj, proj, proj)


def _merge_kernel(hg_ref, at_ref, ga_ref, gb_ref, x_ref, wa_ref, wb_ref, wo_ref, g1_ref, b1_ref,
                  wr_ref, br_ref, x1_ref, eid_ref, wt_ref):
    ya = _dot(hg_ref[...], wa_ref[...])
    yb = _dot(at_ref[...], wb_ref[...])
    merged = jax.nn.sigmoid(ga_ref[...]) * ya + jax.nn.sigmoid(gb_ref[...]) * yb
    z = DN_ALPHA * x_ref[...] + _dot(merged.astype(BF16), wo_ref[...])
    x1 = _layer_norm(z, g1_ref[...], b1_ref[...])
    x1_ref[...] = x1

    lg = lax.dot_general(wr_ref[...], x1, (((1,), (1,)), ((), ())), precision=lax.Precision.HIGHEST,
                         preferred_element_type=F32) + br_ref[...]
    tm = lg.shape[1]
    g = lg[0:8, :]
    row8 = lax.broadcasted_iota(jnp.int32, (8, tm), 0)
    g = jnp.where(row8 < N_GROUPS, g, -jnp.inf)
    gmax = jnp.max(g, axis=0, keepdims=True)
    gsel = jnp.min(jnp.where(g == gmax, row8, 8), axis=0, keepdims=True)
    gw = 1.0 / jnp.sum(jnp.exp(g - gmax), axis=0, keepdims=True)
    el = jnp.where(gsel == 0, lg[8:16, :],
                   jnp.where(gsel == 1, lg[16:24, :], jnp.where(gsel == 2, lg[24:32, :], lg[32:40, :])))
    v1 = jnp.max(el, axis=0, keepdims=True)
    i1 = jnp.min(jnp.where(el == v1, row8, 8), axis=0, keepdims=True)
    el2 = jnp.where(row8 == i1, -jnp.inf, el)
    v2 = jnp.max(el2, axis=0, keepdims=True)
    i2 = jnp.min(jnp.where(el2 == v2, row8, 8), axis=0, keepdims=True)
    e2 = jnp.exp(v2 - v1)
    den = 1.0 + e2
    eid_ref[...] = jnp.concatenate([gsel * EPG + i1, gsel * EPG + i2], axis=0)
    wt_ref[...] = jnp.concatenate([gw / den, gw * e2 / den], axis=0)


def _merge(hg, att, proj, x2, wa, wb, wo, g1, b1, wr, br, tm):
    T = x2.shape[0]
    row = lambda w, cb=0: pl.BlockSpec((tm, w), lambda i: (i, cb))
    const = lambda shape: pl.BlockSpec(shape, lambda i: (0,) * len(shape))
    return pl.pallas_call(
        _merge_kernel,
        grid=(T // tm,),
        in_specs=[row(HG_W), row(ATT_QW), row(D_MODEL, 0), row(D_MODEL, 1), row(D_MODEL),
                  const((HG_W, D_MODEL)), const((ATT_QW, D_MODEL)), const((D_MODEL, D_MODEL)),
                  const((1, D_MODEL)), const((1, D_MODEL)), const((40, D_MODEL)), const((40, 1))],
        out_specs=[row(D_MODEL), pl.BlockSpec((2, tm), lambda i: (0, i)), pl.BlockSpec((2, tm), lambda i: (0, i))],
        out_shape=[jax.ShapeDtypeStruct((T, D_MODEL), F32),
                   jax.ShapeDtypeStruct((2, T), jnp.int32),
                   jax.ShapeDtypeStruct((2, T), F32)],
        compiler_params=_cparams(("arbitrary",)),
        name="merge",
    )(hg, att, proj, proj, x2, wa, wb, wo, g1, b1, wr, br)


def _gather_rows(idx_hbm, tile, idx_smem, src_hbm, dst_ref, isem, sem, n):
    icp = pltpu.make_async_copy(idx_hbm.at[tile], idx_smem, isem)
    icp.start()
    icp.wait()

    def row_copy(i, row):
        return pltpu.make_async_copy(src_hbm.at[pl.ds(row, 1), :], dst_ref.at[pl.ds(i, 1), :], sem)

    def start(i, carry):
        row_copy(i, idx_smem[0, i]).start()
        return carry

    def wait(i, carry):
        row_copy(i, 0).wait()
        return carry

    lax.fori_loop(0, n, start, 0, unroll=8)
    lax.fori_loop(0, n, wait, 0, unroll=8)


def _expert_kernel(be_ref, tok_hbm, x1_hbm, wg_ref, wu_ref, wd_ref, y_ref, idx_smem, xb_ref, isem, sem):
    i = pl.program_id(0)
    _gather_rows(tok_hbm, i, idx_smem, x1_hbm, xb_ref, isem, sem, MOE_BLOCK)
    xb = xb_ref[...].astype(BF16)
    g = _dot(xb, wg_ref[0].astype(BF16))
    u = _dot(xb, wu_ref[0].astype(BF16))
    h = (g * jax.nn.sigmoid(g) * u).astype(BF16)
    y_ref[...] = _dot(h, wd_ref[0].astype(BF16))


def _experts(block_expert, slot_tok, x1, w_gate, w_up, w_down):
    P = slot_tok.shape[0]
    nb = P // MOE_BLOCK
    wspec = lambda a, b: pl.BlockSpec((1, a, b), lambda i, be: (be[i], 0, 0))
    return pl.pallas_call(
        _expert_kernel,
        grid_spec=pltpu.PrefetchScalarGridSpec(
            num_scalar_prefetch=1,
            grid=(nb,),
            in_specs=[pl.BlockSpec(memory_space=pl.ANY), pl.BlockSpec(memory_space=pl.ANY),
                      wspec(D_MODEL, D_EXPERT), wspec(D_MODEL, D_EXPERT), wspec(D_EXPERT, D_MODEL)],
            out_specs=pl.BlockSpec((MOE_BLOCK, D_MODEL), lambda i, be: (i, 0)),
            scratch_shapes=[pltpu.SMEM((1, MOE_BLOCK), jnp.int32),
                            pltpu.VMEM((MOE_BLOCK, D_MODEL), F32),
                            pltpu.SemaphoreType.DMA(()), pltpu.SemaphoreType.DMA(())]),
        out_shape=jax.ShapeDtypeStruct((P, D_MODEL), F32),
        compiler_params=_cparams(("arbitrary",)),
        name="experts",
    )(block_expert, slot_tok.reshape(nb, 1, MOE_BLOCK), x1, w_gate, w_up, w_down)


def _combine_kernel(pos_hbm, y_hbm, x1_ref, wt_ref, g2_ref, b2_ref, o_ref, idx_smem, r_ref, isem, sem, *, tm):
    i = pl.program_id(0)
    _gather_rows(pos_hbm, i, idx_smem, y_hbm, r_ref, isem, sem, 2 * tm)
    w = wt_ref[...]
    y = w[:, 0:1] * r_ref[0:tm, :] + w[:, 1:2] * r_ref[tm:2 * tm, :]
    o_ref[...] = _layer_norm(DN_ALPHA * x1_ref[...] + y, g2_ref[...], b2_ref[...])


def _combine(pos_tiles, ybuf, x1, wt, g2, b2, tm):
    T = x1.shape[0]
    const = lambda shape: pl.BlockSpec(shape, lambda i: (0,) * len(shape))
    return pl.pallas_call(
        functools.partial(_combine_kernel, tm=tm),
        grid=(T // tm,),
        in_specs=[pl.BlockSpec(memory_space=pl.ANY), pl.BlockSpec(memory_space=pl.ANY),
                  pl.BlockSpec((tm, D_MODEL), lambda i: (i, 0)), pl.BlockSpec((tm, 2), lambda i: (i, 0)),
                  const((1, D_MODEL)), const((1, D_MODEL))],
        out_specs=pl.BlockSpec((tm, D_MODEL), lambda i: (i, 0)),
        out_shape=jax.ShapeDtypeStruct((T, D_MODEL), F32),
        scratch_shapes=[pltpu.SMEM((1, 2 * tm), jnp.int32), pltpu.VMEM((2 * tm, D_MODEL), F32),
                        pltpu.SemaphoreType.DMA(()), pltpu.SemaphoreType.DMA(())],
        compiler_params=_cparams(("arbitrary",)),
        name="combine",
    )(pos_tiles, ybuf, x1, wt, g2, b2)


def _slot_layout(eid, T):
    A = 2 * T
    P = A + N_EXPERTS * MOE_BLOCK
    nb = P // MOE_BLOCK
    flat_e = eid.T.reshape(-1)
    onehot = (flat_e[:, None] == jnp.arange(N_EXPERTS, dtype=jnp.int32)[None, :]).astype(jnp.int32)
    csum = jnp.cumsum(onehot, axis=0)
    rank = jnp.sum(onehot * csum, axis=1) - 1
    counts = csum[-1]
    padded = ((counts + MOE_BLOCK - 1) // MOE_BLOCK) * MOE_BLOCK
    pend = jnp.cumsum(padded)
    pstart = pend - padded
    pos = pstart[flat_e] + rank
    slot_tok = jnp.zeros((P,), jnp.int32).at[pos].set(jnp.arange(A, dtype=jnp.int32) // 2)
    block_expert = jnp.clip(jnp.searchsorted(pend, jnp.arange(nb, dtype=jnp.int32) * MOE_BLOCK, side='right'),
                            0, N_EXPERTS - 1).astype(jnp.int32)
    return pos.astype(jnp.int32), slot_tok, block_expert


def kernel(x, lb_logits, w_in, hg_norm_w, sinks, w_branch_a, w_branch_b, w_out, ln1_g, ln1_b,
           router_group_w, router_group_b, router_expert_w, router_expert_b,
           w_exp_gate, w_exp_up, w_exp_down, ln2_g, ln2_b):
    B, S, D = x.shape
    assert D == D_MODEL and S % ATT_BLK == 0 and w_in.shape[0] == DEPTH == 1
    T = B * S
    tm = 256 if T % 256 == 0 else ATT_BLK
    ts = 512 if S % 512 == 0 else ATT_BLK
    x2 = x.reshape(T, D)

    lb_all = jnp.cumsum(jax.nn.softmax(lb_logits.astype(F32), axis=0), axis=0)
    lb = lb_all[0].reshape(1, HG_W)
    w_rot = jnp.concatenate([w_in[0][:, PROJ_ROT:], w_in[0][:, :PROJ_ROT]], axis=1).astype(BF16)

    proj = _proj(x2, w_rot, tm)
    hg = _hgrn(proj, lb, hg_norm_w[0].reshape(1, HG_DV).astype(F32), B, S, ts)
    att = _attn(proj, sinks[0].astype(F32), B, S)

    wr = jnp.zeros((40, D), F32).at[0:N_GROUPS].set(router_group_w[0].T).at[8:40].set(router_expert_w[0].T)
    br = jnp.zeros((40, 1), F32).at[0:N_GROUPS, 0].set(router_group_b[0]).at[8:40, 0].set(router_expert_b[0])
    x1, eid, wt = _merge(hg, att, proj, x2, w_branch_a[0].astype(BF16), w_branch_b[0].astype(BF16),
                         w_out[0].astype(BF16), ln1_g[0].reshape(1, D), ln1_b[0].reshape(1, D), wr, br, tm)

    pos, slot_tok, block_expert = _slot_layout(eid, T)
    ybuf = _experts(block_expert, slot_tok, x1, w_exp_gate[0], w_exp_up[0], w_exp_down[0])

    pos_tiles = pos.reshape(T // tm, tm, 2).transpose(0, 2, 1).reshape(T // tm, 1, 2 * tm)
    out = _combine(pos_tiles, ybuf, x1, wt.T, ln2_g[0].reshape(1, D), ln2_b[0].reshape(1, D), tm)
    return out.reshape(B, S, D)
```

```python
import functools

import numpy as np
import jax
import jax.numpy as jnp
from jax import lax
from jax.experimental import pallas as pl
from jax.experimental.pallas import tpu as pltpu
from jax.experimental.pallas import tpu_sc as plsc

F32 = jnp.float32
BF16 = jnp.bfloat16

D_MODEL = 1024
DEPTH = 1
HG_HEADS = 4
HG_DK = 128
HG_DV = 128
HG_W = HG_HEADS * HG_DK
CHUNK = 64
ATT_Q_HEADS = 8
ATT_KV_HEADS = 2
ATT_GROUP = ATT_Q_HEADS // ATT_KV_HEADS
ATT_HD = 64
ATT_QW = ATT_Q_HEADS * ATT_HD
ATT_KVW = ATT_KV_HEADS * ATT_HD
ATT_BLK = 128
N_GROUPS = 4
EPG = 8
N_EXPERTS = N_GROUPS * EPG
D_EXPERT = 512
MOE_BLOCK = 256
DN_ALPHA = (2.0 * DEPTH) ** 0.25
LN_EPS = 1e-5
RMS_EPS = 1e-6
NEG_INF = -1e30

PROJ_W = 4 * HG_W + ATT_QW + 2 * ATT_KVW + 2 * D_MODEL
PROJ_ROT = 4 * HG_W + ATT_QW + 2 * ATT_KVW
N_LEVELS = 6
N_ARG_GROUPS = N_LEVELS + 2

VMEM_LIMIT = 56 * 1024 * 1024


def _cparams(sem):
    return pltpu.CompilerParams(dimension_semantics=sem, vmem_limit_bytes=VMEM_LIMIT)


def _dot(a, b):
    return jnp.dot(a, b, preferred_element_type=F32)


def _dot_nt(a, b):
    return lax.dot_general(a, b, (((1,), (1,)), ((), ())), preferred_element_type=F32)


def _dot_tn(a, b):
    return lax.dot_general(a, b, (((0,), (0,)), ((), ())), preferred_element_type=F32)


def _layer_norm(z, g, b):
    mu = jnp.mean(z, axis=-1, keepdims=True)
    zc = z - mu
    var = jnp.mean(zc * zc, axis=-1, keepdims=True)
    return zc * lax.rsqrt(var + LN_EPS) * g + b


def _proj_kernel(x_ref, w_ref, o_ref):
    o_ref[...] = _dot(x_ref[...].astype(BF16), w_ref[...])


def _proj(x2, w_bf, tm):
    T = x2.shape[0]
    return pl.pallas_call(
        _proj_kernel,
        grid=(T // tm,),
        in_specs=[pl.BlockSpec((tm, D_MODEL), lambda i: (i, 0)),
                  pl.BlockSpec((D_MODEL, PROJ_W), lambda i: (0, 0))],
        out_specs=pl.BlockSpec((tm, PROJ_W), lambda i: (i, 0)),
        out_shape=jax.ShapeDtypeStruct((T, PROJ_W), F32),
        compiler_params=_cparams(("arbitrary",)),
        name="proj",
    )(x2, w_bf)


def _hgrn_tables():
    C = CHUNK
    w = np.zeros((N_ARG_GROUPS, C, C), np.float32)
    masks = np.zeros((N_LEVELS, C, C), np.float32)
    for lvl in range(N_LEVELS):
        h = 1 << lvl
        for t in range(C):
            base = (t // (2 * h)) * 2 * h
            m = base + h - 1
            if (t // h) % 2 == 1:
                w[lvl, t, m + 1:t + 1] = 1.0
                masks[lvl, t, base:base + h] = 1.0
            else:
                w[lvl, t, t + 1:m + 1] = 1.0
    for t in range(C):
        w[N_LEVELS, t, :t + 1] = 1.0
        w[N_LEVELS + 1, t, t + 1:] = 1.0
    w = w.reshape(N_ARG_GROUPS * C, C)
    right = np.zeros((N_LEVELS, C, 1), np.float32)
    for lvl in range(N_LEVELS):
        h = 1 << lvl
        right[lvl, :, 0] = ((np.arange(C) // h) % 2 == 1)
    return np.concatenate([w, w], axis=1), masks, right


def _hgrn_kernel(q_ref, f_ref, i_ref, g_ref, lb_ref, nw_ref, seg_ref, mask_ref, right_ref, o_ref, state_ref,
                 *, n_chunks):
    @pl.when(pl.program_id(1) == 0)
    def _():
        state_ref[...] = jnp.zeros_like(state_ref)

    lb = lb_ref[...]
    nw = nw_ref[...]
    seg = seg_ref[...]

    def chunk(c, carry):
        rows = pl.ds(pl.multiple_of(c * CHUNK, CHUNK), CHUNK)
        q = q_ref[rows, :]
        qf = q * jax.nn.sigmoid(q)
        f = lb + (1.0 - lb) * jax.nn.sigmoid(f_ref[rows, :])
        logf = jnp.log(f)
        k = 1.0 - f
        v = i_ref[rows, :]
        v_bf = v.astype(BF16)

        l_hi = logf.astype(BF16)
        l_lo = (logf - l_hi.astype(F32)).astype(BF16)
        args = _dot(seg, jnp.concatenate([l_hi, l_lo], axis=0))
        e = jnp.exp(args)

        scores = [jnp.zeros((CHUNK, CHUNK), F32) for _ in range(HG_HEADS)]
        for lvl in range(N_LEVELS):
            e_l = e[lvl * CHUNK:(lvl + 1) * CHUNK, :]
            a = (jnp.where(right_ref[lvl] > 0.5, qf, k) * e_l).astype(BF16)
            m = mask_ref[lvl]
            for hd in range(HG_HEADS):
                a_h = a[:, hd * HG_DK:(hd + 1) * HG_DK]
                scores[hd] = scores[hd] + m * _dot_nt(a_h, a_h)

        e_cum = e[N_LEVELS * CHUNK:(N_LEVELS + 1) * CHUNK, :]
        e_suf = e[(N_LEVELS + 1) * CHUNK:(N_LEVELS + 2) * CHUNK, :]
        q_in = (qf * e_cum).astype(BF16)
        k_out = (k * e_suf).astype(BF16)
        e_last = e_cum[CHUNK - 1:CHUNK, :]
        qk = qf * k

        outs = []
        for hd in range(HG_HEADS):
            cols = slice(hd * HG_DK, (hd + 1) * HG_DK)
            st = state_ref[hd]
            diag = jnp.sum(qk[:, cols], axis=-1, keepdims=True)
            o = (_dot_nt(q_in[:, cols], st.astype(BF16))
                 + _dot(scores[hd].astype(BF16), v_bf[:, cols])
                 + diag * v[:, cols])
            state_ref[hd] = st * e_last[:, cols] + _dot_tn(v_bf[:, cols], k_out[:, cols])
            o = o * lax.rsqrt(jnp.mean(o * o, axis=-1, keepdims=True) + RMS_EPS) * nw
            outs.append(o)
        gate = g_ref[rows, :]
        o_all = jnp.concatenate(outs, axis=1) * (gate * jax.nn.sigmoid(gate))
        o_ref[rows, :] = o_all.astype(o_ref.dtype)
        return carry

    lax.fori_loop(0, n_chunks, chunk, 0)


def _hgrn(proj, lb, norm_w, B, S, ts):
    T = B * S
    ns = S // ts
    seg, masks, right = _hgrn_tables()
    col = lambda cb: pl.BlockSpec((ts, HG_W), lambda b, s: (b * ns + s, cb))
    const2 = lambda shape: pl.BlockSpec(shape, lambda b, s: (0,) * len(shape))
    base = (2 * D_MODEL) // HG_W
    return pl.pallas_call(
        functools.partial(_hgrn_kernel, n_chunks=ts // CHUNK),
        grid=(B, ns),
        in_specs=[col(base), col(base + 1), col(base + 2), col(base + 3),
                  const2((1, HG_W)), const2((1, HG_DV)),
                  const2((N_ARG_GROUPS * CHUNK, 2 * CHUNK)),
                  const2((N_LEVELS, CHUNK, CHUNK)), const2((N_LEVELS, CHUNK, 1))],
        out_specs=pl.BlockSpec((ts, HG_W), lambda b, s: (b * ns + s, 0)),
        out_shape=jax.ShapeDtypeStruct((T, HG_W), BF16),
        scratch_shapes=[pltpu.VMEM((HG_HEADS, HG_DV, HG_DK), F32)],
        compiler_params=_cparams(("arbitrary", "arbitrary")),
        name="hgrn2",
    )(proj, proj, proj, proj, lb, norm_w, jnp.asarray(seg, BF16), jnp.asarray(masks), jnp.asarray(right))


def _attn_kernel(sink_ref, q_ref, kv_ref, kvp_ref, o_ref):
    first = pl.program_id(1) == 0
    r = lax.broadcasted_iota(jnp.int32, (ATT_BLK, 2 * ATT_BLK), 0)
    c = lax.broadcasted_iota(jnp.int32, (ATT_BLK, 2 * ATT_BLK), 1)
    dist = r + ATT_BLK - c
    valid = (dist >= 0) & (dist < ATT_BLK) & ((c >= ATT_BLK) | jnp.logical_not(first))
    dist_f = dist.astype(F32)
    lane = lax.broadcasted_iota(jnp.int32, (2 * ATT_BLK, 2 * ATT_KVW), 1)

    kv = jnp.concatenate([kvp_ref[...], kv_ref[...]], axis=0)
    kv_sw = jnp.concatenate([kv[:, ATT_HD:ATT_KVW], kv[:, :ATT_HD],
                             kv[:, ATT_KVW + ATT_HD:], kv[:, ATT_KVW:ATT_KVW + ATT_HD]], axis=1)
    lo = (lane % ATT_KVW) < ATT_HD

    def variant(h, off, vpart):
        src = kv if h == off else kv_sw
        sel = jnp.where(lo if off == 0 else jnp.logical_not(lo), src, 0.0)
        return sel[:, ATT_KVW:] if vpart else sel[:, :ATT_KVW]

    scale = ATT_HD ** -0.5
    for pair in range(ATT_Q_HEADS // 2):
        qp = q_ref[:, pair * 2 * ATT_HD:(pair + 1) * 2 * ATT_HD].astype(BF16)
        acc = jnp.zeros((ATT_BLK, 2 * ATT_HD), F32)
        for off in range(2):
            j = 2 * pair + off
            h = j // ATT_GROUP
            slope = float(2.0 ** (-8.0 * (j + 1) / ATT_Q_HEADS))
            sink = sink_ref[j]
            s = _dot_nt(qp, variant(h, off, False).astype(BF16)) * scale
            logits = jnp.where(valid, s - slope * dist_f, NEG_INF)
            m = jnp.maximum(jnp.max(logits, axis=-1, keepdims=True), sink)
            p = jnp.exp(logits - m)
            den = jnp.sum(p, axis=-1, keepdims=True) + jnp.exp(sink - m)
            acc = acc + _dot(p.astype(BF16), variant(h, off, True).astype(BF16)) / den
        o_ref[:, pair * 2 * ATT_HD:(pair + 1) * 2 * ATT_HD] = acc.astype(o_ref.dtype)


def _attn(proj, sinks, B, S):
    T = B * S
    nb = S // ATT_BLK
    qcol = (2 * D_MODEL + 4 * HG_W) // ATT_QW
    kvcol = (2 * D_MODEL + 4 * HG_W + ATT_QW) // (2 * ATT_KVW)
    return pl.pallas_call(
        _attn_kernel,
        grid=(B, nb),
        in_specs=[pl.BlockSpec(memory_space=pltpu.SMEM),
                  pl.BlockSpec((ATT_BLK, ATT_QW), lambda b, n: (b * nb + n, qcol)),
                  pl.BlockSpec((ATT_BLK, 2 * ATT_KVW), lambda b, n: (b * nb + n, kvcol)),
                  pl.BlockSpec((ATT_BLK, 2 * ATT_KVW), lambda b, n: (b * nb + jnp.maximum(n - 1, 0), kvcol))],
        out_specs=pl.BlockSpec((ATT_BLK, ATT_QW), lambda b, n: (b * nb + n, 0)),
        out_shape=jax.ShapeDtypeStruct((T, ATT_QW), BF16),
        compiler_params=_cparams(("arbitrary", "arbitrary")),
        name="attn",
    )(sinks, proj, proj, proj)


def _merge_kernel(hg_ref, at_ref, ga_ref, gb_ref, x_ref, wa_ref, wb_ref, wo_ref, g1_ref, b1_ref,
                  wr_ref, br_ref, tri_ref, x1_ref, eid_ref, wt_ref, rank_ref, cnt_ref, base_ref):
    ya = _dot(hg_ref[...], wa_ref[...])
    yb = _dot(at_ref[...], wb_ref[...])
    merged = jax.nn.sigmoid(ga_ref[...]) * ya + jax.nn.sigmoid(gb_ref[...]) * yb
    z = DN_ALPHA * x_ref[...] + _dot(merged.astype(BF16), wo_ref[...])
    x1 = _layer_norm(z, g1_ref[...], b1_ref[...])
    x1_ref[...] = x1

    lg = lax.dot_general(wr_ref[...], x1, (((1,), (1,)), ((), ())), precision=lax.Precision.HIGHEST,
                         preferred_element_type=F32) + br_ref[...]
    tm = lg.shape[1]
    g = lg[0:8, :]
    row8 = lax.broadcasted_iota(jnp.int32, (8, tm), 0)
    g = jnp.where(row8 < N_GROUPS, g, -jnp.inf)
    gmax = jnp.max(g, axis=0, keepdims=True)
    gsel = jnp.min(jnp.where(g == gmax, row8, 8), axis=0, keepdims=True)
    gw = 1.0 / jnp.sum(jnp.exp(g - gmax), axis=0, keepdims=True)
    el = jnp.where(gsel == 0, lg[8:16, :],
                   jnp.where(gsel == 1, lg[16:24, :], jnp.where(gsel == 2, lg[24:32, :], lg[32:40, :])))
    v1 = jnp.max(el, axis=0, keepdims=True)
    i1 = jnp.min(jnp.where(el == v1, row8, 8), axis=0, keepdims=True)
    el2 = jnp.where(row8 == i1, -jnp.inf, el)
    v2 = jnp.max(el2, axis=0, keepdims=True)
    i2 = jnp.min(jnp.where(el2 == v2, row8, 8), axis=0, keepdims=True)
    e2 = jnp.exp(v2 - v1)
    den = 1.0 + e2
    e_a = gsel * EPG + i1
    e_b = gsel * EPG + i2
    eid_ref[...] = jnp.concatenate([e_a, e_b], axis=0)
    wt_ref[...] = jnp.concatenate([gw / den, gw * e2 / den], axis=0)

    @pl.when(pl.program_id(0) == 0)
    def _():
        base_ref[...] = jnp.zeros_like(base_ref)

    row_e = lax.broadcasted_iota(jnp.int32, (N_EXPERTS, tm), 0)
    oh_a = jnp.where(row_e == e_a, 1.0, 0.0)
    oh_b = jnp.where(row_e == e_b, 1.0, 0.0)
    tri = tri_ref[...]
    pre_a = _dot(oh_a.astype(BF16), tri)
    pre_b = _dot(oh_b.astype(BF16), tri)
    cnt_a = jnp.sum(oh_a, axis=1, keepdims=True)
    cnt_b = jnp.sum(oh_b, axis=1, keepdims=True)
    base = base_ref[...]
    rank_a = jnp.sum(oh_a * (base + pre_a), axis=0, keepdims=True)
    rank_b = jnp.sum(oh_b * (base + cnt_a + pre_b), axis=0, keepdims=True)
    rank_ref[...] = jnp.concatenate([rank_a, rank_b], axis=0).astype(jnp.int32)
    base = base + cnt_a + cnt_b
    base_ref[...] = base
    cnt_ref[...] = jnp.broadcast_to(base, cnt_ref.shape)


def _merge(hg, att, proj, x2, wa, wb, wo, g1, b1, wr, br, tm):
    T = x2.shape[0]
    row = lambda w, cb=0: pl.BlockSpec((tm, w), lambda i: (i, cb))
    const = lambda shape: pl.BlockSpec(shape, lambda i: (0,) * len(shape))
    lanes = pl.BlockSpec((2, tm), lambda i: (0, i))
    tri = jnp.asarray(np.triu(np.ones((tm, tm), np.float32), 1), BF16)
    return pl.pallas_call(
        _merge_kernel,
        grid=(T // tm,),
        in_specs=[row(HG_W), row(ATT_QW), row(D_MODEL, 0), row(D_MODEL, 1), row(D_MODEL),
                  const((HG_W, D_MODEL)), const((ATT_QW, D_MODEL)), const((D_MODEL, D_MODEL)),
                  const((1, D_MODEL)), const((1, D_MODEL)), const((40, D_MODEL)), const((40, 1)),
                  const((tm, tm))],
        out_specs=[row(D_MODEL), lanes, lanes, lanes, const((N_EXPERTS, 128))],
        out_shape=[jax.ShapeDtypeStruct((T, D_MODEL), F32),
                   jax.ShapeDtypeStruct((2, T), jnp.int32),
                   jax.ShapeDtypeStruct((2, T), F32),
                   jax.ShapeDtypeStruct((2, T), jnp.int32),
                   jax.ShapeDtypeStruct((N_EXPERTS, 128), F32)],
        scratch_shapes=[pltpu.VMEM((N_EXPERTS, 1), F32)],
        compiler_params=_cparams(("arbitrary",)),
        name="merge",
    )(hg, att, proj, proj, x2, wa, wb, wo, g1, b1, wr, br, tri)


SC_WINDOW = 32
SC_IDX_LANES = 128


def _pad_indices(idx):
    rows = idx.reshape(-1, SC_WINDOW)
    return jnp.pad(rows, ((0, 0), (0, SC_IDX_LANES - SC_WINDOW)))


def _sc_mesh():
    return plsc.VectorSubcoreMesh(core_axis_name="core", subcore_axis_name="subcore")


def _sc_scatter2(x, idx_a, idx_b, n_out):
    T, d = x.shape

    @pl.kernel(out_type=jax.ShapeDtypeStruct((n_out, d), x.dtype), mesh=_sc_mesh())
    def scatter(x_hbm, ia_hbm, ib_hbm, o_hbm):
        def body(x_vmem, ia_vmem, ib_vmem):
            pltpu.sync_copy(x_vmem, o_hbm.at[ia_vmem.at[0, pl.ds(0, SC_WINDOW)]])
            pltpu.sync_copy(x_vmem, o_hbm.at[ib_vmem.at[0, pl.ds(0, SC_WINDOW)]])

        idx_spec = pl.BlockSpec((1, SC_IDX_LANES), lambda i: (i, 0))
        pltpu.emit_pipeline(
            body, grid=(T // SC_WINDOW,),
            in_specs=[pl.BlockSpec((SC_WINDOW, d), lambda i: (i, 0)), idx_spec, idx_spec],
            out_specs=[],
            core_axis_name=("core", "subcore"),
            dimension_semantics=(pltpu.PARALLEL,),
        )(x_hbm, ia_hbm, ib_hbm)

    return scatter(x, _pad_indices(idx_a), _pad_indices(idx_b))


def _sc_gather(x, idx):
    n = idx.shape[0]
    d = x.shape[1]

    @pl.kernel(out_type=jax.ShapeDtypeStruct((n, d), x.dtype), mesh=_sc_mesh())
    def gather(x_hbm, i_hbm, o_hbm):
        def body(i_vmem, o_vmem):
            pltpu.sync_copy(x_hbm.at[i_vmem.at[0, pl.ds(0, SC_WINDOW)]], o_vmem)

        pltpu.emit_pipeline(
            body, grid=(n // SC_WINDOW,),
            in_specs=[pl.BlockSpec((1, SC_IDX_LANES), lambda i: (i, 0))],
            out_specs=[pl.BlockSpec((SC_WINDOW, d), lambda i: (i, 0))],
            core_axis_name=("core", "subcore"),
            dimension_semantics=(pltpu.PARALLEL,),
        )(i_hbm, o_hbm)

    return gather(x, _pad_indices(idx))


def _expert_kernel(be_ref, nv_ref, xb_ref, wg_ref, wu_ref, wd_ref, y_ref):
    i = pl.program_id(0)
    nv = nv_ref[i]

    @pl.when(nv > 0)
    def _():
        row = lax.broadcasted_iota(jnp.int32, (MOE_BLOCK, 1), 0)
        xb = jnp.where(row < nv, xb_ref[...], 0.0).astype(BF16)
        g = _dot(xb, wg_ref[0].astype(BF16))
        u = _dot(xb, wu_ref[0].astype(BF16))
        h = (g * jax.nn.sigmoid(g) * u).astype(BF16)
        y_ref[...] = _dot(h, wd_ref[0].astype(BF16))


def _experts(block_expert, block_valid, xbuf, w_gate, w_up, w_down):
    P = xbuf.shape[0]
    nb = P // MOE_BLOCK
    wspec = lambda a, b: pl.BlockSpec((1, a, b), lambda i, be, nv: (be[i], 0, 0))
    rows = pl.BlockSpec((MOE_BLOCK, D_MODEL), lambda i, be, nv: (i, 0))
    return pl.pallas_call(
        _expert_kernel,
        grid_spec=pltpu.PrefetchScalarGridSpec(
            num_scalar_prefetch=2,
            grid=(nb,),
            in_specs=[rows, wspec(D_MODEL, D_EXPERT), wspec(D_MODEL, D_EXPERT), wspec(D_EXPERT, D_MODEL)],
            out_specs=rows),
        out_shape=jax.ShapeDtypeStruct((P, D_MODEL), F32),
        compiler_params=_cparams(("arbitrary",)),
        name="experts",
    )(block_expert, block_valid, xbuf, w_gate, w_up, w_down)


def _combine_kernel(ra_ref, rb_ref, x1_ref, wt_ref, g2_ref, b2_ref, o_ref):
    w = wt_ref[...]
    y = w[:, 0:1] * ra_ref[...] + w[:, 1:2] * rb_ref[...]
    o_ref[...] = _layer_norm(DN_ALPHA * x1_ref[...] + y, g2_ref[...], b2_ref[...])


def _combine(rows2, x1, wt, g2, b2, tm):
    T = x1.shape[0]
    nt = T // tm
    const = lambda shape: pl.BlockSpec(shape, lambda i: (0,) * len(shape))
    return pl.pallas_call(
        _combine_kernel,
        grid=(nt,),
        in_specs=[pl.BlockSpec((tm, D_MODEL), lambda i: (i, 0)), pl.BlockSpec((tm, D_MODEL), lambda i: (i + nt, 0)),
                  pl.BlockSpec((tm, D_MODEL), lambda i: (i, 0)), pl.BlockSpec((tm, 2), lambda i: (i, 0)),
                  const((1, D_MODEL)), const((1, D_MODEL))],
        out_specs=pl.BlockSpec((tm, D_MODEL), lambda i: (i, 0)),
        out_shape=jax.ShapeDtypeStruct((T, D_MODEL), F32),
        compiler_params=_cparams(("arbitrary",)),
        name="combine",
    )(rows2, rows2, x1, wt, g2, b2)


def _slot_layout(eid, rank, counts, T):
    P = 2 * T + N_EXPERTS * MOE_BLOCK
    nb = P // MOE_BLOCK
    padded = ((counts + MOE_BLOCK - 1) // MOE_BLOCK) * MOE_BLOCK
    pend = jnp.cumsum(padded)
    pstart = pend - padded
    experts = jnp.arange(N_EXPERTS, dtype=jnp.int32)
    pos = rank + jnp.sum(jnp.where(eid[:, :, None] == experts, pstart, 0), axis=-1)
    block_start = jnp.arange(nb, dtype=jnp.int32) * MOE_BLOCK
    block_expert = jnp.minimum(jnp.sum(block_start[:, None] >= pend[None, :], axis=1), N_EXPERTS - 1)
    block_valid = jnp.clip(pstart[block_expert] + counts[block_expert] - block_start, 0, MOE_BLOCK)
    block_valid = jnp.where(block_start < pend[-1], block_valid, 0)
    return pos.astype(jnp.int32), block_expert.astype(jnp.int32), block_valid.astype(jnp.int32)


def kernel(x, lb_logits, w_in, hg_norm_w, sinks, w_branch_a, w_branch_b, w_out, ln1_g, ln1_b,
           router_group_w, router_group_b, router_expert_w, router_expert_b,
           w_exp_gate, w_exp_up, w_exp_down, ln2_g, ln2_b):
    B, S, D = x.shape
    assert D == D_MODEL and S % ATT_BLK == 0 and w_in.shape[0] == DEPTH == 1
    T = B * S
    tm = 256 if T % 256 == 0 else ATT_BLK
    ts = 512 if S % 512 == 0 else ATT_BLK
    x2 = x.reshape(T, D)

    lb_all = jnp.cumsum(jax.nn.softmax(lb_logits.astype(F32), axis=0), axis=0)
    lb = lb_all[0].reshape(1, HG_W)
    w_rot = jnp.concatenate([w_in[0][:, PROJ_ROT:], w_in[0][:, :PROJ_ROT]], axis=1).astype(BF16)

    proj = _proj(x2, w_rot, tm)
    hg = _hgrn(proj, lb, hg_norm_w[0].reshape(1, HG_DV).astype(F32), B, S, ts)
    att = _attn(proj, sinks[0].astype(F32), B, S)

    wr = jnp.zeros((40, D), F32).at[0:N_GROUPS].set(router_group_w[0].T).at[8:40].set(router_expert_w[0].T)
    br = jnp.zeros((40, 1), F32).at[0:N_GROUPS, 0].set(router_group_b[0]).at[8:40, 0].set(router_expert_b[0])
    x1, eid, wt, rank, cnt = _merge(hg, att, proj, x2, w_branch_a[0].astype(BF16), w_branch_b[0].astype(BF16),
                                    w_out[0].astype(BF16), ln1_g[0].reshape(1, D), ln1_b[0].reshape(1, D),
                                    wr, br, tm)

    pos, block_expert, block_valid = _slot_layout(eid, rank, cnt[:, 0].astype(jnp.int32), T)
    xbuf = _sc_scatter2(x1, pos[0], pos[1], 2 * T + N_EXPERTS * MOE_BLOCK)
    ybuf = _experts(block_expert, block_valid, xbuf, w_exp_gate[0], w_exp_up[0], w_exp_down[0])
    rows2 = _sc_gather(ybuf, pos.reshape(-1))
    out = _combine(rows2, x1, wt.T, ln2_g[0].reshape(1, D), ln2_b[0].reshape(1, D), tm)
    return out.reshape(B, S, D)
```

```python
import functools

import numpy as np
import jax
import jax.numpy as jnp
from jax import lax
from jax.experimental import pallas as pl
from jax.experimental.pallas import tpu as pltpu
from jax.experimental.pallas import tpu_sc as plsc

F32 = jnp.float32
BF16 = jnp.bfloat16

D_MODEL = 1024
DEPTH = 1
HG_HEADS = 4
HG_DK = 128
HG_DV = 128
HG_W = HG_HEADS * HG_DK
CHUNK = 64
ATT_Q_HEADS = 8
ATT_KV_HEADS = 2
ATT_GROUP = ATT_Q_HEADS // ATT_KV_HEADS
ATT_HD = 64
ATT_QW = ATT_Q_HEADS * ATT_HD
ATT_KVW = ATT_KV_HEADS * ATT_HD
ATT_BLK = 128
N_GROUPS = 4
EPG = 8
N_EXPERTS = N_GROUPS * EPG
D_EXPERT = 512
MOE_BLOCK = 256
DN_ALPHA = (2.0 * DEPTH) ** 0.25
LN_EPS = 1e-5
RMS_EPS = 1e-6
NEG_INF = -1e30

PROJ_W = 4 * HG_W + ATT_QW + 2 * ATT_KVW + 2 * D_MODEL
PROJ_ROT = 4 * HG_W + ATT_QW + 2 * ATT_KVW
N_LEVELS = 6
N_ARG_GROUPS = N_LEVELS + 2

VMEM_LIMIT = 56 * 1024 * 1024
MERGE_SUB = 256


def _cparams(sem):
    return pltpu.CompilerParams(dimension_semantics=sem, vmem_limit_bytes=VMEM_LIMIT)


def _dot(a, b):
    return jnp.dot(a, b, preferred_element_type=F32)


def _dot_nt(a, b):
    return lax.dot_general(a, b, (((1,), (1,)), ((), ())), preferred_element_type=F32)


def _dot_tn(a, b):
    return lax.dot_general(a, b, (((0,), (0,)), ((), ())), preferred_element_type=F32)


def _sigmoid(x):
    return 0.5 * jnp.tanh(0.5 * x) + 0.5


def _silu(x):
    return x * _sigmoid(x)


def _layer_norm(z, g, b):
    mu = jnp.mean(z, axis=-1, keepdims=True)
    zc = z - mu
    var = jnp.mean(zc * zc, axis=-1, keepdims=True)
    return zc * lax.rsqrt(var + LN_EPS) * g + b


def _proj_kernel(x_ref, w_ref, o_ref):
    o_ref[...] = _dot(x_ref[...].astype(BF16), w_ref[...])


def _proj(x2, w_bf, tm):
    T = x2.shape[0]
    return pl.pallas_call(
        _proj_kernel,
        grid=(T // tm,),
        in_specs=[pl.BlockSpec((tm, D_MODEL), lambda i: (i, 0)),
                  pl.BlockSpec((D_MODEL, PROJ_W), lambda i: (0, 0))],
        out_specs=pl.BlockSpec((tm, PROJ_W), lambda i: (i, 0)),
        out_shape=jax.ShapeDtypeStruct((T, PROJ_W), F32),
        compiler_params=_cparams(("arbitrary",)),
        name="proj",
    )(x2, w_bf)


def _hgrn_tables():
    C = CHUNK
    w = np.zeros((N_ARG_GROUPS, C, C), np.float32)
    masks = np.zeros((N_LEVELS, C, C), np.float32)
    for lvl in range(N_LEVELS):
        h = 1 << lvl
        for t in range(C):
            base = (t // (2 * h)) * 2 * h
            m = base + h - 1
            if (t // h) % 2 == 1:
                w[lvl, t, m + 1:t + 1] = 1.0
                masks[lvl, t, base:base + h] = 1.0
            else:
                w[lvl, t, t + 1:m + 1] = 1.0
    for t in range(C):
        w[N_LEVELS, t, :t + 1] = 1.0
        w[N_LEVELS + 1, t, t + 1:] = 1.0
    w = w.reshape(N_ARG_GROUPS * C, C)
    right = np.zeros((N_LEVELS, C, 1), np.float32)
    for lvl in range(N_LEVELS):
        h = 1 << lvl
        right[lvl, :, 0] = ((np.arange(C) // h) % 2 == 1)
    return np.concatenate([w, w], axis=1), masks, right


def _hgrn_kernel(q_ref, f_ref, i_ref, g_ref, lb_ref, nw_ref, seg_ref, mask_ref, right_ref, o_ref, state_ref,
                 *, n_chunks):
    @pl.when(pl.program_id(1) == 0)
    def _():
        state_ref[...] = jnp.zeros_like(state_ref)

    lb = lb_ref[...]
    nw = nw_ref[...]
    seg = seg_ref[...]

    def chunk(c, carry):
        rows = pl.ds(pl.multiple_of(c * CHUNK, CHUNK), CHUNK)
        q = q_ref[rows, :]
        qf = _silu(q)
        f = lb + (1.0 - lb) * _sigmoid(f_ref[rows, :])
        logf = jnp.log(f)
        k = 1.0 - f
        v = i_ref[rows, :]
        v_bf = v.astype(BF16)

        l_hi = logf.astype(BF16)
        l_lo = (logf - l_hi.astype(F32)).astype(BF16)
        args = _dot(seg, jnp.concatenate([l_hi, l_lo], axis=0))
        e = jnp.exp(args)

        scores = [jnp.zeros((CHUNK, CHUNK), F32) for _ in range(HG_HEADS)]
        for lvl in range(N_LEVELS):
            e_l = e[lvl * CHUNK:(lvl + 1) * CHUNK, :]
            a = (jnp.where(right_ref[lvl] > 0.5, qf, k) * e_l).astype(BF16)
            m = mask_ref[lvl]
            for hd in range(HG_HEADS):
                a_h = a[:, hd * HG_DK:(hd + 1) * HG_DK]
                scores[hd] = scores[hd] + m * _dot_nt(a_h, a_h)

        e_cum = e[N_LEVELS * CHUNK:(N_LEVELS + 1) * CHUNK, :]
        e_suf = e[(N_LEVELS + 1) * CHUNK:(N_LEVELS + 2) * CHUNK, :]
        q_in = (qf * e_cum).astype(BF16)
        k_out = (k * e_suf).astype(BF16)
        e_last = e_cum[CHUNK - 1:CHUNK, :]
        qk = qf * k

        outs = []
        for hd in range(HG_HEADS):
            cols = slice(hd * HG_DK, (hd + 1) * HG_DK)
            st = state_ref[hd]
            diag = jnp.sum(qk[:, cols], axis=-1, keepdims=True)
            o = (_dot_nt(q_in[:, cols], st.astype(BF16))
                 + _dot(scores[hd].astype(BF16), v_bf[:, cols])
                 + diag * v[:, cols])
            state_ref[hd] = st * e_last[:, cols] + _dot_tn(v_bf[:, cols], k_out[:, cols])
            o = o * lax.rsqrt(jnp.mean(o * o, axis=-1, keepdims=True) + RMS_EPS) * nw
            outs.append(o)
        gate = g_ref[rows, :]
        o_all = jnp.concatenate(outs, axis=1) * _silu(gate)
        o_ref[rows, :] = o_all.astype(o_ref.dtype)
        return carry

    lax.fori_loop(0, n_chunks, chunk, 0, unroll=2)


def _hgrn(proj, lb, norm_w, B, S, ts):
    T = B * S
    ns = S // ts
    seg, masks, right = _hgrn_tables()
    col = lambda cb: pl.BlockSpec((ts, HG_W), lambda b, s: (b * ns + s, cb))
    const2 = lambda shape: pl.BlockSpec(shape, lambda b, s: (0,) * len(shape))
    base = (2 * D_MODEL) // HG_W
    return pl.pallas_call(
        functools.partial(_hgrn_kernel, n_chunks=ts // CHUNK),
        grid=(B, ns),
        in_specs=[col(base), col(base + 1), col(base + 2), col(base + 3),
                  const2((1, HG_W)), const2((1, HG_DV)),
                  const2((N_ARG_GROUPS * CHUNK, 2 * CHUNK)),
                  const2((N_LEVELS, CHUNK, CHUNK)), const2((N_LEVELS, CHUNK, 1))],
        out_specs=pl.BlockSpec((ts, HG_W), lambda b, s: (b * ns + s, 0)),
        out_shape=jax.ShapeDtypeStruct((T, HG_W), BF16),
        scratch_shapes=[pltpu.VMEM((HG_HEADS, HG_DV, HG_DK), F32)],
        compiler_params=_cparams(("arbitrary", "arbitrary")),
        name="hgrn2",
    )(proj, proj, proj, proj, lb, norm_w, jnp.asarray(seg, BF16), jnp.asarray(masks), jnp.asarray(right))


def _attn_bias():
    r = np.arange(ATT_BLK)[:, None]
    c = np.arange(2 * ATT_BLK)[None, :]
    dist = r + ATT_BLK - c
    window = (dist >= 0) & (dist < ATT_BLK)
    slopes = np.exp2(-8.0 * (np.arange(ATT_Q_HEADS, dtype=np.float32) + 1.0) / ATT_Q_HEADS).astype(np.float32)
    alibi = -slopes[:, None, None] * dist.astype(np.float32)[None]
    later = np.where(window[None], alibi, np.float32(NEG_INF))
    first = np.where((window & (c >= ATT_BLK))[None], alibi, np.float32(NEG_INF))
    return np.stack([later, first]).astype(np.float32)


def _attn_kernel(sink_ref, q_ref, kv_ref, kvp_ref, bias_ref, o_ref):
    table = jnp.where(pl.program_id(1) == 0, 1, 0)
    lane = lax.broadcasted_iota(jnp.int32, (2 * ATT_BLK, 2 * ATT_KVW), 1)
    lo = (lane % ATT_KVW) < ATT_HD

    kv = jnp.concatenate([kvp_ref[...], kv_ref[...]], axis=0).astype(BF16)
    kv_sw = jnp.concatenate([kv[:, ATT_HD:ATT_KVW], kv[:, :ATT_HD],
                             kv[:, ATT_KVW + ATT_HD:], kv[:, ATT_KVW:ATT_KVW + ATT_HD]], axis=1)
    zero = jnp.zeros_like(kv)
    placed = {}
    for h in range(ATT_KV_HEADS):
        for off in range(2):
            src = kv if h == off else kv_sw
            placed[h, off] = jnp.where(lo if off == 0 else jnp.logical_not(lo), src, zero)

    scale = ATT_HD ** -0.5

    def scores(pair):
        qp = (q_ref[:, pair * 2 * ATT_HD:(pair + 1) * 2 * ATT_HD] * scale).astype(BF16)
        return [_dot_nt(qp, placed[(2 * pair + off) // ATT_GROUP, off][:, :ATT_KVW]) for off in range(2)]

    n_pairs = ATT_Q_HEADS // 2
    nxt = scores(0)
    for pair in range(n_pairs):
        cur = nxt
        if pair + 1 < n_pairs:
            nxt = scores(pair + 1)
        acc = jnp.zeros((ATT_BLK, 2 * ATT_HD), F32)
        for off in range(2):
            j = 2 * pair + off
            sink = sink_ref[j]
            logits = cur[off] + bias_ref[table, j]
            m = jnp.maximum(jnp.max(logits, axis=-1, keepdims=True), sink)
            p = jnp.exp(logits - m)
            den = jnp.sum(p, axis=-1, keepdims=True) + jnp.exp(sink - m)
            acc = acc + _dot(p.astype(BF16), placed[j // ATT_GROUP, off][:, ATT_KVW:]) / den
        o_ref[:, pair * 2 * ATT_HD:(pair + 1) * 2 * ATT_HD] = acc.astype(o_ref.dtype)


def _attn(proj, sinks, B, S):
    T = B * S
    nb = S // ATT_BLK
    qcol = (2 * D_MODEL + 4 * HG_W) // ATT_QW
    kvcol = (2 * D_MODEL + 4 * HG_W + ATT_QW) // (2 * ATT_KVW)
    return pl.pallas_call(
        _attn_kernel,
        grid=(B, nb),
        in_specs=[pl.BlockSpec(memory_space=pltpu.SMEM),
                  pl.BlockSpec((ATT_BLK, ATT_QW), lambda b, n: (b * nb + n, qcol)),
                  pl.BlockSpec((ATT_BLK, 2 * ATT_KVW), lambda b, n: (b * nb + n, kvcol)),
                  pl.BlockSpec((ATT_BLK, 2 * ATT_KVW), lambda b, n: (b * nb + jnp.maximum(n - 1, 0), kvcol)),
                  pl.BlockSpec((2, ATT_Q_HEADS, ATT_BLK, 2 * ATT_BLK), lambda b, n: (0, 0, 0, 0))],
        out_specs=pl.BlockSpec((ATT_BLK, ATT_QW), lambda b, n: (b * nb + n, 0)),
        out_shape=jax.ShapeDtypeStruct((T, ATT_QW), BF16),
        compiler_params=_cparams(("arbitrary", "arbitrary")),
        name="attn",
    )(sinks, proj, proj, proj, jnp.asarray(_attn_bias()))


def _merge_kernel(hg_ref, at_ref, ga_ref, gb_ref, x_ref, wa_ref, wb_ref, wo_ref, g1_ref, b1_ref,
                  wr_ref, br_ref, tri_ref, x1_ref, eid_ref, wt_ref, rank_ref, cnt_ref, base_ref, x1b_ref):
    tm = x_ref.shape[0]
    sub = min(tm, MERGE_SUB)
    for r0 in range(0, tm, sub):
        rows = slice(r0, r0 + sub)
        ya = _dot(hg_ref[rows, :], wa_ref[...])
        yb = _dot(at_ref[rows, :], wb_ref[...])
        merged = _sigmoid(ga_ref[rows, :]) * ya + _sigmoid(gb_ref[rows, :]) * yb
        z = DN_ALPHA * x_ref[rows, :] + _dot(merged.astype(BF16), wo_ref[...])
        x1 = _layer_norm(z, g1_ref[...], b1_ref[...])
        x1_ref[rows, :] = x1
        x1b_ref[rows, :] = x1.astype(BF16)

    lg = _dot_nt(wr_ref[...], x1b_ref[...]) + br_ref[...]
    g = lg[0:8, :]
    row8 = lax.broadcasted_iota(jnp.int32, (8, tm), 0)
    g = jnp.where(row8 < N_GROUPS, g, -jnp.inf)
    gmax = jnp.max(g, axis=0, keepdims=True)
    gsel = jnp.min(jnp.where(g == gmax, row8, 8), axis=0, keepdims=True)
    gw = 1.0 / jnp.sum(jnp.exp(g - gmax), axis=0, keepdims=True)
    el = jnp.where(gsel == 0, lg[8:16, :],
                   jnp.where(gsel == 1, lg[16:24, :], jnp.where(gsel == 2, lg[24:32, :], lg[32:40, :])))
    v1 = jnp.max(el, axis=0, keepdims=True)
    i1 = jnp.min(jnp.where(el == v1, row8, 8), axis=0, keepdims=True)
    el2 = jnp.where(row8 == i1, -jnp.inf, el)
    v2 = jnp.max(el2, axis=0, keepdims=True)
    i2 = jnp.min(jnp.where(el2 == v2, row8, 8), axis=0, keepdims=True)
    e2 = jnp.exp(v2 - v1)
    den = 1.0 + e2
    e_a = gsel * EPG + i1
    e_b = gsel * EPG + i2
    eid_ref[...] = jnp.concatenate([e_a, e_b], axis=0)
    wt_ref[...] = jnp.concatenate([gw / den, gw * e2 / den], axis=0)

    @pl.when(pl.program_id(0) == 0)
    def _():
        base_ref[...] = jnp.zeros_like(base_ref)

    row_e = lax.broadcasted_iota(jnp.int32, (N_EXPERTS, tm), 0)
    oh_a = jnp.where(row_e == e_a, 1.0, 0.0)
    oh_b = jnp.where(row_e == e_b, 1.0, 0.0)
    tri = tri_ref[...]
    pre_a = _dot(oh_a.astype(BF16), tri)
    pre_b = _dot(oh_b.astype(BF16), tri)
    cnt_a = jnp.sum(oh_a, axis=1, keepdims=True)
    cnt_b = jnp.sum(oh_b, axis=1, keepdims=True)
    base = base_ref[...]
    rank_a = jnp.sum(oh_a * (base + pre_a), axis=0, keepdims=True)
    rank_b = jnp.sum(oh_b * (base + cnt_a + pre_b), axis=0, keepdims=True)
    rank_ref[...] = jnp.concatenate([rank_a, rank_b], axis=0).astype(jnp.int32)
    base = base + cnt_a + cnt_b
    base_ref[...] = base
    cnt_ref[...] = jnp.broadcast_to(base, cnt_ref.shape)


def _merge(hg, att, proj, x2, wa, wb, wo, g1, b1, wr, br, tm):
    T = x2.shape[0]
    row = lambda w, cb=0: pl.BlockSpec((tm, w), lambda i: (i, cb))
    const = lambda shape: pl.BlockSpec(shape, lambda i: (0,) * len(shape))
    lanes = pl.BlockSpec((2, tm), lambda i: (0, i))
    tri = jnp.asarray(np.triu(np.ones((tm, tm), np.float32), 1), BF16)
    return pl.pallas_call(
        _merge_kernel,
        grid=(T // tm,),
        in_specs=[row(HG_W), row(ATT_QW), row(D_MODEL, 0), row(D_MODEL, 1), row(D_MODEL),
                  const((HG_W, D_MODEL)), const((ATT_QW, D_MODEL)), const((D_MODEL, D_MODEL)),
                  const((1, D_MODEL)), const((1, D_MODEL)), const((40, D_MODEL)), const((40, 1)),
                  const((tm, tm))],
        out_specs=[row(D_MODEL), lanes, lanes, lanes, const((N_EXPERTS, 128))],
        out_shape=[jax.ShapeDtypeStruct((T, D_MODEL), F32),
                   jax.ShapeDtypeStruct((2, T), jnp.int32),
                   jax.ShapeDtypeStruct((2, T), F32),
                   jax.ShapeDtypeStruct((2, T), jnp.int32),
                   jax.ShapeDtypeStruct((N_EXPERTS, 128), F32)],
        scratch_shapes=[pltpu.VMEM((N_EXPERTS, 1), F32), pltpu.VMEM((tm, D_MODEL), BF16)],
        compiler_params=_cparams(("arbitrary",)),
        name="merge",
    )(hg, att, proj, proj, x2, wa, wb, wo, g1, b1, wr, br, tri)


SC_WINDOW = 32
SC_IDX_LANES = 128


def _pad_indices(idx):
    rows = idx.reshape(-1, SC_WINDOW)
    return jnp.pad(rows, ((0, 0), (0, SC_IDX_LANES - SC_WINDOW)))


def _sc_mesh():
    return plsc.VectorSubcoreMesh(core_axis_name="core", subcore_axis_name="subcore")


def _sc_scatter2(x, idx_a, idx_b, n_out):
    T, d = x.shape

    @pl.kernel(out_type=jax.ShapeDtypeStruct((n_out, d), x.dtype), mesh=_sc_mesh())
    def scatter(x_hbm, ia_hbm, ib_hbm, o_hbm):
        def body(x_vmem, ia_vmem, ib_vmem):
            pltpu.sync_copy(x_vmem, o_hbm.at[ia_vmem.at[0, pl.ds(0, SC_WINDOW)]])
            pltpu.sync_copy(x_vmem, o_hbm.at[ib_vmem.at[0, pl.ds(0, SC_WINDOW)]])

        idx_spec = pl.BlockSpec((1, SC_IDX_LANES), lambda i: (i, 0))
        pltpu.emit_pipeline(
            body, grid=(T // SC_WINDOW,),
            in_specs=[pl.BlockSpec((SC_WINDOW, d), lambda i: (i, 0)), idx_spec, idx_spec],
            out_specs=[],
            core_axis_name=("core", "subcore"),
            dimension_semantics=(pltpu.PARALLEL,),
        )(x_hbm, ia_hbm, ib_hbm)

    return scatter(x, _pad_indices(idx_a), _pad_indices(idx_b))


def _sc_gather(x, idx):
    n = idx.shape[0]
    d = x.shape[1]

    @pl.kernel(out_type=jax.ShapeDtypeStruct((n, d), x.dtype), mesh=_sc_mesh())
    def gather(x_hbm, i_hbm, o_hbm):
        def body(i_vmem, o_vmem):
            pltpu.sync_copy(x_hbm.at[i_vmem.at[0, pl.ds(0, SC_WINDOW)]], o_vmem)

        pltpu.emit_pipeline(
            body, grid=(n // SC_WINDOW,),
            in_specs=[pl.BlockSpec((1, SC_IDX_LANES), lambda i: (i, 0))],
            out_specs=[pl.BlockSpec((SC_WINDOW, d), lambda i: (i, 0))],
            core_axis_name=("core", "subcore"),
            dimension_semantics=(pltpu.PARALLEL,),
        )(i_hbm, o_hbm)

    return gather(x, _pad_indices(idx))


def _expert_kernel(be_ref, nv_ref, xb_ref, wg_ref, wu_ref, wd_ref, y_ref, wg_bf, wu_bf, wd_bf):
    i = pl.program_id(0)
    nv = nv_ref[i]

    @pl.when((i == 0) | (be_ref[i] != be_ref[jnp.maximum(i - 1, 0)]))
    def _():
        wg_bf[...] = wg_ref[0].astype(BF16)
        wu_bf[...] = wu_ref[0].astype(BF16)
        wd_bf[...] = wd_ref[0].astype(BF16)

    @pl.when(nv > 0)
    def _():
        row = lax.broadcasted_iota(jnp.int32, (MOE_BLOCK, 1), 0)
        xb = jnp.where(row < nv, xb_ref[...], 0.0).astype(BF16)
        g = _dot(xb, wg_bf[...])
        u = _dot(xb, wu_bf[...])
        h = (_silu(g) * u).astype(BF16)
        y_ref[...] = _dot(h, wd_bf[...])


def _experts(block_expert, block_valid, xbuf, w_gate, w_up, w_down):
    P = xbuf.shape[0]
    nb = P // MOE_BLOCK
    wspec = lambda a, b: pl.BlockSpec((1, a, b), lambda i, be, nv: (be[i], 0, 0))
    rows = pl.BlockSpec((MOE_BLOCK, D_MODEL), lambda i, be, nv: (i, 0))
    return pl.pallas_call(
        _expert_kernel,
        grid_spec=pltpu.PrefetchScalarGridSpec(
            num_scalar_prefetch=2,
            grid=(nb,),
            in_specs=[rows, wspec(D_MODEL, D_EXPERT), wspec(D_MODEL, D_EXPERT), wspec(D_EXPERT, D_MODEL)],
            out_specs=rows,
            scratch_shapes=[pltpu.VMEM((D_MODEL, D_EXPERT), BF16), pltpu.VMEM((D_MODEL, D_EXPERT), BF16),
                            pltpu.VMEM((D_EXPERT, D_MODEL), BF16)]),
        out_shape=jax.ShapeDtypeStruct((P, D_MODEL), F32),
        compiler_params=_cparams(("arbitrary",)),
        name="experts",
    )(block_expert, block_valid, xbuf, w_gate, w_up, w_down)


def _combine_kernel(ra_ref, rb_ref, x1_ref, wt_ref, g2_ref, b2_ref, o_ref):
    w = wt_ref[...]
    y = w[:, 0:1] * ra_ref[...] + w[:, 1:2] * rb_ref[...]
    o_ref[...] = _layer_norm(DN_ALPHA * x1_ref[...] + y, g2_ref[...], b2_ref[...])


def _combine(rows2, x1, wt, g2, b2, tm):
    T = x1.shape[0]
    nt = T // tm
    const = lambda shape: pl.BlockSpec(shape, lambda i: (0,) * len(shape))
    return pl.pallas_call(
        _combine_kernel,
        grid=(nt,),
        in_specs=[pl.BlockSpec((tm, D_MODEL), lambda i: (i, 0)), pl.BlockSpec((tm, D_MODEL), lambda i: (i + nt, 0)),
                  pl.BlockSpec((tm, D_MODEL), lambda i: (i, 0)), pl.BlockSpec((tm, 2), lambda i: (i, 0)),
                  const((1, D_MODEL)), const((1, D_MODEL))],
        out_specs=pl.BlockSpec((tm, D_MODEL), lambda i: (i, 0)),
        out_shape=jax.ShapeDtypeStruct((T, D_MODEL), F32),
        compiler_params=_cparams(("arbitrary",)),
        name="combine",
    )(rows2, rows2, x1, wt, g2, b2)


def _slot_layout(eid, rank, counts, T):
    P = 2 * T + N_EXPERTS * MOE_BLOCK
    nb = P // MOE_BLOCK
    padded = ((counts + MOE_BLOCK - 1) // MOE_BLOCK) * MOE_BLOCK
    pend = jnp.cumsum(padded)
    pstart = pend - padded
    experts = jnp.arange(N_EXPERTS, dtype=jnp.int32)
    pos = rank + jnp.sum(jnp.where(eid[:, :, None] == experts, pstart, 0), axis=-1)
    block_start = jnp.arange(nb, dtype=jnp.int32) * MOE_BLOCK
    block_expert = jnp.minimum(jnp.sum(block_start[:, None] >= pend[None, :], axis=1), N_EXPERTS - 1)
    block_valid = jnp.clip(pstart[block_expert] + counts[block_expert] - block_start, 0, MOE_BLOCK)
    block_valid = jnp.where(block_start < pend[-1], block_valid, 0)
    return pos.astype(jnp.int32), block_expert.astype(jnp.int32), block_valid.astype(jnp.int32)


def kernel(x, lb_logits, w_in, hg_norm_w, sinks, w_branch_a, w_branch_b, w_out, ln1_g, ln1_b,
           router_group_w, router_group_b, router_expert_w, router_expert_b,
           w_exp_gate, w_exp_up, w_exp_down, ln2_g, ln2_b):
    B, S, D = x.shape
    assert D == D_MODEL and S % ATT_BLK == 0 and w_in.shape[0] == DEPTH == 1
    T = B * S
    tm = 256 if T % 256 == 0 else ATT_BLK
    ts = 512 if S % 512 == 0 else ATT_BLK
    x2 = x.reshape(T, D)

    lb_all = jnp.cumsum(jax.nn.softmax(lb_logits.astype(F32), axis=0), axis=0)
    lb = lb_all[0].reshape(1, HG_W)
    w_rot = jnp.concatenate([w_in[0][:, PROJ_ROT:], w_in[0][:, :PROJ_ROT]], axis=1).astype(BF16)

    proj = _proj(x2, w_rot, tm)
    hg = _hgrn(proj, lb, hg_norm_w[0].reshape(1, HG_DV).astype(F32), B, S, ts)
    att = _attn(proj, sinks[0].astype(F32), B, S)

    wr = jnp.zeros((40, D), F32).at[0:N_GROUPS].set(router_group_w[0].T).at[8:40].set(router_expert_w[0].T)
    br = jnp.zeros((40, 1), F32).at[0:N_GROUPS, 0].set(router_group_b[0]).at[8:40, 0].set(router_expert_b[0])
    x1, eid, wt, rank, cnt = _merge(hg, att, proj, x2, w_branch_a[0].astype(BF16), w_branch_b[0].astype(BF16),
                                    w_out[0].astype(BF16), ln1_g[0].reshape(1, D), ln1_b[0].reshape(1, D),
                                    wr.astype(BF16), br, 512 if T % 512 == 0 else tm)

    pos, block_expert, block_valid = _slot_layout(eid, rank, cnt[:, 0].astype(jnp.int32), T)
    xbuf = _sc_scatter2(x1, pos[0], pos[1], 2 * T + N_EXPERTS * MOE_BLOCK)
    ybuf = _experts(block_expert, block_valid, xbuf, w_exp_gate[0], w_exp_up[0], w_exp_down[0])
    rows2 = _sc_gather(ybuf, pos.reshape(-1))
    out = _combine(rows2, x1, wt.T, ln2_g[0].reshape(1, D), ln2_b[0].reshape(1, D), tm)
    return out.reshape(B, S, D)
```

```python
import functools

import numpy as np
import jax
import jax.numpy as jnp
from jax import lax
from jax.experimental import pallas as pl
from jax.experimental.pallas import tpu as pltpu
from jax.experimental.pallas import tpu_sc as plsc

F32 = jnp.float32
BF16 = jnp.bfloat16

D_MODEL = 1024
DEPTH = 1
HG_HEADS = 4
HG_DK = 128
HG_DV = 128
HG_W = HG_HEADS * HG_DK
CHUNK = 64
ATT_Q_HEADS = 8
ATT_KV_HEADS = 2
ATT_GROUP = ATT_Q_HEADS // ATT_KV_HEADS
ATT_HD = 64
ATT_QW = ATT_Q_HEADS * ATT_HD
ATT_KVW = ATT_KV_HEADS * ATT_HD
ATT_BLK = 128
N_GROUPS = 4
EPG = 8
N_EXPERTS = N_GROUPS * EPG
D_EXPERT = 512
MOE_BLOCK = 256
DN_ALPHA = (2.0 * DEPTH) ** 0.25
LN_EPS = 1e-5
RMS_EPS = 1e-6
NEG_INF = -1e30

PROJ_W = 4 * HG_W + ATT_QW + 2 * ATT_KVW + 2 * D_MODEL
PROJ_ROT = 4 * HG_W + ATT_QW + 2 * ATT_KVW
N_LEVELS = 6
N_ARG_GROUPS = N_LEVELS + 2

VMEM_LIMIT = 56 * 1024 * 1024
MERGE_SUB = 256


def _cparams(sem):
    return pltpu.CompilerParams(dimension_semantics=sem, vmem_limit_bytes=VMEM_LIMIT)


def _dot(a, b):
    return jnp.dot(a, b, preferred_element_type=F32)


def _dot_nt(a, b):
    return lax.dot_general(a, b, (((1,), (1,)), ((), ())), preferred_element_type=F32)


def _dot_tn(a, b):
    return lax.dot_general(a, b, (((0,), (0,)), ((), ())), preferred_element_type=F32)


def _sigmoid(x):
    return 0.5 * jnp.tanh(0.5 * x) + 0.5


def _silu(x):
    return x * _sigmoid(x)


def _pack_bf16_pairs(xb):
    n = xb.shape[1] // 2
    lo = lax.bitcast_convert_type(xb[:, :n].astype(F32), jnp.uint32)
    hi = lax.bitcast_convert_type(xb[:, n:].astype(F32), jnp.uint32)
    return (lo >> 16) | hi


def _unpack_bf16_pairs(w):
    lo = lax.bitcast_convert_type(w << 16, F32)
    hi = lax.bitcast_convert_type(w & jnp.uint32(0xFFFF0000), F32)
    return jnp.concatenate([lo, hi], axis=1).astype(BF16)


def _layer_norm(z, g, b):
    mu = jnp.mean(z, axis=-1, keepdims=True)
    zc = z - mu
    var = jnp.mean(zc * zc, axis=-1, keepdims=True)
    return zc * lax.rsqrt(var + LN_EPS) * g + b


def _proj_kernel(x_ref, w_ref, o_ref):
    o_ref[...] = _dot(x_ref[...].astype(BF16), w_ref[...])


def _proj(x2, w_bf, tm):
    T = x2.shape[0]
    return pl.pallas_call(
        _proj_kernel,
        grid=(T // tm,),
        in_specs=[pl.BlockSpec((tm, D_MODEL), lambda i: (i, 0)),
                  pl.BlockSpec((D_MODEL, PROJ_W), lambda i: (0, 0))],
        out_specs=pl.BlockSpec((tm, PROJ_W), lambda i: (i, 0)),
        out_shape=jax.ShapeDtypeStruct((T, PROJ_W), F32),
        compiler_params=_cparams(("arbitrary",)),
        name="proj",
    )(x2, w_bf)


def _hgrn_tables():
    C = CHUNK
    w = np.zeros((N_ARG_GROUPS, C, C), np.float32)
    masks = np.zeros((N_LEVELS, C, C), np.float32)
    for lvl in range(N_LEVELS):
        h = 1 << lvl
        for t in range(C):
            base = (t // (2 * h)) * 2 * h
            m = base + h - 1
            if (t // h) % 2 == 1:
                w[lvl, t, m + 1:t + 1] = 1.0
                masks[lvl, t, base:base + h] = 1.0
            else:
                w[lvl, t, t + 1:m + 1] = 1.0
    for t in range(C):
        w[N_LEVELS, t, :t + 1] = 1.0
        w[N_LEVELS + 1, t, t + 1:] = 1.0
    w = w.reshape(N_ARG_GROUPS * C, C)
    right = np.zeros((N_LEVELS, C, 1), np.float32)
    for lvl in range(N_LEVELS):
        h = 1 << lvl
        right[lvl, :, 0] = ((np.arange(C) // h) % 2 == 1)
    return np.concatenate([w, w], axis=1), masks, right


def _hgrn_kernel(q_ref, f_ref, i_ref, g_ref, lb_ref, nw_ref, seg_ref, mask_ref, right_ref, o_ref, state_ref,
                 *, n_chunks):
    @pl.when(pl.program_id(1) == 0)
    def _():
        state_ref[...] = jnp.zeros_like(state_ref)

    lb = lb_ref[...]
    nw = nw_ref[...]
    seg = seg_ref[...]

    def chunk(c, carry):
        rows = pl.ds(pl.multiple_of(c * CHUNK, CHUNK), CHUNK)
        q = q_ref[rows, :]
        qf = _silu(q)
        f = lb + (1.0 - lb) * _sigmoid(f_ref[rows, :])
        logf = jnp.log(f)
        k = 1.0 - f
        v = i_ref[rows, :]
        v_bf = v.astype(BF16)

        l_hi = logf.astype(BF16)
        l_lo = (logf - l_hi.astype(F32)).astype(BF16)
        args = _dot(seg, jnp.concatenate([l_hi, l_lo], axis=0))
        e = jnp.exp(args)

        scores = [jnp.zeros((CHUNK, CHUNK), F32) for _ in range(HG_HEADS)]
        for lvl in range(N_LEVELS):
            e_l = e[lvl * CHUNK:(lvl + 1) * CHUNK, :]
            a = (jnp.where(right_ref[lvl] > 0.5, qf, k) * e_l).astype(BF16)
            m = mask_ref[lvl]
            for hd in range(HG_HEADS):
                a_h = a[:, hd * HG_DK:(hd + 1) * HG_DK]
                scores[hd] = scores[hd] + m * _dot_nt(a_h, a_h)

        e_cum = e[N_LEVELS * CHUNK:(N_LEVELS + 1) * CHUNK, :]
        e_suf = e[(N_LEVELS + 1) * CHUNK:(N_LEVELS + 2) * CHUNK, :]
        q_in = (qf * e_cum).astype(BF16)
        k_out = (k * e_suf).astype(BF16)
        e_last = e_cum[CHUNK - 1:CHUNK, :]
        qk = qf * k

        outs = []
        for hd in range(HG_HEADS):
            cols = slice(hd * HG_DK, (hd + 1) * HG_DK)
            st = state_ref[hd]
            diag = jnp.sum(qk[:, cols], axis=-1, keepdims=True)
            o = (_dot_nt(q_in[:, cols], st.astype(BF16))
                 + _dot(scores[hd].astype(BF16), v_bf[:, cols])
                 + diag * v[:, cols])
            state_ref[hd] = st * e_last[:, cols] + _dot_tn(v_bf[:, cols], k_out[:, cols])
            o = o * lax.rsqrt(jnp.mean(o * o, axis=-1, keepdims=True) + RMS_EPS) * nw
            outs.append(o)
        gate = g_ref[rows, :]
        o_all = jnp.concatenate(outs, axis=1) * _silu(gate)
        o_ref[rows, :] = o_all.astype(o_ref.dtype)
        return carry

    lax.fori_loop(0, n_chunks, chunk, 0, unroll=2)


def _hgrn(proj, lb, norm_w, B, S, ts):
    T = B * S
    ns = S // ts
    seg, masks, right = _hgrn_tables()
    col = lambda cb: pl.BlockSpec((ts, HG_W), lambda b, s: (b * ns + s, cb))
    const2 = lambda shape: pl.BlockSpec(shape, lambda b, s: (0,) * len(shape))
    base = (2 * D_MODEL) // HG_W
    return pl.pallas_call(
        functools.partial(_hgrn_kernel, n_chunks=ts // CHUNK),
        grid=(B, ns),
        in_specs=[col(base), col(base + 1), col(base + 2), col(base + 3),
                  const2((1, HG_W)), const2((1, HG_DV)),
                  const2((N_ARG_GROUPS * CHUNK, 2 * CHUNK)),
                  const2((N_LEVELS, CHUNK, CHUNK)), const2((N_LEVELS, CHUNK, 1))],
        out_specs=pl.BlockSpec((ts, HG_W), lambda b, s: (b * ns + s, 0)),
        out_shape=jax.ShapeDtypeStruct((T, HG_W), BF16),
        scratch_shapes=[pltpu.VMEM((HG_HEADS, HG_DV, HG_DK), F32)],
        compiler_params=_cparams(("arbitrary", "arbitrary")),
        name="hgrn2",
    )(proj, proj, proj, proj, lb, norm_w, jnp.asarray(seg, BF16), jnp.asarray(masks), jnp.asarray(right))


def _attn_bias():
    r = np.arange(ATT_BLK)[:, None]
    c = np.arange(2 * ATT_BLK)[None, :]
    dist = r + ATT_BLK - c
    window = (dist >= 0) & (dist < ATT_BLK)
    slopes = np.exp2(-8.0 * (np.arange(ATT_Q_HEADS, dtype=np.float32) + 1.0) / ATT_Q_HEADS).astype(np.float32)
    alibi = -slopes[:, None, None] * dist.astype(np.float32)[None]
    later = np.where(window[None], alibi, np.float32(NEG_INF))
    first = np.where((window & (c >= ATT_BLK))[None], alibi, np.float32(NEG_INF))
    return np.stack([later, first]).astype(np.float32)


def _attn_kernel(sink_ref, q_ref, kv_ref, kvp_ref, bias_ref, o_ref):
    table = jnp.where(pl.program_id(1) == 0, 1, 0)
    lane = lax.broadcasted_iota(jnp.int32, (2 * ATT_BLK, 2 * ATT_KVW), 1)
    lo = (lane % ATT_KVW) < ATT_HD

    kv = jnp.concatenate([kvp_ref[...], kv_ref[...]], axis=0).astype(BF16)
    kv_sw = jnp.concatenate([kv[:, ATT_HD:ATT_KVW], kv[:, :ATT_HD],
                             kv[:, ATT_KVW + ATT_HD:], kv[:, ATT_KVW:ATT_KVW + ATT_HD]], axis=1)
    zero = jnp.zeros_like(kv)
    placed = {}
    for h in range(ATT_KV_HEADS):
        for off in range(2):
            src = kv if h == off else kv_sw
            placed[h, off] = jnp.where(lo if off == 0 else jnp.logical_not(lo), src, zero)

    scale = ATT_HD ** -0.5

    def scores(pair):
        qp = (q_ref[:, pair * 2 * ATT_HD:(pair + 1) * 2 * ATT_HD] * scale).astype(BF16)
        return [_dot_nt(qp, placed[(2 * pair + off) // ATT_GROUP, off][:, :ATT_KVW]) for off in range(2)]

    n_pairs = ATT_Q_HEADS // 2
    nxt = scores(0)
    for pair in range(n_pairs):
        cur = nxt
        if pair + 1 < n_pairs:
            nxt = scores(pair + 1)
        acc = jnp.zeros((ATT_BLK, 2 * ATT_HD), F32)
        for off in range(2):
            j = 2 * pair + off
            sink = sink_ref[j]
            logits = cur[off] + bias_ref[table, j]
            m = jnp.maximum(jnp.max(logits, axis=-1, keepdims=True), sink)
            p = jnp.exp(logits - m)
            den = jnp.sum(p, axis=-1, keepdims=True) + jnp.exp(sink - m)
            acc = acc + _dot(p.astype(BF16), placed[j // ATT_GROUP, off][:, ATT_KVW:]) / den
        o_ref[:, pair * 2 * ATT_HD:(pair + 1) * 2 * ATT_HD] = acc.astype(o_ref.dtype)


def _attn(proj, sinks, B, S):
    T = B * S
    nb = S // ATT_BLK
    qcol = (2 * D_MODEL + 4 * HG_W) // ATT_QW
    kvcol = (2 * D_MODEL + 4 * HG_W + ATT_QW) // (2 * ATT_KVW)
    return pl.pallas_call(
        _attn_kernel,
        grid=(B, nb),
        in_specs=[pl.BlockSpec(memory_space=pltpu.SMEM),
                  pl.BlockSpec((ATT_BLK, ATT_QW), lambda b, n: (b * nb + n, qcol)),
                  pl.BlockSpec((ATT_BLK, 2 * ATT_KVW), lambda b, n: (b * nb + n, kvcol)),
                  pl.BlockSpec((ATT_BLK, 2 * ATT_KVW), lambda b, n: (b * nb + jnp.maximum(n - 1, 0), kvcol)),
                  pl.BlockSpec((2, ATT_Q_HEADS, ATT_BLK, 2 * ATT_BLK), lambda b, n: (0, 0, 0, 0))],
        out_specs=pl.BlockSpec((ATT_BLK, ATT_QW), lambda b, n: (b * nb + n, 0)),
        out_shape=jax.ShapeDtypeStruct((T, ATT_QW), BF16),
        compiler_params=_cparams(("arbitrary", "arbitrary")),
        name="attn",
    )(sinks, proj, proj, proj, jnp.asarray(_attn_bias()))


def _merge_kernel(hg_ref, at_ref, ga_ref, gb_ref, x_ref, wa_ref, wb_ref, wo_ref, g1_ref, b1_ref,
                  wr_ref, br_ref, tri_ref, x1_ref, xp_ref, eid_ref, wt_ref, rank_ref, cnt_ref, base_ref, x1b_ref):
    tm = x_ref.shape[0]
    sub = min(tm, MERGE_SUB)
    for r0 in range(0, tm, sub):
        rows = slice(r0, r0 + sub)
        ya = _dot(hg_ref[rows, :], wa_ref[...])
        yb = _dot(at_ref[rows, :], wb_ref[...])
        merged = _sigmoid(ga_ref[rows, :]) * ya + _sigmoid(gb_ref[rows, :]) * yb
        z = DN_ALPHA * x_ref[rows, :] + _dot(merged.astype(BF16), wo_ref[...])
        x1 = _layer_norm(z, g1_ref[...], b1_ref[...])
        x1_ref[rows, :] = x1
        x1b = x1.astype(BF16)
        x1b_ref[rows, :] = x1b
        xp_ref[rows, :] = _pack_bf16_pairs(x1b)

    lg = _dot_nt(wr_ref[...], x1b_ref[...]) + br_ref[...]
    g = lg[0:8, :]
    row8 = lax.broadcasted_iota(jnp.int32, (8, tm), 0)
    g = jnp.where(row8 < N_GROUPS, g, -jnp.inf)
    gmax = jnp.max(g, axis=0, keepdims=True)
    gsel = jnp.min(jnp.where(g == gmax, row8, 8), axis=0, keepdims=True)
    gw = 1.0 / jnp.sum(jnp.exp(g - gmax), axis=0, keepdims=True)
    el = jnp.where(gsel == 0, lg[8:16, :],
                   jnp.where(gsel == 1, lg[16:24, :], jnp.where(gsel == 2, lg[24:32, :], lg[32:40, :])))
    v1 = jnp.max(el, axis=0, keepdims=True)
    i1 = jnp.min(jnp.where(el == v1, row8, 8), axis=0, keepdims=True)
    el2 = jnp.where(row8 == i1, -jnp.inf, el)
    v2 = jnp.max(el2, axis=0, keepdims=True)
    i2 = jnp.min(jnp.where(el2 == v2, row8, 8), axis=0, keepdims=True)
    e2 = jnp.exp(v2 - v1)
    den = 1.0 + e2
    e_a = gsel * EPG + i1
    e_b = gsel * EPG + i2
    eid_ref[...] = jnp.concatenate([e_a, e_b], axis=0)
    wt_ref[...] = jnp.concatenate([gw / den, gw * e2 / den], axis=0)

    @pl.when(pl.program_id(0) == 0)
    def _():
        base_ref[...] = jnp.zeros_like(base_ref)

    row_e = lax.broadcasted_iota(jnp.int32, (N_EXPERTS, tm), 0)
    oh_a = jnp.where(row_e == e_a, 1.0, 0.0)
    oh_b = jnp.where(row_e == e_b, 1.0, 0.0)
    tri = tri_ref[...]
    pre_a = _dot(oh_a.astype(BF16), tri)
    pre_b = _dot(oh_b.astype(BF16), tri)
    cnt_a = jnp.sum(oh_a, axis=1, keepdims=True)
    cnt_b = jnp.sum(oh_b, axis=1, keepdims=True)
    base = base_ref[...]
    rank_a = jnp.sum(oh_a * (base + pre_a), axis=0, keepdims=True)
    rank_b = jnp.sum(oh_b * (base + cnt_a + pre_b), axis=0, keepdims=True)
    rank_ref[...] = jnp.concatenate([rank_a, rank_b], axis=0).astype(jnp.int32)
    base = base + cnt_a + cnt_b
    base_ref[...] = base
    cnt_ref[...] = jnp.broadcast_to(base, cnt_ref.shape)


def _merge(hg, att, proj, x2, wa, wb, wo, g1, b1, wr, br, tm):
    T = x2.shape[0]
    row = lambda w, cb=0: pl.BlockSpec((tm, w), lambda i: (i, cb))
    const = lambda shape: pl.BlockSpec(shape, lambda i: (0,) * len(shape))
    lanes = pl.BlockSpec((2, tm), lambda i: (0, i))
    tri = jnp.asarray(np.triu(np.ones((tm, tm), np.float32), 1), BF16)
    return pl.pallas_call(
        _merge_kernel,
        grid=(T // tm,),
        in_specs=[row(HG_W), row(ATT_QW), row(D_MODEL, 0), row(D_MODEL, 1), row(D_MODEL),
                  const((HG_W, D_MODEL)), const((ATT_QW, D_MODEL)), const((D_MODEL, D_MODEL)),
                  const((1, D_MODEL)), const((1, D_MODEL)), const((40, D_MODEL)), const((40, 1)),
                  const((tm, tm))],
        out_specs=[row(D_MODEL), row(D_MODEL // 2), lanes, lanes, lanes, const((N_EXPERTS, 128))],
        out_shape=[jax.ShapeDtypeStruct((T, D_MODEL), F32),
                   jax.ShapeDtypeStruct((T, D_MODEL // 2), jnp.uint32),
                   jax.ShapeDtypeStruct((2, T), jnp.int32),
                   jax.ShapeDtypeStruct((2, T), F32),
                   jax.ShapeDtypeStruct((2, T), jnp.int32),
                   jax.ShapeDtypeStruct((N_EXPERTS, 128), F32)],
        scratch_shapes=[pltpu.VMEM((N_EXPERTS, 1), F32), pltpu.VMEM((tm, D_MODEL), BF16)],
        compiler_params=_cparams(("arbitrary",)),
        name="merge",
    )(hg, att, proj, proj, x2, wa, wb, wo, g1, b1, wr, br, tri)


SC_WINDOW = 32
SC_IDX_LANES = 128


def _pad_indices(idx):
    rows = idx.reshape(-1, SC_WINDOW)
    return jnp.pad(rows, ((0, 0), (0, SC_IDX_LANES - SC_WINDOW)))


def _sc_mesh():
    return plsc.VectorSubcoreMesh(core_axis_name="core", subcore_axis_name="subcore")


def _sc_scatter2(x, idx_a, idx_b, n_out):
    T, d = x.shape

    @pl.kernel(out_type=jax.ShapeDtypeStruct((n_out, d), x.dtype), mesh=_sc_mesh())
    def scatter(x_hbm, ia_hbm, ib_hbm, o_hbm):
        def body(x_vmem, ia_vmem, ib_vmem):
            pltpu.sync_copy(x_vmem, o_hbm.at[ia_vmem.at[0, pl.ds(0, SC_WINDOW)]])
            pltpu.sync_copy(x_vmem, o_hbm.at[ib_vmem.at[0, pl.ds(0, SC_WINDOW)]])

        idx_spec = pl.BlockSpec((1, SC_IDX_LANES), lambda i: (i, 0))
        pltpu.emit_pipeline(
            body, grid=(T // SC_WINDOW,),
            in_specs=[pl.BlockSpec((SC_WINDOW, d), lambda i: (i, 0)), idx_spec, idx_spec],
            out_specs=[],
            core_axis_name=("core", "subcore"),
            dimension_semantics=(pltpu.PARALLEL,),
        )(x_hbm, ia_hbm, ib_hbm)

    return scatter(x, _pad_indices(idx_a), _pad_indices(idx_b))


def _sc_gather(x, idx):
    n = idx.shape[0]
    d = x.shape[1]

    @pl.kernel(out_type=jax.ShapeDtypeStruct((n, d), x.dtype), mesh=_sc_mesh())
    def gather(x_hbm, i_hbm, o_hbm):
        def body(i_vmem, o_vmem):
            pltpu.sync_copy(x_hbm.at[i_vmem.at[0, pl.ds(0, SC_WINDOW)]], o_vmem)

        pltpu.emit_pipeline(
            body, grid=(n // SC_WINDOW,),
            in_specs=[pl.BlockSpec((1, SC_IDX_LANES), lambda i: (i, 0))],
            out_specs=[pl.BlockSpec((SC_WINDOW, d), lambda i: (i, 0))],
            core_axis_name=("core", "subcore"),
            dimension_semantics=(pltpu.PARALLEL,),
        )(i_hbm, o_hbm)

    return gather(x, _pad_indices(idx))


def _expert_kernel(be_ref, nv_ref, xb_ref, wg_ref, wu_ref, wd_ref, y_ref, wg_bf, wu_bf, wd_bf):
    i = pl.program_id(0)
    nv = nv_ref[i]

    @pl.when((i == 0) | (be_ref[i] != be_ref[jnp.maximum(i - 1, 0)]))
    def _():
        wg_bf[...] = wg_ref[0].astype(BF16)
        wu_bf[...] = wu_ref[0].astype(BF16)
        wd_bf[...] = wd_ref[0].astype(BF16)

    @pl.when(nv > 0)
    def _():
        row = lax.broadcasted_iota(jnp.int32, (MOE_BLOCK, 1), 0)
        xb = _unpack_bf16_pairs(jnp.where(row < nv, xb_ref[...], jnp.uint32(0)))
        g = _dot(xb, wg_bf[...])
        u = _dot(xb, wu_bf[...])
        h = (_silu(g) * u).astype(BF16)
        y_ref[...] = _dot(h, wd_bf[...])


def _experts(block_expert, block_valid, xbuf, w_gate, w_up, w_down):
    P = xbuf.shape[0]
    nb = P // MOE_BLOCK
    wspec = lambda a, b: pl.BlockSpec((1, a, b), lambda i, be, nv: (be[i], 0, 0))
    rows = lambda w: pl.BlockSpec((MOE_BLOCK, w), lambda i, be, nv: (i, 0))
    return pl.pallas_call(
        _expert_kernel,
        grid_spec=pltpu.PrefetchScalarGridSpec(
            num_scalar_prefetch=2,
            grid=(nb,),
            in_specs=[rows(D_MODEL // 2), wspec(D_MODEL, D_EXPERT), wspec(D_MODEL, D_EXPERT),
                      wspec(D_EXPERT, D_MODEL)],
            out_specs=rows(D_MODEL),
            scratch_shapes=[pltpu.VMEM((D_MODEL, D_EXPERT), BF16), pltpu.VMEM((D_MODEL, D_EXPERT), BF16),
                            pltpu.VMEM((D_EXPERT, D_MODEL), BF16)]),
        out_shape=jax.ShapeDtypeStruct((P, D_MODEL), F32),
        compiler_params=_cparams(("arbitrary",)),
        name="experts",
    )(block_expert, block_valid, xbuf, w_gate, w_up, w_down)


def _combine_kernel(ra_ref, rb_ref, x1_ref, wt_ref, g2_ref, b2_ref, o_ref):
    w = wt_ref[...]
    y = w[:, 0:1] * ra_ref[...] + w[:, 1:2] * rb_ref[...]
    o_ref[...] = _layer_norm(DN_ALPHA * x1_ref[...] + y, g2_ref[...], b2_ref[...])


def _combine(rows2, x1, wt, g2, b2, tm):
    T = x1.shape[0]
    nt = T // tm
    const = lambda shape: pl.BlockSpec(shape, lambda i: (0,) * len(shape))
    return pl.pallas_call(
        _combine_kernel,
        grid=(nt,),
        in_specs=[pl.BlockSpec((tm, D_MODEL), lambda i: (i, 0)), pl.BlockSpec((tm, D_MODEL), lambda i: (i + nt, 0)),
                  pl.BlockSpec((tm, D_MODEL), lambda i: (i, 0)), pl.BlockSpec((tm, 2), lambda i: (i, 0)),
                  const((1, D_MODEL)), const((1, D_MODEL))],
        out_specs=pl.BlockSpec((tm, D_MODEL), lambda i: (i, 0)),
        out_shape=jax.ShapeDtypeStruct((T, D_MODEL), F32),
        compiler_params=_cparams(("arbitrary",)),
        name="combine",
    )(rows2, rows2, x1, wt, g2, b2)


def _slot_layout(eid, rank, counts, T):
    P = 2 * T + N_EXPERTS * MOE_BLOCK
    nb = P // MOE_BLOCK
    padded = ((counts + MOE_BLOCK - 1) // MOE_BLOCK) * MOE_BLOCK
    pend = jnp.cumsum(padded)
    pstart = pend - padded
    experts = jnp.arange(N_EXPERTS, dtype=jnp.int32)
    pos = rank + jnp.sum(jnp.where(eid[:, :, None] == experts, pstart, 0), axis=-1)
    block_start = jnp.arange(nb, dtype=jnp.int32) * MOE_BLOCK
    block_expert = jnp.minimum(jnp.sum(block_start[:, None] >= pend[None, :], axis=1), N_EXPERTS - 1)
    block_valid = jnp.clip(pstart[block_expert] + counts[block_expert] - block_start, 0, MOE_BLOCK)
    block_valid = jnp.where(block_start < pend[-1], block_valid, 0)
    return pos.astype(jnp.int32), block_expert.astype(jnp.int32), block_valid.astype(jnp.int32)


def kernel(x, lb_logits, w_in, hg_norm_w, sinks, w_branch_a, w_branch_b, w_out, ln1_g, ln1_b,
           router_group_w, router_group_b, router_expert_w, router_expert_b,
           w_exp_gate, w_exp_up, w_exp_down, ln2_g, ln2_b):
    B, S, D = x.shape
    assert D == D_MODEL and S % ATT_BLK == 0 and w_in.shape[0] == DEPTH == 1
    T = B * S
    tm = 256 if T % 256 == 0 else ATT_BLK
    ts = 512 if S % 512 == 0 else ATT_BLK
    x2 = x.reshape(T, D)

    lb_all = jnp.cumsum(jax.nn.softmax(lb_logits.astype(F32), axis=0), axis=0)
    lb = lb_all[0].reshape(1, HG_W)
    w_rot = jnp.concatenate([w_in[0][:, PROJ_ROT:], w_in[0][:, :PROJ_ROT]], axis=1).astype(BF16)

    proj = _proj(x2, w_rot, tm)
    hg = _hgrn(proj, lb, hg_norm_w[0].reshape(1, HG_DV).astype(F32), B, S, ts)
    att = _attn(proj, sinks[0].astype(F32), B, S)

    wr = jnp.zeros((40, D), F32).at[0:N_GROUPS].set(router_group_w[0].T).at[8:40].set(router_expert_w[0].T)
    br = jnp.zeros((40, 1), F32).at[0:N_GROUPS, 0].set(router_group_b[0]).at[8:40, 0].set(router_expert_b[0])
    x1, xp, eid, wt, rank, cnt = _merge(hg, att, proj, x2, w_branch_a[0].astype(BF16), w_branch_b[0].astype(BF16),
                                    w_out[0].astype(BF16), ln1_g[0].reshape(1, D), ln1_b[0].reshape(1, D),
                                    wr.astype(BF16), br, 512 if T % 512 == 0 else tm)

    pos, block_expert, block_valid = _slot_layout(eid, rank, cnt[:, 0].astype(jnp.int32), T)
    xbuf = _sc_scatter2(xp, pos[0], pos[1], 2 * T + N_EXPERTS * MOE_BLOCK)
    ybuf = _experts(block_expert, block_valid, xbuf, w_exp_gate[0], w_exp_up[0], w_exp_down[0])
    rows2 = _sc_gather(ybuf, pos.reshape(-1))
    out = _combine(rows2, x1, wt.T, ln2_g[0].reshape(1, D), ln2_b[0].reshape(1, D), tm)
    return out.reshape(B, S, D)
```

```python
import functools

import numpy as np
import jax
import jax.numpy as jnp
from jax import lax
from jax.experimental import pallas as pl
from jax.experimental.pallas import tpu as pltpu
from jax.experimental.pallas import tpu_sc as plsc

F32 = jnp.float32
BF16 = jnp.bfloat16

D_MODEL = 1024
DEPTH = 1
HG_HEADS = 4
HG_DK = 128
HG_DV = 128
HG_W = HG_HEADS * HG_DK
CHUNK = 64
ATT_Q_HEADS = 8
ATT_KV_HEADS = 2
ATT_GROUP = ATT_Q_HEADS // ATT_KV_HEADS
ATT_HD = 64
ATT_QW = ATT_Q_HEADS * ATT_HD
ATT_KVW = ATT_KV_HEADS * ATT_HD
ATT_BLK = 128
N_GROUPS = 4
EPG = 8
N_EXPERTS = N_GROUPS * EPG
D_EXPERT = 512
MOE_BLOCK = 256
DN_ALPHA = (2.0 * DEPTH) ** 0.25
LN_EPS = 1e-5
RMS_EPS = 1e-6
NEG_INF = -1e30

PROJ_W = 4 * HG_W + ATT_QW + 2 * ATT_KVW + 2 * D_MODEL
PROJ_ROT = 4 * HG_W + ATT_QW + 2 * ATT_KVW
N_LEVELS = 6
N_ARG_GROUPS = N_LEVELS + 2

VMEM_LIMIT = 56 * 1024 * 1024
MERGE_SUB = 256
N_PARTS = 2


def _cparams(sem):
    return pltpu.CompilerParams(dimension_semantics=sem, vmem_limit_bytes=VMEM_LIMIT)


def _dot(a, b):
    return jnp.dot(a, b, preferred_element_type=F32)


def _dot_nt(a, b):
    return lax.dot_general(a, b, (((1,), (1,)), ((), ())), preferred_element_type=F32)


def _dot_tn(a, b):
    return lax.dot_general(a, b, (((0,), (0,)), ((), ())), preferred_element_type=F32)


def _sigmoid(x):
    return 0.5 * jnp.tanh(0.5 * x) + 0.5


def _silu(x):
    return x * _sigmoid(x)


def _pack_bf16_pairs(xb):
    n = xb.shape[1] // 2
    lo = lax.bitcast_convert_type(xb[:, :n].astype(F32), jnp.uint32)
    hi = lax.bitcast_convert_type(xb[:, n:].astype(F32), jnp.uint32)
    return (lo >> 16) | hi


def _unpack_bf16_pairs(w):
    lo = lax.bitcast_convert_type(w << 16, F32)
    hi = lax.bitcast_convert_type(w & jnp.uint32(0xFFFF0000), F32)
    return jnp.concatenate([lo, hi], axis=1).astype(BF16)


def _layer_norm(z, g, b):
    mu = jnp.mean(z, axis=-1, keepdims=True)
    zc = z - mu
    var = jnp.mean(zc * zc, axis=-1, keepdims=True)
    return zc * lax.rsqrt(var + LN_EPS) * g + b


def _proj_kernel(x_ref, w_ref, o_ref):
    o_ref[...] = _dot(x_ref[...].astype(BF16), w_ref[...])


def _proj(x2, w_bf, tm, row0, T):
    tile0 = row0 // tm
    return pl.pallas_call(
        _proj_kernel,
        grid=(T // tm,),
        in_specs=[pl.BlockSpec((tm, D_MODEL), lambda i: (i + tile0, 0)),
                  pl.BlockSpec((D_MODEL, PROJ_W), lambda i: (0, 0))],
        out_specs=pl.BlockSpec((tm, PROJ_W), lambda i: (i, 0)),
        out_shape=jax.ShapeDtypeStruct((T, PROJ_W), F32),
        compiler_params=_cparams(("arbitrary",)),
        name="proj",
    )(x2, w_bf)


def _hgrn_tables():
    C = CHUNK
    w = np.zeros((N_ARG_GROUPS, C, C), np.float32)
    masks = np.zeros((N_LEVELS, C, C), np.float32)
    for lvl in range(N_LEVELS):
        h = 1 << lvl
        for t in range(C):
            base = (t // (2 * h)) * 2 * h
            m = base + h - 1
            if (t // h) % 2 == 1:
                w[lvl, t, m + 1:t + 1] = 1.0
                masks[lvl, t, base:base + h] = 1.0
            else:
                w[lvl, t, t + 1:m + 1] = 1.0
    for t in range(C):
        w[N_LEVELS, t, :t + 1] = 1.0
        w[N_LEVELS + 1, t, t + 1:] = 1.0
    w = w.reshape(N_ARG_GROUPS * C, C)
    right = np.zeros((N_LEVELS, C, 1), np.float32)
    for lvl in range(N_LEVELS):
        h = 1 << lvl
        right[lvl, :, 0] = ((np.arange(C) // h) % 2 == 1)
    return np.concatenate([w, w], axis=1), masks, right


def _hgrn_kernel(q_ref, f_ref, i_ref, g_ref, lb_ref, nw_ref, seg_ref, mask_ref, right_ref, o_ref, state_ref,
                 *, n_chunks):
    @pl.when(pl.program_id(1) == 0)
    def _():
        state_ref[...] = jnp.zeros_like(state_ref)

    lb = lb_ref[...]
    nw = nw_ref[...]
    seg = seg_ref[...]

    def chunk(c, carry):
        rows = pl.ds(pl.multiple_of(c * CHUNK, CHUNK), CHUNK)
        q = q_ref[rows, :]
        qf = _silu(q)
        f = lb + (1.0 - lb) * _sigmoid(f_ref[rows, :])
        logf = jnp.log(f)
        k = 1.0 - f
        v = i_ref[rows, :]
        v_bf = v.astype(BF16)

        l_hi = logf.astype(BF16)
        l_lo = (logf - l_hi.astype(F32)).astype(BF16)
        args = _dot(seg, jnp.concatenate([l_hi, l_lo], axis=0))
        e = jnp.exp(args)

        scores = [jnp.zeros((CHUNK, CHUNK), F32) for _ in range(HG_HEADS)]
        for lvl in range(N_LEVELS):
            e_l = e[lvl * CHUNK:(lvl + 1) * CHUNK, :]
            a = (jnp.where(right_ref[lvl] > 0.5, qf, k) * e_l).astype(BF16)
            m = mask_ref[lvl]
            for hd in range(HG_HEADS):
                a_h = a[:, hd * HG_DK:(hd + 1) * HG_DK]
                scores[hd] = scores[hd] + m * _dot_nt(a_h, a_h)

        e_cum = e[N_LEVELS * CHUNK:(N_LEVELS + 1) * CHUNK, :]
        e_suf = e[(N_LEVELS + 1) * CHUNK:(N_LEVELS + 2) * CHUNK, :]
        q_in = (qf * e_cum).astype(BF16)
        k_out = (k * e_suf).astype(BF16)
        e_last = e_cum[CHUNK - 1:CHUNK, :]
        qk = qf * k

        outs = []
        for hd in range(HG_HEADS):
            cols = slice(hd * HG_DK, (hd + 1) * HG_DK)
            st = state_ref[hd]
            diag = jnp.sum(qk[:, cols], axis=-1, keepdims=True)
            o = (_dot_nt(q_in[:, cols], st.astype(BF16))
                 + _dot(scores[hd].astype(BF16), v_bf[:, cols])
                 + diag * v[:, cols])
            state_ref[hd] = st * e_last[:, cols] + _dot_tn(v_bf[:, cols], k_out[:, cols])
            o = o * lax.rsqrt(jnp.mean(o * o, axis=-1, keepdims=True) + RMS_EPS) * nw
            outs.append(o)
        gate = g_ref[rows, :]
        o_all = jnp.concatenate(outs, axis=1) * _silu(gate)
        o_ref[rows, :] = o_all.astype(o_ref.dtype)
        return carry

    lax.fori_loop(0, n_chunks, chunk, 0, unroll=2)


def _hgrn(proj, lb, norm_w, B, S, ts):
    T = B * S
    ns = S // ts
    seg, masks, right = _hgrn_tables()
    col = lambda cb: pl.BlockSpec((ts, HG_W), lambda b, s: (b * ns + s, cb))
    const2 = lambda shape: pl.BlockSpec(shape, lambda b, s: (0,) * len(shape))
    base = (2 * D_MODEL) // HG_W
    return pl.pallas_call(
        functools.partial(_hgrn_kernel, n_chunks=ts // CHUNK),
        grid=(B, ns),
        in_specs=[col(base), col(base + 1), col(base + 2), col(base + 3),
                  const2((1, HG_W)), const2((1, HG_DV)),
                  const2((N_ARG_GROUPS * CHUNK, 2 * CHUNK)),
                  const2((N_LEVELS, CHUNK, CHUNK)), const2((N_LEVELS, CHUNK, 1))],
        out_specs=pl.BlockSpec((ts, HG_W), lambda b, s: (b * ns + s, 0)),
        out_shape=jax.ShapeDtypeStruct((T, HG_W), BF16),
        scratch_shapes=[pltpu.VMEM((HG_HEADS, HG_DV, HG_DK), F32)],
        compiler_params=_cparams(("arbitrary", "arbitrary")),
        name="hgrn2",
    )(proj, proj, proj, proj, lb, norm_w, jnp.asarray(seg, BF16), jnp.asarray(masks), jnp.asarray(right))


def _attn_bias():
    r = np.arange(ATT_BLK)[:, None]
    c = np.arange(2 * ATT_BLK)[None, :]
    dist = r + ATT_BLK - c
    window = (dist >= 0) & (dist < ATT_BLK)
    slopes = np.exp2(-8.0 * (np.arange(ATT_Q_HEADS, dtype=np.float32) + 1.0) / ATT_Q_HEADS).astype(np.float32)
    alibi = -slopes[:, None, None] * dist.astype(np.float32)[None]
    later = np.where(window[None], alibi, np.float32(NEG_INF))
    first = np.where((window & (c >= ATT_BLK))[None], alibi, np.float32(NEG_INF))
    return np.stack([later, first]).astype(np.float32)


def _attn_kernel(sink_ref, q_ref, kv_ref, kvp_ref, bias_ref, o_ref):
    table = jnp.where(pl.program_id(1) == 0, 1, 0)
    lane = lax.broadcasted_iota(jnp.int32, (2 * ATT_BLK, 2 * ATT_KVW), 1)
    lo = (lane % ATT_KVW) < ATT_HD

    kv = jnp.concatenate([kvp_ref[...], kv_ref[...]], axis=0).astype(BF16)
    kv_sw = jnp.concatenate([kv[:, ATT_HD:ATT_KVW], kv[:, :ATT_HD],
                             kv[:, ATT_KVW + ATT_HD:], kv[:, ATT_KVW:ATT_KVW + ATT_HD]], axis=1)
    zero = jnp.zeros_like(kv)
    placed = {}
    for h in range(ATT_KV_HEADS):
        for off in range(2):
            src = kv if h == off else kv_sw
            placed[h, off] = jnp.where(lo if off == 0 else jnp.logical_not(lo), src, zero)

    scale = ATT_HD ** -0.5

    def scores(pair):
        qp = (q_ref[:, pair * 2 * ATT_HD:(pair + 1) * 2 * ATT_HD] * scale).astype(BF16)
        return [_dot_nt(qp, placed[(2 * pair + off) // ATT_GROUP, off][:, :ATT_KVW]) for off in range(2)]

    n_pairs = ATT_Q_HEADS // 2
    nxt = scores(0)
    for pair in range(n_pairs):
        cur = nxt
        if pair + 1 < n_pairs:
            nxt = scores(pair + 1)
        acc = jnp.zeros((ATT_BLK, 2 * ATT_HD), F32)
        for off in range(2):
            j = 2 * pair + off
            sink = sink_ref[j]
            logits = cur[off] + bias_ref[table, j]
            m = jnp.maximum(jnp.max(logits, axis=-1, keepdims=True), sink)
            p = jnp.exp(logits - m)
            den = jnp.sum(p, axis=-1, keepdims=True) + jnp.exp(sink - m)
            acc = acc + _dot(p.astype(BF16), placed[j // ATT_GROUP, off][:, ATT_KVW:]) / den
        o_ref[:, pair * 2 * ATT_HD:(pair + 1) * 2 * ATT_HD] = acc.astype(o_ref.dtype)


def _attn(proj, sinks, B, S):
    T = B * S
    nb = S // ATT_BLK
    qcol = (2 * D_MODEL + 4 * HG_W) // ATT_QW
    kvcol = (2 * D_MODEL + 4 * HG_W + ATT_QW) // (2 * ATT_KVW)
    return pl.pallas_call(
        _attn_kernel,
        grid=(B, nb),
        in_specs=[pl.BlockSpec(memory_space=pltpu.SMEM),
                  pl.BlockSpec((ATT_BLK, ATT_QW), lambda b, n: (b * nb + n, qcol)),
                  pl.BlockSpec((ATT_BLK, 2 * ATT_KVW), lambda b, n: (b * nb + n, kvcol)),
                  pl.BlockSpec((ATT_BLK, 2 * ATT_KVW), lambda b, n: (b * nb + jnp.maximum(n - 1, 0), kvcol)),
                  pl.BlockSpec((2, ATT_Q_HEADS, ATT_BLK, 2 * ATT_BLK), lambda b, n: (0, 0, 0, 0))],
        out_specs=pl.BlockSpec((ATT_BLK, ATT_QW), lambda b, n: (b * nb + n, 0)),
        out_shape=jax.ShapeDtypeStruct((T, ATT_QW), BF16),
        compiler_params=_cparams(("arbitrary", "arbitrary")),
        name="attn",
    )(sinks, proj, proj, proj, jnp.asarray(_attn_bias()))


def _merge_kernel(hg_ref, at_ref, ga_ref, gb_ref, x_ref, wa_ref, wb_ref, wo_ref, g1_ref, b1_ref,
                  wr_ref, br_ref, tri_ref, x1_ref, xp_ref, eid_ref, wt_ref, rank_ref, cnt_ref, base_ref, x1b_ref):
    tm = x_ref.shape[0]
    sub = min(tm, MERGE_SUB)
    for r0 in range(0, tm, sub):
        rows = slice(r0, r0 + sub)
        ya = _dot(hg_ref[rows, :], wa_ref[...])
        yb = _dot(at_ref[rows, :], wb_ref[...])
        merged = _sigmoid(ga_ref[rows, :]) * ya + _sigmoid(gb_ref[rows, :]) * yb
        z = DN_ALPHA * x_ref[rows, :] + _dot(merged.astype(BF16), wo_ref[...])
        x1 = _layer_norm(z, g1_ref[...], b1_ref[...])
        x1_ref[rows, :] = x1
        x1b = x1.astype(BF16)
        x1b_ref[rows, :] = x1b
        xp_ref[rows, :] = _pack_bf16_pairs(x1b)

    lg = _dot_nt(wr_ref[...], x1b_ref[...]) + br_ref[...]
    g = lg[0:8, :]
    row8 = lax.broadcasted_iota(jnp.int32, (8, tm), 0)
    g = jnp.where(row8 < N_GROUPS, g, -jnp.inf)
    gmax = jnp.max(g, axis=0, keepdims=True)
    gsel = jnp.min(jnp.where(g == gmax, row8, 8), axis=0, keepdims=True)
    gw = 1.0 / jnp.sum(jnp.exp(g - gmax), axis=0, keepdims=True)
    el = jnp.where(gsel == 0, lg[8:16, :],
                   jnp.where(gsel == 1, lg[16:24, :], jnp.where(gsel == 2, lg[24:32, :], lg[32:40, :])))
    v1 = jnp.max(el, axis=0, keepdims=True)
    i1 = jnp.min(jnp.where(el == v1, row8, 8), axis=0, keepdims=True)
    el2 = jnp.where(row8 == i1, -jnp.inf, el)
    v2 = jnp.max(el2, axis=0, keepdims=True)
    i2 = jnp.min(jnp.where(el2 == v2, row8, 8), axis=0, keepdims=True)
    e2 = jnp.exp(v2 - v1)
    den = 1.0 + e2
    e_a = gsel * EPG + i1
    e_b = gsel * EPG + i2
    eid_ref[...] = jnp.concatenate([e_a, e_b], axis=0)
    wt_ref[...] = jnp.concatenate([gw / den, gw * e2 / den], axis=0)

    @pl.when(pl.program_id(0) == 0)
    def _():
        base_ref[...] = jnp.zeros_like(base_ref)

    row_e = lax.broadcasted_iota(jnp.int32, (N_EXPERTS, tm), 0)
    oh_a = jnp.where(row_e == e_a, 1.0, 0.0)
    oh_b = jnp.where(row_e == e_b, 1.0, 0.0)
    tri = tri_ref[...]
    pre_a = _dot(oh_a.astype(BF16), tri)
    pre_b = _dot(oh_b.astype(BF16), tri)
    cnt_a = jnp.sum(oh_a, axis=1, keepdims=True)
    cnt_b = jnp.sum(oh_b, axis=1, keepdims=True)
    base = base_ref[...]
    rank_a = jnp.sum(oh_a * (base + pre_a), axis=0, keepdims=True)
    rank_b = jnp.sum(oh_b * (base + cnt_a + pre_b), axis=0, keepdims=True)
    rank_ref[...] = jnp.concatenate([rank_a, rank_b], axis=0).astype(jnp.int32)
    base = base + cnt_a + cnt_b
    base_ref[...] = base
    cnt_ref[...] = jnp.broadcast_to(base, cnt_ref.shape)


def _merge(hg, att, proj, x2, wa, wb, wo, g1, b1, wr, br, tm, row0):
    T = hg.shape[0]
    tile0 = row0 // tm
    row = lambda w, cb=0: pl.BlockSpec((tm, w), lambda i: (i, cb))
    const = lambda shape: pl.BlockSpec(shape, lambda i: (0,) * len(shape))
    lanes = pl.BlockSpec((2, tm), lambda i: (0, i))
    tri = jnp.asarray(np.triu(np.ones((tm, tm), np.float32), 1), BF16)
    return pl.pallas_call(
        _merge_kernel,
        grid=(T // tm,),
        in_specs=[row(HG_W), row(ATT_QW), row(D_MODEL, 0), row(D_MODEL, 1),
                  pl.BlockSpec((tm, D_MODEL), lambda i: (i + tile0, 0)),
                  const((HG_W, D_MODEL)), const((ATT_QW, D_MODEL)), const((D_MODEL, D_MODEL)),
                  const((1, D_MODEL)), const((1, D_MODEL)), const((40, D_MODEL)), const((40, 1)),
                  const((tm, tm))],
        out_specs=[row(D_MODEL), row(D_MODEL // 2), lanes, lanes, lanes, const((N_EXPERTS, 128))],
        out_shape=[jax.ShapeDtypeStruct((T, D_MODEL), F32),
                   jax.ShapeDtypeStruct((T, D_MODEL // 2), jnp.uint32),
                   jax.ShapeDtypeStruct((2, T), jnp.int32),
                   jax.ShapeDtypeStruct((2, T), F32),
                   jax.ShapeDtypeStruct((2, T), jnp.int32),
                   jax.ShapeDtypeStruct((N_EXPERTS, 128), F32)],
        scratch_shapes=[pltpu.VMEM((N_EXPERTS, 1), F32), pltpu.VMEM((tm, D_MODEL), BF16)],
        compiler_params=_cparams(("arbitrary",)),
        name="merge",
    )(hg, att, proj, proj, x2, wa, wb, wo, g1, b1, wr, br, tri)


SC_WINDOW = 32
SC_IDX_LANES = 128


def _pad_indices(idx):
    rows = idx.reshape(-1, SC_WINDOW)
    return jnp.pad(rows, ((0, 0), (0, SC_IDX_LANES - SC_WINDOW)))


def _sc_mesh():
    return plsc.VectorSubcoreMesh(core_axis_name="core", subcore_axis_name="subcore")


def _sc_scatter2(x, idx_a, idx_b, n_out):
    T, d = x.shape

    @pl.kernel(out_type=jax.ShapeDtypeStruct((n_out, d), x.dtype), mesh=_sc_mesh())
    def scatter(x_hbm, ia_hbm, ib_hbm, o_hbm):
        def body(x_vmem, ia_vmem, ib_vmem):
            pltpu.sync_copy(x_vmem, o_hbm.at[ia_vmem.at[0, pl.ds(0, SC_WINDOW)]])
            pltpu.sync_copy(x_vmem, o_hbm.at[ib_vmem.at[0, pl.ds(0, SC_WINDOW)]])

        idx_spec = pl.BlockSpec((1, SC_IDX_LANES), lambda i: (i, 0))
        pltpu.emit_pipeline(
            body, grid=(T // SC_WINDOW,),
            in_specs=[pl.BlockSpec((SC_WINDOW, d), lambda i: (i, 0)), idx_spec, idx_spec],
            out_specs=[],
            core_axis_name=("core", "subcore"),
            dimension_semantics=(pltpu.PARALLEL,),
        )(x_hbm, ia_hbm, ib_hbm)

    return scatter(x, _pad_indices(idx_a), _pad_indices(idx_b))


def _sc_gather(x, idx):
    n = idx.shape[0]
    d = x.shape[1]

    @pl.kernel(out_type=jax.ShapeDtypeStruct((n, d), x.dtype), mesh=_sc_mesh())
    def gather(x_hbm, i_hbm, o_hbm):
        def body(i_vmem, o_vmem):
            pltpu.sync_copy(x_hbm.at[i_vmem.at[0, pl.ds(0, SC_WINDOW)]], o_vmem)

        pltpu.emit_pipeline(
            body, grid=(n // SC_WINDOW,),
            in_specs=[pl.BlockSpec((1, SC_IDX_LANES), lambda i: (i, 0))],
            out_specs=[pl.BlockSpec((SC_WINDOW, d), lambda i: (i, 0))],
            core_axis_name=("core", "subcore"),
            dimension_semantics=(pltpu.PARALLEL,),
        )(i_hbm, o_hbm)

    return gather(x, _pad_indices(idx))


def _expert_kernel(be_ref, nv_ref, xb_ref, wg_ref, wu_ref, wd_ref, y_ref, wg_bf, wu_bf, wd_bf):
    i = pl.program_id(0)
    nv = nv_ref[i]

    @pl.when((i == 0) | (be_ref[i] != be_ref[jnp.maximum(i - 1, 0)]))
    def _():
        wg_bf[...] = wg_ref[0].astype(BF16)
        wu_bf[...] = wu_ref[0].astype(BF16)
        wd_bf[...] = wd_ref[0].astype(BF16)

    @pl.when(nv > 0)
    def _():
        row = lax.broadcasted_iota(jnp.int32, (MOE_BLOCK, 1), 0)
        xb = _unpack_bf16_pairs(jnp.where(row < nv, xb_ref[...], jnp.uint32(0)))
        g = _dot(xb, wg_bf[...])
        u = _dot(xb, wu_bf[...])
        h = (_silu(g) * u).astype(BF16)
        y_ref[...] = _dot(h, wd_bf[...])


def _experts(block_expert, block_valid, xbuf, w_gate, w_up, w_down):
    P = xbuf.shape[0]
    nb = P // MOE_BLOCK
    wspec = lambda a, b: pl.BlockSpec((1, a, b), lambda i, be, nv: (be[i], 0, 0))
    rows = lambda w: pl.BlockSpec((MOE_BLOCK, w), lambda i, be, nv: (i, 0))
    return pl.pallas_call(
        _expert_kernel,
        grid_spec=pltpu.PrefetchScalarGridSpec(
            num_scalar_prefetch=2,
            grid=(nb,),
            in_specs=[rows(D_MODEL // 2), wspec(D_MODEL, D_EXPERT), wspec(D_MODEL, D_EXPERT),
                      wspec(D_EXPERT, D_MODEL)],
            out_specs=rows(D_MODEL),
            scratch_shapes=[pltpu.VMEM((D_MODEL, D_EXPERT), BF16), pltpu.VMEM((D_MODEL, D_EXPERT), BF16),
                            pltpu.VMEM((D_EXPERT, D_MODEL), BF16)]),
        out_shape=jax.ShapeDtypeStruct((P, D_MODEL), F32),
        compiler_params=_cparams(("arbitrary",)),
        name="experts",
    )(block_expert, block_valid, xbuf, w_gate, w_up, w_down)


def _combine_kernel(ra_ref, rb_ref, x1_ref, wt_ref, g2_ref, b2_ref, *rest):
    o_ref = rest[-1]
    w = wt_ref[...]
    y = w[:, 0:1] * ra_ref[...] + w[:, 1:2] * rb_ref[...]
    o_ref[...] = _layer_norm(DN_ALPHA * x1_ref[...] + y, g2_ref[...], b2_ref[...])


def _combine(rows2, x1, wt, g2, b2, tm, out_prev, tile0, t_total):
    T = x1.shape[0]
    nt = T // tm
    const = lambda shape: pl.BlockSpec(shape, lambda i: (0,) * len(shape))
    in_specs = [pl.BlockSpec((tm, D_MODEL), lambda i: (i, 0)), pl.BlockSpec((tm, D_MODEL), lambda i: (i + nt, 0)),
                pl.BlockSpec((tm, D_MODEL), lambda i: (i, 0)), pl.BlockSpec((tm, 2), lambda i: (i, 0)),
                const((1, D_MODEL)), const((1, D_MODEL))]
    args = [rows2, rows2, x1, wt, g2, b2]
    aliases = {}
    if out_prev is not None:
        in_specs.append(pl.BlockSpec(memory_space=pl.ANY))
        args.append(out_prev)
        aliases = {len(args) - 1: 0}
    return pl.pallas_call(
        _combine_kernel,
        grid=(nt,),
        in_specs=in_specs,
        out_specs=pl.BlockSpec((tm, D_MODEL), lambda i: (i + tile0, 0)),
        out_shape=jax.ShapeDtypeStruct((t_total, D_MODEL), F32),
        input_output_aliases=aliases,
        compiler_params=_cparams(("arbitrary",)),
        name="combine",
    )(*args)


def _slot_layout(eid, rank, counts, T):
    P = 2 * T + N_EXPERTS * MOE_BLOCK
    nb = P // MOE_BLOCK
    padded = ((counts + MOE_BLOCK - 1) // MOE_BLOCK) * MOE_BLOCK
    pend = jnp.cumsum(padded)
    pstart = pend - padded
    experts = jnp.arange(N_EXPERTS, dtype=jnp.int32)
    pos = rank + jnp.sum(jnp.where(eid[:, :, None] == experts, pstart, 0), axis=-1)
    block_start = jnp.arange(nb, dtype=jnp.int32) * MOE_BLOCK
    block_expert = jnp.minimum(jnp.sum(block_start[:, None] >= pend[None, :], axis=1), N_EXPERTS - 1)
    block_valid = jnp.clip(pstart[block_expert] + counts[block_expert] - block_start, 0, MOE_BLOCK)
    block_valid = jnp.where(block_start < pend[-1], block_valid, 0)
    return pos.astype(jnp.int32), block_expert.astype(jnp.int32), block_valid.astype(jnp.int32)


def kernel(x, lb_logits, w_in, hg_norm_w, sinks, w_branch_a, w_branch_b, w_out, ln1_g, ln1_b,
           router_group_w, router_group_b, router_expert_w, router_expert_b,
           w_exp_gate, w_exp_up, w_exp_down, ln2_g, ln2_b):
    B, S, D = x.shape
    assert D == D_MODEL and S % ATT_BLK == 0 and w_in.shape[0] == DEPTH == 1
    lb_all = jnp.cumsum(jax.nn.softmax(lb_logits.astype(F32), axis=0), axis=0)
    lb = lb_all[0].reshape(1, HG_W)
    w_rot = jnp.concatenate([w_in[0][:, PROJ_ROT:], w_in[0][:, :PROJ_ROT]], axis=1).astype(BF16)
    nw = hg_norm_w[0].reshape(1, HG_DV).astype(F32)
    wa, wb, wo = w_branch_a[0].astype(BF16), w_branch_b[0].astype(BF16), w_out[0].astype(BF16)
    wr = jnp.zeros((40, D), F32).at[0:N_GROUPS].set(router_group_w[0].T).at[8:40].set(router_expert_w[0].T)
    br = jnp.zeros((40, 1), F32).at[0:N_GROUPS, 0].set(router_group_b[0]).at[8:40, 0].set(router_expert_b[0])
    g1, b1 = ln1_g[0].reshape(1, D), ln1_b[0].reshape(1, D)
    g2, b2 = ln2_g[0].reshape(1, D), ln2_b[0].reshape(1, D)

    n_parts = N_PARTS if B % N_PARTS == 0 else 1
    bp = B // n_parts
    tp = bp * S
    tm = 256 if tp % 256 == 0 else ATT_BLK
    ts = 512 if S % 512 == 0 else ATT_BLK
    x2 = x.reshape(B * S, D)
    out = None
    for part in range(n_parts):
        proj = _proj(x2, w_rot, tm, part * tp, tp)
        hg = _hgrn(proj, lb, nw, bp, S, ts)
        att = _attn(proj, sinks[0].astype(F32), bp, S)
        x1, xp, eid, wt, rank, cnt = _merge(hg, att, proj, x2, wa, wb, wo, g1, b1, wr.astype(BF16), br,
                                            512 if tp % 512 == 0 else tm, part * tp)
        pos, block_expert, block_valid = _slot_layout(eid, rank, cnt[:, 0].astype(jnp.int32), tp)
        xbuf = _sc_scatter2(xp, pos[0], pos[1], 2 * tp + N_EXPERTS * MOE_BLOCK)
        ybuf = _experts(block_expert, block_valid, xbuf, w_exp_gate[0], w_exp_up[0], w_exp_down[0])
        rows2 = _sc_gather(ybuf, pos.reshape(-1))
        out = _combine(rows2, x1, wt.T, g2, b2, tm, out, part * (tp // tm), B * S)
    return out.reshape(B, S, D)
```

```python
import functools

import numpy as np
import jax
import jax.numpy as jnp
from jax import lax
from jax.experimental import pallas as pl
from jax.experimental.pallas import tpu as pltpu
from jax.experimental.pallas import tpu_sc as plsc

F32 = jnp.float32
BF16 = jnp.bfloat16

D_MODEL = 1024
DEPTH = 1
HG_HEADS = 4
HG_DK = 128
HG_DV = 128
HG_W = HG_HEADS * HG_DK
CHUNK = 64
ATT_Q_HEADS = 8
ATT_KV_HEADS = 2
ATT_GROUP = ATT_Q_HEADS // ATT_KV_HEADS
ATT_HD = 64
ATT_QW = ATT_Q_HEADS * ATT_HD
ATT_KVW = ATT_KV_HEADS * ATT_HD
ATT_BLK = 128
N_GROUPS = 4
EPG = 8
N_EXPERTS = N_GROUPS * EPG
D_EXPERT = 512
MOE_BLOCK = 256
DN_ALPHA = (2.0 * DEPTH) ** 0.25
LN_EPS = 1e-5
RMS_EPS = 1e-6
NEG_INF = -1e30

PROJ_W = 4 * HG_W + ATT_QW + 2 * ATT_KVW + 2 * D_MODEL
PROJ_ROT = 4 * HG_W + ATT_QW + 2 * ATT_KVW
N_LEVELS = 6
N_ARG_GROUPS = N_LEVELS + 2

VMEM_LIMIT = 56 * 1024 * 1024
MERGE_SUB = 256
N_PARTS = 2


def _cparams(sem):
    return pltpu.CompilerParams(dimension_semantics=sem, vmem_limit_bytes=VMEM_LIMIT)


def _dot(a, b):
    return jnp.dot(a, b, preferred_element_type=F32)


def _dot_nt(a, b):
    return lax.dot_general(a, b, (((1,), (1,)), ((), ())), preferred_element_type=F32)


def _dot_tn(a, b):
    return lax.dot_general(a, b, (((0,), (0,)), ((), ())), preferred_element_type=F32)


def _sigmoid(x):
    return 0.5 * jnp.tanh(0.5 * x) + 0.5


def _silu(x):
    return x * _sigmoid(x)


def _pack_bf16_pairs(xb):
    n = xb.shape[1] // 2
    lo = lax.bitcast_convert_type(xb[:, :n].astype(F32), jnp.uint32)
    hi = lax.bitcast_convert_type(xb[:, n:].astype(F32), jnp.uint32)
    return (lo >> 16) | hi


def _unpack_bf16_pairs(w):
    lo = lax.bitcast_convert_type(w << 16, F32)
    hi = lax.bitcast_convert_type(w & jnp.uint32(0xFFFF0000), F32)
    return jnp.concatenate([lo, hi], axis=1)


def _layer_norm(z, g, b):
    mu = jnp.mean(z, axis=-1, keepdims=True)
    zc = z - mu
    var = jnp.mean(zc * zc, axis=-1, keepdims=True)
    return zc * lax.rsqrt(var + LN_EPS) * g + b


def _proj_kernel(x_ref, w_ref, o_ref):
    o_ref[...] = _dot(x_ref[...].astype(BF16), w_ref[...])


def _proj(x2, w_bf, tm, row0, T):
    tile0 = row0 // tm
    return pl.pallas_call(
        _proj_kernel,
        grid=(T // tm,),
        in_specs=[pl.BlockSpec((tm, D_MODEL), lambda i: (i + tile0, 0)),
                  pl.BlockSpec((D_MODEL, PROJ_W), lambda i: (0, 0))],
        out_specs=pl.BlockSpec((tm, PROJ_W), lambda i: (i, 0)),
        out_shape=jax.ShapeDtypeStruct((T, PROJ_W), F32),
        compiler_params=_cparams(("arbitrary",)),
        name="proj",
    )(x2, w_bf)


def _hgrn_tables():
    C = CHUNK
    w = np.zeros((N_ARG_GROUPS, C, C), np.float32)
    masks = np.zeros((N_LEVELS, C, C), np.float32)
    for lvl in range(N_LEVELS):
        h = 1 << lvl
        for t in range(C):
            base = (t // (2 * h)) * 2 * h
            m = base + h - 1
            if (t // h) % 2 == 1:
                w[lvl, t, m + 1:t + 1] = 1.0
                masks[lvl, t, base:base + h] = 1.0
            else:
                w[lvl, t, t + 1:m + 1] = 1.0
    for t in range(C):
        w[N_LEVELS, t, :t + 1] = 1.0
        w[N_LEVELS + 1, t, t + 1:] = 1.0
    w = w.reshape(N_ARG_GROUPS * C, C)
    right = np.zeros((N_LEVELS, C, 1), np.float32)
    for lvl in range(N_LEVELS):
        h = 1 << lvl
        right[lvl, :, 0] = ((np.arange(C) // h) % 2 == 1)
    return np.concatenate([w, w], axis=1), masks, right


def _hgrn_kernel(q_ref, f_ref, i_ref, g_ref, lb_ref, nw_ref, seg_ref, mask_ref, right_ref, o_ref, state_ref,
                 *, n_chunks):
    @pl.when(pl.program_id(1) == 0)
    def _():
        state_ref[...] = jnp.zeros_like(state_ref)

    lb = lb_ref[...]
    nw = nw_ref[...]
    seg = seg_ref[...]

    def chunk(c, carry):
        rows = pl.ds(pl.multiple_of(c * CHUNK, CHUNK), CHUNK)
        q = q_ref[rows, :]
        qf = _silu(q)
        f = lb + (1.0 - lb) * _sigmoid(f_ref[rows, :])
        logf = jnp.log(f)
        k = 1.0 - f
        v = i_ref[rows, :]
        v_bf = v.astype(BF16)

        l_hi = logf.astype(BF16)
        l_lo = (logf - l_hi.astype(F32)).astype(BF16)
        args = _dot(seg, jnp.concatenate([l_hi, l_lo], axis=0))
        e = jnp.exp(args)

        scores = [jnp.zeros((CHUNK, CHUNK), F32) for _ in range(HG_HEADS)]
        for lvl in range(N_LEVELS):
            e_l = e[lvl * CHUNK:(lvl + 1) * CHUNK, :]
            a = (jnp.where(right_ref[lvl] > 0.5, qf, k) * e_l).astype(BF16)
            m = mask_ref[lvl]
            for hd in range(HG_HEADS):
                a_h = a[:, hd * HG_DK:(hd + 1) * HG_DK]
                scores[hd] = scores[hd] + m * _dot_nt(a_h, a_h)

        e_cum = e[N_LEVELS * CHUNK:(N_LEVELS + 1) * CHUNK, :]
        e_suf = e[(N_LEVELS + 1) * CHUNK:(N_LEVELS + 2) * CHUNK, :]
        q_in = (qf * e_cum).astype(BF16)
        k_out = (k * e_suf).astype(BF16)
        e_last = e_cum[CHUNK - 1:CHUNK, :]
        qk = qf * k

        outs = []
        for hd in range(HG_HEADS):
            cols = slice(hd * HG_DK, (hd + 1) * HG_DK)
            st = state_ref[hd]
            diag = jnp.sum(qk[:, cols], axis=-1, keepdims=True)
            o = (_dot_nt(q_in[:, cols], st.astype(BF16))
                 + _dot(scores[hd].astype(BF16), v_bf[:, cols])
                 + diag * v[:, cols])
            state_ref[hd] = st * e_last[:, cols] + _dot_tn(v_bf[:, cols], k_out[:, cols])
            o = o * lax.rsqrt(jnp.mean(o * o, axis=-1, keepdims=True) + RMS_EPS) * nw
            outs.append(o)
        gate = g_ref[rows, :]
        o_all = jnp.concatenate(outs, axis=1) * _silu(gate)
        o_ref[rows, :] = o_all.astype(o_ref.dtype)
        return carry

    lax.fori_loop(0, n_chunks, chunk, 0, unroll=2)


def _hgrn(proj, lb, norm_w, B, S, ts):
    T = B * S
    ns = S // ts
    seg, masks, right = _hgrn_tables()
    col = lambda cb: pl.BlockSpec((ts, HG_W), lambda b, s: (b * ns + s, cb))
    const2 = lambda shape: pl.BlockSpec(shape, lambda b, s: (0,) * len(shape))
    base = (2 * D_MODEL) // HG_W
    return pl.pallas_call(
        functools.partial(_hgrn_kernel, n_chunks=ts // CHUNK),
        grid=(B, ns),
        in_specs=[col(base), col(base + 1), col(base + 2), col(base + 3),
                  const2((1, HG_W)), const2((1, HG_DV)),
                  const2((N_ARG_GROUPS * CHUNK, 2 * CHUNK)),
                  const2((N_LEVELS, CHUNK, CHUNK)), const2((N_LEVELS, CHUNK, 1))],
        out_specs=pl.BlockSpec((ts, HG_W), lambda b, s: (b * ns + s, 0)),
        out_shape=jax.ShapeDtypeStruct((T, HG_W), BF16),
        scratch_shapes=[pltpu.VMEM((HG_HEADS, HG_DV, HG_DK), F32)],
        compiler_params=_cparams(("arbitrary", "arbitrary")),
        name="hgrn2",
    )(proj, proj, proj, proj, lb, norm_w, jnp.asarray(seg, BF16), jnp.asarray(masks), jnp.asarray(right))


def _attn_bias():
    r = np.arange(ATT_BLK)[:, None]
    c = np.arange(2 * ATT_BLK)[None, :]
    dist = r + ATT_BLK - c
    window = (dist >= 0) & (dist < ATT_BLK)
    slopes = np.exp2(-8.0 * (np.arange(ATT_Q_HEADS, dtype=np.float32) + 1.0) / ATT_Q_HEADS).astype(np.float32)
    alibi = -slopes[:, None, None] * dist.astype(np.float32)[None]
    later = np.where(window[None], alibi, np.float32(NEG_INF))
    first = np.where((window & (c >= ATT_BLK))[None], alibi, np.float32(NEG_INF))
    return np.stack([later, first]).astype(np.float32)


def _attn_kernel(sink_ref, q_ref, kv_ref, kvp_ref, bias_ref, o_ref):
    table = jnp.where(pl.program_id(1) == 0, 1, 0)
    lane = lax.broadcasted_iota(jnp.int32, (2 * ATT_BLK, 2 * ATT_KVW), 1)
    lo = (lane % ATT_KVW) < ATT_HD

    kv = jnp.concatenate([kvp_ref[...], kv_ref[...]], axis=0).astype(BF16)
    kv_sw = jnp.concatenate([kv[:, ATT_HD:ATT_KVW], kv[:, :ATT_HD],
                             kv[:, ATT_KVW + ATT_HD:], kv[:, ATT_KVW:ATT_KVW + ATT_HD]], axis=1)
    zero = jnp.zeros_like(kv)
    placed = {}
    for h in range(ATT_KV_HEADS):
        for off in range(2):
            src = kv if h == off else kv_sw
            placed[h, off] = jnp.where(lo if off == 0 else jnp.logical_not(lo), src, zero)

    scale = ATT_HD ** -0.5

    def scores(pair):
        qp = (q_ref[:, pair * 2 * ATT_HD:(pair + 1) * 2 * ATT_HD] * scale).astype(BF16)
        return [_dot_nt(qp, placed[(2 * pair + off) // ATT_GROUP, off][:, :ATT_KVW]) for off in range(2)]

    n_pairs = ATT_Q_HEADS // 2
    nxt = scores(0)
    for pair in range(n_pairs):
        cur = nxt
        if pair + 1 < n_pairs:
            nxt = scores(pair + 1)
        acc = jnp.zeros((ATT_BLK, 2 * ATT_HD), F32)
        for off in range(2):
            j = 2 * pair + off
            sink = sink_ref[j]
            logits = cur[off] + bias_ref[table, j]
            m = jnp.maximum(jnp.max(logits, axis=-1, keepdims=True), sink)
            p = jnp.exp(logits - m)
            den = jnp.sum(p, axis=-1, keepdims=True) + jnp.exp(sink - m)
            acc = acc + _dot(p.astype(BF16), placed[j // ATT_GROUP, off][:, ATT_KVW:]) / den
        o_ref[:, pair * 2 * ATT_HD:(pair + 1) * 2 * ATT_HD] = acc.astype(o_ref.dtype)


def _attn(proj, sinks, B, S):
    T = B * S
    nb = S // ATT_BLK
    qcol = (2 * D_MODEL + 4 * HG_W) // ATT_QW
    kvcol = (2 * D_MODEL + 4 * HG_W + ATT_QW) // (2 * ATT_KVW)
    return pl.pallas_call(
        _attn_kernel,
        grid=(B, nb),
        in_specs=[pl.BlockSpec(memory_space=pltpu.SMEM),
                  pl.BlockSpec((ATT_BLK, ATT_QW), lambda b, n: (b * nb + n, qcol)),
                  pl.BlockSpec((ATT_BLK, 2 * ATT_KVW), lambda b, n: (b * nb + n, kvcol)),
                  pl.BlockSpec((ATT_BLK, 2 * ATT_KVW), lambda b, n: (b * nb + jnp.maximum(n - 1, 0), kvcol)),
                  pl.BlockSpec((2, ATT_Q_HEADS, ATT_BLK, 2 * ATT_BLK), lambda b, n: (0, 0, 0, 0))],
        out_specs=pl.BlockSpec((ATT_BLK, ATT_QW), lambda b, n: (b * nb + n, 0)),
        out_shape=jax.ShapeDtypeStruct((T, ATT_QW), BF16),
        compiler_params=_cparams(("arbitrary", "arbitrary")),
        name="attn",
    )(sinks, proj, proj, proj, jnp.asarray(_attn_bias()))


def _merge_kernel(hg_ref, at_ref, ga_ref, gb_ref, x_ref, wa_ref, wb_ref, wo_ref, g1_ref, b1_ref,
                  wr_ref, br_ref, tri_ref, x1_ref, xp_ref, eid_ref, wt_ref, rank_ref, cnt_ref, base_ref, x1b_ref):
    tm = x_ref.shape[0]
    sub = min(tm, MERGE_SUB)
    for r0 in range(0, tm, sub):
        rows = slice(r0, r0 + sub)
        ya = _dot(hg_ref[rows, :], wa_ref[...])
        yb = _dot(at_ref[rows, :], wb_ref[...])
        merged = _sigmoid(ga_ref[rows, :]) * ya + _sigmoid(gb_ref[rows, :]) * yb
        z = DN_ALPHA * x_ref[rows, :] + _dot(merged.astype(BF16), wo_ref[...])
        x1 = _layer_norm(z, g1_ref[...], b1_ref[...])
        x1_ref[rows, :] = x1
        x1b = x1.astype(BF16)
        x1b_ref[rows, :] = x1b
        xp_ref[rows, :] = _pack_bf16_pairs(x1b)

    lg = _dot_nt(wr_ref[...], x1b_ref[...]) + br_ref[...]
    g = lg[0:8, :]
    row8 = lax.broadcasted_iota(jnp.int32, (8, tm), 0)
    g = jnp.where(row8 < N_GROUPS, g, -jnp.inf)
    gmax = jnp.max(g, axis=0, keepdims=True)
    gsel = jnp.min(jnp.where(g == gmax, row8, 8), axis=0, keepdims=True)
    gw = 1.0 / jnp.sum(jnp.exp(g - gmax), axis=0, keepdims=True)
    el = jnp.where(gsel == 0, lg[8:16, :],
                   jnp.where(gsel == 1, lg[16:24, :], jnp.where(gsel == 2, lg[24:32, :], lg[32:40, :])))
    v1 = jnp.max(el, axis=0, keepdims=True)
    i1 = jnp.min(jnp.where(el == v1, row8, 8), axis=0, keepdims=True)
    el2 = jnp.where(row8 == i1, -jnp.inf, el)
    v2 = jnp.max(el2, axis=0, keepdims=True)
    i2 = jnp.min(jnp.where(el2 == v2, row8, 8), axis=0, keepdims=True)
    e2 = jnp.exp(v2 - v1)
    den = 1.0 + e2
    e_a = gsel * EPG + i1
    e_b = gsel * EPG + i2
    eid_ref[...] = jnp.concatenate([e_a, e_b], axis=0)
    wt_ref[...] = jnp.concatenate([gw / den, gw * e2 / den], axis=0)

    @pl.when(pl.program_id(0) == 0)
    def _():
        base_ref[...] = jnp.zeros_like(base_ref)

    row_e = lax.broadcasted_iota(jnp.int32, (N_EXPERTS, tm), 0)
    oh_a = jnp.where(row_e == e_a, 1.0, 0.0)
    oh_b = jnp.where(row_e == e_b, 1.0, 0.0)
    tri = tri_ref[...]
    pre_a = _dot(oh_a.astype(BF16), tri)
    pre_b = _dot(oh_b.astype(BF16), tri)
    cnt_a = jnp.sum(oh_a, axis=1, keepdims=True)
    cnt_b = jnp.sum(oh_b, axis=1, keepdims=True)
    base = base_ref[...]
    rank_a = jnp.sum(oh_a * (base + pre_a), axis=0, keepdims=True)
    rank_b = jnp.sum(oh_b * (base + cnt_a + pre_b), axis=0, keepdims=True)
    rank_ref[...] = jnp.concatenate([rank_a, rank_b], axis=0).astype(jnp.int32)
    base = base + cnt_a + cnt_b
    base_ref[...] = base
    cnt_ref[...] = jnp.broadcast_to(base, cnt_ref.shape)


def _merge(hg, att, proj, x2, wa, wb, wo, g1, b1, wr, br, tm, row0):
    T = hg.shape[0]
    tile0 = row0 // tm
    row = lambda w, cb=0: pl.BlockSpec((tm, w), lambda i: (i, cb))
    const = lambda shape: pl.BlockSpec(shape, lambda i: (0,) * len(shape))
    lanes = pl.BlockSpec((2, tm), lambda i: (0, i))
    tri = jnp.asarray(np.triu(np.ones((tm, tm), np.float32), 1), BF16)
    return pl.pallas_call(
        _merge_kernel,
        grid=(T // tm,),
        in_specs=[row(HG_W), row(ATT_QW), row(D_MODEL, 0), row(D_MODEL, 1),
                  pl.BlockSpec((tm, D_MODEL), lambda i: (i + tile0, 0)),
                  const((HG_W, D_MODEL)), const((ATT_QW, D_MODEL)), const((D_MODEL, D_MODEL)),
                  const((1, D_MODEL)), const((1, D_MODEL)), const((40, D_MODEL)), const((40, 1)),
                  const((tm, tm))],
        out_specs=[row(D_MODEL), row(D_MODEL // 2), lanes, lanes, lanes, const((N_EXPERTS, 128))],
        out_shape=[jax.ShapeDtypeStruct((T, D_MODEL), F32),
                   jax.ShapeDtypeStruct((T, D_MODEL // 2), jnp.uint32),
                   jax.ShapeDtypeStruct((2, T), jnp.int32),
                   jax.ShapeDtypeStruct((2, T), F32),
                   jax.ShapeDtypeStruct((2, T), jnp.int32),
                   jax.ShapeDtypeStruct((N_EXPERTS, 128), F32)],
        scratch_shapes=[pltpu.VMEM((N_EXPERTS, 1), F32), pltpu.VMEM((tm, D_MODEL), BF16)],
        compiler_params=_cparams(("arbitrary",)),
        name="merge",
    )(hg, att, proj, proj, x2, wa, wb, wo, g1, b1, wr, br, tri)


SC_WINDOW = 32
SC_IDX_LANES = 128


def _pad_indices(idx):
    rows = idx.reshape(-1, SC_WINDOW)
    return jnp.pad(rows, ((0, 0), (0, SC_IDX_LANES - SC_WINDOW)))


def _sc_mesh():
    return plsc.VectorSubcoreMesh(core_axis_name="core", subcore_axis_name="subcore")


def _sc_scatter2(x, idx_a, idx_b, n_out):
    T, d = x.shape

    @pl.kernel(out_type=jax.ShapeDtypeStruct((n_out, d), x.dtype), mesh=_sc_mesh())
    def scatter(x_hbm, ia_hbm, ib_hbm, o_hbm):
        def body(x_vmem, ia_vmem, ib_vmem):
            pltpu.sync_copy(x_vmem, o_hbm.at[ia_vmem.at[0, pl.ds(0, SC_WINDOW)]])
            pltpu.sync_copy(x_vmem, o_hbm.at[ib_vmem.at[0, pl.ds(0, SC_WINDOW)]])

        idx_spec = pl.BlockSpec((1, SC_IDX_LANES), lambda i: (i, 0))
        pltpu.emit_pipeline(
            body, grid=(T // SC_WINDOW,),
            in_specs=[pl.BlockSpec((SC_WINDOW, d), lambda i: (i, 0)), idx_spec, idx_spec],
            out_specs=[],
            core_axis_name=("core", "subcore"),
            dimension_semantics=(pltpu.PARALLEL,),
        )(x_hbm, ia_hbm, ib_hbm)

    return scatter(x, _pad_indices(idx_a), _pad_indices(idx_b))


def _sc_gather(x, idx):
    n = idx.shape[0]
    d = x.shape[1]

    @pl.kernel(out_type=jax.ShapeDtypeStruct((n, d), x.dtype), mesh=_sc_mesh())
    def gather(x_hbm, i_hbm, o_hbm):
        def body(i_vmem, o_vmem):
            pltpu.sync_copy(x_hbm.at[i_vmem.at[0, pl.ds(0, SC_WINDOW)]], o_vmem)

        pltpu.emit_pipeline(
            body, grid=(n // SC_WINDOW,),
            in_specs=[pl.BlockSpec((1, SC_IDX_LANES), lambda i: (i, 0))],
            out_specs=[pl.BlockSpec((SC_WINDOW, d), lambda i: (i, 0))],
            core_axis_name=("core", "subcore"),
            dimension_semantics=(pltpu.PARALLEL,),
        )(i_hbm, o_hbm)

    return gather(x, _pad_indices(idx))


def _expert_kernel(be_ref, nv_ref, xb_ref, wg_ref, wu_ref, wd_ref, y_ref, wg_bf, wu_bf, wd_bf):
    i = pl.program_id(0)
    nv = nv_ref[i]

    @pl.when((i == 0) | (be_ref[i] != be_ref[jnp.maximum(i - 1, 0)]))
    def _():
        wg_bf[...] = wg_ref[0].astype(BF16)
        wu_bf[...] = wu_ref[0].astype(BF16)
        wd_bf[...] = wd_ref[0].astype(BF16)

    @pl.when(nv > 0)
    def _():
        row = lax.broadcasted_iota(jnp.int32, (MOE_BLOCK, 1), 0)
        xb = _unpack_bf16_pairs(jnp.where(row < nv, xb_ref[...], jnp.uint32(0))).astype(BF16)
        g = _dot(xb, wg_bf[...])
        u = _dot(xb, wu_bf[...])
        h = (_silu(g) * u).astype(BF16)
        y_ref[...] = _pack_bf16_pairs(_dot(h, wd_bf[...]).astype(BF16))


def _experts(block_expert, block_valid, xbuf, w_gate, w_up, w_down):
    P = xbuf.shape[0]
    nb = P // MOE_BLOCK
    wspec = lambda a, b: pl.BlockSpec((1, a, b), lambda i, be, nv: (be[i], 0, 0))
    rows = lambda w: pl.BlockSpec((MOE_BLOCK, w), lambda i, be, nv: (i, 0))
    return pl.pallas_call(
        _expert_kernel,
        grid_spec=pltpu.PrefetchScalarGridSpec(
            num_scalar_prefetch=2,
            grid=(nb,),
            in_specs=[rows(D_MODEL // 2), wspec(D_MODEL, D_EXPERT), wspec(D_MODEL, D_EXPERT),
                      wspec(D_EXPERT, D_MODEL)],
            out_specs=rows(D_MODEL // 2),
            scratch_shapes=[pltpu.VMEM((D_MODEL, D_EXPERT), BF16), pltpu.VMEM((D_MODEL, D_EXPERT), BF16),
                            pltpu.VMEM((D_EXPERT, D_MODEL), BF16)]),
        out_shape=jax.ShapeDtypeStruct((P, D_MODEL // 2), jnp.uint32),
        compiler_params=_cparams(("arbitrary",)),
        name="experts",
    )(block_expert, block_valid, xbuf, w_gate, w_up, w_down)


def _combine_kernel(ra_ref, rb_ref, x1_ref, wt_ref, g2_ref, b2_ref, *rest):
    o_ref = rest[-1]
    w = wt_ref[...]
    y = w[:, 0:1] * _unpack_bf16_pairs(ra_ref[...]) + w[:, 1:2] * _unpack_bf16_pairs(rb_ref[...])
    o_ref[...] = _layer_norm(DN_ALPHA * x1_ref[...] + y, g2_ref[...], b2_ref[...])


def _combine(rows2, x1, wt, g2, b2, tm, out_prev, tile0, t_total):
    T = x1.shape[0]
    nt = T // tm
    const = lambda shape: pl.BlockSpec(shape, lambda i: (0,) * len(shape))
    in_specs = [pl.BlockSpec((tm, D_MODEL // 2), lambda i: (i, 0)),
                pl.BlockSpec((tm, D_MODEL // 2), lambda i: (i + nt, 0)),
                pl.BlockSpec((tm, D_MODEL), lambda i: (i, 0)), pl.BlockSpec((tm, 2), lambda i: (i, 0)),
                const((1, D_MODEL)), const((1, D_MODEL))]
    args = [rows2, rows2, x1, wt, g2, b2]
    aliases = {}
    if out_prev is not None:
        in_specs.append(pl.BlockSpec(memory_space=pl.ANY))
        args.append(out_prev)
        aliases = {len(args) - 1: 0}
    return pl.pallas_call(
        _combine_kernel,
        grid=(nt,),
        in_specs=in_specs,
        out_specs=pl.BlockSpec((tm, D_MODEL), lambda i: (i + tile0, 0)),
        out_shape=jax.ShapeDtypeStruct((t_total, D_MODEL), F32),
        input_output_aliases=aliases,
        compiler_params=_cparams(("arbitrary",)),
        name="combine",
    )(*args)


def _slot_layout(eid, rank, counts, T):
    P = 2 * T + N_EXPERTS * MOE_BLOCK
    nb = P // MOE_BLOCK
    padded = ((counts + MOE_BLOCK - 1) // MOE_BLOCK) * MOE_BLOCK
    pend = jnp.cumsum(padded)
    pstart = pend - padded
    experts = jnp.arange(N_EXPERTS, dtype=jnp.int32)
    pos = rank + jnp.sum(jnp.where(eid[:, :, None] == experts, pstart, 0), axis=-1)
    block_start = jnp.arange(nb, dtype=jnp.int32) * MOE_BLOCK
    block_expert = jnp.minimum(jnp.sum(block_start[:, None] >= pend[None, :], axis=1), N_EXPERTS - 1)
    block_valid = jnp.clip(pstart[block_expert] + counts[block_expert] - block_start, 0, MOE_BLOCK)
    block_valid = jnp.where(block_start < pend[-1], block_valid, 0)
    return pos.astype(jnp.int32), block_expert.astype(jnp.int32), block_valid.astype(jnp.int32)


def kernel(x, lb_logits, w_in, hg_norm_w, sinks, w_branch_a, w_branch_b, w_out, ln1_g, ln1_b,
           router_group_w, router_group_b, router_expert_w, router_expert_b,
           w_exp_gate, w_exp_up, w_exp_down, ln2_g, ln2_b):
    B, S, D = x.shape
    assert D == D_MODEL and S % ATT_BLK == 0 and w_in.shape[0] == DEPTH == 1
    lb_all = jnp.cumsum(jax.nn.softmax(lb_logits.astype(F32), axis=0), axis=0)
    lb = lb_all[0].reshape(1, HG_W)
    w_rot = jnp.concatenate([w_in[0][:, PROJ_ROT:], w_in[0][:, :PROJ_ROT]], axis=1).astype(BF16)
    nw = hg_norm_w[0].reshape(1, HG_DV).astype(F32)
    wa, wb, wo = w_branch_a[0].astype(BF16), w_branch_b[0].astype(BF16), w_out[0].astype(BF16)
    wr = jnp.zeros((40, D), F32).at[0:N_GROUPS].set(router_group_w[0].T).at[8:40].set(router_expert_w[0].T)
    br = jnp.zeros((40, 1), F32).at[0:N_GROUPS, 0].set(router_group_b[0]).at[8:40, 0].set(router_expert_b[0])
    g1, b1 = ln1_g[0].reshape(1, D), ln1_b[0].reshape(1, D)
    g2, b2 = ln2_g[0].reshape(1, D), ln2_b[0].reshape(1, D)

    n_parts = N_PARTS if B % N_PARTS == 0 else 1
    bp = B // n_parts
    tp = bp * S
    tm = 256 if tp % 256 == 0 else ATT_BLK
    ts = 512 if S % 512 == 0 else ATT_BLK
    x2 = x.reshape(B * S, D)
    out = None
    for part in range(n_parts):
        proj = _proj(x2, w_rot, tm, part * tp, tp)
        hg = _hgrn(proj, lb, nw, bp, S, ts)
        att = _attn(proj, sinks[0].astype(F32), bp, S)
        x1, xp, eid, wt, rank, cnt = _merge(hg, att, proj, x2, wa, wb, wo, g1, b1, wr.astype(BF16), br,
                                            512 if tp % 512 == 0 else tm, part * tp)
        pos, block_expert, block_valid = _slot_layout(eid, rank, cnt[:, 0].astype(jnp.int32), tp)
        xbuf = _sc_scatter2(xp, pos[0], pos[1], 2 * tp + N_EXPERTS * MOE_BLOCK)
        ybuf = _experts(block_expert, block_valid, xbuf, w_exp_gate[0], w_exp_up[0], w_exp_down[0])
        rows2 = _sc_gather(ybuf, pos.reshape(-1))
        out = _combine(rows2, x1, wt.T, g2, b2, tm, out, part * (tp // tm), B * S)
    return out.reshape(B, S, D)
```

```python
import functools

import numpy as np
import jax
import jax.numpy as jnp
from jax import lax
from jax.experimental import pallas as pl
from jax.experimental.pallas import tpu as pltpu
from jax.experimental.pallas import tpu_sc as plsc

F32 = jnp.float32
BF16 = jnp.bfloat16

D_MODEL = 1024
DEPTH = 1
HG_HEADS = 4
HG_DK = 128
HG_DV = 128
HG_W = HG_HEADS * HG_DK
CHUNK = 64
ATT_Q_HEADS = 8
ATT_KV_HEADS = 2
ATT_GROUP = ATT_Q_HEADS // ATT_KV_HEADS
ATT_HD = 64
ATT_QW = ATT_Q_HEADS * ATT_HD
ATT_KVW = ATT_KV_HEADS * ATT_HD
ATT_BLK = 128
N_GROUPS = 4
EPG = 8
N_EXPERTS = N_GROUPS * EPG
D_EXPERT = 512
MOE_BLOCK = 256
DN_ALPHA = (2.0 * DEPTH) ** 0.25
LN_EPS = 1e-5
RMS_EPS = 1e-6
NEG_INF = -1e30

PROJ_W = 4 * HG_W + ATT_QW + 2 * ATT_KVW + 2 * D_MODEL
PROJ_ROT = 4 * HG_W + ATT_QW + 2 * ATT_KVW
N_LEVELS = 6
N_ARG_GROUPS = N_LEVELS + 2

VMEM_LIMIT = 56 * 1024 * 1024
MERGE_SUB = 256
N_PARTS = 2


def _cparams(sem):
    return pltpu.CompilerParams(dimension_semantics=sem, vmem_limit_bytes=VMEM_LIMIT)


def _dot(a, b):
    return jnp.dot(a, b, preferred_element_type=F32)


def _dot_nt(a, b):
    return lax.dot_general(a, b, (((1,), (1,)), ((), ())), preferred_element_type=F32)


def _dot_tn(a, b):
    return lax.dot_general(a, b, (((0,), (0,)), ((), ())), preferred_element_type=F32)


def _sigmoid(x):
    return 0.5 * jnp.tanh(0.5 * x) + 0.5


def _silu(x):
    return x * _sigmoid(x)


def _pack_bf16_pairs(xb):
    n = xb.shape[1] // 2
    lo = lax.bitcast_convert_type(xb[:, :n].astype(F32), jnp.uint32)
    hi = lax.bitcast_convert_type(xb[:, n:].astype(F32), jnp.uint32)
    return (lo >> 16) | hi


def _unpack_bf16_pairs(w):
    lo = lax.bitcast_convert_type(w << 16, F32)
    hi = lax.bitcast_convert_type(w & jnp.uint32(0xFFFF0000), F32)
    return jnp.concatenate([lo, hi], axis=1)


def _layer_norm(z, g, b):
    mu = jnp.mean(z, axis=-1, keepdims=True)
    zc = z - mu
    var = jnp.mean(zc * zc, axis=-1, keepdims=True)
    return zc * lax.rsqrt(var + LN_EPS) * g + b


def _proj_kernel(x_ref, w_ref, o_ref):
    o_ref[...] = _dot(x_ref[...].astype(BF16), w_ref[...])


def _proj(x2, w_bf, tm, row0, T):
    tile0 = row0 // tm
    return pl.pallas_call(
        _proj_kernel,
        grid=(T // tm,),
        in_specs=[pl.BlockSpec((tm, D_MODEL), lambda i: (i + tile0, 0)),
                  pl.BlockSpec((D_MODEL, PROJ_W), lambda i: (0, 0))],
        out_specs=pl.BlockSpec((tm, PROJ_W), lambda i: (i, 0)),
        out_shape=jax.ShapeDtypeStruct((T, PROJ_W), F32),
        compiler_params=_cparams(("arbitrary",)),
        name="proj",
    )(x2, w_bf)


def _hgrn_tables():
    C = CHUNK
    w = np.zeros((N_ARG_GROUPS, C, C), np.float32)
    masks = np.zeros((N_LEVELS, C, C), np.float32)
    for lvl in range(N_LEVELS):
        h = 1 << lvl
        for t in range(C):
            base = (t // (2 * h)) * 2 * h
            m = base + h - 1
            if (t // h) % 2 == 1:
                w[lvl, t, m + 1:t + 1] = 1.0
                masks[lvl, t, base:base + h] = 1.0
            else:
                w[lvl, t, t + 1:m + 1] = 1.0
    for t in range(C):
        w[N_LEVELS, t, :t + 1] = 1.0
        w[N_LEVELS + 1, t, t + 1:] = 1.0
    w = w.reshape(N_ARG_GROUPS * C, C)
    return np.concatenate([w, w], axis=1), masks


def _hgrn_chunk(q, fz, v, gate, c0, c1, nw, seg, mask_ref, state_ref):
    h = 0.5 * q
    qf = h + h * jnp.tanh(h)
    t1 = c1 * jnp.tanh(0.5 * fz)
    f = c0 + t1
    k = c1 - t1
    l2 = jnp.log2(f)
    v_bf = v.astype(BF16)
    qf_bf = qf.astype(BF16)
    k_bf = k.astype(BF16)

    l_hi = l2.astype(BF16)
    l_lo = (l2 - l_hi.astype(F32)).astype(BF16)
    args = _dot(seg, jnp.concatenate([l_hi, l_lo], axis=0))
    e = jnp.exp2(args)

    row = lax.broadcasted_iota(jnp.int32, (CHUNK, 1), 0)
    scores = [jnp.zeros((CHUNK, CHUNK), F32) for _ in range(HG_HEADS)]
    for lvl in range(N_LEVELS):
        half = 1 << lvl
        e_l = e[lvl * CHUNK:(lvl + 1) * CHUNK, :].astype(BF16)
        if half >= 16:
            sel = jnp.concatenate([(qf_bf if (r0 // half) % 2 else k_bf)[r0:r0 + half] for r0 in range(0, CHUNK, half)],
                                  axis=0)
        else:
            sel = jnp.where((row // half) % 2 == 1, qf_bf, k_bf)
        a = sel * e_l
        m = mask_ref[lvl]
        for hd in range(HG_HEADS):
            a_h = a[:, hd * HG_DK:(hd + 1) * HG_DK]
            scores[hd] = scores[hd] + m * _dot_nt(a_h, a_h)

    e_cum = e[N_LEVELS * CHUNK:(N_LEVELS + 1) * CHUNK, :]
    e_suf = e[(N_LEVELS + 1) * CHUNK:(N_LEVELS + 2) * CHUNK, :]
    q_in = qf_bf * e_cum.astype(BF16)
    k_out = k_bf * e_suf.astype(BF16)
    e_last = e_cum[CHUNK - 1:CHUNK, :]
    qk = qf * k

    outs = []
    for hd in range(HG_HEADS):
        cols = slice(hd * HG_DK, (hd + 1) * HG_DK)
        st = state_ref[hd]
        diag = jnp.sum(qk[:, cols], axis=-1, keepdims=True)
        o = (_dot_nt(q_in[:, cols], st.astype(BF16))
             + _dot(scores[hd].astype(BF16), v_bf[:, cols])
             + diag * v[:, cols])
        state_ref[hd] = st * e_last[:, cols] + _dot_tn(v_bf[:, cols], k_out[:, cols])
        o = o * lax.rsqrt(jnp.mean(o * o, axis=-1, keepdims=True) + RMS_EPS) * nw
        outs.append(o)
    hg = 0.5 * gate
    return jnp.concatenate(outs, axis=1) * (hg + hg * jnp.tanh(hg))


def _attn_bias():
    r = np.arange(ATT_BLK)[:, None]
    c = np.arange(2 * ATT_BLK)[None, :]
    dist = r + ATT_BLK - c
    window = (dist >= 0) & (dist < ATT_BLK)
    slopes = np.exp2(-8.0 * (np.arange(ATT_Q_HEADS, dtype=np.float32) + 1.0) / ATT_Q_HEADS).astype(np.float32)
    alibi = -slopes[:, None, None] * dist.astype(np.float32)[None]
    later = np.where(window[None], alibi, np.float32(NEG_INF))
    first = np.where((window & (c >= ATT_BLK))[None], alibi, np.float32(NEG_INF))
    return np.stack([later, first]).astype(np.float32)


def _attn_block(q_ref, kv_cur, kv_prev, bias_ref, table, sink_ref, o_ref):
    lane = lax.broadcasted_iota(jnp.int32, (2 * ATT_BLK, 2 * ATT_KVW), 1)
    lo = (lane % ATT_KVW) < ATT_HD

    kv = jnp.concatenate([kv_prev, kv_cur], axis=0).astype(BF16)
    kv_sw = jnp.concatenate([kv[:, ATT_HD:ATT_KVW], kv[:, :ATT_HD],
                             kv[:, ATT_KVW + ATT_HD:], kv[:, ATT_KVW:ATT_KVW + ATT_HD]], axis=1)
    zero = jnp.zeros_like(kv)
    placed = {}
    for h in range(ATT_KV_HEADS):
        for off in range(2):
            src = kv if h == off else kv_sw
            placed[h, off] = jnp.where(lo if off == 0 else jnp.logical_not(lo), src, zero)

    scale = ATT_HD ** -0.5
    for pair in range(ATT_Q_HEADS // 2):
        qp = (q_ref[:, pair * 2 * ATT_HD:(pair + 1) * 2 * ATT_HD] * scale).astype(BF16)
        acc = jnp.zeros((ATT_BLK, 2 * ATT_HD), F32)
        for off in range(2):
            j = 2 * pair + off
            kvh = placed[j // ATT_GROUP, off]
            sink = sink_ref[j]
            logits = _dot_nt(qp, kvh[:, :ATT_KVW]) + bias_ref[table, j]
            m = jnp.maximum(jnp.max(logits, axis=-1, keepdims=True), sink)
            p = jnp.exp(logits - m)
            den = jnp.sum(p, axis=-1, keepdims=True) + jnp.exp(sink - m)
            acc = acc + _dot(p.astype(BF16), kvh[:, ATT_KVW:]) / den
        o_ref[:, pair * 2 * ATT_HD:(pair + 1) * 2 * ATT_HD] = acc.astype(o_ref.dtype)


def _mixers_kernel(sink_ref, q_ref, f_ref, i_ref, g_ref, aq_ref, kv_ref, kvp_ref, lb_ref, nw_ref, seg_ref,
                   mask_ref, bias_ref, hg_ref, at_ref, state_ref):
    first = pl.program_id(1) == 0

    @pl.when(first)
    def _():
        state_ref[...] = jnp.zeros_like(state_ref)

    lb = lb_ref[...]
    c0 = 0.5 + 0.5 * lb
    c1 = 0.5 - 0.5 * lb
    nw = nw_ref[...]
    seg = seg_ref[...]
    for r0 in range(0, ATT_BLK, CHUNK):
        rows = slice(r0, r0 + CHUNK)
        o = _hgrn_chunk(q_ref[rows, :], f_ref[rows, :], i_ref[rows, :], g_ref[rows, :], c0, c1, nw, seg, mask_ref,
                        state_ref)
        hg_ref[rows, :] = o.astype(hg_ref.dtype)
    _attn_block(aq_ref, kv_ref[...], kvp_ref[...], bias_ref, jnp.where(first, 1, 0), sink_ref, at_ref)


def _mixers(proj, lb, norm_w, sinks, B, S):
    T = B * S
    nb = S // ATT_BLK
    seg, masks = _hgrn_tables()
    const = lambda shape: pl.BlockSpec(shape, lambda b, n: (0,) * len(shape))
    hcol = lambda cb: pl.BlockSpec((ATT_BLK, HG_W), lambda b, n: (b * nb + n, cb))
    hbase = (2 * D_MODEL) // HG_W
    qcol = (2 * D_MODEL + 4 * HG_W) // ATT_QW
    kvcol = (2 * D_MODEL + 4 * HG_W + ATT_QW) // (2 * ATT_KVW)
    out = pl.BlockSpec((ATT_BLK, HG_W), lambda b, n: (b * nb + n, 0))
    return pl.pallas_call(
        _mixers_kernel,
        grid=(B, nb),
        in_specs=[pl.BlockSpec(memory_space=pltpu.SMEM),
                  hcol(hbase), hcol(hbase + 1), hcol(hbase + 2), hcol(hbase + 3),
                  pl.BlockSpec((ATT_BLK, ATT_QW), lambda b, n: (b * nb + n, qcol)),
                  pl.BlockSpec((ATT_BLK, 2 * ATT_KVW), lambda b, n: (b * nb + n, kvcol)),
                  pl.BlockSpec((ATT_BLK, 2 * ATT_KVW), lambda b, n: (b * nb + jnp.maximum(n - 1, 0), kvcol)),
                  const((1, HG_W)), const((1, HG_DV)), const((N_ARG_GROUPS * CHUNK, 2 * CHUNK)),
                  const((N_LEVELS, CHUNK, CHUNK)), const((2, ATT_Q_HEADS, ATT_BLK, 2 * ATT_BLK))],
        out_specs=[out, out],
        out_shape=[jax.ShapeDtypeStruct((T, HG_W), BF16), jax.ShapeDtypeStruct((T, ATT_QW), BF16)],
        scratch_shapes=[pltpu.VMEM((HG_HEADS, HG_DV, HG_DK), F32)],
        compiler_params=_cparams(("arbitrary", "arbitrary")),
        name="mixers",
    )(sinks, proj, proj, proj, proj, proj, proj, proj, lb, norm_w, jnp.asarray(seg, BF16), jnp.asarray(masks),
      jnp.asarray(_attn_bias()))


def _merge_kernel(hg_ref, at_ref, ga_ref, gb_ref, x_ref, wa_ref, wb_ref, wo_ref, g1_ref, b1_ref,
                  wr_ref, br_ref, tri_ref, x1_ref, xp_ref, eid_ref, wt_ref, rank_ref, cnt_ref, base_ref, x1b_ref):
    tm = x_ref.shape[0]
    sub = min(tm, MERGE_SUB)
    for r0 in range(0, tm, sub):
        rows = slice(r0, r0 + sub)
        ya = _dot(hg_ref[rows, :], wa_ref[...])
        yb = _dot(at_ref[rows, :], wb_ref[...])
        merged = _sigmoid(ga_ref[rows, :]) * ya + _sigmoid(gb_ref[rows, :]) * yb
        z = DN_ALPHA * x_ref[rows, :] + _dot(merged.astype(BF16), wo_ref[...])
        x1 = _layer_norm(z, g1_ref[...], b1_ref[...])
        x1_ref[rows, :] = x1
        x1b = x1.astype(BF16)
        x1b_ref[rows, :] = x1b
        xp_ref[rows, :] = _pack_bf16_pairs(x1b)

    lg = _dot_nt(wr_ref[...], x1b_ref[...]) + br_ref[...]
    g = lg[0:8, :]
    row8 = lax.broadcasted_iota(jnp.int32, (8, tm), 0)
    g = jnp.where(row8 < N_GROUPS, g, -jnp.inf)
    gmax = jnp.max(g, axis=0, keepdims=True)
    gsel = jnp.min(jnp.where(g == gmax, row8, 8), axis=0, keepdims=True)
    gw = 1.0 / jnp.sum(jnp.exp(g - gmax), axis=0, keepdims=True)
    el = jnp.where(gsel == 0, lg[8:16, :],
                   jnp.where(gsel == 1, lg[16:24, :], jnp.where(gsel == 2, lg[24:32, :], lg[32:40, :])))
    v1 = jnp.max(el, axis=0, keepdims=True)
    i1 = jnp.min(jnp.where(el == v1, row8, 8), axis=0, keepdims=True)
    el2 = jnp.where(row8 == i1, -jnp.inf, el)
    v2 = jnp.max(el2, axis=0, keepdims=True)
    i2 = jnp.min(jnp.where(el2 == v2, row8, 8), axis=0, keepdims=True)
    e2 = jnp.exp(v2 - v1)
    den = 1.0 + e2
    e_a = gsel * EPG + i1
    e_b = gsel * EPG + i2
    eid_ref[...] = jnp.concatenate([e_a, e_b], axis=0)
    wt_ref[...] = jnp.concatenate([gw / den, gw * e2 / den], axis=0)

    @pl.when(pl.program_id(0) == 0)
    def _():
        base_ref[...] = jnp.zeros_like(base_ref)

    row_e = lax.broadcasted_iota(jnp.int32, (N_EXPERTS, tm), 0)
    oh_a = jnp.where(row_e == e_a, 1.0, 0.0)
    oh_b = jnp.where(row_e == e_b, 1.0, 0.0)
    tri = tri_ref[...]
    pre_a = _dot(oh_a.astype(BF16), tri)
    pre_b = _dot(oh_b.astype(BF16), tri)
    cnt_a = jnp.sum(oh_a, axis=1, keepdims=True)
    cnt_b = jnp.sum(oh_b, axis=1, keepdims=True)
    base = base_ref[...]
    rank_a = jnp.sum(oh_a * (base + pre_a), axis=0, keepdims=True)
    rank_b = jnp.sum(oh_b * (base + cnt_a + pre_b), axis=0, keepdims=True)
    rank_ref[...] = jnp.concatenate([rank_a, rank_b], axis=0).astype(jnp.int32)
    base = base + cnt_a + cnt_b
    base_ref[...] = base
    cnt_ref[...] = jnp.broadcast_to(base, cnt_ref.shape)


def _merge(hg, att, proj, x2, wa, wb, wo, g1, b1, wr, br, tm, row0):
    T = hg.shape[0]
    tile0 = row0 // tm
    row = lambda w, cb=0: pl.BlockSpec((tm, w), lambda i: (i, cb))
    const = lambda shape: pl.BlockSpec(shape, lambda i: (0,) * len(shape))
    lanes = pl.BlockSpec((2, tm), lambda i: (0, i))
    tri = jnp.asarray(np.triu(np.ones((tm, tm), np.float32), 1), BF16)
    return pl.pallas_call(
        _merge_kernel,
        grid=(T // tm,),
        in_specs=[row(HG_W), row(ATT_QW), row(D_MODEL, 0), row(D_MODEL, 1),
                  pl.BlockSpec((tm, D_MODEL), lambda i: (i + tile0, 0)),
                  const((HG_W, D_MODEL)), const((ATT_QW, D_MODEL)), const((D_MODEL, D_MODEL)),
                  const((1, D_MODEL)), const((1, D_MODEL)), const((40, D_MODEL)), const((40, 1)),
                  const((tm, tm))],
        out_specs=[row(D_MODEL), row(D_MODEL // 2), lanes, lanes, lanes, const((N_EXPERTS, 128))],
        out_shape=[jax.ShapeDtypeStruct((T, D_MODEL), F32),
                   jax.ShapeDtypeStruct((T, D_MODEL // 2), jnp.uint32),
                   jax.ShapeDtypeStruct((2, T), jnp.int32),
                   jax.ShapeDtypeStruct((2, T), F32),
                   jax.ShapeDtypeStruct((2, T), jnp.int32),
                   jax.ShapeDtypeStruct((N_EXPERTS, 128), F32)],
        scratch_shapes=[pltpu.VMEM((N_EXPERTS, 1), F32), pltpu.VMEM((tm, D_MODEL), BF16)],
        compiler_params=_cparams(("arbitrary",)),
        name="merge",
    )(hg, att, proj, proj, x2, wa, wb, wo, g1, b1, wr, br, tri)


SC_WINDOW = 32
SC_IDX_LANES = 128


def _pad_indices(idx):
    rows = idx.reshape(-1, SC_WINDOW)
    return jnp.pad(rows, ((0, 0), (0, SC_IDX_LANES - SC_WINDOW)))


def _sc_mesh():
    return plsc.VectorSubcoreMesh(core_axis_name="core", subcore_axis_name="subcore")


def _sc_scatter2(x, idx_a, idx_b, n_out):
    T, d = x.shape

    @pl.kernel(out_type=jax.ShapeDtypeStruct((n_out, d), x.dtype), mesh=_sc_mesh())
    def scatter(x_hbm, ia_hbm, ib_hbm, o_hbm):
        def body(x_vmem, ia_vmem, ib_vmem):
            pltpu.sync_copy(x_vmem, o_hbm.at[ia_vmem.at[0, pl.ds(0, SC_WINDOW)]])
            pltpu.sync_copy(x_vmem, o_hbm.at[ib_vmem.at[0, pl.ds(0, SC_WINDOW)]])

        idx_spec = pl.BlockSpec((1, SC_IDX_LANES), lambda i: (i, 0))
        pltpu.emit_pipeline(
            body, grid=(T // SC_WINDOW,),
            in_specs=[pl.BlockSpec((SC_WINDOW, d), lambda i: (i, 0)), idx_spec, idx_spec],
            out_specs=[],
            core_axis_name=("core", "subcore"),
            dimension_semantics=(pltpu.PARALLEL,),
        )(x_hbm, ia_hbm, ib_hbm)

    return scatter(x, _pad_indices(idx_a), _pad_indices(idx_b))


def _sc_gather(x, idx):
    n = idx.shape[0]
    d = x.shape[1]

    @pl.kernel(out_type=jax.ShapeDtypeStruct((n, d), x.dtype), mesh=_sc_mesh())
    def gather(x_hbm, i_hbm, o_hbm):
        def body(i_vmem, o_vmem):
            pltpu.sync_copy(x_hbm.at[i_vmem.at[0, pl.ds(0, SC_WINDOW)]], o_vmem)

        pltpu.emit_pipeline(
            body, grid=(n // SC_WINDOW,),
            in_specs=[pl.BlockSpec((1, SC_IDX_LANES), lambda i: (i, 0))],
            out_specs=[pl.BlockSpec((SC_WINDOW, d), lambda i: (i, 0))],
            core_axis_name=("core", "subcore"),
            dimension_semantics=(pltpu.PARALLEL,),
        )(i_hbm, o_hbm)

    return gather(x, _pad_indices(idx))


def _expert_kernel(be_ref, nv_ref, xb_ref, wg_ref, wu_ref, wd_ref, y_ref, wg_bf, wu_bf, wd_bf):
    i = pl.program_id(0)
    nv = nv_ref[i]

    @pl.when((i == 0) | (be_ref[i] != be_ref[jnp.maximum(i - 1, 0)]))
    def _():
        wg_bf[...] = wg_ref[0].astype(BF16)
        wu_bf[...] = wu_ref[0].astype(BF16)
        wd_bf[...] = wd_ref[0].astype(BF16)

    @pl.when(nv > 0)
    def _():
        row = lax.broadcasted_iota(jnp.int32, (MOE_BLOCK, 1), 0)
        xb = _unpack_bf16_pairs(jnp.where(row < nv, xb_ref[...], jnp.uint32(0))).astype(BF16)
        g = _dot(xb, wg_bf[...])
        u = _dot(xb, wu_bf[...])
        h = (_silu(g) * u).astype(BF16)
        y_ref[...] = _pack_bf16_pairs(_dot(h, wd_bf[...]).astype(BF16))


def _experts(block_expert, block_valid, xbuf, w_gate, w_up, w_down):
    P = xbuf.shape[0]
    nb = P // MOE_BLOCK
    wspec = lambda a, b: pl.BlockSpec((1, a, b), lambda i, be, nv: (be[i], 0, 0))
    rows = lambda w: pl.BlockSpec((MOE_BLOCK, w), lambda i, be, nv: (i, 0))
    return pl.pallas_call(
        _expert_kernel,
        grid_spec=pltpu.PrefetchScalarGridSpec(
            num_scalar_prefetch=2,
            grid=(nb,),
            in_specs=[rows(D_MODEL // 2), wspec(D_MODEL, D_EXPERT), wspec(D_MODEL, D_EXPERT),
                      wspec(D_EXPERT, D_MODEL)],
            out_specs=rows(D_MODEL // 2),
            scratch_shapes=[pltpu.VMEM((D_MODEL, D_EXPERT), BF16), pltpu.VMEM((D_MODEL, D_EXPERT), BF16),
                            pltpu.VMEM((D_EXPERT, D_MODEL), BF16)]),
        out_shape=jax.ShapeDtypeStruct((P, D_MODEL // 2), jnp.uint32),
        compiler_params=_cparams(("arbitrary",)),
        name="experts",
    )(block_expert, block_valid, xbuf, w_gate, w_up, w_down)


def _combine_kernel(ra_ref, rb_ref, x1_ref, wt_ref, g2_ref, b2_ref, *rest):
    o_ref = rest[-1]
    w = wt_ref[...]
    y = w[:, 0:1] * _unpack_bf16_pairs(ra_ref[...]) + w[:, 1:2] * _unpack_bf16_pairs(rb_ref[...])
    o_ref[...] = _layer_norm(DN_ALPHA * x1_ref[...] + y, g2_ref[...], b2_ref[...])


def _combine(rows2, x1, wt, g2, b2, tm, out_prev, tile0, t_total):
    T = x1.shape[0]
    nt = T // tm
    const = lambda shape: pl.BlockSpec(shape, lambda i: (0,) * len(shape))
    in_specs = [pl.BlockSpec((tm, D_MODEL // 2), lambda i: (i, 0)),
                pl.BlockSpec((tm, D_MODEL // 2), lambda i: (i + nt, 0)),
                pl.BlockSpec((tm, D_MODEL), lambda i: (i, 0)), pl.BlockSpec((tm, 2), lambda i: (i, 0)),
                const((1, D_MODEL)), const((1, D_MODEL))]
    args = [rows2, rows2, x1, wt, g2, b2]
    aliases = {}
    if out_prev is not None:
        in_specs.append(pl.BlockSpec(memory_space=pl.ANY))
        args.append(out_prev)
        aliases = {len(args) - 1: 0}
    return pl.pallas_call(
        _combine_kernel,
        grid=(nt,),
        in_specs=in_specs,
        out_specs=pl.BlockSpec((tm, D_MODEL), lambda i: (i + tile0, 0)),
        out_shape=jax.ShapeDtypeStruct((t_total, D_MODEL), F32),
        input_output_aliases=aliases,
        compiler_params=_cparams(("arbitrary",)),
        name="combine",
    )(*args)


def _slot_layout(eid, rank, counts, T):
    P = 2 * T + N_EXPERTS * MOE_BLOCK
    nb = P // MOE_BLOCK
    padded = ((counts + MOE_BLOCK - 1) // MOE_BLOCK) * MOE_BLOCK
    pend = jnp.cumsum(padded)
    pstart = pend - padded
    experts = jnp.arange(N_EXPERTS, dtype=jnp.int32)
    pos = rank + jnp.sum(jnp.where(eid[:, :, None] == experts, pstart, 0), axis=-1)
    block_start = jnp.arange(nb, dtype=jnp.int32) * MOE_BLOCK
    block_expert = jnp.minimum(jnp.sum(block_start[:, None] >= pend[None, :], axis=1), N_EXPERTS - 1)
    block_valid = jnp.clip(pstart[block_expert] + counts[block_expert] - block_start, 0, MOE_BLOCK)
    block_valid = jnp.where(block_start < pend[-1], block_valid, 0)
    return pos.astype(jnp.int32), block_expert.astype(jnp.int32), block_valid.astype(jnp.int32)


def kernel(x, lb_logits, w_in, hg_norm_w, sinks, w_branch_a, w_branch_b, w_out, ln1_g, ln1_b,
           router_group_w, router_group_b, router_expert_w, router_expert_b,
           w_exp_gate, w_exp_up, w_exp_down, ln2_g, ln2_b):
    B, S, D = x.shape
    assert D == D_MODEL and S % ATT_BLK == 0 and w_in.shape[0] == DEPTH == 1
    lb_all = jnp.cumsum(jax.nn.softmax(lb_logits.astype(F32), axis=0), axis=0)
    lb = lb_all[0].reshape(1, HG_W)
    w_rot = jnp.concatenate([w_in[0][:, PROJ_ROT:], w_in[0][:, :PROJ_ROT]], axis=1).astype(BF16)
    nw = hg_norm_w[0].reshape(1, HG_DV).astype(F32)
    wa, wb, wo = w_branch_a[0].astype(BF16), w_branch_b[0].astype(BF16), w_out[0].astype(BF16)
    wr = jnp.zeros((40, D), F32).at[0:N_GROUPS].set(router_group_w[0].T).at[8:40].set(router_expert_w[0].T)
    br = jnp.zeros((40, 1), F32).at[0:N_GROUPS, 0].set(router_group_b[0]).at[8:40, 0].set(router_expert_b[0])
    g1, b1 = ln1_g[0].reshape(1, D), ln1_b[0].reshape(1, D)
    g2, b2 = ln2_g[0].reshape(1, D), ln2_b[0].reshape(1, D)

    n_parts = N_PARTS if B % N_PARTS == 0 else 1
    bp = B // n_parts
    tp = bp * S
    tm = 256 if tp % 256 == 0 else ATT_BLK
    ts = 512 if S % 512 == 0 else ATT_BLK
    x2 = x.reshape(B * S, D)
    out = None
    for part in range(n_parts):
        proj = _proj(x2, w_rot, tm, part * tp, tp)
        hg, att = _mixers(proj, lb, nw, sinks[0].astype(F32), bp, S)
        x1, xp, eid, wt, rank, cnt = _merge(hg, att, proj, x2, wa, wb, wo, g1, b1, wr.astype(BF16), br,
                                            512 if tp % 512 == 0 else tm, part * tp)
        pos, block_expert, block_valid = _slot_layout(eid, rank, cnt[:, 0].astype(jnp.int32), tp)
        xbuf = _sc_scatter2(xp, pos[0], pos[1], 2 * tp + N_EXPERTS * MOE_BLOCK)
        ybuf = _experts(block_expert, block_valid, xbuf, w_exp_gate[0], w_exp_up[0], w_exp_down[0])
        rows2 = _sc_gather(ybuf, pos.reshape(-1))
        out = _combine(rows2, x1, wt.T, g2, b2, tm, out, part * (tp // tm), B * S)
    return out.reshape(B, S, D)
```

```python
import functools

import numpy as np
import jax
import jax.numpy as jnp
from jax import lax
from jax.experimental import pallas as pl
from jax.experimental.pallas import tpu as pltpu
from jax.experimental.pallas import tpu_sc as plsc

F32 = jnp.float32
BF16 = jnp.bfloat16

D_MODEL = 1024
DEPTH = 1
HG_HEADS = 4
HG_DK = 128
HG_DV = 128
HG_W = HG_HEADS * HG_DK
CHUNK = 64
ATT_Q_HEADS = 8
ATT_KV_HEADS = 2
ATT_GROUP = ATT_Q_HEADS // ATT_KV_HEADS
ATT_HD = 64
ATT_QW = ATT_Q_HEADS * ATT_HD
ATT_KVW = ATT_KV_HEADS * ATT_HD
ATT_BLK = 128
N_GROUPS = 4
EPG = 8
N_EXPERTS = N_GROUPS * EPG
D_EXPERT = 512
MOE_BLOCK = 256
DN_ALPHA = (2.0 * DEPTH) ** 0.25
LN_EPS = 1e-5
RMS_EPS = 1e-6
NEG_INF = -1e30

PROJ_W = 4 * HG_W + ATT_QW + 2 * ATT_KVW + 2 * D_MODEL
PROJ_ROT = 4 * HG_W + ATT_QW + 2 * ATT_KVW
N_LEVELS = 6
N_ARG_GROUPS = N_LEVELS + 2

VMEM_LIMIT = 56 * 1024 * 1024
MERGE_SUB = 256
N_PARTS = 2


def _cparams(sem):
    return pltpu.CompilerParams(dimension_semantics=sem, vmem_limit_bytes=VMEM_LIMIT)


def _dot(a, b):
    return jnp.dot(a, b, preferred_element_type=F32)


def _dot_nt(a, b):
    return lax.dot_general(a, b, (((1,), (1,)), ((), ())), preferred_element_type=F32)


def _dot_tn(a, b):
    return lax.dot_general(a, b, (((0,), (0,)), ((), ())), preferred_element_type=F32)


def _sigmoid(x):
    return 0.5 * jnp.tanh(0.5 * x) + 0.5


def _silu(x):
    return x * _sigmoid(x)


def _pack_bf16_pairs(xb):
    n = xb.shape[1] // 2
    lo = lax.bitcast_convert_type(xb[:, :n].astype(F32), jnp.uint32)
    hi = lax.bitcast_convert_type(xb[:, n:].astype(F32), jnp.uint32)
    return (lo >> 16) | hi


def _unpack_bf16_pairs(w):
    lo = lax.bitcast_convert_type(w << 16, F32)
    hi = lax.bitcast_convert_type(w & jnp.uint32(0xFFFF0000), F32)
    return jnp.concatenate([lo, hi], axis=1)


def _layer_norm(z, g, b):
    mu = jnp.mean(z, axis=-1, keepdims=True)
    zc = z - mu
    var = jnp.mean(zc * zc, axis=-1, keepdims=True)
    return zc * lax.rsqrt(var + LN_EPS) * g + b


def _proj_kernel(x_ref, w_ref, o_ref):
    o_ref[...] = _dot(x_ref[...].astype(BF16), w_ref[...])


def _proj(x2, w_bf, tm, row0, T):
    tile0 = row0 // tm
    return pl.pallas_call(
        _proj_kernel,
        grid=(T // tm,),
        in_specs=[pl.BlockSpec((tm, D_MODEL), lambda i: (i + tile0, 0)),
                  pl.BlockSpec((D_MODEL, PROJ_W), lambda i: (0, 0))],
        out_specs=pl.BlockSpec((tm, PROJ_W), lambda i: (i, 0)),
        out_shape=jax.ShapeDtypeStruct((T, PROJ_W), F32),
        compiler_params=_cparams(("arbitrary",)),
        name="proj",
    )(x2, w_bf)


def _hgrn_tables():
    C = CHUNK
    w = np.zeros((N_ARG_GROUPS, C, C), np.float32)
    masks = np.zeros((N_LEVELS, C, C), np.float32)
    for lvl in range(N_LEVELS):
        h = 1 << lvl
        for t in range(C):
            base = (t // (2 * h)) * 2 * h
            m = base + h - 1
            if (t // h) % 2 == 1:
                w[lvl, t, m + 1:t + 1] = 1.0
                masks[lvl, t, base:base + h] = 1.0
            else:
                w[lvl, t, t + 1:m + 1] = 1.0
    for t in range(C):
        w[N_LEVELS, t, :t + 1] = 1.0
        w[N_LEVELS + 1, t, t + 1:] = 1.0
    w = w.reshape(N_ARG_GROUPS * C, C)
    return np.concatenate([w, w], axis=1), masks


def _hgrn_chunk(q, fz, v, gate, c0, c1, nw, seg, mask_ref, state_ref):
    h = 0.5 * q
    qf = h + h * jnp.tanh(h)
    t1 = c1 * jnp.tanh(0.5 * fz)
    f = c0 + t1
    k = c1 - t1
    l2 = jnp.log2(f)
    v_bf = v.astype(BF16)
    qf_bf = qf.astype(BF16)
    k_bf = k.astype(BF16)

    l_hi = l2.astype(BF16)
    l_lo = (l2 - l_hi.astype(F32)).astype(BF16)
    args = _dot(seg, jnp.concatenate([l_hi, l_lo], axis=0))
    e = jnp.exp2(args)

    row = lax.broadcasted_iota(jnp.int32, (CHUNK, 1), 0)
    scores = [jnp.zeros((CHUNK, CHUNK), F32) for _ in range(HG_HEADS)]
    for lvl in range(N_LEVELS):
        half = 1 << lvl
        e_l = e[lvl * CHUNK:(lvl + 1) * CHUNK, :].astype(BF16)
        if half >= 16:
            sel = jnp.concatenate([(qf_bf if (r0 // half) % 2 else k_bf)[r0:r0 + half] for r0 in range(0, CHUNK, half)],
                                  axis=0)
        else:
            sel = jnp.where((row // half) % 2 == 1, qf_bf, k_bf)
        a = sel * e_l
        m = mask_ref[lvl]
        for hd in range(HG_HEADS):
            a_h = a[:, hd * HG_DK:(hd + 1) * HG_DK]
            scores[hd] = scores[hd] + m * _dot_nt(a_h, a_h)

    e_cum = e[N_LEVELS * CHUNK:(N_LEVELS + 1) * CHUNK, :]
    e_suf = e[(N_LEVELS + 1) * CHUNK:(N_LEVELS + 2) * CHUNK, :]
    q_in = qf_bf * e_cum.astype(BF16)
    k_out = k_bf * e_suf.astype(BF16)
    e_last = e_cum[CHUNK - 1:CHUNK, :]
    qk = qf * k

    outs = []
    for hd in range(HG_HEADS):
        cols = slice(hd * HG_DK, (hd + 1) * HG_DK)
        st = state_ref[hd]
        diag = jnp.sum(qk[:, cols], axis=-1, keepdims=True)
        o = (_dot_nt(q_in[:, cols], st.astype(BF16))
             + _dot(scores[hd].astype(BF16), v_bf[:, cols])
             + diag * v[:, cols])
        state_ref[hd] = st * e_last[:, cols] + _dot_tn(v_bf[:, cols], k_out[:, cols])
        o = o * lax.rsqrt(jnp.mean(o * o, axis=-1, keepdims=True) + RMS_EPS) * nw
        outs.append(o)
    hg = 0.5 * gate
    return jnp.concatenate(outs, axis=1) * (hg + hg * jnp.tanh(hg))


def _attn_bias():
    r = np.arange(ATT_BLK)[:, None]
    c = np.arange(2 * ATT_BLK)[None, :]
    dist = r + ATT_BLK - c
    window = (dist >= 0) & (dist < ATT_BLK)
    slopes = np.exp2(-8.0 * (np.arange(ATT_Q_HEADS, dtype=np.float32) + 1.0) / ATT_Q_HEADS).astype(np.float32)
    alibi = -slopes[:, None, None] * dist.astype(np.float32)[None]
    later = np.where(window[None], alibi, np.float32(NEG_INF))
    first = np.where((window & (c >= ATT_BLK))[None], alibi, np.float32(NEG_INF))
    return np.stack([later, first]).astype(np.float32)


def _attn_block(q_ref, kv_cur, kv_prev, bias_ref, table, sink_ref, o_ref):
    lane = lax.broadcasted_iota(jnp.int32, (2 * ATT_BLK, 2 * ATT_KVW), 1)
    lo = (lane % ATT_KVW) < ATT_HD

    kv = jnp.concatenate([kv_prev, kv_cur], axis=0).astype(BF16)
    kv_sw = jnp.concatenate([kv[:, ATT_HD:ATT_KVW], kv[:, :ATT_HD],
                             kv[:, ATT_KVW + ATT_HD:], kv[:, ATT_KVW:ATT_KVW + ATT_HD]], axis=1)
    zero = jnp.zeros_like(kv)
    placed = {}
    for h in range(ATT_KV_HEADS):
        for off in range(2):
            src = kv if h == off else kv_sw
            placed[h, off] = jnp.where(lo if off == 0 else jnp.logical_not(lo), src, zero)

    scale = ATT_HD ** -0.5
    for pair in range(ATT_Q_HEADS // 2):
        qp = (q_ref[:, pair * 2 * ATT_HD:(pair + 1) * 2 * ATT_HD] * scale).astype(BF16)
        acc = jnp.zeros((ATT_BLK, 2 * ATT_HD), F32)
        for off in range(2):
            j = 2 * pair + off
            kvh = placed[j // ATT_GROUP, off]
            sink = sink_ref[j]
            logits = _dot_nt(qp, kvh[:, :ATT_KVW]) + bias_ref[table, j]
            m = jnp.maximum(jnp.max(logits, axis=-1, keepdims=True), sink)
            p = jnp.exp(logits - m)
            den = jnp.sum(p, axis=-1, keepdims=True) + jnp.exp(sink - m)
            acc = acc + _dot(p.astype(BF16), kvh[:, ATT_KVW:]) / den
        o_ref[:, pair * 2 * ATT_HD:(pair + 1) * 2 * ATT_HD] = acc.astype(o_ref.dtype)


def _mixers_kernel(sink_ref, q_ref, f_ref, i_ref, g_ref, aq_ref, kv_ref, kvp_ref, lb_ref, nw_ref, seg_ref,
                   mask_ref, bias_ref, hg_ref, at_ref, state_ref):
    first = pl.program_id(1) == 0

    @pl.when(first)
    def _():
        state_ref[...] = jnp.zeros_like(state_ref)

    lb = lb_ref[...]
    c0 = 0.5 + 0.5 * lb
    c1 = 0.5 - 0.5 * lb
    nw = nw_ref[...]
    seg = seg_ref[...]
    for r0 in range(0, ATT_BLK, CHUNK):
        rows = slice(r0, r0 + CHUNK)
        o = _hgrn_chunk(q_ref[rows, :], f_ref[rows, :], i_ref[rows, :], g_ref[rows, :], c0, c1, nw, seg, mask_ref,
                        state_ref)
        hg_ref[rows, :] = o.astype(hg_ref.dtype)
    _attn_block(aq_ref, kv_ref[...], kvp_ref[...], bias_ref, jnp.where(first, 1, 0), sink_ref, at_ref)


def _mixers(proj, lb, norm_w, sinks, B, S):
    T = B * S
    nb = S // ATT_BLK
    seg, masks = _hgrn_tables()
    const = lambda shape: pl.BlockSpec(shape, lambda b, n: (0,) * len(shape))
    hcol = lambda cb: pl.BlockSpec((ATT_BLK, HG_W), lambda b, n: (b * nb + n, cb))
    hbase = (2 * D_MODEL) // HG_W
    qcol = (2 * D_MODEL + 4 * HG_W) // ATT_QW
    kvcol = (2 * D_MODEL + 4 * HG_W + ATT_QW) // (2 * ATT_KVW)
    out = pl.BlockSpec((ATT_BLK, HG_W), lambda b, n: (b * nb + n, 0))
    return pl.pallas_call(
        _mixers_kernel,
        grid=(B, nb),
        in_specs=[pl.BlockSpec(memory_space=pltpu.SMEM),
                  hcol(hbase), hcol(hbase + 1), hcol(hbase + 2), hcol(hbase + 3),
                  pl.BlockSpec((ATT_BLK, ATT_QW), lambda b, n: (b * nb + n, qcol)),
                  pl.BlockSpec((ATT_BLK, 2 * ATT_KVW), lambda b, n: (b * nb + n, kvcol)),
                  pl.BlockSpec((ATT_BLK, 2 * ATT_KVW), lambda b, n: (b * nb + jnp.maximum(n - 1, 0), kvcol)),
                  const((1, HG_W)), const((1, HG_DV)), const((N_ARG_GROUPS * CHUNK, 2 * CHUNK)),
                  const((N_LEVELS, CHUNK, CHUNK)), const((2, ATT_Q_HEADS, ATT_BLK, 2 * ATT_BLK))],
        out_specs=[out, out],
        out_shape=[jax.ShapeDtypeStruct((T, HG_W), BF16), jax.ShapeDtypeStruct((T, ATT_QW), BF16)],
        scratch_shapes=[pltpu.VMEM((HG_HEADS, HG_DV, HG_DK), F32)],
        compiler_params=_cparams(("arbitrary", "arbitrary")),
        name="mixers",
    )(sinks, proj, proj, proj, proj, proj, proj, proj, lb, norm_w, jnp.asarray(seg, BF16), jnp.asarray(masks),
      jnp.asarray(_attn_bias()))


def _merge_kernel(hg_ref, at_ref, ga_ref, gb_ref, x_ref, wa_ref, wb_ref, wo_ref, g1_ref, b1_ref,
                  wr_ref, br_ref, tri_ref, x1_ref, xp_ref, eid_ref, wt_ref, rank_ref, cnt_ref, base_ref, x1b_ref):
    tm = x_ref.shape[0]
    sub = min(tm, MERGE_SUB)
    for r0 in range(0, tm, sub):
        rows = slice(r0, r0 + sub)
        ya = _dot(hg_ref[rows, :], wa_ref[...])
        yb = _dot(at_ref[rows, :], wb_ref[...])
        merged = _sigmoid(ga_ref[rows, :]) * ya + _sigmoid(gb_ref[rows, :]) * yb
        z = DN_ALPHA * x_ref[rows, :] + _dot(merged.astype(BF16), wo_ref[...])
        x1 = _layer_norm(z, g1_ref[...], b1_ref[...])
        x1_ref[rows, :] = x1
        x1b = x1.astype(BF16)
        x1b_ref[rows, :] = x1b
        xp_ref[rows, :] = _pack_bf16_pairs(x1b)

    lg = _dot_nt(wr_ref[...], x1b_ref[...]) + br_ref[...]
    g = lg[0:8, :]
    row8 = lax.broadcasted_iota(jnp.int32, (8, tm), 0)
    g = jnp.where(row8 < N_GROUPS, g, -jnp.inf)
    gmax = jnp.max(g, axis=0, keepdims=True)
    gsel = jnp.min(jnp.where(g == gmax, row8, 8), axis=0, keepdims=True)
    gw = 1.0 / jnp.sum(jnp.exp(g - gmax), axis=0, keepdims=True)
    el = jnp.where(gsel == 0, lg[8:16, :],
                   jnp.where(gsel == 1, lg[16:24, :], jnp.where(gsel == 2, lg[24:32, :], lg[32:40, :])))
    v1 = jnp.max(el, axis=0, keepdims=True)
    i1 = jnp.min(jnp.where(el == v1, row8, 8), axis=0, keepdims=True)
    el2 = jnp.where(row8 == i1, -jnp.inf, el)
    v2 = jnp.max(el2, axis=0, keepdims=True)
    i2 = jnp.min(jnp.where(el2 == v2, row8, 8), axis=0, keepdims=True)
    e2 = jnp.exp(v2 - v1)
    den = 1.0 + e2
    e_a = gsel * EPG + i1
    e_b = gsel * EPG + i2
    eid_ref[...] = jnp.concatenate([e_a, e_b], axis=0)
    wt_ref[...] = jnp.concatenate([gw / den, gw * e2 / den], axis=0)

    @pl.when(pl.program_id(0) == 0)
    def _():
        base_ref[...] = jnp.zeros_like(base_ref)

    row_e = lax.broadcasted_iota(jnp.int32, (N_EXPERTS, tm), 0)
    oh_a = jnp.where(row_e == e_a, 1.0, 0.0)
    oh_b = jnp.where(row_e == e_b, 1.0, 0.0)
    tri = tri_ref[...]
    pre_a = _dot(oh_a.astype(BF16), tri)
    pre_b = _dot(oh_b.astype(BF16), tri)
    cnt_a = jnp.sum(oh_a, axis=1, keepdims=True)
    cnt_b = jnp.sum(oh_b, axis=1, keepdims=True)
    base = base_ref[...]
    rank_a = jnp.sum(oh_a * (base + pre_a), axis=0, keepdims=True)
    rank_b = jnp.sum(oh_b * (base + cnt_a + pre_b), axis=0, keepdims=True)
    rank_ref[...] = jnp.concatenate([rank_a, rank_b], axis=0).astype(jnp.int32)
    base = base + cnt_a + cnt_b
    base_ref[...] = base
    cnt_ref[...] = jnp.broadcast_to(base, cnt_ref.shape)


def _merge(hg, att, proj, x2, wa, wb, wo, g1, b1, wr, br, tm, row0):
    T = hg.shape[0]
    tile0 = row0 // tm
    row = lambda w, cb=0: pl.BlockSpec((tm, w), lambda i: (i, cb))
    const = lambda shape: pl.BlockSpec(shape, lambda i: (0,) * len(shape))
    lanes = pl.BlockSpec((2, tm), lambda i: (0, i))
    tri = jnp.asarray(np.triu(np.ones((tm, tm), np.float32), 1), BF16)
    return pl.pallas_call(
        _merge_kernel,
        grid=(T // tm,),
        in_specs=[row(HG_W), row(ATT_QW), row(D_MODEL, 0), row(D_MODEL, 1),
                  pl.BlockSpec((tm, D_MODEL), lambda i: (i + tile0, 0)),
                  const((HG_W, D_MODEL)), const((ATT_QW, D_MODEL)), const((D_MODEL, D_MODEL)),
                  const((1, D_MODEL)), const((1, D_MODEL)), const((40, D_MODEL)), const((40, 1)),
                  const((tm, tm))],
        out_specs=[row(D_MODEL), row(D_MODEL // 2), lanes, lanes, lanes, const((N_EXPERTS, 128))],
        out_shape=[jax.ShapeDtypeStruct((T, D_MODEL), F32),
                   jax.ShapeDtypeStruct((T, D_MODEL // 2), jnp.uint32),
                   jax.ShapeDtypeStruct((2, T), jnp.int32),
                   jax.ShapeDtypeStruct((2, T), F32),
                   jax.ShapeDtypeStruct((2, T), jnp.int32),
                   jax.ShapeDtypeStruct((N_EXPERTS, 128), F32)],
        scratch_shapes=[pltpu.VMEM((N_EXPERTS, 1), F32), pltpu.VMEM((tm, D_MODEL), BF16)],
        compiler_params=_cparams(("arbitrary",)),
        name="merge",
    )(hg, att, proj, proj, x2, wa, wb, wo, g1, b1, wr, br, tri)


SC_WINDOW = 32
SC_IDX_LANES = 128


def _pad_indices(idx):
    rows = idx.reshape(-1, SC_WINDOW)
    return jnp.pad(rows, ((0, 0), (0, SC_IDX_LANES - SC_WINDOW)))


def _sc_mesh():
    return plsc.VectorSubcoreMesh(core_axis_name="core", subcore_axis_name="subcore")


def _sc_scatter2(x, idx_a, idx_b, n_out):
    T, d = x.shape

    @pl.kernel(out_type=jax.ShapeDtypeStruct((n_out, d), x.dtype), mesh=_sc_mesh())
    def scatter(x_hbm, ia_hbm, ib_hbm, o_hbm):
        def body(x_vmem, ia_vmem, ib_vmem):
            pltpu.sync_copy(x_vmem, o_hbm.at[ia_vmem.at[0, pl.ds(0, SC_WINDOW)]])
            pltpu.sync_copy(x_vmem, o_hbm.at[ib_vmem.at[0, pl.ds(0, SC_WINDOW)]])

        idx_spec = pl.BlockSpec((1, SC_IDX_LANES), lambda i: (i, 0))
        pltpu.emit_pipeline(
            body, grid=(T // SC_WINDOW,),
            in_specs=[pl.BlockSpec((SC_WINDOW, d), lambda i: (i, 0)), idx_spec, idx_spec],
            out_specs=[],
            core_axis_name=("core", "subcore"),
            dimension_semantics=(pltpu.PARALLEL,),
        )(x_hbm, ia_hbm, ib_hbm)

    return scatter(x, _pad_indices(idx_a), _pad_indices(idx_b))


def _sc_gather(x, idx):
    n = idx.shape[0]
    d = x.shape[1]

    @pl.kernel(out_type=jax.ShapeDtypeStruct((n, d), x.dtype), mesh=_sc_mesh())
    def gather(x_hbm, i_hbm, o_hbm):
        def body(i_vmem, o_vmem):
            pltpu.sync_copy(x_hbm.at[i_vmem.at[0, pl.ds(0, SC_WINDOW)]], o_vmem)

        pltpu.emit_pipeline(
            body, grid=(n // SC_WINDOW,),
            in_specs=[pl.BlockSpec((1, SC_IDX_LANES), lambda i: (i, 0))],
            out_specs=[pl.BlockSpec((SC_WINDOW, d), lambda i: (i, 0))],
            core_axis_name=("core", "subcore"),
            dimension_semantics=(pltpu.PARALLEL,),
        )(i_hbm, o_hbm)

    return gather(x, _pad_indices(idx))


def _expert_kernel(be_ref, nv_ref, nxt_ref, par_ref, xb_ref, wg_hbm, wu_hbm, wd_hbm, y_ref,
                   wg_st, wu_st, wd_st, wg_bf, wu_bf, wd_bf, sem):
    i = pl.program_id(0)
    nv = nv_ref[i]

    def fetch(expert, slot):
        return [pltpu.make_async_copy(src.at[expert], dst.at[slot], sem.at[slot, j])
                for j, (src, dst) in enumerate(((wg_hbm, wg_st), (wu_hbm, wu_st), (wd_hbm, wd_st)))]

    @pl.when(i == 0)
    def _():
        for cp in fetch(be_ref[0], par_ref[0]):
            cp.start()

    @pl.when((i == 0) | (be_ref[i] != be_ref[jnp.maximum(i - 1, 0)]))
    def _():
        slot = par_ref[i]
        for cp in fetch(be_ref[i], slot):
            cp.wait()
        wg_bf[...] = wg_st[slot].astype(BF16)
        wu_bf[...] = wu_st[slot].astype(BF16)
        wd_bf[...] = wd_st[slot].astype(BF16)

        @pl.when(nxt_ref[i] >= 0)
        def _():
            for cp in fetch(nxt_ref[i], 1 - slot):
                cp.start()

    @pl.when(nv > 0)
    def _():
        row = lax.broadcasted_iota(jnp.int32, (MOE_BLOCK, 1), 0)
        xb = _unpack_bf16_pairs(jnp.where(row < nv, xb_ref[...], jnp.uint32(0))).astype(BF16)
        g = _dot(xb, wg_bf[...])
        u = _dot(xb, wu_bf[...])
        h = (_silu(g) * u).astype(BF16)
        y_ref[...] = _pack_bf16_pairs(_dot(h, wd_bf[...]).astype(BF16))


def _experts(block_expert, block_valid, xbuf, w_gate, w_up, w_down):
    P = xbuf.shape[0]
    nb = P // MOE_BLOCK
    experts = jnp.arange(N_EXPERTS, dtype=jnp.int32)
    run_end = jnp.sum(block_expert[None, :] <= experts[:, None], axis=1)
    nxt_idx = run_end[block_expert]
    nxt = jnp.where(nxt_idx < nb, block_expert[jnp.minimum(nxt_idx, nb - 1)], -1).astype(jnp.int32)
    present = jnp.any(block_expert[None, :] == experts[:, None], axis=1)
    runs_before = jnp.cumsum(present.astype(jnp.int32)) - present.astype(jnp.int32)
    par = (runs_before[block_expert] % 2).astype(jnp.int32)

    rows = pl.BlockSpec((MOE_BLOCK, D_MODEL // 2), lambda i, *_: (i, 0))
    hbm = pl.BlockSpec(memory_space=pl.ANY)
    return pl.pallas_call(
        _expert_kernel,
        grid_spec=pltpu.PrefetchScalarGridSpec(
            num_scalar_prefetch=4,
            grid=(nb,),
            in_specs=[rows, hbm, hbm, hbm],
            out_specs=rows,
            scratch_shapes=[pltpu.VMEM((2, D_MODEL, D_EXPERT), F32), pltpu.VMEM((2, D_MODEL, D_EXPERT), F32),
                            pltpu.VMEM((2, D_EXPERT, D_MODEL), F32),
                            pltpu.VMEM((D_MODEL, D_EXPERT), BF16), pltpu.VMEM((D_MODEL, D_EXPERT), BF16),
                            pltpu.VMEM((D_EXPERT, D_MODEL), BF16),
                            pltpu.SemaphoreType.DMA((2, 3))]),
        out_shape=jax.ShapeDtypeStruct((P, D_MODEL // 2), jnp.uint32),
        compiler_params=_cparams(("arbitrary",)),
        name="experts",
    )(block_expert, block_valid, nxt, par, xbuf, w_gate, w_up, w_down)


def _combine_kernel(ra_ref, rb_ref, x1_ref, wt_ref, g2_ref, b2_ref, *rest):
    o_ref = rest[-1]
    w = wt_ref[...]
    y = w[:, 0:1] * _unpack_bf16_pairs(ra_ref[...]) + w[:, 1:2] * _unpack_bf16_pairs(rb_ref[...])
    o_ref[...] = _layer_norm(DN_ALPHA * x1_ref[...] + y, g2_ref[...], b2_ref[...])


def _combine(rows2, x1, wt, g2, b2, tm, out_prev, tile0, t_total):
    T = x1.shape[0]
    nt = T // tm
    const = lambda shape: pl.BlockSpec(shape, lambda i: (0,) * len(shape))
    in_specs = [pl.BlockSpec((tm, D_MODEL // 2), lambda i: (i, 0)),
                pl.BlockSpec((tm, D_MODEL // 2), lambda i: (i + nt, 0)),
                pl.BlockSpec((tm, D_MODEL), lambda i: (i, 0)), pl.BlockSpec((tm, 2), lambda i: (i, 0)),
                const((1, D_MODEL)), const((1, D_MODEL))]
    args = [rows2, rows2, x1, wt, g2, b2]
    aliases = {}
    if out_prev is not None:
        in_specs.append(pl.BlockSpec(memory_space=pl.ANY))
        args.append(out_prev)
        aliases = {len(args) - 1: 0}
    return pl.pallas_call(
        _combine_kernel,
        grid=(nt,),
        in_specs=in_specs,
        out_specs=pl.BlockSpec((tm, D_MODEL), lambda i: (i + tile0, 0)),
        out_shape=jax.ShapeDtypeStruct((t_total, D_MODEL), F32),
        input_output_aliases=aliases,
        compiler_params=_cparams(("arbitrary",)),
        name="combine",
    )(*args)


def _slot_layout(eid, rank, counts, T):
    P = 2 * T + N_EXPERTS * MOE_BLOCK
    nb = P // MOE_BLOCK
    padded = ((counts + MOE_BLOCK - 1) // MOE_BLOCK) * MOE_BLOCK
    pend = jnp.cumsum(padded)
    pstart = pend - padded
    experts = jnp.arange(N_EXPERTS, dtype=jnp.int32)
    pos = rank + jnp.sum(jnp.where(eid[:, :, None] == experts, pstart, 0), axis=-1)
    block_start = jnp.arange(nb, dtype=jnp.int32) * MOE_BLOCK
    block_expert = jnp.minimum(jnp.sum(block_start[:, None] >= pend[None, :], axis=1), N_EXPERTS - 1)
    block_valid = jnp.clip(pstart[block_expert] + counts[block_expert] - block_start, 0, MOE_BLOCK)
    block_valid = jnp.where(block_start < pend[-1], block_valid, 0)
    return pos.astype(jnp.int32), block_expert.astype(jnp.int32), block_valid.astype(jnp.int32)


def kernel(x, lb_logits, w_in, hg_norm_w, sinks, w_branch_a, w_branch_b, w_out, ln1_g, ln1_b,
           router_group_w, router_group_b, router_expert_w, router_expert_b,
           w_exp_gate, w_exp_up, w_exp_down, ln2_g, ln2_b):
    B, S, D = x.shape
    assert D == D_MODEL and S % ATT_BLK == 0 and w_in.shape[0] == DEPTH == 1
    lb_all = jnp.cumsum(jax.nn.softmax(lb_logits.astype(F32), axis=0), axis=0)
    lb = lb_all[0].reshape(1, HG_W)
    w_rot = jnp.concatenate([w_in[0][:, PROJ_ROT:], w_in[0][:, :PROJ_ROT]], axis=1).astype(BF16)
    nw = hg_norm_w[0].reshape(1, HG_DV).astype(F32)
    wa, wb, wo = w_branch_a[0].astype(BF16), w_branch_b[0].astype(BF16), w_out[0].astype(BF16)
    wr = jnp.zeros((40, D), F32).at[0:N_GROUPS].set(router_group_w[0].T).at[8:40].set(router_expert_w[0].T)
    br = jnp.zeros((40, 1), F32).at[0:N_GROUPS, 0].set(router_group_b[0]).at[8:40, 0].set(router_expert_b[0])
    g1, b1 = ln1_g[0].reshape(1, D), ln1_b[0].reshape(1, D)
    g2, b2 = ln2_g[0].reshape(1, D), ln2_b[0].reshape(1, D)

    n_parts = N_PARTS if B % N_PARTS == 0 else 1
    bp = B // n_parts
    tp = bp * S
    tm = 256 if tp % 256 == 0 else ATT_BLK
    ts = 512 if S % 512 == 0 else ATT_BLK
    x2 = x.reshape(B * S, D)
    out = None
    for part in range(n_parts):
        proj = _proj(x2, w_rot, tm, part * tp, tp)
        hg, att = _mixers(proj, lb, nw, sinks[0].astype(F32), bp, S)
        x1, xp, eid, wt, rank, cnt = _merge(hg, att, proj, x2, wa, wb, wo, g1, b1, wr.astype(BF16), br,
                                            512 if tp % 512 == 0 else tm, part * tp)
        pos, block_expert, block_valid = _slot_layout(eid, rank, cnt[:, 0].astype(jnp.int32), tp)
        xbuf = _sc_scatter2(xp, pos[0], pos[1], 2 * tp + N_EXPERTS * MOE_BLOCK)
        ybuf = _experts(block_expert, block_valid, xbuf, w_exp_gate[0], w_exp_up[0], w_exp_down[0])
        rows2 = _sc_gather(ybuf, pos.reshape(-1))
        out = _combine(rows2, x1, wt.T, g2, b2, tm, out, part * (tp // tm), B * S)
    return out.reshape(B, S, D)
```

```python
import functools

import numpy as np
import jax
import jax.numpy as jnp
from jax import lax
from jax.experimental import pallas as pl
from jax.experimental.pallas import tpu as pltpu
from jax.experimental.pallas import tpu_sc as plsc

F32 = jnp.float32
BF16 = jnp.bfloat16

D_MODEL = 1024
DEPTH = 1
HG_HEADS = 4
HG_DK = 128
HG_DV = 128
HG_W = HG_HEADS * HG_DK
CHUNK = 64
ATT_Q_HEADS = 8
ATT_KV_HEADS = 2
ATT_GROUP = ATT_Q_HEADS // ATT_KV_HEADS
ATT_HD = 64
ATT_QW = ATT_Q_HEADS * ATT_HD
ATT_KVW = ATT_KV_HEADS * ATT_HD
ATT_BLK = 128
N_GROUPS = 4
EPG = 8
N_EXPERTS = N_GROUPS * EPG
D_EXPERT = 512
MOE_BLOCK = 256
DN_ALPHA = (2.0 * DEPTH) ** 0.25
LN_EPS = 1e-5
RMS_EPS = 1e-6
NEG_INF = -1e30

PROJ_W = 4 * HG_W + ATT_QW + 2 * ATT_KVW + 2 * D_MODEL
PROJ_ROT = 4 * HG_W + ATT_QW + 2 * ATT_KVW
N_LEVELS = 6
N_ARG_GROUPS = N_LEVELS + 2

VMEM_LIMIT = 56 * 1024 * 1024
MERGE_SUB = 256
N_PARTS = 2


def _cparams(sem):
    return pltpu.CompilerParams(dimension_semantics=sem, vmem_limit_bytes=VMEM_LIMIT)


def _dot(a, b):
    return jnp.dot(a, b, preferred_element_type=F32)


def _dot_nt(a, b):
    return lax.dot_general(a, b, (((1,), (1,)), ((), ())), preferred_element_type=F32)


def _dot_tn(a, b):
    return lax.dot_general(a, b, (((0,), (0,)), ((), ())), preferred_element_type=F32)


def _sigmoid(x):
    return 0.5 * jnp.tanh(0.5 * x) + 0.5


def _silu(x):
    return x * _sigmoid(x)


def _pack_bf16_pairs(xb):
    n = xb.shape[1] // 2
    lo = lax.bitcast_convert_type(xb[:, :n].astype(F32), jnp.uint32)
    hi = lax.bitcast_convert_type(xb[:, n:].astype(F32), jnp.uint32)
    return (lo >> 16) | hi


def _unpack_bf16_pairs(w):
    lo = lax.bitcast_convert_type(w << 16, F32)
    hi = lax.bitcast_convert_type(w & jnp.uint32(0xFFFF0000), F32)
    return jnp.concatenate([lo, hi], axis=1)


def _layer_norm(z, g, b):
    mu = jnp.mean(z, axis=-1, keepdims=True)
    zc = z - mu
    var = jnp.mean(zc * zc, axis=-1, keepdims=True)
    return zc * lax.rsqrt(var + LN_EPS) * g + b


def _proj_kernel(x_ref, w_ref, o_ref):
    o_ref[...] = _dot(x_ref[...].astype(BF16), w_ref[...])


def _proj(x2, w_bf, tm, row0, T):
    tile0 = row0 // tm
    return pl.pallas_call(
        _proj_kernel,
        grid=(T // tm,),
        in_specs=[pl.BlockSpec((tm, D_MODEL), lambda i: (i + tile0, 0)),
                  pl.BlockSpec((D_MODEL, PROJ_W), lambda i: (0, 0))],
        out_specs=pl.BlockSpec((tm, PROJ_W), lambda i: (i, 0)),
        out_shape=jax.ShapeDtypeStruct((T, PROJ_W), F32),
        compiler_params=_cparams(("arbitrary",)),
        name="proj",
    )(x2, w_bf)


def _hgrn_tables():
    C = CHUNK
    w = np.zeros((N_ARG_GROUPS, C, C), np.float32)
    masks = np.zeros((N_LEVELS, C, C), np.float32)
    for lvl in range(N_LEVELS):
        h = 1 << lvl
        for t in range(C):
            base = (t // (2 * h)) * 2 * h
            m = base + h - 1
            if (t // h) % 2 == 1:
                w[lvl, t, m + 1:t + 1] = 1.0
                masks[lvl, t, base:base + h] = 1.0
            else:
                w[lvl, t, t + 1:m + 1] = 1.0
    for t in range(C):
        w[N_LEVELS, t, :t + 1] = 1.0
        w[N_LEVELS + 1, t, t + 1:] = 1.0
    w = w.reshape(N_ARG_GROUPS * C, C)
    return np.concatenate([w, w], axis=1), masks


def _hgrn_chunk(q, fz, v, gate, c0, c1, nw, seg, mask_ref, state_ref):
    h = 0.5 * q
    qf = h + h * jnp.tanh(h)
    t1 = c1 * jnp.tanh(0.5 * fz)
    f = c0 + t1
    k = c1 - t1
    l2 = jnp.log2(f)
    v_bf = v.astype(BF16)
    qf_bf = qf.astype(BF16)
    k_bf = k.astype(BF16)

    l_hi = l2.astype(BF16)
    l_lo = (l2 - l_hi.astype(F32)).astype(BF16)
    args = _dot(seg, jnp.concatenate([l_hi, l_lo], axis=0))
    e = jnp.exp2(args)

    row = lax.broadcasted_iota(jnp.int32, (CHUNK, 1), 0)
    scores = [jnp.zeros((CHUNK, CHUNK), F32) for _ in range(HG_HEADS)]
    for lvl in range(N_LEVELS):
        half = 1 << lvl
        e_l = e[lvl * CHUNK:(lvl + 1) * CHUNK, :].astype(BF16)
        if half >= 16:
            sel = jnp.concatenate([(qf_bf if (r0 // half) % 2 else k_bf)[r0:r0 + half] for r0 in range(0, CHUNK, half)],
                                  axis=0)
        else:
            sel = jnp.where((row // half) % 2 == 1, qf_bf, k_bf)
        a = sel * e_l
        m = mask_ref[lvl]
        for hd in range(HG_HEADS):
            a_h = a[:, hd * HG_DK:(hd + 1) * HG_DK]
            scores[hd] = scores[hd] + m * _dot_nt(a_h, a_h)

    e_cum = e[N_LEVELS * CHUNK:(N_LEVELS + 1) * CHUNK, :]
    e_suf = e[(N_LEVELS + 1) * CHUNK:(N_LEVELS + 2) * CHUNK, :]
    q_in = qf_bf * e_cum.astype(BF16)
    k_out = k_bf * e_suf.astype(BF16)
    e_last = e_cum[CHUNK - 1:CHUNK, :]
    qk = qf * k

    outs = []
    for hd in range(HG_HEADS):
        cols = slice(hd * HG_DK, (hd + 1) * HG_DK)
        st = state_ref[hd]
        diag = jnp.sum(qk[:, cols], axis=-1, keepdims=True)
        o = (_dot_nt(q_in[:, cols], st.astype(BF16))
             + _dot(scores[hd].astype(BF16), v_bf[:, cols])
             + diag * v[:, cols])
        state_ref[hd] = st * e_last[:, cols] + _dot_tn(v_bf[:, cols], k_out[:, cols])
        o = o * lax.rsqrt(jnp.mean(o * o, axis=-1, keepdims=True) + RMS_EPS) * nw
        outs.append(o)
    hg = 0.5 * gate
    return jnp.concatenate(outs, axis=1) * (hg + hg * jnp.tanh(hg))


def _attn_bias():
    r = np.arange(ATT_BLK)[:, None]
    c = np.arange(2 * ATT_BLK)[None, :]
    dist = r + ATT_BLK - c
    window = (dist >= 0) & (dist < ATT_BLK)
    slopes = np.exp2(-8.0 * (np.arange(ATT_Q_HEADS, dtype=np.float32) + 1.0) / ATT_Q_HEADS).astype(np.float32)
    alibi = -slopes[:, None, None] * dist.astype(np.float32)[None]
    later = np.where(window[None], alibi, np.float32(NEG_INF))
    first = np.where((window & (c >= ATT_BLK))[None], alibi, np.float32(NEG_INF))
    return np.stack([later, first]).astype(np.float32)


def _attn_block(q_ref, kv_cur, kv_prev, bias_ref, table, sink_ref, o_ref):
    lane = lax.broadcasted_iota(jnp.int32, (2 * ATT_BLK, 2 * ATT_KVW), 1)
    lo = (lane % ATT_KVW) < ATT_HD

    kv = jnp.concatenate([kv_prev, kv_cur], axis=0).astype(BF16)
    kv_sw = jnp.concatenate([kv[:, ATT_HD:ATT_KVW], kv[:, :ATT_HD],
                             kv[:, ATT_KVW + ATT_HD:], kv[:, ATT_KVW:ATT_KVW + ATT_HD]], axis=1)
    zero = jnp.zeros_like(kv)
    placed = {}
    for h in range(ATT_KV_HEADS):
        for off in range(2):
            src = kv if h == off else kv_sw
            placed[h, off] = jnp.where(lo if off == 0 else jnp.logical_not(lo), src, zero)

    scale = ATT_HD ** -0.5
    for pair in range(ATT_Q_HEADS // 2):
        qp = (q_ref[:, pair * 2 * ATT_HD:(pair + 1) * 2 * ATT_HD] * scale).astype(BF16)
        acc = jnp.zeros((ATT_BLK, 2 * ATT_HD), F32)
        for off in range(2):
            j = 2 * pair + off
            kvh = placed[j // ATT_GROUP, off]
            sink = sink_ref[j]
            logits = _dot_nt(qp, kvh[:, :ATT_KVW]) + bias_ref[table, j]
            m = jnp.maximum(jnp.max(logits, axis=-1, keepdims=True), sink)
            p = jnp.exp(logits - m)
            den = jnp.sum(p, axis=-1, keepdims=True) + jnp.exp(sink - m)
            acc = acc + _dot(p.astype(BF16), kvh[:, ATT_KVW:]) / den
        o_ref[:, pair * 2 * ATT_HD:(pair + 1) * 2 * ATT_HD] = acc.astype(o_ref.dtype)


def _mixers_kernel(sink_ref, q_ref, f_ref, i_ref, g_ref, aq_ref, kv_ref, kvp_ref, lb_ref, nw_ref, seg_ref,
                   mask_ref, bias_ref, hg_ref, at_ref, state_ref):
    first = pl.program_id(1) == 0

    @pl.when(first)
    def _():
        state_ref[...] = jnp.zeros_like(state_ref)

    lb = lb_ref[...]
    c0 = 0.5 + 0.5 * lb
    c1 = 0.5 - 0.5 * lb
    nw = nw_ref[...]
    seg = seg_ref[...]
    for r0 in range(0, ATT_BLK, CHUNK):
        rows = slice(r0, r0 + CHUNK)
        o = _hgrn_chunk(q_ref[rows, :], f_ref[rows, :], i_ref[rows, :], g_ref[rows, :], c0, c1, nw, seg, mask_ref,
                        state_ref)
        hg_ref[rows, :] = o.astype(hg_ref.dtype)
    _attn_block(aq_ref, kv_ref[...], kvp_ref[...], bias_ref, jnp.where(first, 1, 0), sink_ref, at_ref)


def _mixers(proj, lb, norm_w, sinks, B, S):
    T = B * S
    nb = S // ATT_BLK
    seg, masks = _hgrn_tables()
    const = lambda shape: pl.BlockSpec(shape, lambda b, n: (0,) * len(shape))
    hcol = lambda cb: pl.BlockSpec((ATT_BLK, HG_W), lambda b, n: (b * nb + n, cb))
    hbase = (2 * D_MODEL) // HG_W
    qcol = (2 * D_MODEL + 4 * HG_W) // ATT_QW
    kvcol = (2 * D_MODEL + 4 * HG_W + ATT_QW) // (2 * ATT_KVW)
    out = pl.BlockSpec((ATT_BLK, HG_W), lambda b, n: (b * nb + n, 0))
    return pl.pallas_call(
        _mixers_kernel,
        grid=(B, nb),
        in_specs=[pl.BlockSpec(memory_space=pltpu.SMEM),
                  hcol(hbase), hcol(hbase + 1), hcol(hbase + 2), hcol(hbase + 3),
                  pl.BlockSpec((ATT_BLK, ATT_QW), lambda b, n: (b * nb + n, qcol)),
                  pl.BlockSpec((ATT_BLK, 2 * ATT_KVW), lambda b, n: (b * nb + n, kvcol)),
                  pl.BlockSpec((ATT_BLK, 2 * ATT_KVW), lambda b, n: (b * nb + jnp.maximum(n - 1, 0), kvcol)),
                  const((1, HG_W)), const((1, HG_DV)), const((N_ARG_GROUPS * CHUNK, 2 * CHUNK)),
                  const((N_LEVELS, CHUNK, CHUNK)), const((2, ATT_Q_HEADS, ATT_BLK, 2 * ATT_BLK))],
        out_specs=[out, out],
        out_shape=[jax.ShapeDtypeStruct((T, HG_W), BF16), jax.ShapeDtypeStruct((T, ATT_QW), BF16)],
        scratch_shapes=[pltpu.VMEM((HG_HEADS, HG_DV, HG_DK), F32)],
        compiler_params=_cparams(("arbitrary", "arbitrary")),
        name="mixers",
    )(sinks, proj, proj, proj, proj, proj, proj, proj, lb, norm_w, jnp.asarray(seg, BF16), jnp.asarray(masks),
      jnp.asarray(_attn_bias()))


def _merge_kernel(hg_ref, at_ref, ga_ref, gb_ref, x_ref, wa_ref, wb_ref, wo_ref, g1_ref, b1_ref,
                  wr_ref, br_ref, tri_ref, x1_ref, xp_ref, eid_ref, wt_ref, rank_ref, cnt_ref, base_ref, x1b_ref):
    tm = x_ref.shape[0]
    sub = min(tm, MERGE_SUB)
    for r0 in range(0, tm, sub):
        rows = slice(r0, r0 + sub)
        ya = _dot(hg_ref[rows, :], wa_ref[...])
        yb = _dot(at_ref[rows, :], wb_ref[...])
        merged = _sigmoid(ga_ref[rows, :]) * ya + _sigmoid(gb_ref[rows, :]) * yb
        z = DN_ALPHA * x_ref[rows, :] + _dot(merged.astype(BF16), wo_ref[...])
        x1 = _layer_norm(z, g1_ref[...], b1_ref[...])
        x1_ref[rows, :] = x1
        x1b = x1.astype(BF16)
        x1b_ref[rows, :] = x1b
        xp_ref[rows, :] = _pack_bf16_pairs(x1b)

    lg = _dot_nt(wr_ref[...], x1b_ref[...]) + br_ref[...]
    g = lg[0:8, :]
    row8 = lax.broadcasted_iota(jnp.int32, (8, tm), 0)
    g = jnp.where(row8 < N_GROUPS, g, -jnp.inf)
    gmax = jnp.max(g, axis=0, keepdims=True)
    gsel = jnp.min(jnp.where(g == gmax, row8, 8), axis=0, keepdims=True)
    gw = 1.0 / jnp.sum(jnp.exp(g - gmax), axis=0, keepdims=True)
    el = jnp.where(gsel == 0, lg[8:16, :],
                   jnp.where(gsel == 1, lg[16:24, :], jnp.where(gsel == 2, lg[24:32, :], lg[32:40, :])))
    v1 = jnp.max(el, axis=0, keepdims=True)
    i1 = jnp.min(jnp.where(el == v1, row8, 8), axis=0, keepdims=True)
    el2 = jnp.where(row8 == i1, -jnp.inf, el)
    v2 = jnp.max(el2, axis=0, keepdims=True)
    i2 = jnp.min(jnp.where(el2 == v2, row8, 8), axis=0, keepdims=True)
    e2 = jnp.exp(v2 - v1)
    den = 1.0 + e2
    e_a = gsel * EPG + i1
    e_b = gsel * EPG + i2
    eid_ref[...] = jnp.concatenate([e_a, e_b], axis=0)
    wt_ref[...] = jnp.concatenate([gw / den, gw * e2 / den], axis=0)

    @pl.when(pl.program_id(0) == 0)
    def _():
        base_ref[...] = jnp.zeros_like(base_ref)

    row_e = lax.broadcasted_iota(jnp.int32, (N_EXPERTS, tm), 0)
    oh_a = jnp.where(row_e == e_a, 1.0, 0.0)
    oh_b = jnp.where(row_e == e_b, 1.0, 0.0)
    tri = tri_ref[...]
    pre_a = _dot(oh_a.astype(BF16), tri)
    pre_b = _dot(oh_b.astype(BF16), tri)
    cnt_a = jnp.sum(oh_a, axis=1, keepdims=True)
    cnt_b = jnp.sum(oh_b, axis=1, keepdims=True)
    base = base_ref[...]
    rank_a = jnp.sum(oh_a * (base + pre_a), axis=0, keepdims=True)
    rank_b = jnp.sum(oh_b * (base + cnt_a + pre_b), axis=0, keepdims=True)
    rank_ref[...] = jnp.concatenate([rank_a, rank_b], axis=0).astype(jnp.int32)
    base = base + cnt_a + cnt_b
    base_ref[...] = base
    cnt_ref[...] = jnp.broadcast_to(base, cnt_ref.shape)


def _merge(hg, att, proj, x2, wa, wb, wo, g1, b1, wr, br, tm, row0):
    T = hg.shape[0]
    tile0 = row0 // tm
    row = lambda w, cb=0: pl.BlockSpec((tm, w), lambda i: (i, cb))
    const = lambda shape: pl.BlockSpec(shape, lambda i: (0,) * len(shape))
    lanes = pl.BlockSpec((2, tm), lambda i: (0, i))
    tri = jnp.asarray(np.triu(np.ones((tm, tm), np.float32), 1), BF16)
    return pl.pallas_call(
        _merge_kernel,
        grid=(T // tm,),
        in_specs=[row(HG_W), row(ATT_QW), row(D_MODEL, 0), row(D_MODEL, 1),
                  pl.BlockSpec((tm, D_MODEL), lambda i: (i + tile0, 0)),
                  const((HG_W, D_MODEL)), const((ATT_QW, D_MODEL)), const((D_MODEL, D_MODEL)),
                  const((1, D_MODEL)), const((1, D_MODEL)), const((40, D_MODEL)), const((40, 1)),
                  const((tm, tm))],
        out_specs=[row(D_MODEL), row(D_MODEL // 2), lanes, lanes, lanes, const((N_EXPERTS, 128))],
        out_shape=[jax.ShapeDtypeStruct((T, D_MODEL), F32),
                   jax.ShapeDtypeStruct((T, D_MODEL // 2), jnp.uint32),
                   jax.ShapeDtypeStruct((2, T), jnp.int32),
                   jax.ShapeDtypeStruct((2, T), F32),
                   jax.ShapeDtypeStruct((2, T), jnp.int32),
                   jax.ShapeDtypeStruct((N_EXPERTS, 128), F32)],
        scratch_shapes=[pltpu.VMEM((N_EXPERTS, 1), F32), pltpu.VMEM((tm, D_MODEL), BF16)],
        compiler_params=_cparams(("arbitrary",)),
        name="merge",
    )(hg, att, proj, proj, x2, wa, wb, wo, g1, b1, wr, br, tri)


SC_WINDOW = 32
SC_IDX_LANES = 128


def _pad_indices(idx):
    rows = idx.reshape(-1, SC_WINDOW)
    return jnp.pad(rows, ((0, 0), (0, SC_IDX_LANES - SC_WINDOW)))


def _sc_mesh():
    return plsc.VectorSubcoreMesh(core_axis_name="core", subcore_axis_name="subcore")


def _sc_scatter2(x, idx_a, idx_b, n_out):
    T, d = x.shape

    @pl.kernel(out_type=jax.ShapeDtypeStruct((n_out, d), x.dtype), mesh=_sc_mesh())
    def scatter(x_hbm, ia_hbm, ib_hbm, o_hbm):
        def body(x_vmem, ia_vmem, ib_vmem):
            pltpu.sync_copy(x_vmem, o_hbm.at[ia_vmem.at[0, pl.ds(0, SC_WINDOW)]])
            pltpu.sync_copy(x_vmem, o_hbm.at[ib_vmem.at[0, pl.ds(0, SC_WINDOW)]])

        idx_spec = pl.BlockSpec((1, SC_IDX_LANES), lambda i: (i, 0))
        pltpu.emit_pipeline(
            body, grid=(T // SC_WINDOW,),
            in_specs=[pl.BlockSpec((SC_WINDOW, d), lambda i: (i, 0)), idx_spec, idx_spec],
            out_specs=[],
            core_axis_name=("core", "subcore"),
            dimension_semantics=(pltpu.PARALLEL,),
        )(x_hbm, ia_hbm, ib_hbm)

    return scatter(x, _pad_indices(idx_a), _pad_indices(idx_b))


def _sc_gather(x, idx):
    n = idx.shape[0]
    d = x.shape[1]

    @pl.kernel(out_type=jax.ShapeDtypeStruct((n, d), x.dtype), mesh=_sc_mesh())
    def gather(x_hbm, i_hbm, o_hbm):
        def body(i_vmem, o_vmem):
            pltpu.sync_copy(x_hbm.at[i_vmem.at[0, pl.ds(0, SC_WINDOW)]], o_vmem)

        pltpu.emit_pipeline(
            body, grid=(n // SC_WINDOW,),
            in_specs=[pl.BlockSpec((1, SC_IDX_LANES), lambda i: (i, 0))],
            out_specs=[pl.BlockSpec((SC_WINDOW, d), lambda i: (i, 0))],
            core_axis_name=("core", "subcore"),
            dimension_semantics=(pltpu.PARALLEL,),
        )(i_hbm, o_hbm)

    return gather(x, _pad_indices(idx))


def _expert_kernel(be_ref, nv_ref, nxt_ref, par_ref, xb_ref, wg_hbm, wu_hbm, wd_hbm, y_ref,
                   wg_st, wu_st, wd_st, wg_bf, wu_bf, wd_bf, sem):
    i = pl.program_id(0)
    nv = nv_ref[i]

    def fetch(expert, slot):
        return [pltpu.make_async_copy(src.at[expert], dst.at[slot], sem.at[slot, j])
                for j, (src, dst) in enumerate(((wg_hbm, wg_st), (wu_hbm, wu_st), (wd_hbm, wd_st)))]

    @pl.when(i == 0)
    def _():
        for cp in fetch(be_ref[0], par_ref[0]):
            cp.start()

    @pl.when((i == 0) | (be_ref[i] != be_ref[jnp.maximum(i - 1, 0)]))
    def _():
        slot = par_ref[i]
        for cp in fetch(be_ref[i], slot):
            cp.wait()
        wg_bf[...] = wg_st[slot].astype(BF16)
        wu_bf[...] = wu_st[slot].astype(BF16)
        wd_bf[...] = wd_st[slot].astype(BF16)

        @pl.when(nxt_ref[i] >= 0)
        def _():
            for cp in fetch(nxt_ref[i], 1 - slot):
                cp.start()

    @pl.when(nv > 0)
    def _():
        row = lax.broadcasted_iota(jnp.int32, (MOE_BLOCK, 1), 0)
        xb = _unpack_bf16_pairs(jnp.where(row < nv, xb_ref[...], jnp.uint32(0))).astype(BF16)
        g = _dot(xb, wg_bf[...])
        u = _dot(xb, wu_bf[...])
        h = (_silu(g) * u).astype(BF16)
        y_ref[...] = _pack_bf16_pairs(_dot(h, wd_bf[...]).astype(BF16))


def _experts(block_meta, xbuf, w_gate, w_up, w_down):
    P = xbuf.shape[0]
    nb = P // MOE_BLOCK
    rows = pl.BlockSpec((MOE_BLOCK, D_MODEL // 2), lambda i, *_: (i, 0))
    hbm = pl.BlockSpec(memory_space=pl.ANY)
    return pl.pallas_call(
        _expert_kernel,
        grid_spec=pltpu.PrefetchScalarGridSpec(
            num_scalar_prefetch=4,
            grid=(nb,),
            in_specs=[rows, hbm, hbm, hbm],
            out_specs=rows,
            scratch_shapes=[pltpu.VMEM((2, D_MODEL, D_EXPERT), F32), pltpu.VMEM((2, D_MODEL, D_EXPERT), F32),
                            pltpu.VMEM((2, D_EXPERT, D_MODEL), F32),
                            pltpu.VMEM((D_MODEL, D_EXPERT), BF16), pltpu.VMEM((D_MODEL, D_EXPERT), BF16),
                            pltpu.VMEM((D_EXPERT, D_MODEL), BF16),
                            pltpu.SemaphoreType.DMA((2, 3))]),
        out_shape=jax.ShapeDtypeStruct((P, D_MODEL // 2), jnp.uint32),
        compiler_params=_cparams(("arbitrary",)),
        name="experts",
    )(*block_meta, xbuf, w_gate, w_up, w_down)


def _combine_kernel(ra_ref, rb_ref, x1_ref, wt_ref, g2_ref, b2_ref, *rest):
    o_ref = rest[-1]
    w2 = wt_ref[...]
    w = jnp.concatenate([w2, jnp.zeros((6, w2.shape[1]), F32)], axis=0).T
    y = w[:, 0:1] * _unpack_bf16_pairs(ra_ref[...]) + w[:, 1:2] * _unpack_bf16_pairs(rb_ref[...])
    o_ref[...] = _layer_norm(DN_ALPHA * x1_ref[...] + y, g2_ref[...], b2_ref[...])


def _combine(rows2, x1, wt, g2, b2, tm, out_prev, tile0, t_total):
    T = x1.shape[0]
    nt = T // tm
    const = lambda shape: pl.BlockSpec(shape, lambda i: (0,) * len(shape))
    in_specs = [pl.BlockSpec((tm, D_MODEL // 2), lambda i: (i, 0)),
                pl.BlockSpec((tm, D_MODEL // 2), lambda i: (i + nt, 0)),
                pl.BlockSpec((tm, D_MODEL), lambda i: (i, 0)), pl.BlockSpec((2, tm), lambda i: (0, i)),
                const((1, D_MODEL)), const((1, D_MODEL))]
    args = [rows2, rows2, x1, wt, g2, b2]
    aliases = {}
    if out_prev is not None:
        in_specs.append(pl.BlockSpec(memory_space=pl.ANY))
        args.append(out_prev)
        aliases = {len(args) - 1: 0}
    return pl.pallas_call(
        _combine_kernel,
        grid=(nt,),
        in_specs=in_specs,
        out_specs=pl.BlockSpec((tm, D_MODEL), lambda i: (i + tile0, 0)),
        out_shape=jax.ShapeDtypeStruct((t_total, D_MODEL), F32),
        input_output_aliases=aliases,
        compiler_params=_cparams(("arbitrary",)),
        name="combine",
    )(*args)


def _slot_layout(eid, rank, counts, T):
    P = 2 * T + N_EXPERTS * MOE_BLOCK
    nb = P // MOE_BLOCK
    padded = ((counts + MOE_BLOCK - 1) // MOE_BLOCK) * MOE_BLOCK
    pend = jnp.cumsum(padded)
    pstart = pend - padded
    experts = jnp.arange(N_EXPERTS, dtype=jnp.int32)
    pos = rank + jnp.sum(jnp.where(eid[:, :, None] == experts, pstart, 0), axis=-1)

    block_start = jnp.arange(nb, dtype=jnp.int32) * MOE_BLOCK
    block_expert = jnp.minimum(jnp.sum(block_start[:, None] >= pend[None, :], axis=1), N_EXPERTS - 1)
    onehot = block_expert[:, None] == experts[None, :]
    look = lambda table: jnp.sum(jnp.where(onehot, table[None, :], 0), axis=1)
    block_valid = jnp.clip(look(pstart + counts) - block_start, 0, MOE_BLOCK)
    block_valid = jnp.where(block_start < pend[-1], block_valid, 0)
    present = (counts > 0) | (experts == N_EXPERTS - 1)
    later = present[None, :] & (experts[None, :] > experts[:, None])
    next_present = jnp.min(jnp.where(later, experts[None, :], N_EXPERTS), axis=1)
    next_present = jnp.where(next_present < N_EXPERTS, next_present, -1)
    runs_before = jnp.sum(present[None, :] & (experts[None, :] < experts[:, None]), axis=1)
    block_next = look(next_present)
    block_slot = look(runs_before % 2)
    i32 = lambda a: a.astype(jnp.int32)
    return i32(pos), (i32(block_expert), i32(block_valid), i32(block_next), i32(block_slot))


def kernel(x, lb_logits, w_in, hg_norm_w, sinks, w_branch_a, w_branch_b, w_out, ln1_g, ln1_b,
           router_group_w, router_group_b, router_expert_w, router_expert_b,
           w_exp_gate, w_exp_up, w_exp_down, ln2_g, ln2_b):
    B, S, D = x.shape
    assert D == D_MODEL and S % ATT_BLK == 0 and w_in.shape[0] == DEPTH == 1
    lb_all = jnp.cumsum(jax.nn.softmax(lb_logits.astype(F32), axis=0), axis=0)
    lb = lb_all[0].reshape(1, HG_W)
    w_rot = jnp.concatenate([w_in[0][:, PROJ_ROT:], w_in[0][:, :PROJ_ROT]], axis=1).astype(BF16)
    nw = hg_norm_w[0].reshape(1, HG_DV).astype(F32)
    wa, wb, wo = w_branch_a[0].astype(BF16), w_branch_b[0].astype(BF16), w_out[0].astype(BF16)
    wr = jnp.zeros((40, D), F32).at[0:N_GROUPS].set(router_group_w[0].T).at[8:40].set(router_expert_w[0].T)
    br = jnp.zeros((40, 1), F32).at[0:N_GROUPS, 0].set(router_group_b[0]).at[8:40, 0].set(router_expert_b[0])
    g1, b1 = ln1_g[0].reshape(1, D), ln1_b[0].reshape(1, D)
    g2, b2 = ln2_g[0].reshape(1, D), ln2_b[0].reshape(1, D)

    n_parts = N_PARTS if B % N_PARTS == 0 else 1
    bp = B // n_parts
    tp = bp * S
    tm = 256 if tp % 256 == 0 else ATT_BLK
    ts = 512 if S % 512 == 0 else ATT_BLK
    x2 = x.reshape(B * S, D)
    out = None
    for part in range(n_parts):
        proj = _proj(x2, w_rot, tm, part * tp, tp)
        hg, att = _mixers(proj, lb, nw, sinks[0].astype(F32), bp, S)
        x1, xp, eid, wt, rank, cnt = _merge(hg, att, proj, x2, wa, wb, wo, g1, b1, wr.astype(BF16), br,
                                            512 if tp % 512 == 0 else tm, part * tp)
        pos, block_meta = _slot_layout(eid, rank, cnt[:, 0].astype(jnp.int32), tp)
        xbuf = _sc_scatter2(xp, pos[0], pos[1], 2 * tp + N_EXPERTS * MOE_BLOCK)
        ybuf = _experts(block_meta, xbuf, w_exp_gate[0], w_exp_up[0], w_exp_down[0])
        rows2 = _sc_gather(ybuf, pos.reshape(-1))
        out = _combine(rows2, x1, wt, g2, b2, tm, out, part * (tp // tm), B * S)
    return out.reshape(B, S, D)
```

```python
import functools

import numpy as np
import jax
import jax.numpy as jnp
from jax import lax
from jax.experimental import pallas as pl
from jax.experimental.pallas import tpu as pltpu
from jax.experimental.pallas import tpu_sc as plsc

F32 = jnp.float32
BF16 = jnp.bfloat16

D_MODEL = 1024
DEPTH = 1
HG_HEADS = 4
HG_DK = 128
HG_DV = 128
HG_W = HG_HEADS * HG_DK
CHUNK = 64
ATT_Q_HEADS = 8
ATT_KV_HEADS = 2
ATT_GROUP = ATT_Q_HEADS // ATT_KV_HEADS
ATT_HD = 64
ATT_QW = ATT_Q_HEADS * ATT_HD
ATT_KVW = ATT_KV_HEADS * ATT_HD
ATT_BLK = 128
N_GROUPS = 4
EPG = 8
N_EXPERTS = N_GROUPS * EPG
D_EXPERT = 512
MOE_BLOCK = 256
DN_ALPHA = (2.0 * DEPTH) ** 0.25
LN_EPS = 1e-5
RMS_EPS = 1e-6
NEG_INF = -1e30

PROJ_W = 4 * HG_W + ATT_QW + 2 * ATT_KVW + 2 * D_MODEL
PROJ_ROT = 4 * HG_W + ATT_QW + 2 * ATT_KVW
N_LEVELS = 6
N_ARG_GROUPS = N_LEVELS + 2

VMEM_LIMIT = 56 * 1024 * 1024
MERGE_SUB = 256
MIX_TILE = 512
N_PARTS = 2


def _cparams(sem):
    return pltpu.CompilerParams(dimension_semantics=sem, vmem_limit_bytes=VMEM_LIMIT)


def _dot(a, b):
    return jnp.dot(a, b, preferred_element_type=F32)


def _dot_nt(a, b):
    return lax.dot_general(a, b, (((1,), (1,)), ((), ())), preferred_element_type=F32)


def _dot_tn(a, b):
    return lax.dot_general(a, b, (((0,), (0,)), ((), ())), preferred_element_type=F32)


def _sigmoid(x):
    return 0.5 * jnp.tanh(0.5 * x) + 0.5


def _silu(x):
    return x * _sigmoid(x)


def _pack_bf16_pairs(xb):
    n = xb.shape[1] // 2
    lo = lax.bitcast_convert_type(xb[:, :n].astype(F32), jnp.uint32)
    hi = lax.bitcast_convert_type(xb[:, n:].astype(F32), jnp.uint32)
    return (lo >> 16) | hi


def _unpack_bf16_pairs(w):
    lo = lax.bitcast_convert_type(w << 16, F32)
    hi = lax.bitcast_convert_type(w & jnp.uint32(0xFFFF0000), F32)
    return jnp.concatenate([lo, hi], axis=1)


def _layer_norm(z, g, b):
    mu = jnp.mean(z, axis=-1, keepdims=True)
    zc = z - mu
    var = jnp.mean(zc * zc, axis=-1, keepdims=True)
    return zc * lax.rsqrt(var + LN_EPS) * g + b


def _proj_kernel(x_ref, w_ref, o_ref):
    o_ref[...] = _dot(x_ref[...].astype(BF16), w_ref[...])


def _proj(x2, w_bf, tm, row0, T):
    tile0 = row0 // tm
    return pl.pallas_call(
        _proj_kernel,
        grid=(T // tm,),
        in_specs=[pl.BlockSpec((tm, D_MODEL), lambda i: (i + tile0, 0)),
                  pl.BlockSpec((D_MODEL, PROJ_W), lambda i: (0, 0))],
        out_specs=pl.BlockSpec((tm, PROJ_W), lambda i: (i, 0)),
        out_shape=jax.ShapeDtypeStruct((T, PROJ_W), F32),
        compiler_params=_cparams(("arbitrary",)),
        name="proj",
    )(x2, w_bf)


def _hgrn_tables():
    C = CHUNK
    w = np.zeros((N_ARG_GROUPS, C, C), np.float32)
    masks = np.zeros((N_LEVELS, C, C), np.float32)
    for lvl in range(N_LEVELS):
        h = 1 << lvl
        for t in range(C):
            base = (t // (2 * h)) * 2 * h
            m = base + h - 1
            if (t // h) % 2 == 1:
                w[lvl, t, m + 1:t + 1] = 1.0
                masks[lvl, t, base:base + h] = 1.0
            else:
                w[lvl, t, t + 1:m + 1] = 1.0
    for t in range(C):
        w[N_LEVELS, t, :t + 1] = 1.0
        w[N_LEVELS + 1, t, t + 1:] = 1.0
    w = w.reshape(N_ARG_GROUPS * C, C)
    return np.concatenate([w, w], axis=1), masks


def _hgrn_chunk(q, fz, v, gate, c0, c1, nw, seg, mask_ref, state_ref):
    h = 0.5 * q
    qf = h + h * jnp.tanh(h)
    t1 = c1 * jnp.tanh(0.5 * fz)
    f = c0 + t1
    k = c1 - t1
    l2 = jnp.log2(f)
    v_bf = v.astype(BF16)
    qf_bf = qf.astype(BF16)
    k_bf = k.astype(BF16)

    l_hi = l2.astype(BF16)
    l_lo = (l2 - l_hi.astype(F32)).astype(BF16)
    args = _dot(seg, jnp.concatenate([l_hi, l_lo], axis=0))
    e = jnp.exp2(args)

    row = lax.broadcasted_iota(jnp.int32, (CHUNK, 1), 0)
    scores = [jnp.zeros((CHUNK, CHUNK), F32) for _ in range(HG_HEADS)]
    for lvl in range(N_LEVELS):
        half = 1 << lvl
        e_l = e[lvl * CHUNK:(lvl + 1) * CHUNK, :].astype(BF16)
        if half >= 16:
            sel = jnp.concatenate([(qf_bf if (r0 // half) % 2 else k_bf)[r0:r0 + half] for r0 in range(0, CHUNK, half)],
                                  axis=0)
        else:
            sel = jnp.where((row // half) % 2 == 1, qf_bf, k_bf)
        a = sel * e_l
        m = mask_ref[lvl]
        for hd in range(HG_HEADS):
            a_h = a[:, hd * HG_DK:(hd + 1) * HG_DK]
            scores[hd] = scores[hd] + m * _dot_nt(a_h, a_h)

    e_cum = e[N_LEVELS * CHUNK:(N_LEVELS + 1) * CHUNK, :]
    e_suf = e[(N_LEVELS + 1) * CHUNK:(N_LEVELS + 2) * CHUNK, :]
    q_in = qf_bf * e_cum.astype(BF16)
    k_out = k_bf * e_suf.astype(BF16)
    e_last = e_cum[CHUNK - 1:CHUNK, :]
    qk = qf * k

    outs = []
    for hd in range(HG_HEADS):
        cols = slice(hd * HG_DK, (hd + 1) * HG_DK)
        st = state_ref[hd]
        diag = jnp.sum(qk[:, cols], axis=-1, keepdims=True)
        o = (_dot_nt(q_in[:, cols], st.astype(BF16))
             + _dot(scores[hd].astype(BF16), v_bf[:, cols])
             + diag * v[:, cols])
        state_ref[hd] = st * e_last[:, cols] + _dot_tn(v_bf[:, cols], k_out[:, cols])
        o = o * lax.rsqrt(jnp.mean(o * o, axis=-1, keepdims=True) + RMS_EPS) * nw
        outs.append(o)
    hg = 0.5 * gate
    return jnp.concatenate(outs, axis=1) * (hg + hg * jnp.tanh(hg))


def _attn_bias():
    r = np.arange(ATT_BLK)[:, None]
    c = np.arange(2 * ATT_BLK)[None, :]
    dist = r + ATT_BLK - c
    window = (dist >= 0) & (dist < ATT_BLK)
    slopes = np.exp2(-8.0 * (np.arange(ATT_Q_HEADS, dtype=np.float32) + 1.0) / ATT_Q_HEADS).astype(np.float32)
    alibi = -slopes[:, None, None] * dist.astype(np.float32)[None]
    later = np.where(window[None], alibi, np.float32(NEG_INF))
    first = np.where((window & (c >= ATT_BLK))[None], alibi, np.float32(NEG_INF))
    return np.stack([later, first]).astype(np.float32)


def _attn_block(q_ref, kv_cur, kv_prev, bias_ref, table, sink_ref, o_ref):
    lane = lax.broadcasted_iota(jnp.int32, (2 * ATT_BLK, 2 * ATT_KVW), 1)
    lo = (lane % ATT_KVW) < ATT_HD

    kv = jnp.concatenate([kv_prev, kv_cur], axis=0).astype(BF16)
    kv_sw = jnp.concatenate([kv[:, ATT_HD:ATT_KVW], kv[:, :ATT_HD],
                             kv[:, ATT_KVW + ATT_HD:], kv[:, ATT_KVW:ATT_KVW + ATT_HD]], axis=1)
    zero = jnp.zeros_like(kv)
    placed = {}
    for h in range(ATT_KV_HEADS):
        for off in range(2):
            src = kv if h == off else kv_sw
            placed[h, off] = jnp.where(lo if off == 0 else jnp.logical_not(lo), src, zero)

    scale = ATT_HD ** -0.5
    for pair in range(ATT_Q_HEADS // 2):
        qp = (q_ref[:, pair * 2 * ATT_HD:(pair + 1) * 2 * ATT_HD] * scale).astype(BF16)
        acc = jnp.zeros((ATT_BLK, 2 * ATT_HD), F32)
        for off in range(2):
            j = 2 * pair + off
            kvh = placed[j // ATT_GROUP, off]
            sink = sink_ref[j]
            logits = _dot_nt(qp, kvh[:, :ATT_KVW]) + bias_ref[table, j]
            m = jnp.maximum(jnp.max(logits, axis=-1, keepdims=True), sink)
            p = jnp.exp(logits - m)
            den = jnp.sum(p, axis=-1, keepdims=True) + jnp.exp(sink - m)
            acc = acc + _dot(p.astype(BF16), kvh[:, ATT_KVW:]) / den
        o_ref[:, pair * 2 * ATT_HD:(pair + 1) * 2 * ATT_HD] = acc.astype(o_ref.dtype)


def _mixers_kernel(sink_ref, q_ref, f_ref, i_ref, g_ref, aq_ref, kv_ref, kvp_ref, lb_ref, nw_ref, seg_ref,
                   mask_ref, bias_ref, hg_ref, at_ref, state_ref):
    first = pl.program_id(1) == 0

    @pl.when(first)
    def _():
        state_ref[...] = jnp.zeros_like(state_ref)

    lb = lb_ref[...]
    c0 = 0.5 + 0.5 * lb
    c1 = 0.5 - 0.5 * lb
    nw = nw_ref[...]
    seg = seg_ref[...]
    tile = q_ref.shape[0]
    for r0 in range(0, tile, CHUNK):
        rows = slice(r0, r0 + CHUNK)
        o = _hgrn_chunk(q_ref[rows, :], f_ref[rows, :], i_ref[rows, :], g_ref[rows, :], c0, c1, nw, seg, mask_ref,
                        state_ref)
        hg_ref[rows, :] = o.astype(hg_ref.dtype)
    for r0 in range(0, tile, ATT_BLK):
        rows = slice(r0, r0 + ATT_BLK)
        if r0 == 0:
            prev, table = kvp_ref[...], jnp.where(first, 1, 0)
        else:
            prev, table = kv_ref[r0 - ATT_BLK:r0, :], 0
        _attn_block(aq_ref.at[rows, :], kv_ref[rows, :], prev, bias_ref, table, sink_ref, at_ref.at[rows, :])


def _mixers(proj, lb, norm_w, sinks, B, S):
    T = B * S
    tile = MIX_TILE if S % MIX_TILE == 0 else ATT_BLK
    nb = S // tile
    per = tile // ATT_BLK
    seg, masks = _hgrn_tables()
    const = lambda shape: pl.BlockSpec(shape, lambda b, n: (0,) * len(shape))
    hcol = lambda cb: pl.BlockSpec((tile, HG_W), lambda b, n: (b * nb + n, cb))
    hbase = (2 * D_MODEL) // HG_W
    qcol = (2 * D_MODEL + 4 * HG_W) // ATT_QW
    kvcol = (2 * D_MODEL + 4 * HG_W + ATT_QW) // (2 * ATT_KVW)
    out = pl.BlockSpec((tile, HG_W), lambda b, n: (b * nb + n, 0))
    return pl.pallas_call(
        _mixers_kernel,
        grid=(B, nb),
        in_specs=[pl.BlockSpec(memory_space=pltpu.SMEM),
                  hcol(hbase), hcol(hbase + 1), hcol(hbase + 2), hcol(hbase + 3),
                  pl.BlockSpec((tile, ATT_QW), lambda b, n: (b * nb + n, qcol)),
                  pl.BlockSpec((tile, 2 * ATT_KVW), lambda b, n: (b * nb + n, kvcol)),
                  pl.BlockSpec((ATT_BLK, 2 * ATT_KVW),
                               lambda b, n: ((b * nb + n) * per - jnp.where(n > 0, 1, 0), kvcol)),
                  const((1, HG_W)), const((1, HG_DV)), const((N_ARG_GROUPS * CHUNK, 2 * CHUNK)),
                  const((N_LEVELS, CHUNK, CHUNK)), const((2, ATT_Q_HEADS, ATT_BLK, 2 * ATT_BLK))],
        out_specs=[out, out],
        out_shape=[jax.ShapeDtypeStruct((T, HG_W), BF16), jax.ShapeDtypeStruct((T, ATT_QW), BF16)],
        scratch_shapes=[pltpu.VMEM((HG_HEADS, HG_DV, HG_DK), F32)],
        compiler_params=_cparams(("arbitrary", "arbitrary")),
        name="mixers",
    )(sinks, proj, proj, proj, proj, proj, proj, proj, lb, norm_w, jnp.asarray(seg, BF16), jnp.asarray(masks),
      jnp.asarray(_attn_bias()))


def _merge_kernel(hg_ref, at_ref, ga_ref, gb_ref, x_ref, wa_ref, wb_ref, wo_ref, g1_ref, b1_ref,
                  wr_ref, br_ref, tri_ref, x1_ref, xp_ref, eid_ref, wt_ref, rank_ref, cnt_ref, base_ref, x1b_ref):
    tm = x_ref.shape[0]
    sub = min(tm, MERGE_SUB)
    for r0 in range(0, tm, sub):
        rows = slice(r0, r0 + sub)
        ya = _dot(hg_ref[rows, :], wa_ref[...])
        yb = _dot(at_ref[rows, :], wb_ref[...])
        merged = _sigmoid(ga_ref[rows, :]) * ya + _sigmoid(gb_ref[rows, :]) * yb
        z = DN_ALPHA * x_ref[rows, :] + _dot(merged.astype(BF16), wo_ref[...])
        x1 = _layer_norm(z, g1_ref[...], b1_ref[...])
        x1_ref[rows, :] = x1
        x1b = x1.astype(BF16)
        x1b_ref[rows, :] = x1b
        xp_ref[rows, :] = _pack_bf16_pairs(x1b)

    lg = _dot_nt(wr_ref[...], x1b_ref[...]) + br_ref[...]
    g = lg[0:8, :]
    row8 = lax.broadcasted_iota(jnp.int32, (8, tm), 0)
    g = jnp.where(row8 < N_GROUPS, g, -jnp.inf)
    gmax = jnp.max(g, axis=0, keepdims=True)
    gsel = jnp.min(jnp.where(g == gmax, row8, 8), axis=0, keepdims=True)
    gw = 1.0 / jnp.sum(jnp.exp(g - gmax), axis=0, keepdims=True)
    el = jnp.where(gsel == 0, lg[8:16, :],
                   jnp.where(gsel == 1, lg[16:24, :], jnp.where(gsel == 2, lg[24:32, :], lg[32:40, :])))
    v1 = jnp.max(el, axis=0, keepdims=True)
    i1 = jnp.min(jnp.where(el == v1, row8, 8), axis=0, keepdims=True)
    el2 = jnp.where(row8 == i1, -jnp.inf, el)
    v2 = jnp.max(el2, axis=0, keepdims=True)
    i2 = jnp.min(jnp.where(el2 == v2, row8, 8), axis=0, keepdims=True)
    e2 = jnp.exp(v2 - v1)
    den = 1.0 + e2
    e_a = gsel * EPG + i1
    e_b = gsel * EPG + i2
    eid_ref[...] = jnp.concatenate([e_a, e_b], axis=0)
    wt_ref[...] = jnp.concatenate([gw / den, gw * e2 / den], axis=0)

    @pl.when(pl.program_id(0) == 0)
    def _():
        base_ref[...] = jnp.zeros_like(base_ref)

    row_e = lax.broadcasted_iota(jnp.int32, (N_EXPERTS, tm), 0)
    oh_a = jnp.where(row_e == e_a, 1.0, 0.0)
    oh_b = jnp.where(row_e == e_b, 1.0, 0.0)
    tri = tri_ref[...]
    pre_a = _dot(oh_a.astype(BF16), tri)
    pre_b = _dot(oh_b.astype(BF16), tri)
    cnt_a = jnp.sum(oh_a, axis=1, keepdims=True)
    cnt_b = jnp.sum(oh_b, axis=1, keepdims=True)
    base = base_ref[...]
    rank_a = jnp.sum(oh_a * (base + pre_a), axis=0, keepdims=True)
    rank_b = jnp.sum(oh_b * (base + cnt_a + pre_b), axis=0, keepdims=True)
    rank_ref[...] = jnp.concatenate([rank_a, rank_b], axis=0).astype(jnp.int32)
    base = base + cnt_a + cnt_b
    base_ref[...] = base
    cnt_ref[...] = jnp.broadcast_to(base, cnt_ref.shape)


def _merge(hg, att, proj, x2, wa, wb, wo, g1, b1, wr, br, tm, row0):
    T = hg.shape[0]
    tile0 = row0 // tm
    row = lambda w, cb=0: pl.BlockSpec((tm, w), lambda i: (i, cb))
    const = lambda shape: pl.BlockSpec(shape, lambda i: (0,) * len(shape))
    lanes = pl.BlockSpec((2, tm), lambda i: (0, i))
    tri = jnp.asarray(np.triu(np.ones((tm, tm), np.float32), 1), BF16)
    return pl.pallas_call(
        _merge_kernel,
        grid=(T // tm,),
        in_specs=[row(HG_W), row(ATT_QW), row(D_MODEL, 0), row(D_MODEL, 1),
                  pl.BlockSpec((tm, D_MODEL), lambda i: (i + tile0, 0)),
                  const((HG_W, D_MODEL)), const((ATT_QW, D_MODEL)), const((D_MODEL, D_MODEL)),
                  const((1, D_MODEL)), const((1, D_MODEL)), const((40, D_MODEL)), const((40, 1)),
                  const((tm, tm))],
        out_specs=[row(D_MODEL), row(D_MODEL // 2), lanes, lanes, lanes, const((N_EXPERTS, 128))],
        out_shape=[jax.ShapeDtypeStruct((T, D_MODEL), F32),
                   jax.ShapeDtypeStruct((T, D_MODEL // 2), jnp.uint32),
                   jax.ShapeDtypeStruct((2, T), jnp.int32),
                   jax.ShapeDtypeStruct((2, T), F32),
                   jax.ShapeDtypeStruct((2, T), jnp.int32),
                   jax.ShapeDtypeStruct((N_EXPERTS, 128), F32)],
        scratch_shapes=[pltpu.VMEM((N_EXPERTS, 1), F32), pltpu.VMEM((tm, D_MODEL), BF16)],
        compiler_params=_cparams(("arbitrary",)),
        name="merge",
    )(hg, att, proj, proj, x2, wa, wb, wo, g1, b1, wr, br, tri)


SC_WINDOW = 32
SC_IDX_LANES = 128


def _pad_indices(idx):
    rows = idx.reshape(-1, SC_WINDOW)
    return jnp.pad(rows, ((0, 0), (0, SC_IDX_LANES - SC_WINDOW)))


def _sc_mesh():
    return plsc.VectorSubcoreMesh(core_axis_name="core", subcore_axis_name="subcore")


def _sc_scatter2(x, idx_a, idx_b, n_out):
    T, d = x.shape

    @pl.kernel(out_type=jax.ShapeDtypeStruct((n_out, d), x.dtype), mesh=_sc_mesh())
    def scatter(x_hbm, ia_hbm, ib_hbm, o_hbm):
        def body(x_vmem, ia_vmem, ib_vmem):
            pltpu.sync_copy(x_vmem, o_hbm.at[ia_vmem.at[0, pl.ds(0, SC_WINDOW)]])
            pltpu.sync_copy(x_vmem, o_hbm.at[ib_vmem.at[0, pl.ds(0, SC_WINDOW)]])

        idx_spec = pl.BlockSpec((1, SC_IDX_LANES), lambda i: (i, 0))
        pltpu.emit_pipeline(
            body, grid=(T // SC_WINDOW,),
            in_specs=[pl.BlockSpec((SC_WINDOW, d), lambda i: (i, 0)), idx_spec, idx_spec],
            out_specs=[],
            core_axis_name=("core", "subcore"),
            dimension_semantics=(pltpu.PARALLEL,),
        )(x_hbm, ia_hbm, ib_hbm)

    return scatter(x, _pad_indices(idx_a), _pad_indices(idx_b))


def _sc_gather(x, idx):
    n = idx.shape[0]
    d = x.shape[1]

    @pl.kernel(out_type=jax.ShapeDtypeStruct((n, d), x.dtype), mesh=_sc_mesh())
    def gather(x_hbm, i_hbm, o_hbm):
        def body(i_vmem, o_vmem):
            pltpu.sync_copy(x_hbm.at[i_vmem.at[0, pl.ds(0, SC_WINDOW)]], o_vmem)

        pltpu.emit_pipeline(
            body, grid=(n // SC_WINDOW,),
            in_specs=[pl.BlockSpec((1, SC_IDX_LANES), lambda i: (i, 0))],
            out_specs=[pl.BlockSpec((SC_WINDOW, d), lambda i: (i, 0))],
            core_axis_name=("core", "subcore"),
            dimension_semantics=(pltpu.PARALLEL,),
        )(i_hbm, o_hbm)

    return gather(x, _pad_indices(idx))


def _expert_kernel(be_ref, nv_ref, nxt_ref, par_ref, xb_ref, wg_hbm, wu_hbm, wd_hbm, y_ref,
                   wg_st, wu_st, wd_st, wg_bf, wu_bf, wd_bf, sem):
    i = pl.program_id(0)
    nv = nv_ref[i]

    def fetch(expert, slot):
        return [pltpu.make_async_copy(src.at[expert], dst.at[slot], sem.at[slot, j])
                for j, (src, dst) in enumerate(((wg_hbm, wg_st), (wu_hbm, wu_st), (wd_hbm, wd_st)))]

    @pl.when(i == 0)
    def _():
        for cp in fetch(be_ref[0], par_ref[0]):
            cp.start()

    @pl.when((i == 0) | (be_ref[i] != be_ref[jnp.maximum(i - 1, 0)]))
    def _():
        slot = par_ref[i]
        for cp in fetch(be_ref[i], slot):
            cp.wait()
        wg_bf[...] = wg_st[slot].astype(BF16)
        wu_bf[...] = wu_st[slot].astype(BF16)
        wd_bf[...] = wd_st[slot].astype(BF16)

        @pl.when(nxt_ref[i] >= 0)
        def _():
            for cp in fetch(nxt_ref[i], 1 - slot):
                cp.start()

    @pl.when(nv > 0)
    def _():
        row = lax.broadcasted_iota(jnp.int32, (MOE_BLOCK, 1), 0)
        xb = _unpack_bf16_pairs(jnp.where(row < nv, xb_ref[...], jnp.uint32(0))).astype(BF16)
        g = _dot(xb, wg_bf[...])
        u = _dot(xb, wu_bf[...])
        h = (_silu(g) * u).astype(BF16)
        y_ref[...] = _pack_bf16_pairs(_dot(h, wd_bf[...]).astype(BF16))


def _experts(block_meta, xbuf, w_gate, w_up, w_down):
    P = xbuf.shape[0]
    nb = P // MOE_BLOCK
    rows = pl.BlockSpec((MOE_BLOCK, D_MODEL // 2), lambda i, *_: (i, 0))
    hbm = pl.BlockSpec(memory_space=pl.ANY)
    return pl.pallas_call(
        _expert_kernel,
        grid_spec=pltpu.PrefetchScalarGridSpec(
            num_scalar_prefetch=4,
            grid=(nb,),
            in_specs=[rows, hbm, hbm, hbm],
            out_specs=rows,
            scratch_shapes=[pltpu.VMEM((2, D_MODEL, D_EXPERT), F32), pltpu.VMEM((2, D_MODEL, D_EXPERT), F32),
                            pltpu.VMEM((2, D_EXPERT, D_MODEL), F32),
                            pltpu.VMEM((D_MODEL, D_EXPERT), BF16), pltpu.VMEM((D_MODEL, D_EXPERT), BF16),
                            pltpu.VMEM((D_EXPERT, D_MODEL), BF16),
                            pltpu.SemaphoreType.DMA((2, 3))]),
        out_shape=jax.ShapeDtypeStruct((P, D_MODEL // 2), jnp.uint32),
        compiler_params=_cparams(("arbitrary",)),
        name="experts",
    )(*block_meta, xbuf, w_gate, w_up, w_down)


def _combine_kernel(ra_ref, rb_ref, x1_ref, wt_ref, g2_ref, b2_ref, *rest):
    o_ref = rest[-1]
    w2 = wt_ref[...]
    w = jnp.concatenate([w2, jnp.zeros((6, w2.shape[1]), F32)], axis=0).T
    y = w[:, 0:1] * _unpack_bf16_pairs(ra_ref[...]) + w[:, 1:2] * _unpack_bf16_pairs(rb_ref[...])
    o_ref[...] = _layer_norm(DN_ALPHA * x1_ref[...] + y, g2_ref[...], b2_ref[...])


def _combine(rows2, x1, wt, g2, b2, tm, out_prev, tile0, t_total):
    T = x1.shape[0]
    nt = T // tm
    const = lambda shape: pl.BlockSpec(shape, lambda i: (0,) * len(shape))
    in_specs = [pl.BlockSpec((tm, D_MODEL // 2), lambda i: (i, 0)),
                pl.BlockSpec((tm, D_MODEL // 2), lambda i: (i + nt, 0)),
                pl.BlockSpec((tm, D_MODEL), lambda i: (i, 0)), pl.BlockSpec((2, tm), lambda i: (0, i)),
                const((1, D_MODEL)), const((1, D_MODEL))]
    args = [rows2, rows2, x1, wt, g2, b2]
    aliases = {}
    if out_prev is not None:
        in_specs.append(pl.BlockSpec(memory_space=pl.ANY))
        args.append(out_prev)
        aliases = {len(args) - 1: 0}
    return pl.pallas_call(
        _combine_kernel,
        grid=(nt,),
        in_specs=in_specs,
        out_specs=pl.BlockSpec((tm, D_MODEL), lambda i: (i + tile0, 0)),
        out_shape=jax.ShapeDtypeStruct((t_total, D_MODEL), F32),
        input_output_aliases=aliases,
        compiler_params=_cparams(("arbitrary",)),
        name="combine",
    )(*args)


def _slot_layout(eid, rank, counts, T):
    P = 2 * T + N_EXPERTS * MOE_BLOCK
    nb = P // MOE_BLOCK
    padded = ((counts + MOE_BLOCK - 1) // MOE_BLOCK) * MOE_BLOCK
    pend = jnp.cumsum(padded)
    pstart = pend - padded
    experts = jnp.arange(N_EXPERTS, dtype=jnp.int32)
    pos = rank + jnp.sum(jnp.where(eid[:, :, None] == experts, pstart, 0), axis=-1)

    block_start = jnp.arange(nb, dtype=jnp.int32) * MOE_BLOCK
    block_expert = jnp.minimum(jnp.sum(block_start[:, None] >= pend[None, :], axis=1), N_EXPERTS - 1)
    onehot = block_expert[:, None] == experts[None, :]
    look = lambda table: jnp.sum(jnp.where(onehot, table[None, :], 0), axis=1)
    block_valid = jnp.clip(look(pstart + counts) - block_start, 0, MOE_BLOCK)
    block_valid = jnp.where(block_start < pend[-1], block_valid, 0)
    present = (counts > 0) | (experts == N_EXPERTS - 1)
    later = present[None, :] & (experts[None, :] > experts[:, None])
    next_present = jnp.min(jnp.where(later, experts[None, :], N_EXPERTS), axis=1)
    next_present = jnp.where(next_present < N_EXPERTS, next_present, -1)
    runs_before = jnp.sum(present[None, :] & (experts[None, :] < experts[:, None]), axis=1)
    block_next = look(next_present)
    block_slot = look(runs_before % 2)
    i32 = lambda a: a.astype(jnp.int32)
    return i32(pos), (i32(block_expert), i32(block_valid), i32(block_next), i32(block_slot))


def kernel(x, lb_logits, w_in, hg_norm_w, sinks, w_branch_a, w_branch_b, w_out, ln1_g, ln1_b,
           router_group_w, router_group_b, router_expert_w, router_expert_b,
           w_exp_gate, w_exp_up, w_exp_down, ln2_g, ln2_b):
    B, S, D = x.shape
    assert D == D_MODEL and S % ATT_BLK == 0 and w_in.shape[0] == DEPTH == 1
    lb_all = jnp.cumsum(jax.nn.softmax(lb_logits.astype(F32), axis=0), axis=0)
    lb = lb_all[0].reshape(1, HG_W)
    w_rot = jnp.concatenate([w_in[0][:, PROJ_ROT:], w_in[0][:, :PROJ_ROT]], axis=1).astype(BF16)
    nw = hg_norm_w[0].reshape(1, HG_DV).astype(F32)
    wa, wb, wo = w_branch_a[0].astype(BF16), w_branch_b[0].astype(BF16), w_out[0].astype(BF16)
    wr = jnp.zeros((40, D), F32).at[0:N_GROUPS].set(router_group_w[0].T).at[8:40].set(router_expert_w[0].T)
    br = jnp.zeros((40, 1), F32).at[0:N_GROUPS, 0].set(router_group_b[0]).at[8:40, 0].set(router_expert_b[0])
    g1, b1 = ln1_g[0].reshape(1, D), ln1_b[0].reshape(1, D)
    g2, b2 = ln2_g[0].reshape(1, D), ln2_b[0].reshape(1, D)

    n_parts = N_PARTS if B % N_PARTS == 0 else 1
    bp = B // n_parts
    tp = bp * S
    tm = 256 if tp % 256 == 0 else ATT_BLK
    ts = 512 if S % 512 == 0 else ATT_BLK
    x2 = x.reshape(B * S, D)
    out = None
    for part in range(n_parts):
        proj = _proj(x2, w_rot, tm, part * tp, tp)
        hg, att = _mixers(proj, lb, nw, sinks[0].astype(F32), bp, S)
        x1, xp, eid, wt, rank, cnt = _merge(hg, att, proj, x2, wa, wb, wo, g1, b1, wr.astype(BF16), br,
                                            512 if tp % 512 == 0 else tm, part * tp)
        pos, block_meta = _slot_layout(eid, rank, cnt[:, 0].astype(jnp.int32), tp)
        xbuf = _sc_scatter2(xp, pos[0], pos[1], 2 * tp + N_EXPERTS * MOE_BLOCK)
        ybuf = _experts(block_meta, xbuf, w_exp_gate[0], w_exp_up[0], w_exp_down[0])
        rows2 = _sc_gather(ybuf, pos.reshape(-1))
        out = _combine(rows2, x1, wt, g2, b2, tm, out, part * (tp // tm), B * S)
    return out.reshape(B, S, D)
```

```python
import functools

import numpy as np
import jax
import jax.numpy as jnp
from jax import lax
from jax.experimental import pallas as pl
from jax.experimental.pallas import tpu as pltpu
from jax.experimental.pallas import tpu_sc as plsc

F32 = jnp.float32
BF16 = jnp.bfloat16

D_MODEL = 1024
DEPTH = 1
HG_HEADS = 4
HG_DK = 128
HG_DV = 128
HG_W = HG_HEADS * HG_DK
CHUNK = 64
ATT_Q_HEADS = 8
ATT_KV_HEADS = 2
ATT_GROUP = ATT_Q_HEADS // ATT_KV_HEADS
ATT_HD = 64
ATT_QW = ATT_Q_HEADS * ATT_HD
ATT_KVW = ATT_KV_HEADS * ATT_HD
ATT_BLK = 128
N_GROUPS = 4
EPG = 8
N_EXPERTS = N_GROUPS * EPG
D_EXPERT = 512
MOE_BLOCK = 256
DN_ALPHA = (2.0 * DEPTH) ** 0.25
LN_EPS = 1e-5
RMS_EPS = 1e-6
NEG_INF = -1e30

PROJ_W = 4 * HG_W + ATT_QW + 2 * ATT_KVW + 2 * D_MODEL
PROJ_ROT = 4 * HG_W + ATT_QW + 2 * ATT_KVW
N_LEVELS = 6
N_ARG_GROUPS = N_LEVELS + 2

VMEM_LIMIT = 56 * 1024 * 1024
MERGE_SUB = 256
MIX_TILE = 512
N_PARTS = 1


def _cparams(sem):
    return pltpu.CompilerParams(dimension_semantics=sem, vmem_limit_bytes=VMEM_LIMIT)


def _dot(a, b):
    return jnp.dot(a, b, preferred_element_type=F32)


def _dot_nt(a, b):
    return lax.dot_general(a, b, (((1,), (1,)), ((), ())), preferred_element_type=F32)


def _dot_tn(a, b):
    return lax.dot_general(a, b, (((0,), (0,)), ((), ())), preferred_element_type=F32)


def _sigmoid(x):
    return 0.5 * jnp.tanh(0.5 * x) + 0.5


def _silu(x):
    return x * _sigmoid(x)


def _pack_bf16_pairs(xb):
    n = xb.shape[1] // 2
    lo = lax.bitcast_convert_type(xb[:, :n].astype(F32), jnp.uint32)
    hi = lax.bitcast_convert_type(xb[:, n:].astype(F32), jnp.uint32)
    return (lo >> 16) | hi


def _unpack_bf16_pairs(w):
    lo = lax.bitcast_convert_type(w << 16, F32)
    hi = lax.bitcast_convert_type(w & jnp.uint32(0xFFFF0000), F32)
    return jnp.concatenate([lo, hi], axis=1)


def _layer_norm(z, g, b):
    mu = jnp.mean(z, axis=-1, keepdims=True)
    zc = z - mu
    var = jnp.mean(zc * zc, axis=-1, keepdims=True)
    return zc * lax.rsqrt(var + LN_EPS) * g + b


def _proj_kernel(x_ref, w_ref, o_ref):
    o_ref[...] = _dot(x_ref[...].astype(BF16), w_ref[...])


def _proj(x2, w_bf, tm, row0, T):
    tile0 = row0 // tm
    return pl.pallas_call(
        _proj_kernel,
        grid=(T // tm,),
        in_specs=[pl.BlockSpec((tm, D_MODEL), lambda i: (i + tile0, 0)),
                  pl.BlockSpec((D_MODEL, PROJ_W), lambda i: (0, 0))],
        out_specs=pl.BlockSpec((tm, PROJ_W), lambda i: (i, 0)),
        out_shape=jax.ShapeDtypeStruct((T, PROJ_W), F32),
        compiler_params=_cparams(("arbitrary",)),
        name="proj",
    )(x2, w_bf)


def _hgrn_tables():
    C = CHUNK
    w = np.zeros((N_ARG_GROUPS, C, C), np.float32)
    masks = np.zeros((N_LEVELS, C, C), np.float32)
    for lvl in range(N_LEVELS):
        h = 1 << lvl
        for t in range(C):
            base = (t // (2 * h)) * 2 * h
            m = base + h - 1
            if (t // h) % 2 == 1:
                w[lvl, t, m + 1:t + 1] = 1.0
                masks[lvl, t, base:base + h] = 1.0
            else:
                w[lvl, t, t + 1:m + 1] = 1.0
    for t in range(C):
        w[N_LEVELS, t, :t + 1] = 1.0
        w[N_LEVELS + 1, t, t + 1:] = 1.0
    w = w.reshape(N_ARG_GROUPS * C, C)
    return np.concatenate([w, w], axis=1), masks


def _hgrn_chunk(q, fz, v, gate, c0, c1, nw, seg, mask_ref, state_ref):
    h = 0.5 * q
    qf = h + h * jnp.tanh(h)
    t1 = c1 * jnp.tanh(0.5 * fz)
    f = c0 + t1
    k = c1 - t1
    l2 = jnp.log2(f)
    v_bf = v.astype(BF16)
    qf_bf = qf.astype(BF16)
    k_bf = k.astype(BF16)

    l_hi = l2.astype(BF16)
    l_lo = (l2 - l_hi.astype(F32)).astype(BF16)
    args = _dot(seg, jnp.concatenate([l_hi, l_lo], axis=0))
    e = jnp.exp2(args)

    row = lax.broadcasted_iota(jnp.int32, (CHUNK, 1), 0)
    scores = [jnp.zeros((CHUNK, CHUNK), F32) for _ in range(HG_HEADS)]
    for lvl in range(N_LEVELS):
        half = 1 << lvl
        e_l = e[lvl * CHUNK:(lvl + 1) * CHUNK, :].astype(BF16)
        if half >= 16:
            sel = jnp.concatenate([(qf_bf if (r0 // half) % 2 else k_bf)[r0:r0 + half] for r0 in range(0, CHUNK, half)],
                                  axis=0)
        else:
            sel = jnp.where((row // half) % 2 == 1, qf_bf, k_bf)
        a = sel * e_l
        m = mask_ref[lvl]
        for hd in range(HG_HEADS):
            a_h = a[:, hd * HG_DK:(hd + 1) * HG_DK]
            scores[hd] = scores[hd] + m * _dot_nt(a_h, a_h)

    e_cum = e[N_LEVELS * CHUNK:(N_LEVELS + 1) * CHUNK, :]
    e_suf = e[(N_LEVELS + 1) * CHUNK:(N_LEVELS + 2) * CHUNK, :]
    q_in = qf_bf * e_cum.astype(BF16)
    k_out = k_bf * e_suf.astype(BF16)
    e_last = e_cum[CHUNK - 1:CHUNK, :]
    qk = qf * k

    outs = []
    for hd in range(HG_HEADS):
        cols = slice(hd * HG_DK, (hd + 1) * HG_DK)
        st = state_ref[hd]
        diag = jnp.sum(qk[:, cols], axis=-1, keepdims=True)
        o = (_dot_nt(q_in[:, cols], st.astype(BF16))
             + _dot(scores[hd].astype(BF16), v_bf[:, cols])
             + diag * v[:, cols])
        state_ref[hd] = st * e_last[:, cols] + _dot_tn(v_bf[:, cols], k_out[:, cols])
        o = o * lax.rsqrt(jnp.mean(o * o, axis=-1, keepdims=True) + RMS_EPS) * nw
        outs.append(o)
    hg = 0.5 * gate
    return jnp.concatenate(outs, axis=1) * (hg + hg * jnp.tanh(hg))


def _attn_bias():
    r = np.arange(ATT_BLK)[:, None]
    c = np.arange(2 * ATT_BLK)[None, :]
    dist = r + ATT_BLK - c
    window = (dist >= 0) & (dist < ATT_BLK)
    slopes = np.exp2(-8.0 * (np.arange(ATT_Q_HEADS, dtype=np.float32) + 1.0) / ATT_Q_HEADS).astype(np.float32)
    alibi = -slopes[:, None, None] * dist.astype(np.float32)[None]
    later = np.where(window[None], alibi, np.float32(NEG_INF))
    first = np.where((window & (c >= ATT_BLK))[None], alibi, np.float32(NEG_INF))
    return np.stack([later, first]).astype(np.float32)


def _attn_block(q_ref, kv_cur, kv_prev, bias_ref, table, sink_ref, o_ref):
    lane = lax.broadcasted_iota(jnp.int32, (2 * ATT_BLK, 2 * ATT_KVW), 1)
    lo = (lane % ATT_KVW) < ATT_HD

    kv = jnp.concatenate([kv_prev, kv_cur], axis=0).astype(BF16)
    kv_sw = jnp.concatenate([kv[:, ATT_HD:ATT_KVW], kv[:, :ATT_HD],
                             kv[:, ATT_KVW + ATT_HD:], kv[:, ATT_KVW:ATT_KVW + ATT_HD]], axis=1)
    zero = jnp.zeros_like(kv)
    placed = {}
    for h in range(ATT_KV_HEADS):
        for off in range(2):
            src = kv if h == off else kv_sw
            placed[h, off] = jnp.where(lo if off == 0 else jnp.logical_not(lo), src, zero)

    scale = ATT_HD ** -0.5
    for pair in range(ATT_Q_HEADS // 2):
        qp = (q_ref[:, pair * 2 * ATT_HD:(pair + 1) * 2 * ATT_HD] * scale).astype(BF16)
        acc = jnp.zeros((ATT_BLK, 2 * ATT_HD), F32)
        for off in range(2):
            j = 2 * pair + off
            kvh = placed[j // ATT_GROUP, off]
            sink = sink_ref[j]
            logits = _dot_nt(qp, kvh[:, :ATT_KVW]) + bias_ref[table, j]
            m = jnp.maximum(jnp.max(logits, axis=-1, keepdims=True), sink)
            p = jnp.exp(logits - m)
            den = jnp.sum(p, axis=-1, keepdims=True) + jnp.exp(sink - m)
            acc = acc + _dot(p.astype(BF16), kvh[:, ATT_KVW:]) / den
        o_ref[:, pair * 2 * ATT_HD:(pair + 1) * 2 * ATT_HD] = acc.astype(o_ref.dtype)


def _mixers_kernel(sink_ref, q_ref, f_ref, i_ref, g_ref, aq_ref, kv_ref, kvp_ref, lb_ref, nw_ref, seg_ref,
                   mask_ref, bias_ref, hg_ref, at_ref, state_ref):
    first = pl.program_id(1) == 0

    @pl.when(first)
    def _():
        state_ref[...] = jnp.zeros_like(state_ref)

    lb = lb_ref[...]
    c0 = 0.5 + 0.5 * lb
    c1 = 0.5 - 0.5 * lb
    nw = nw_ref[...]
    seg = seg_ref[...]
    tile = q_ref.shape[0]
    for r0 in range(0, tile, CHUNK):
        rows = slice(r0, r0 + CHUNK)
        o = _hgrn_chunk(q_ref[rows, :], f_ref[rows, :], i_ref[rows, :], g_ref[rows, :], c0, c1, nw, seg, mask_ref,
                        state_ref)
        hg_ref[rows, :] = o.astype(hg_ref.dtype)
    for r0 in range(0, tile, ATT_BLK):
        rows = slice(r0, r0 + ATT_BLK)
        if r0 == 0:
            prev, table = kvp_ref[...], jnp.where(first, 1, 0)
        else:
            prev, table = kv_ref[r0 - ATT_BLK:r0, :], 0
        _attn_block(aq_ref.at[rows, :], kv_ref[rows, :], prev, bias_ref, table, sink_ref, at_ref.at[rows, :])


def _mixers(proj, lb, norm_w, sinks, B, S):
    T = B * S
    tile = MIX_TILE if S % MIX_TILE == 0 else ATT_BLK
    nb = S // tile
    per = tile // ATT_BLK
    seg, masks = _hgrn_tables()
    const = lambda shape: pl.BlockSpec(shape, lambda b, n: (0,) * len(shape))
    hcol = lambda cb: pl.BlockSpec((tile, HG_W), lambda b, n: (b * nb + n, cb))
    hbase = (2 * D_MODEL) // HG_W
    qcol = (2 * D_MODEL + 4 * HG_W) // ATT_QW
    kvcol = (2 * D_MODEL + 4 * HG_W + ATT_QW) // (2 * ATT_KVW)
    out = pl.BlockSpec((tile, HG_W), lambda b, n: (b * nb + n, 0))
    return pl.pallas_call(
        _mixers_kernel,
        grid=(B, nb),
        in_specs=[pl.BlockSpec(memory_space=pltpu.SMEM),
                  hcol(hbase), hcol(hbase + 1), hcol(hbase + 2), hcol(hbase + 3),
                  pl.BlockSpec((tile, ATT_QW), lambda b, n: (b * nb + n, qcol)),
                  pl.BlockSpec((tile, 2 * ATT_KVW), lambda b, n: (b * nb + n, kvcol)),
                  pl.BlockSpec((ATT_BLK, 2 * ATT_KVW),
                               lambda b, n: ((b * nb + n) * per - jnp.where(n > 0, 1, 0), kvcol)),
                  const((1, HG_W)), const((1, HG_DV)), const((N_ARG_GROUPS * CHUNK, 2 * CHUNK)),
                  const((N_LEVELS, CHUNK, CHUNK)), const((2, ATT_Q_HEADS, ATT_BLK, 2 * ATT_BLK))],
        out_specs=[out, out],
        out_shape=[jax.ShapeDtypeStruct((T, HG_W), BF16), jax.ShapeDtypeStruct((T, ATT_QW), BF16)],
        scratch_shapes=[pltpu.VMEM((HG_HEADS, HG_DV, HG_DK), F32)],
        compiler_params=_cparams(("arbitrary", "arbitrary")),
        name="mixers",
    )(sinks, proj, proj, proj, proj, proj, proj, proj, lb, norm_w, jnp.asarray(seg, BF16), jnp.asarray(masks),
      jnp.asarray(_attn_bias()))


def _merge_kernel(hg_ref, at_ref, ga_ref, gb_ref, x_ref, wa_ref, wb_ref, wo_ref, g1_ref, b1_ref,
                  wr_ref, br_ref, tri_ref, x1_ref, xp_ref, eid_ref, wt_ref, rank_ref, cnt_ref, base_ref, x1b_ref):
    tm = x_ref.shape[0]
    sub = min(tm, MERGE_SUB)
    for r0 in range(0, tm, sub):
        rows = slice(r0, r0 + sub)
        ya = _dot(hg_ref[rows, :], wa_ref[...])
        yb = _dot(at_ref[rows, :], wb_ref[...])
        merged = _sigmoid(ga_ref[rows, :]) * ya + _sigmoid(gb_ref[rows, :]) * yb
        z = DN_ALPHA * x_ref[rows, :] + _dot(merged.astype(BF16), wo_ref[...])
        x1 = _layer_norm(z, g1_ref[...], b1_ref[...])
        x1_ref[rows, :] = x1
        x1b = x1.astype(BF16)
        x1b_ref[rows, :] = x1b
        xp_ref[rows, :] = _pack_bf16_pairs(x1b)

    lg = _dot_nt(wr_ref[...], x1b_ref[...]) + br_ref[...]
    g = lg[0:8, :]
    row8 = lax.broadcasted_iota(jnp.int32, (8, tm), 0)
    g = jnp.where(row8 < N_GROUPS, g, -jnp.inf)
    gmax = jnp.max(g, axis=0, keepdims=True)
    gsel = jnp.min(jnp.where(g == gmax, row8, 8), axis=0, keepdims=True)
    gw = 1.0 / jnp.sum(jnp.exp(g - gmax), axis=0, keepdims=True)
    el = jnp.where(gsel == 0, lg[8:16, :],
                   jnp.where(gsel == 1, lg[16:24, :], jnp.where(gsel == 2, lg[24:32, :], lg[32:40, :])))
    v1 = jnp.max(el, axis=0, keepdims=True)
    i1 = jnp.min(jnp.where(el == v1, row8, 8), axis=0, keepdims=True)
    el2 = jnp.where(row8 == i1, -jnp.inf, el)
    v2 = jnp.max(el2, axis=0, keepdims=True)
    i2 = jnp.min(jnp.where(el2 == v2, row8, 8), axis=0, keepdims=True)
    e2 = jnp.exp(v2 - v1)
    den = 1.0 + e2
    e_a = gsel * EPG + i1
    e_b = gsel * EPG + i2
    eid_ref[...] = jnp.concatenate([e_a, e_b], axis=0)
    wt_ref[...] = jnp.concatenate([gw / den, gw * e2 / den], axis=0)

    @pl.when(pl.program_id(0) == 0)
    def _():
        base_ref[...] = jnp.zeros_like(base_ref)

    row_e = lax.broadcasted_iota(jnp.int32, (N_EXPERTS, tm), 0)
    oh_a = jnp.where(row_e == e_a, 1.0, 0.0)
    oh_b = jnp.where(row_e == e_b, 1.0, 0.0)
    tri = tri_ref[...]
    pre_a = _dot(oh_a.astype(BF16), tri)
    pre_b = _dot(oh_b.astype(BF16), tri)
    cnt_a = jnp.sum(oh_a, axis=1, keepdims=True)
    cnt_b = jnp.sum(oh_b, axis=1, keepdims=True)
    base = base_ref[...]
    rank_a = jnp.sum(oh_a * (base + pre_a), axis=0, keepdims=True)
    rank_b = jnp.sum(oh_b * (base + cnt_a + pre_b), axis=0, keepdims=True)
    rank_ref[...] = jnp.concatenate([rank_a, rank_b], axis=0).astype(jnp.int32)
    base = base + cnt_a + cnt_b
    base_ref[...] = base
    cnt_ref[...] = jnp.broadcast_to(base, cnt_ref.shape)


def _merge(hg, att, proj, x2, wa, wb, wo, g1, b1, wr, br, tm, row0):
    T = hg.shape[0]
    tile0 = row0 // tm
    row = lambda w, cb=0: pl.BlockSpec((tm, w), lambda i: (i, cb))
    const = lambda shape: pl.BlockSpec(shape, lambda i: (0,) * len(shape))
    lanes = pl.BlockSpec((2, tm), lambda i: (0, i))
    tri = jnp.asarray(np.triu(np.ones((tm, tm), np.float32), 1), BF16)
    return pl.pallas_call(
        _merge_kernel,
        grid=(T // tm,),
        in_specs=[row(HG_W), row(ATT_QW), row(D_MODEL, 0), row(D_MODEL, 1),
                  pl.BlockSpec((tm, D_MODEL), lambda i: (i + tile0, 0)),
                  const((HG_W, D_MODEL)), const((ATT_QW, D_MODEL)), const((D_MODEL, D_MODEL)),
                  const((1, D_MODEL)), const((1, D_MODEL)), const((40, D_MODEL)), const((40, 1)),
                  const((tm, tm))],
        out_specs=[row(D_MODEL), row(D_MODEL // 2), lanes, lanes, lanes, const((N_EXPERTS, 128))],
        out_shape=[jax.ShapeDtypeStruct((T, D_MODEL), F32),
                   jax.ShapeDtypeStruct((T, D_MODEL // 2), jnp.uint32),
                   jax.ShapeDtypeStruct((2, T), jnp.int32),
                   jax.ShapeDtypeStruct((2, T), F32),
                   jax.ShapeDtypeStruct((2, T), jnp.int32),
                   jax.ShapeDtypeStruct((N_EXPERTS, 128), F32)],
        scratch_shapes=[pltpu.VMEM((N_EXPERTS, 1), F32), pltpu.VMEM((tm, D_MODEL), BF16)],
        compiler_params=_cparams(("arbitrary",)),
        name="merge",
    )(hg, att, proj, proj, x2, wa, wb, wo, g1, b1, wr, br, tri)


SC_WINDOW = 32
SC_IDX_LANES = 128


def _pad_indices(idx):
    rows = idx.reshape(-1, SC_WINDOW)
    return jnp.pad(rows, ((0, 0), (0, SC_IDX_LANES - SC_WINDOW)))


def _sc_mesh():
    return plsc.VectorSubcoreMesh(core_axis_name="core", subcore_axis_name="subcore")


def _sc_scatter2(x, idx_a, idx_b, n_out):
    T, d = x.shape

    @pl.kernel(out_type=jax.ShapeDtypeStruct((n_out, d), x.dtype), mesh=_sc_mesh())
    def scatter(x_hbm, ia_hbm, ib_hbm, o_hbm):
        def body(x_vmem, ia_vmem, ib_vmem):
            pltpu.sync_copy(x_vmem, o_hbm.at[ia_vmem.at[0, pl.ds(0, SC_WINDOW)]])
            pltpu.sync_copy(x_vmem, o_hbm.at[ib_vmem.at[0, pl.ds(0, SC_WINDOW)]])

        idx_spec = pl.BlockSpec((1, SC_IDX_LANES), lambda i: (i, 0))
        pltpu.emit_pipeline(
            body, grid=(T // SC_WINDOW,),
            in_specs=[pl.BlockSpec((SC_WINDOW, d), lambda i: (i, 0)), idx_spec, idx_spec],
            out_specs=[],
            core_axis_name=("core", "subcore"),
            dimension_semantics=(pltpu.PARALLEL,),
        )(x_hbm, ia_hbm, ib_hbm)

    return scatter(x, _pad_indices(idx_a), _pad_indices(idx_b))


def _sc_gather(x, idx):
    n = idx.shape[0]
    d = x.shape[1]

    @pl.kernel(out_type=jax.ShapeDtypeStruct((n, d), x.dtype), mesh=_sc_mesh())
    def gather(x_hbm, i_hbm, o_hbm):
        def body(i_vmem, o_vmem):
            pltpu.sync_copy(x_hbm.at[i_vmem.at[0, pl.ds(0, SC_WINDOW)]], o_vmem)

        pltpu.emit_pipeline(
            body, grid=(n // SC_WINDOW,),
            in_specs=[pl.BlockSpec((1, SC_IDX_LANES), lambda i: (i, 0))],
            out_specs=[pl.BlockSpec((SC_WINDOW, d), lambda i: (i, 0))],
            core_axis_name=("core", "subcore"),
            dimension_semantics=(pltpu.PARALLEL,),
        )(i_hbm, o_hbm)

    return gather(x, _pad_indices(idx))


def _expert_kernel(be_ref, nv_ref, nxt_ref, par_ref, xb_ref, wg_hbm, wu_hbm, wd_hbm, y_ref,
                   wg_st, wu_st, wd_st, wg_bf, wu_bf, wd_bf, sem):
    i = pl.program_id(0)
    nv = nv_ref[i]

    def fetch(expert, slot):
        return [pltpu.make_async_copy(src.at[expert], dst.at[slot], sem.at[slot, j])
                for j, (src, dst) in enumerate(((wg_hbm, wg_st), (wu_hbm, wu_st), (wd_hbm, wd_st)))]

    @pl.when(i == 0)
    def _():
        for cp in fetch(be_ref[0], par_ref[0]):
            cp.start()

    @pl.when((i == 0) | (be_ref[i] != be_ref[jnp.maximum(i - 1, 0)]))
    def _():
        slot = par_ref[i]
        for cp in fetch(be_ref[i], slot):
            cp.wait()
        wg_bf[...] = wg_st[slot].astype(BF16)
        wu_bf[...] = wu_st[slot].astype(BF16)
        wd_bf[...] = wd_st[slot].astype(BF16)

        @pl.when(nxt_ref[i] >= 0)
        def _():
            for cp in fetch(nxt_ref[i], 1 - slot):
                cp.start()

    @pl.when(nv > 0)
    def _():
        row = lax.broadcasted_iota(jnp.int32, (MOE_BLOCK, 1), 0)
        xb = _unpack_bf16_pairs(jnp.where(row < nv, xb_ref[...], jnp.uint32(0))).astype(BF16)
        g = _dot(xb, wg_bf[...])
        u = _dot(xb, wu_bf[...])
        h = (_silu(g) * u).astype(BF16)
        y_ref[...] = _pack_bf16_pairs(_dot(h, wd_bf[...]).astype(BF16))


def _experts(block_meta, xbuf, w_gate, w_up, w_down):
    P = xbuf.shape[0]
    nb = P // MOE_BLOCK
    rows = pl.BlockSpec((MOE_BLOCK, D_MODEL // 2), lambda i, *_: (i, 0))
    hbm = pl.BlockSpec(memory_space=pl.ANY)
    return pl.pallas_call(
        _expert_kernel,
        grid_spec=pltpu.PrefetchScalarGridSpec(
            num_scalar_prefetch=4,
            grid=(nb,),
            in_specs=[rows, hbm, hbm, hbm],
            out_specs=rows,
            scratch_shapes=[pltpu.VMEM((2, D_MODEL, D_EXPERT), F32), pltpu.VMEM((2, D_MODEL, D_EXPERT), F32),
                            pltpu.VMEM((2, D_EXPERT, D_MODEL), F32),
                            pltpu.VMEM((D_MODEL, D_EXPERT), BF16), pltpu.VMEM((D_MODEL, D_EXPERT), BF16),
                            pltpu.VMEM((D_EXPERT, D_MODEL), BF16),
                            pltpu.SemaphoreType.DMA((2, 3))]),
        out_shape=jax.ShapeDtypeStruct((P, D_MODEL // 2), jnp.uint32),
        compiler_params=_cparams(("arbitrary",)),
        name="experts",
    )(*block_meta, xbuf, w_gate, w_up, w_down)


def _combine_kernel(ra_ref, rb_ref, x1_ref, wt_ref, g2_ref, b2_ref, *rest):
    o_ref = rest[-1]
    w2 = wt_ref[...]
    w = jnp.concatenate([w2, jnp.zeros((6, w2.shape[1]), F32)], axis=0).T
    y = w[:, 0:1] * _unpack_bf16_pairs(ra_ref[...]) + w[:, 1:2] * _unpack_bf16_pairs(rb_ref[...])
    o_ref[...] = _layer_norm(DN_ALPHA * x1_ref[...] + y, g2_ref[...], b2_ref[...])


def _combine(rows2, x1, wt, g2, b2, tm, out_prev, tile0, t_total):
    T = x1.shape[0]
    nt = T // tm
    const = lambda shape: pl.BlockSpec(shape, lambda i: (0,) * len(shape))
    in_specs = [pl.BlockSpec((tm, D_MODEL // 2), lambda i: (i, 0)),
                pl.BlockSpec((tm, D_MODEL // 2), lambda i: (i + nt, 0)),
                pl.BlockSpec((tm, D_MODEL), lambda i: (i, 0)), pl.BlockSpec((2, tm), lambda i: (0, i)),
                const((1, D_MODEL)), const((1, D_MODEL))]
    args = [rows2, rows2, x1, wt, g2, b2]
    aliases = {}
    if out_prev is not None:
        in_specs.append(pl.BlockSpec(memory_space=pl.ANY))
        args.append(out_prev)
        aliases = {len(args) - 1: 0}
    return pl.pallas_call(
        _combine_kernel,
        grid=(nt,),
        in_specs=in_specs,
        out_specs=pl.BlockSpec((tm, D_MODEL), lambda i: (i + tile0, 0)),
        out_shape=jax.ShapeDtypeStruct((t_total, D_MODEL), F32),
        input_output_aliases=aliases,
        compiler_params=_cparams(("arbitrary",)),
        name="combine",
    )(*args)


def _slot_layout(eid, rank, counts, T):
    P = 2 * T + N_EXPERTS * MOE_BLOCK
    nb = P // MOE_BLOCK
    padded = ((counts + MOE_BLOCK - 1) // MOE_BLOCK) * MOE_BLOCK
    pend = jnp.cumsum(padded)
    pstart = pend - padded
    experts = jnp.arange(N_EXPERTS, dtype=jnp.int32)
    pos = rank + jnp.sum(jnp.where(eid[:, :, None] == experts, pstart, 0), axis=-1)

    block_start = jnp.arange(nb, dtype=jnp.int32) * MOE_BLOCK
    block_expert = jnp.minimum(jnp.sum(block_start[:, None] >= pend[None, :], axis=1), N_EXPERTS - 1)
    onehot = block_expert[:, None] == experts[None, :]
    look = lambda table: jnp.sum(jnp.where(onehot, table[None, :], 0), axis=1)
    block_valid = jnp.clip(look(pstart + counts) - block_start, 0, MOE_BLOCK)
    block_valid = jnp.where(block_start < pend[-1], block_valid, 0)
    present = (counts > 0) | (experts == N_EXPERTS - 1)
    later = present[None, :] & (experts[None, :] > experts[:, None])
    next_present = jnp.min(jnp.where(later, experts[None, :], N_EXPERTS), axis=1)
    next_present = jnp.where(next_present < N_EXPERTS, next_present, -1)
    runs_before = jnp.sum(present[None, :] & (experts[None, :] < experts[:, None]), axis=1)
    block_next = look(next_present)
    block_slot = look(runs_before % 2)
    i32 = lambda a: a.astype(jnp.int32)
    return i32(pos), (i32(block_expert), i32(block_valid), i32(block_next), i32(block_slot))


def kernel(x, lb_logits, w_in, hg_norm_w, sinks, w_branch_a, w_branch_b, w_out, ln1_g, ln1_b,
           router_group_w, router_group_b, router_expert_w, router_expert_b,
           w_exp_gate, w_exp_up, w_exp_down, ln2_g, ln2_b):
    B, S, D = x.shape
    assert D == D_MODEL and S % ATT_BLK == 0 and w_in.shape[0] == DEPTH == 1
    lb_all = jnp.cumsum(jax.nn.softmax(lb_logits.astype(F32), axis=0), axis=0)
    lb = lb_all[0].reshape(1, HG_W)
    w_rot = jnp.concatenate([w_in[0][:, PROJ_ROT:], w_in[0][:, :PROJ_ROT]], axis=1).astype(BF16)
    nw = hg_norm_w[0].reshape(1, HG_DV).astype(F32)
    wa, wb, wo = w_branch_a[0].astype(BF16), w_branch_b[0].astype(BF16), w_out[0].astype(BF16)
    wr = jnp.zeros((40, D), F32).at[0:N_GROUPS].set(router_group_w[0].T).at[8:40].set(router_expert_w[0].T)
    br = jnp.zeros((40, 1), F32).at[0:N_GROUPS, 0].set(router_group_b[0]).at[8:40, 0].set(router_expert_b[0])
    g1, b1 = ln1_g[0].reshape(1, D), ln1_b[0].reshape(1, D)
    g2, b2 = ln2_g[0].reshape(1, D), ln2_b[0].reshape(1, D)

    n_parts = N_PARTS if B % N_PARTS == 0 else 1
    bp = B // n_parts
    tp = bp * S
    tm = 256 if tp % 256 == 0 else ATT_BLK
    ts = 512 if S % 512 == 0 else ATT_BLK
    x2 = x.reshape(B * S, D)
    out = None
    for part in range(n_parts):
        proj = _proj(x2, w_rot, tm, part * tp, tp)
        hg, att = _mixers(proj, lb, nw, sinks[0].astype(F32), bp, S)
        x1, xp, eid, wt, rank, cnt = _merge(hg, att, proj, x2, wa, wb, wo, g1, b1, wr.astype(BF16), br,
                                            512 if tp % 512 == 0 else tm, part * tp)
        pos, block_meta = _slot_layout(eid, rank, cnt[:, 0].astype(jnp.int32), tp)
        xbuf = _sc_scatter2(xp, pos[0], pos[1], 2 * tp + N_EXPERTS * MOE_BLOCK)
        ybuf = _experts(block_meta, xbuf, w_exp_gate[0], w_exp_up[0], w_exp_down[0])
        rows2 = _sc_gather(ybuf, pos.reshape(-1))
        out = _combine(rows2, x1, wt, g2, b2, tm, out, part * (tp // tm), B * S)
    return out.reshape(B, S, D)
```

```python
import functools

import numpy as np
import jax
import jax.numpy as jnp
from jax import lax
from jax.experimental import pallas as pl
from jax.experimental.pallas import tpu as pltpu
from jax.experimental.pallas import tpu_sc as plsc

F32 = jnp.float32
BF16 = jnp.bfloat16

D_MODEL = 1024
DEPTH = 1
HG_HEADS = 4
HG_DK = 128
HG_DV = 128
HG_W = HG_HEADS * HG_DK
CHUNK = 64
ATT_Q_HEADS = 8
ATT_KV_HEADS = 2
ATT_GROUP = ATT_Q_HEADS // ATT_KV_HEADS
ATT_HD = 64
ATT_QW = ATT_Q_HEADS * ATT_HD
ATT_KVW = ATT_KV_HEADS * ATT_HD
ATT_BLK = 128
N_GROUPS = 4
EPG = 8
N_EXPERTS = N_GROUPS * EPG
D_EXPERT = 512
MOE_BLOCK = 256
DN_ALPHA = (2.0 * DEPTH) ** 0.25
LN_EPS = 1e-5
RMS_EPS = 1e-6
NEG_INF = -1e30

PROJ_W = 4 * HG_W + ATT_QW + 2 * ATT_KVW
N_LEVELS = 6
N_ARG_GROUPS = N_LEVELS + 2

VMEM_LIMIT = 56 * 1024 * 1024
MERGE_SUB = 256
MIX_TILE = 512
N_PARTS = 2


def _cparams(sem):
    return pltpu.CompilerParams(dimension_semantics=sem, vmem_limit_bytes=VMEM_LIMIT)


def _dot(a, b):
    return jnp.dot(a, b, preferred_element_type=F32)


def _dot_nt(a, b):
    return lax.dot_general(a, b, (((1,), (1,)), ((), ())), preferred_element_type=F32)


def _dot_tn(a, b):
    return lax.dot_general(a, b, (((0,), (0,)), ((), ())), preferred_element_type=F32)


def _sigmoid(x):
    return 0.5 * jnp.tanh(0.5 * x) + 0.5


def _silu(x):
    return x * _sigmoid(x)


def _pack_bf16_pairs(xb):
    n = xb.shape[1] // 2
    lo = lax.bitcast_convert_type(xb[:, :n].astype(F32), jnp.uint32)
    hi = lax.bitcast_convert_type(xb[:, n:].astype(F32), jnp.uint32)
    return (lo >> 16) | hi


def _unpack_bf16_pairs(w):
    lo = lax.bitcast_convert_type(w << 16, F32)
    hi = lax.bitcast_convert_type(w & jnp.uint32(0xFFFF0000), F32)
    return jnp.concatenate([lo, hi], axis=1)


def _layer_norm(z, g, b):
    mu = jnp.mean(z, axis=-1, keepdims=True)
    zc = z - mu
    var = jnp.mean(zc * zc, axis=-1, keepdims=True)
    return zc * lax.rsqrt(var + LN_EPS) * g + b


def _proj_kernel(x_ref, w_ref, o_ref):
    o_ref[...] = _dot(x_ref[...].astype(BF16), w_ref[...])


def _proj(x2, w_bf, tm, row0, T):
    tile0 = row0 // tm
    return pl.pallas_call(
        _proj_kernel,
        grid=(T // tm,),
        in_specs=[pl.BlockSpec((tm, D_MODEL), lambda i: (i + tile0, 0)),
                  pl.BlockSpec((D_MODEL, PROJ_W), lambda i: (0, 0))],
        out_specs=pl.BlockSpec((tm, PROJ_W), lambda i: (i, 0)),
        out_shape=jax.ShapeDtypeStruct((T, PROJ_W), F32),
        compiler_params=_cparams(("arbitrary",)),
        name="proj",
    )(x2, w_bf)


def _hgrn_tables():
    C = CHUNK
    w = np.zeros((N_ARG_GROUPS, C, C), np.float32)
    masks = np.zeros((N_LEVELS, C, C), np.float32)
    for lvl in range(N_LEVELS):
        h = 1 << lvl
        for t in range(C):
            base = (t // (2 * h)) * 2 * h
            m = base + h - 1
            if (t // h) % 2 == 1:
                w[lvl, t, m + 1:t + 1] = 1.0
                masks[lvl, t, base:base + h] = 1.0
            else:
                w[lvl, t, t + 1:m + 1] = 1.0
    for t in range(C):
        w[N_LEVELS, t, :t + 1] = 1.0
        w[N_LEVELS + 1, t, t + 1:] = 1.0
    w = w.reshape(N_ARG_GROUPS * C, C)
    return np.concatenate([w, w], axis=1), masks


def _hgrn_chunk(q, fz, v, gate, c0, c1, nw, seg, mask_ref, state_ref):
    h = 0.5 * q
    qf = h + h * jnp.tanh(h)
    t1 = c1 * jnp.tanh(0.5 * fz)
    f = c0 + t1
    k = c1 - t1
    l2 = jnp.log2(f)
    v_bf = v.astype(BF16)
    qf_bf = qf.astype(BF16)
    k_bf = k.astype(BF16)

    l_hi = l2.astype(BF16)
    l_lo = (l2 - l_hi.astype(F32)).astype(BF16)
    args = _dot(seg, jnp.concatenate([l_hi, l_lo], axis=0))
    e = jnp.exp2(args)

    row = lax.broadcasted_iota(jnp.int32, (CHUNK, 1), 0)
    scores = [jnp.zeros((CHUNK, CHUNK), F32) for _ in range(HG_HEADS)]
    for lvl in range(N_LEVELS):
        half = 1 << lvl
        e_l = e[lvl * CHUNK:(lvl + 1) * CHUNK, :].astype(BF16)
        if half >= 16:
            sel = jnp.concatenate([(qf_bf if (r0 // half) % 2 else k_bf)[r0:r0 + half] for r0 in range(0, CHUNK, half)],
                                  axis=0)
        else:
            sel = jnp.where((row // half) % 2 == 1, qf_bf, k_bf)
        a = sel * e_l
        m = mask_ref[lvl]
        for hd in range(HG_HEADS):
            a_h = a[:, hd * HG_DK:(hd + 1) * HG_DK]
            scores[hd] = scores[hd] + m * _dot_nt(a_h, a_h)

    e_cum = e[N_LEVELS * CHUNK:(N_LEVELS + 1) * CHUNK, :]
    e_suf = e[(N_LEVELS + 1) * CHUNK:(N_LEVELS + 2) * CHUNK, :]
    q_in = qf_bf * e_cum.astype(BF16)
    k_out = k_bf * e_suf.astype(BF16)
    e_last = e_cum[CHUNK - 1:CHUNK, :]
    qk = qf * k

    outs = []
    for hd in range(HG_HEADS):
        cols = slice(hd * HG_DK, (hd + 1) * HG_DK)
        st = state_ref[hd]
        diag = jnp.sum(qk[:, cols], axis=-1, keepdims=True)
        o = (_dot_nt(q_in[:, cols], st.astype(BF16))
             + _dot(scores[hd].astype(BF16), v_bf[:, cols])
             + diag * v[:, cols])
        state_ref[hd] = st * e_last[:, cols] + _dot_tn(v_bf[:, cols], k_out[:, cols])
        o = o * lax.rsqrt(jnp.mean(o * o, axis=-1, keepdims=True) + RMS_EPS) * nw
        outs.append(o)
    hg = 0.5 * gate
    return jnp.concatenate(outs, axis=1) * (hg + hg * jnp.tanh(hg))


def _attn_bias():
    r = np.arange(ATT_BLK)[:, None]
    c = np.arange(2 * ATT_BLK)[None, :]
    dist = r + ATT_BLK - c
    window = (dist >= 0) & (dist < ATT_BLK)
    slopes = np.exp2(-8.0 * (np.arange(ATT_Q_HEADS, dtype=np.float32) + 1.0) / ATT_Q_HEADS).astype(np.float32)
    alibi = -slopes[:, None, None] * dist.astype(np.float32)[None]
    later = np.where(window[None], alibi, np.float32(NEG_INF))
    first = np.where((window & (c >= ATT_BLK))[None], alibi, np.float32(NEG_INF))
    return np.stack([later, first]).astype(np.float32)


def _attn_block(q_ref, kv_cur, kv_prev, bias_ref, table, sink_ref, o_ref):
    lane = lax.broadcasted_iota(jnp.int32, (2 * ATT_BLK, 2 * ATT_KVW), 1)
    lo = (lane % ATT_KVW) < ATT_HD

    kv = jnp.concatenate([kv_prev, kv_cur], axis=0).astype(BF16)
    kv_sw = jnp.concatenate([kv[:, ATT_HD:ATT_KVW], kv[:, :ATT_HD],
                             kv[:, ATT_KVW + ATT_HD:], kv[:, ATT_KVW:ATT_KVW + ATT_HD]], axis=1)
    zero = jnp.zeros_like(kv)
    placed = {}
    for h in range(ATT_KV_HEADS):
        for off in range(2):
            src = kv if h == off else kv_sw
            placed[h, off] = jnp.where(lo if off == 0 else jnp.logical_not(lo), src, zero)

    scale = ATT_HD ** -0.5
    for pair in range(ATT_Q_HEADS // 2):
        qp = (q_ref[:, pair * 2 * ATT_HD:(pair + 1) * 2 * ATT_HD] * scale).astype(BF16)
        acc = jnp.zeros((ATT_BLK, 2 * ATT_HD), F32)
        for off in range(2):
            j = 2 * pair + off
            kvh = placed[j // ATT_GROUP, off]
            sink = sink_ref[j]
            logits = _dot_nt(qp, kvh[:, :ATT_KVW]) + bias_ref[table, j]
            m = jnp.maximum(jnp.max(logits, axis=-1, keepdims=True), sink)
            p = jnp.exp(logits - m)
            den = jnp.sum(p, axis=-1, keepdims=True) + jnp.exp(sink - m)
            acc = acc + _dot(p.astype(BF16), kvh[:, ATT_KVW:]) / den
        o_ref[:, pair * 2 * ATT_HD:(pair + 1) * 2 * ATT_HD] = acc.astype(o_ref.dtype)


def _mixers_kernel(sink_ref, q_ref, f_ref, i_ref, g_ref, aq_ref, kv_ref, kvp_ref, lb_ref, nw_ref, seg_ref,
                   mask_ref, bias_ref, hg_ref, at_ref, state_ref):
    first = pl.program_id(1) == 0

    @pl.when(first)
    def _():
        state_ref[...] = jnp.zeros_like(state_ref)

    lb = lb_ref[...]
    c0 = 0.5 + 0.5 * lb
    c1 = 0.5 - 0.5 * lb
    nw = nw_ref[...]
    seg = seg_ref[...]
    tile = q_ref.shape[0]
    for r0 in range(0, tile, CHUNK):
        rows = slice(r0, r0 + CHUNK)
        o = _hgrn_chunk(q_ref[rows, :], f_ref[rows, :], i_ref[rows, :], g_ref[rows, :], c0, c1, nw, seg, mask_ref,
                        state_ref)
        hg_ref[rows, :] = o.astype(hg_ref.dtype)
    for r0 in range(0, tile, ATT_BLK):
        rows = slice(r0, r0 + ATT_BLK)
        if r0 == 0:
            prev, table = kvp_ref[...], jnp.where(first, 1, 0)
        else:
            prev, table = kv_ref[r0 - ATT_BLK:r0, :], 0
        _attn_block(aq_ref.at[rows, :], kv_ref[rows, :], prev, bias_ref, table, sink_ref, at_ref.at[rows, :])


def _mixers(proj, lb, norm_w, sinks, B, S):
    T = B * S
    tile = MIX_TILE if S % MIX_TILE == 0 else ATT_BLK
    nb = S // tile
    per = tile // ATT_BLK
    seg, masks = _hgrn_tables()
    const = lambda shape: pl.BlockSpec(shape, lambda b, n: (0,) * len(shape))
    hcol = lambda cb: pl.BlockSpec((tile, HG_W), lambda b, n: (b * nb + n, cb))
    hbase = 0
    qcol = (4 * HG_W) // ATT_QW
    kvcol = (4 * HG_W + ATT_QW) // (2 * ATT_KVW)
    out = pl.BlockSpec((tile, HG_W), lambda b, n: (b * nb + n, 0))
    return pl.pallas_call(
        _mixers_kernel,
        grid=(B, nb),
        in_specs=[pl.BlockSpec(memory_space=pltpu.SMEM),
                  hcol(hbase), hcol(hbase + 1), hcol(hbase + 2), hcol(hbase + 3),
                  pl.BlockSpec((tile, ATT_QW), lambda b, n: (b * nb + n, qcol)),
                  pl.BlockSpec((tile, 2 * ATT_KVW), lambda b, n: (b * nb + n, kvcol)),
                  pl.BlockSpec((ATT_BLK, 2 * ATT_KVW),
                               lambda b, n: ((b * nb + n) * per - jnp.where(n > 0, 1, 0), kvcol)),
                  const((1, HG_W)), const((1, HG_DV)), const((N_ARG_GROUPS * CHUNK, 2 * CHUNK)),
                  const((N_LEVELS, CHUNK, CHUNK)), const((2, ATT_Q_HEADS, ATT_BLK, 2 * ATT_BLK))],
        out_specs=[out, out],
        out_shape=[jax.ShapeDtypeStruct((T, HG_W), BF16), jax.ShapeDtypeStruct((T, ATT_QW), BF16)],
        scratch_shapes=[pltpu.VMEM((HG_HEADS, HG_DV, HG_DK), F32)],
        compiler_params=_cparams(("arbitrary", "arbitrary")),
        name="mixers",
    )(sinks, proj, proj, proj, proj, proj, proj, proj, lb, norm_w, jnp.asarray(seg, BF16), jnp.asarray(masks),
      jnp.asarray(_attn_bias()))


def _merge_kernel(hg_ref, at_ref, x_ref, wgate_ref, wa_ref, wb_ref, wo_ref, g1_ref, b1_ref,
                  wr_ref, br_ref, tri_ref, x1_ref, xp_ref, eid_ref, wt_ref, rank_ref, cnt_ref, base_ref, x1b_ref):
    tm = x_ref.shape[0]
    sub = min(tm, MERGE_SUB)
    for r0 in range(0, tm, sub):
        rows = slice(r0, r0 + sub)
        xs = x_ref[rows, :]
        gates = _dot(xs.astype(BF16), wgate_ref[...])
        ya = _dot(hg_ref[rows, :], wa_ref[...])
        yb = _dot(at_ref[rows, :], wb_ref[...])
        merged = _sigmoid(gates[:, :D_MODEL]) * ya + _sigmoid(gates[:, D_MODEL:]) * yb
        z = DN_ALPHA * xs + _dot(merged.astype(BF16), wo_ref[...])
        x1 = _layer_norm(z, g1_ref[...], b1_ref[...])
        x1_ref[rows, :] = x1
        x1b = x1.astype(BF16)
        x1b_ref[rows, :] = x1b
        xp_ref[rows, :] = _pack_bf16_pairs(x1b)

    lg = _dot_nt(wr_ref[...], x1b_ref[...]) + br_ref[...]
    g = lg[0:8, :]
    row8 = lax.broadcasted_iota(jnp.int32, (8, tm), 0)
    g = jnp.where(row8 < N_GROUPS, g, -jnp.inf)
    gmax = jnp.max(g, axis=0, keepdims=True)
    gsel = jnp.min(jnp.where(g == gmax, row8, 8), axis=0, keepdims=True)
    gw = 1.0 / jnp.sum(jnp.exp(g - gmax), axis=0, keepdims=True)
    el = jnp.where(gsel == 0, lg[8:16, :],
                   jnp.where(gsel == 1, lg[16:24, :], jnp.where(gsel == 2, lg[24:32, :], lg[32:40, :])))
    v1 = jnp.max(el, axis=0, keepdims=True)
    i1 = jnp.min(jnp.where(el == v1, row8, 8), axis=0, keepdims=True)
    el2 = jnp.where(row8 == i1, -jnp.inf, el)
    v2 = jnp.max(el2, axis=0, keepdims=True)
    i2 = jnp.min(jnp.where(el2 == v2, row8, 8), axis=0, keepdims=True)
    e2 = jnp.exp(v2 - v1)
    den = 1.0 + e2
    e_a = gsel * EPG + i1
    e_b = gsel * EPG + i2
    eid_ref[...] = jnp.concatenate([e_a, e_b], axis=0)
    wt_ref[...] = jnp.concatenate([gw / den, gw * e2 / den], axis=0)

    @pl.when(pl.program_id(0) == 0)
    def _():
        base_ref[...] = jnp.zeros_like(base_ref)

    row_e = lax.broadcasted_iota(jnp.int32, (N_EXPERTS, tm), 0)
    oh_a = jnp.where(row_e == e_a, 1.0, 0.0)
    oh_b = jnp.where(row_e == e_b, 1.0, 0.0)
    tri = tri_ref[...]
    pre_a = _dot(oh_a.astype(BF16), tri)
    pre_b = _dot(oh_b.astype(BF16), tri)
    cnt_a = jnp.sum(oh_a, axis=1, keepdims=True)
    cnt_b = jnp.sum(oh_b, axis=1, keepdims=True)
    base = base_ref[...]
    rank_a = jnp.sum(oh_a * (base + pre_a), axis=0, keepdims=True)
    rank_b = jnp.sum(oh_b * (base + cnt_a + pre_b), axis=0, keepdims=True)
    rank_ref[...] = jnp.concatenate([rank_a, rank_b], axis=0).astype(jnp.int32)
    base = base + cnt_a + cnt_b
    base_ref[...] = base
    cnt_ref[...] = jnp.broadcast_to(base, cnt_ref.shape)


def _merge(hg, att, x2, wgate, wa, wb, wo, g1, b1, wr, br, tm, row0):
    T = hg.shape[0]
    tile0 = row0 // tm
    row = lambda w: pl.BlockSpec((tm, w), lambda i: (i, 0))
    const = lambda shape: pl.BlockSpec(shape, lambda i: (0,) * len(shape))
    lanes = pl.BlockSpec((2, tm), lambda i: (0, i))
    tri = jnp.asarray(np.triu(np.ones((tm, tm), np.float32), 1), BF16)
    return pl.pallas_call(
        _merge_kernel,
        grid=(T // tm,),
        in_specs=[row(HG_W), row(ATT_QW),
                  pl.BlockSpec((tm, D_MODEL), lambda i: (i + tile0, 0)),
                  const((D_MODEL, 2 * D_MODEL)), const((HG_W, D_MODEL)), const((ATT_QW, D_MODEL)), const((D_MODEL, D_MODEL)),
                  const((1, D_MODEL)), const((1, D_MODEL)), const((40, D_MODEL)), const((40, 1)),
                  const((tm, tm))],
        out_specs=[row(D_MODEL), row(D_MODEL // 2), lanes, lanes, lanes, const((N_EXPERTS, 128))],
        out_shape=[jax.ShapeDtypeStruct((T, D_MODEL), F32),
                   jax.ShapeDtypeStruct((T, D_MODEL // 2), jnp.uint32),
                   jax.ShapeDtypeStruct((2, T), jnp.int32),
                   jax.ShapeDtypeStruct((2, T), F32),
                   jax.ShapeDtypeStruct((2, T), jnp.int32),
                   jax.ShapeDtypeStruct((N_EXPERTS, 128), F32)],
        scratch_shapes=[pltpu.VMEM((N_EXPERTS, 1), F32), pltpu.VMEM((tm, D_MODEL), BF16)],
        compiler_params=_cparams(("arbitrary",)),
        name="merge",
    )(hg, att, x2, wgate, wa, wb, wo, g1, b1, wr, br, tri)


SC_WINDOW = 32
SC_IDX_LANES = 128


def _pad_indices(idx):
    rows = idx.reshape(-1, SC_WINDOW)
    return jnp.pad(rows, ((0, 0), (0, SC_IDX_LANES - SC_WINDOW)))


def _sc_mesh():
    return plsc.VectorSubcoreMesh(core_axis_name="core", subcore_axis_name="subcore")


def _sc_scatter2(x, idx_a, idx_b, n_out):
    T, d = x.shape

    @pl.kernel(out_type=jax.ShapeDtypeStruct((n_out, d), x.dtype), mesh=_sc_mesh())
    def scatter(x_hbm, ia_hbm, ib_hbm, o_hbm):
        def body(x_vmem, ia_vmem, ib_vmem):
            pltpu.sync_copy(x_vmem, o_hbm.at[ia_vmem.at[0, pl.ds(0, SC_WINDOW)]])
            pltpu.sync_copy(x_vmem, o_hbm.at[ib_vmem.at[0, pl.ds(0, SC_WINDOW)]])

        idx_spec = pl.BlockSpec((1, SC_IDX_LANES), lambda i: (i, 0))
        pltpu.emit_pipeline(
            body, grid=(T // SC_WINDOW,),
            in_specs=[pl.BlockSpec((SC_WINDOW, d), lambda i: (i, 0)), idx_spec, idx_spec],
            out_specs=[],
            core_axis_name=("core", "subcore"),
            dimension_semantics=(pltpu.PARALLEL,),
        )(x_hbm, ia_hbm, ib_hbm)

    return scatter(x, _pad_indices(idx_a), _pad_indices(idx_b))


def _sc_gather(x, idx):
    n = idx.shape[0]
    d = x.shape[1]

    @pl.kernel(out_type=jax.ShapeDtypeStruct((n, d), x.dtype), mesh=_sc_mesh())
    def gather(x_hbm, i_hbm, o_hbm):
        def body(i_vmem, o_vmem):
            pltpu.sync_copy(x_hbm.at[i_vmem.at[0, pl.ds(0, SC_WINDOW)]], o_vmem)

        pltpu.emit_pipeline(
            body, grid=(n // SC_WINDOW,),
            in_specs=[pl.BlockSpec((1, SC_IDX_LANES), lambda i: (i, 0))],
            out_specs=[pl.BlockSpec((SC_WINDOW, d), lambda i: (i, 0))],
            core_axis_name=("core", "subcore"),
            dimension_semantics=(pltpu.PARALLEL,),
        )(i_hbm, o_hbm)

    return gather(x, _pad_indices(idx))


def _expert_kernel(be_ref, nv_ref, nxt_ref, par_ref, xb_ref, wg_hbm, wu_hbm, wd_hbm, y_ref,
                   wg_st, wu_st, wd_st, wg_bf, wu_bf, wd_bf, sem):
    i = pl.program_id(0)
    nv = nv_ref[i]

    def fetch(expert, slot):
        return [pltpu.make_async_copy(src.at[expert], dst.at[slot], sem.at[slot, j])
                for j, (src, dst) in enumerate(((wg_hbm, wg_st), (wu_hbm, wu_st), (wd_hbm, wd_st)))]

    @pl.when(i == 0)
    def _():
        for cp in fetch(be_ref[0], par_ref[0]):
            cp.start()

    @pl.when((i == 0) | (be_ref[i] != be_ref[jnp.maximum(i - 1, 0)]))
    def _():
        slot = par_ref[i]
        for cp in fetch(be_ref[i], slot):
            cp.wait()
        wg_bf[...] = wg_st[slot].astype(BF16)
        wu_bf[...] = wu_st[slot].astype(BF16)
        wd_bf[...] = wd_st[slot].astype(BF16)

        @pl.when(nxt_ref[i] >= 0)
        def _():
            for cp in fetch(nxt_ref[i], 1 - slot):
                cp.start()

    @pl.when(nv > 0)
    def _():
        row = lax.broadcasted_iota(jnp.int32, (MOE_BLOCK, 1), 0)
        xb = _unpack_bf16_pairs(jnp.where(row < nv, xb_ref[...], jnp.uint32(0))).astype(BF16)
        g = _dot(xb, wg_bf[...])
        u = _dot(xb, wu_bf[...])
        h = (_silu(g) * u).astype(BF16)
        y_ref[...] = _pack_bf16_pairs(_dot(h, wd_bf[...]).astype(BF16))


def _experts(block_meta, xbuf, w_gate, w_up, w_down):
    P = xbuf.shape[0]
    nb = P // MOE_BLOCK
    rows = pl.BlockSpec((MOE_BLOCK, D_MODEL // 2), lambda i, *_: (i, 0))
    hbm = pl.BlockSpec(memory_space=pl.ANY)
    return pl.pallas_call(
        _expert_kernel,
        grid_spec=pltpu.PrefetchScalarGridSpec(
            num_scalar_prefetch=4,
            grid=(nb,),
            in_specs=[rows, hbm, hbm, hbm],
            out_specs=rows,
            scratch_shapes=[pltpu.VMEM((2, D_MODEL, D_EXPERT), F32), pltpu.VMEM((2, D_MODEL, D_EXPERT), F32),
                            pltpu.VMEM((2, D_EXPERT, D_MODEL), F32),
                            pltpu.VMEM((D_MODEL, D_EXPERT), BF16), pltpu.VMEM((D_MODEL, D_EXPERT), BF16),
                            pltpu.VMEM((D_EXPERT, D_MODEL), BF16),
                            pltpu.SemaphoreType.DMA((2, 3))]),
        out_shape=jax.ShapeDtypeStruct((P, D_MODEL // 2), jnp.uint32),
        compiler_params=_cparams(("arbitrary",)),
        name="experts",
    )(*block_meta, xbuf, w_gate, w_up, w_down)


def _combine_kernel(ra_ref, rb_ref, x1_ref, wt_ref, g2_ref, b2_ref, *rest):
    o_ref = rest[-1]
    w2 = wt_ref[...]
    w = jnp.concatenate([w2, jnp.zeros((6, w2.shape[1]), F32)], axis=0).T
    y = w[:, 0:1] * _unpack_bf16_pairs(ra_ref[...]) + w[:, 1:2] * _unpack_bf16_pairs(rb_ref[...])
    o_ref[...] = _layer_norm(DN_ALPHA * x1_ref[...] + y, g2_ref[...], b2_ref[...])


def _combine(rows2, x1, wt, g2, b2, tm, out_prev, tile0, t_total):
    T = x1.shape[0]
    nt = T // tm
    const = lambda shape: pl.BlockSpec(shape, lambda i: (0,) * len(shape))
    in_specs = [pl.BlockSpec((tm, D_MODEL // 2), lambda i: (i, 0)),
                pl.BlockSpec((tm, D_MODEL // 2), lambda i: (i + nt, 0)),
                pl.BlockSpec((tm, D_MODEL), lambda i: (i, 0)), pl.BlockSpec((2, tm), lambda i: (0, i)),
                const((1, D_MODEL)), const((1, D_MODEL))]
    args = [rows2, rows2, x1, wt, g2, b2]
    aliases = {}
    if out_prev is not None:
        in_specs.append(pl.BlockSpec(memory_space=pl.ANY))
        args.append(out_prev)
        aliases = {len(args) - 1: 0}
    return pl.pallas_call(
        _combine_kernel,
        grid=(nt,),
        in_specs=in_specs,
        out_specs=pl.BlockSpec((tm, D_MODEL), lambda i: (i + tile0, 0)),
        out_shape=jax.ShapeDtypeStruct((t_total, D_MODEL), F32),
        input_output_aliases=aliases,
        compiler_params=_cparams(("arbitrary",)),
        name="combine",
    )(*args)


def _slot_layout(eid, rank, counts, T):
    P = 2 * T + N_EXPERTS * MOE_BLOCK
    nb = P // MOE_BLOCK
    padded = ((counts + MOE_BLOCK - 1) // MOE_BLOCK) * MOE_BLOCK
    pend = jnp.cumsum(padded)
    pstart = pend - padded
    experts = jnp.arange(N_EXPERTS, dtype=jnp.int32)
    pos = rank + jnp.sum(jnp.where(eid[:, :, None] == experts, pstart, 0), axis=-1)

    block_start = jnp.arange(nb, dtype=jnp.int32) * MOE_BLOCK
    block_expert = jnp.minimum(jnp.sum(block_start[:, None] >= pend[None, :], axis=1), N_EXPERTS - 1)
    onehot = block_expert[:, None] == experts[None, :]
    look = lambda table: jnp.sum(jnp.where(onehot, table[None, :], 0), axis=1)
    block_valid = jnp.clip(look(pstart + counts) - block_start, 0, MOE_BLOCK)
    block_valid = jnp.where(block_start < pend[-1], block_valid, 0)
    present = (counts > 0) | (experts == N_EXPERTS - 1)
    later = present[None, :] & (experts[None, :] > experts[:, None])
    next_present = jnp.min(jnp.where(later, experts[None, :], N_EXPERTS), axis=1)
    next_present = jnp.where(next_present < N_EXPERTS, next_present, -1)
    runs_before = jnp.sum(present[None, :] & (experts[None, :] < experts[:, None]), axis=1)
    block_next = look(next_present)
    block_slot = look(runs_before % 2)
    i32 = lambda a: a.astype(jnp.int32)
    return i32(pos), (i32(block_expert), i32(block_valid), i32(block_next), i32(block_slot))


def kernel(x, lb_logits, w_in, hg_norm_w, sinks, w_branch_a, w_branch_b, w_out, ln1_g, ln1_b,
           router_group_w, router_group_b, router_expert_w, router_expert_b,
           w_exp_gate, w_exp_up, w_exp_down, ln2_g, ln2_b):
    B, S, D = x.shape
    assert D == D_MODEL and S % ATT_BLK == 0 and w_in.shape[0] == DEPTH == 1
    lb_all = jnp.cumsum(jax.nn.softmax(lb_logits.astype(F32), axis=0), axis=0)
    lb = lb_all[0].reshape(1, HG_W)
    w_mix = w_in[0][:, :PROJ_W].astype(BF16)
    w_gate = w_in[0][:, PROJ_W:].astype(BF16)
    nw = hg_norm_w[0].reshape(1, HG_DV).astype(F32)
    wa, wb, wo = w_branch_a[0].astype(BF16), w_branch_b[0].astype(BF16), w_out[0].astype(BF16)
    wr = jnp.zeros((40, D), F32).at[0:N_GROUPS].set(router_group_w[0].T).at[8:40].set(router_expert_w[0].T)
    br = jnp.zeros((40, 1), F32).at[0:N_GROUPS, 0].set(router_group_b[0]).at[8:40, 0].set(router_expert_b[0])
    g1, b1 = ln1_g[0].reshape(1, D), ln1_b[0].reshape(1, D)
    g2, b2 = ln2_g[0].reshape(1, D), ln2_b[0].reshape(1, D)

    n_parts = N_PARTS if B % N_PARTS == 0 else 1
    bp = B // n_parts
    tp = bp * S
    tm = 256 if tp % 256 == 0 else ATT_BLK
    ts = 512 if S % 512 == 0 else ATT_BLK
    x2 = x.reshape(B * S, D)
    out = None
    for part in range(n_parts):
        proj = _proj(x2, w_mix, tm, part * tp, tp)
        hg, att = _mixers(proj, lb, nw, sinks[0].astype(F32), bp, S)
        x1, xp, eid, wt, rank, cnt = _merge(hg, att, x2, w_gate, wa, wb, wo, g1, b1, wr.astype(BF16), br,
                                            512 if tp % 512 == 0 else tm, part * tp)
        pos, block_meta = _slot_layout(eid, rank, cnt[:, 0].astype(jnp.int32), tp)
        xbuf = _sc_scatter2(xp, pos[0], pos[1], 2 * tp + N_EXPERTS * MOE_BLOCK)
        ybuf = _experts(block_meta, xbuf, w_exp_gate[0], w_exp_up[0], w_exp_down[0])
        rows2 = _sc_gather(ybuf, pos.reshape(-1))
        out = _combine(rows2, x1, wt, g2, b2, tm, out, part * (tp // tm), B * S)
    return out.reshape(B, S, D)
```

```python
import functools

import numpy as np
import jax
import jax.numpy as jnp
from jax import lax
from jax.experimental import pallas as pl
from jax.experimental.pallas import tpu as pltpu
from jax.experimental.pallas import tpu_sc as plsc

F32 = jnp.float32
BF16 = jnp.bfloat16

D_MODEL = 1024
DEPTH = 1
HG_HEADS = 4
HG_DK = 128
HG_DV = 128
HG_W = HG_HEADS * HG_DK
CHUNK = 64
ATT_Q_HEADS = 8
ATT_KV_HEADS = 2
ATT_GROUP = ATT_Q_HEADS // ATT_KV_HEADS
ATT_HD = 64
ATT_QW = ATT_Q_HEADS * ATT_HD
ATT_KVW = ATT_KV_HEADS * ATT_HD
ATT_BLK = 128
N_GROUPS = 4
EPG = 8
N_EXPERTS = N_GROUPS * EPG
D_EXPERT = 512
MOE_BLOCK = 256
DN_ALPHA = (2.0 * DEPTH) ** 0.25
LN_EPS = 1e-5
RMS_EPS = 1e-6
NEG_INF = -1e30

PROJ_W = 4 * HG_W + ATT_QW + 2 * ATT_KVW
PROJ_F = 3 * HG_W
PROJ_B = HG_W + ATT_QW + 2 * ATT_KVW
N_LEVELS = 6
N_ARG_GROUPS = N_LEVELS + 2

VMEM_LIMIT = 56 * 1024 * 1024
MERGE_SUB = 256
MIX_TILE = 512
N_PARTS = 2


def _cparams(sem):
    return pltpu.CompilerParams(dimension_semantics=sem, vmem_limit_bytes=VMEM_LIMIT)


def _dot(a, b):
    return jnp.dot(a, b, preferred_element_type=F32)


def _dot_nt(a, b):
    return lax.dot_general(a, b, (((1,), (1,)), ((), ())), preferred_element_type=F32)


def _dot_tn(a, b):
    return lax.dot_general(a, b, (((0,), (0,)), ((), ())), preferred_element_type=F32)


def _sigmoid(x):
    return 0.5 * jnp.tanh(0.5 * x) + 0.5


def _silu(x):
    return x * _sigmoid(x)


def _pack_bf16_pairs(xb):
    n = xb.shape[1] // 2
    lo = lax.bitcast_convert_type(xb[:, :n].astype(F32), jnp.uint32)
    hi = lax.bitcast_convert_type(xb[:, n:].astype(F32), jnp.uint32)
    return (lo >> 16) | hi


def _unpack_bf16_pairs(w):
    lo = lax.bitcast_convert_type(w << 16, F32)
    hi = lax.bitcast_convert_type(w & jnp.uint32(0xFFFF0000), F32)
    return jnp.concatenate([lo, hi], axis=1)


def _layer_norm(z, g, b):
    mu = jnp.mean(z, axis=-1, keepdims=True)
    zc = z - mu
    var = jnp.mean(zc * zc, axis=-1, keepdims=True)
    return zc * lax.rsqrt(var + LN_EPS) * g + b


def _proj_kernel(x_ref, wf_ref, wb_ref, of_ref, ob_ref):
    xb = x_ref[...].astype(BF16)
    of_ref[...] = _dot(xb, wf_ref[...])
    ob_ref[...] = _dot(xb, wb_ref[...]).astype(BF16)


def _proj(x2, w_f, w_b, tm, row0, T):
    tile0 = row0 // tm
    return pl.pallas_call(
        _proj_kernel,
        grid=(T // tm,),
        in_specs=[pl.BlockSpec((tm, D_MODEL), lambda i: (i + tile0, 0)),
                  pl.BlockSpec((D_MODEL, PROJ_F), lambda i: (0, 0)),
                  pl.BlockSpec((D_MODEL, PROJ_B), lambda i: (0, 0))],
        out_specs=[pl.BlockSpec((tm, PROJ_F), lambda i: (i, 0)), pl.BlockSpec((tm, PROJ_B), lambda i: (i, 0))],
        out_shape=[jax.ShapeDtypeStruct((T, PROJ_F), F32), jax.ShapeDtypeStruct((T, PROJ_B), BF16)],
        compiler_params=_cparams(("arbitrary",)),
        name="proj",
    )(x2, w_f, w_b)


def _hgrn_tables():
    C = CHUNK
    w = np.zeros((N_ARG_GROUPS, C, C), np.float32)
    masks = np.zeros((N_LEVELS, C, C), np.float32)
    for lvl in range(N_LEVELS):
        h = 1 << lvl
        for t in range(C):
            base = (t // (2 * h)) * 2 * h
            m = base + h - 1
            if (t // h) % 2 == 1:
                w[lvl, t, m + 1:t + 1] = 1.0
                masks[lvl, t, base:base + h] = 1.0
            else:
                w[lvl, t, t + 1:m + 1] = 1.0
    for t in range(C):
        w[N_LEVELS, t, :t + 1] = 1.0
        w[N_LEVELS + 1, t, t + 1:] = 1.0
    w = w.reshape(N_ARG_GROUPS * C, C)
    return np.concatenate([w, w], axis=1), masks


def _hgrn_chunk(q, fz, v, gate, c0, c1, nw, seg, mask_ref, state_ref):
    h = 0.5 * q
    qf = h + h * jnp.tanh(h)
    t1 = c1 * jnp.tanh(0.5 * fz)
    f = c0 + t1
    k = c1 - t1
    l2 = jnp.log2(f)
    v_bf = v.astype(BF16)
    qf_bf = qf.astype(BF16)
    k_bf = k.astype(BF16)

    l_hi = l2.astype(BF16)
    l_lo = (l2 - l_hi.astype(F32)).astype(BF16)
    args = _dot(seg, jnp.concatenate([l_hi, l_lo], axis=0))
    e = jnp.exp2(args)

    row = lax.broadcasted_iota(jnp.int32, (CHUNK, 1), 0)
    scores = [jnp.zeros((CHUNK, CHUNK), F32) for _ in range(HG_HEADS)]
    for lvl in range(N_LEVELS):
        half = 1 << lvl
        e_l = e[lvl * CHUNK:(lvl + 1) * CHUNK, :].astype(BF16)
        if half >= 16:
            sel = jnp.concatenate([(qf_bf if (r0 // half) % 2 else k_bf)[r0:r0 + half] for r0 in range(0, CHUNK, half)],
                                  axis=0)
        else:
            sel = jnp.where((row // half) % 2 == 1, qf_bf, k_bf)
        a = sel * e_l
        m = mask_ref[lvl]
        for hd in range(HG_HEADS):
            a_h = a[:, hd * HG_DK:(hd + 1) * HG_DK]
            scores[hd] = scores[hd] + m * _dot_nt(a_h, a_h)

    e_cum = e[N_LEVELS * CHUNK:(N_LEVELS + 1) * CHUNK, :]
    e_suf = e[(N_LEVELS + 1) * CHUNK:(N_LEVELS + 2) * CHUNK, :]
    q_in = qf_bf * e_cum.astype(BF16)
    k_out = k_bf * e_suf.astype(BF16)
    e_last = e_cum[CHUNK - 1:CHUNK, :]
    qk = qf * k

    outs = []
    for hd in range(HG_HEADS):
        cols = slice(hd * HG_DK, (hd + 1) * HG_DK)
        st = state_ref[hd]
        diag = jnp.sum(qk[:, cols], axis=-1, keepdims=True)
        o = (_dot_nt(q_in[:, cols], st.astype(BF16))
             + _dot(scores[hd].astype(BF16), v_bf[:, cols])
             + diag * v[:, cols])
        state_ref[hd] = st * e_last[:, cols] + _dot_tn(v_bf[:, cols], k_out[:, cols])
        o = o * lax.rsqrt(jnp.mean(o * o, axis=-1, keepdims=True) + RMS_EPS) * nw
        outs.append(o)
    hg = 0.5 * gate
    return jnp.concatenate(outs, axis=1) * (hg + hg * jnp.tanh(hg))


def _attn_bias():
    r = np.arange(ATT_BLK)[:, None]
    c = np.arange(2 * ATT_BLK)[None, :]
    dist = r + ATT_BLK - c
    window = (dist >= 0) & (dist < ATT_BLK)
    slopes = np.exp2(-8.0 * (np.arange(ATT_Q_HEADS, dtype=np.float32) + 1.0) / ATT_Q_HEADS).astype(np.float32)
    alibi = -slopes[:, None, None] * dist.astype(np.float32)[None]
    later = np.where(window[None], alibi, np.float32(NEG_INF))
    first = np.where((window & (c >= ATT_BLK))[None], alibi, np.float32(NEG_INF))
    return np.stack([later, first]).astype(np.float32)


def _attn_block(q_ref, kv_cur, kv_prev, bias_ref, table, sink_ref, o_ref):
    lane = lax.broadcasted_iota(jnp.int32, (2 * ATT_BLK, 2 * ATT_KVW), 1)
    lo = (lane % ATT_KVW) < ATT_HD

    kv = jnp.concatenate([kv_prev, kv_cur], axis=0).astype(BF16)
    kv_sw = jnp.concatenate([kv[:, ATT_HD:ATT_KVW], kv[:, :ATT_HD],
                             kv[:, ATT_KVW + ATT_HD:], kv[:, ATT_KVW:ATT_KVW + ATT_HD]], axis=1)
    zero = jnp.zeros_like(kv)
    placed = {}
    for h in range(ATT_KV_HEADS):
        for off in range(2):
            src = kv if h == off else kv_sw
            placed[h, off] = jnp.where(lo if off == 0 else jnp.logical_not(lo), src, zero)

    scale = ATT_HD ** -0.5
    for pair in range(ATT_Q_HEADS // 2):
        qp = (q_ref[:, pair * 2 * ATT_HD:(pair + 1) * 2 * ATT_HD] * scale).astype(BF16)
        acc = jnp.zeros((ATT_BLK, 2 * ATT_HD), F32)
        for off in range(2):
            j = 2 * pair + off
            kvh = placed[j // ATT_GROUP, off]
            sink = sink_ref[j]
            logits = _dot_nt(qp, kvh[:, :ATT_KVW]) + bias_ref[table, j]
            m = jnp.maximum(jnp.max(logits, axis=-1, keepdims=True), sink)
            p = jnp.exp(logits - m)
            den = jnp.sum(p, axis=-1, keepdims=True) + jnp.exp(sink - m)
            acc = acc + _dot(p.astype(BF16), kvh[:, ATT_KVW:]) / den
        o_ref[:, pair * 2 * ATT_HD:(pair + 1) * 2 * ATT_HD] = acc.astype(o_ref.dtype)


def _mixers_kernel(sink_ref, q_ref, f_ref, i_ref, g_ref, aq_ref, kv_ref, kvp_ref, lb_ref, nw_ref, seg_ref,
                   mask_ref, bias_ref, hg_ref, at_ref, state_ref):
    first = pl.program_id(1) == 0

    @pl.when(first)
    def _():
        state_ref[...] = jnp.zeros_like(state_ref)

    lb = lb_ref[...]
    c0 = 0.5 + 0.5 * lb
    c1 = 0.5 - 0.5 * lb
    nw = nw_ref[...]
    seg = seg_ref[...]
    tile = q_ref.shape[0]
    for r0 in range(0, tile, CHUNK):
        rows = slice(r0, r0 + CHUNK)
        o = _hgrn_chunk(q_ref[rows, :], f_ref[rows, :], i_ref[rows, :], g_ref[rows, :], c0, c1, nw, seg, mask_ref,
                        state_ref)
        hg_ref[rows, :] = o.astype(hg_ref.dtype)
    for r0 in range(0, tile, ATT_BLK):
        rows = slice(r0, r0 + ATT_BLK)
        if r0 == 0:
            prev, table = kvp_ref[...], jnp.where(first, 1, 0)
        else:
            prev, table = kv_ref[r0 - ATT_BLK:r0, :], 0
        _attn_block(aq_ref.at[rows, :], kv_ref[rows, :], prev, bias_ref, table, sink_ref, at_ref.at[rows, :])


def _mixers(proj_f, proj_b, lb, norm_w, sinks, B, S):
    T = B * S
    tile = MIX_TILE if S % MIX_TILE == 0 else ATT_BLK
    nb = S // tile
    per = tile // ATT_BLK
    seg, masks = _hgrn_tables()
    const = lambda shape: pl.BlockSpec(shape, lambda b, n: (0,) * len(shape))
    hcol = lambda cb: pl.BlockSpec((tile, HG_W), lambda b, n: (b * nb + n, cb))
    kvcol = (HG_W + ATT_QW) // (2 * ATT_KVW)
    out = pl.BlockSpec((tile, HG_W), lambda b, n: (b * nb + n, 0))
    return pl.pallas_call(
        _mixers_kernel,
        grid=(B, nb),
        in_specs=[pl.BlockSpec(memory_space=pltpu.SMEM),
                  hcol(0), hcol(1), hcol(0), hcol(2),
                  pl.BlockSpec((tile, ATT_QW), lambda b, n: (b * nb + n, HG_W // ATT_QW)),
                  pl.BlockSpec((tile, 2 * ATT_KVW), lambda b, n: (b * nb + n, kvcol)),
                  pl.BlockSpec((ATT_BLK, 2 * ATT_KVW),
                               lambda b, n: ((b * nb + n) * per - jnp.where(n > 0, 1, 0), kvcol)),
                  const((1, HG_W)), const((1, HG_DV)), const((N_ARG_GROUPS * CHUNK, 2 * CHUNK)),
                  const((N_LEVELS, CHUNK, CHUNK)), const((2, ATT_Q_HEADS, ATT_BLK, 2 * ATT_BLK))],
        out_specs=[out, out],
        out_shape=[jax.ShapeDtypeStruct((T, HG_W), BF16), jax.ShapeDtypeStruct((T, ATT_QW), BF16)],
        scratch_shapes=[pltpu.VMEM((HG_HEADS, HG_DV, HG_DK), F32)],
        compiler_params=_cparams(("arbitrary", "arbitrary")),
        name="mixers",
    )(sinks, proj_f, proj_f, proj_b, proj_f, proj_b, proj_b, proj_b, lb, norm_w, jnp.asarray(seg, BF16),
      jnp.asarray(masks), jnp.asarray(_attn_bias()))


def _merge_kernel(hg_ref, at_ref, x_ref, wgate_ref, wa_ref, wb_ref, wo_ref, g1_ref, b1_ref,
                  wr_ref, br_ref, tri_ref, x1_ref, xp_ref, eid_ref, wt_ref, rank_ref, cnt_ref, base_ref, x1b_ref):
    tm = x_ref.shape[0]
    sub = min(tm, MERGE_SUB)
    for r0 in range(0, tm, sub):
        rows = slice(r0, r0 + sub)
        xs = x_ref[rows, :]
        gates = _dot(xs.astype(BF16), wgate_ref[...])
        ya = _dot(hg_ref[rows, :], wa_ref[...])
        yb = _dot(at_ref[rows, :], wb_ref[...])
        merged = _sigmoid(gates[:, :D_MODEL]) * ya + _sigmoid(gates[:, D_MODEL:]) * yb
        z = DN_ALPHA * xs + _dot(merged.astype(BF16), wo_ref[...])
        x1 = _layer_norm(z, g1_ref[...], b1_ref[...])
        x1_ref[rows, :] = x1
        x1b = x1.astype(BF16)
        x1b_ref[rows, :] = x1b
        xp_ref[rows, :] = _pack_bf16_pairs(x1b)

    lg = _dot_nt(wr_ref[...], x1b_ref[...]) + br_ref[...]
    g = lg[0:8, :]
    row8 = lax.broadcasted_iota(jnp.int32, (8, tm), 0)
    g = jnp.where(row8 < N_GROUPS, g, -jnp.inf)
    gmax = jnp.max(g, axis=0, keepdims=True)
    gsel = jnp.min(jnp.where(g == gmax, row8, 8), axis=0, keepdims=True)
    gw = 1.0 / jnp.sum(jnp.exp(g - gmax), axis=0, keepdims=True)
    el = jnp.where(gsel == 0, lg[8:16, :],
                   jnp.where(gsel == 1, lg[16:24, :], jnp.where(gsel == 2, lg[24:32, :], lg[32:40, :])))
    v1 = jnp.max(el, axis=0, keepdims=True)
    i1 = jnp.min(jnp.where(el == v1, row8, 8), axis=0, keepdims=True)
    el2 = jnp.where(row8 == i1, -jnp.inf, el)
    v2 = jnp.max(el2, axis=0, keepdims=True)
    i2 = jnp.min(jnp.where(el2 == v2, row8, 8), axis=0, keepdims=True)
    e2 = jnp.exp(v2 - v1)
    den = 1.0 + e2
    e_a = gsel * EPG + i1
    e_b = gsel * EPG + i2
    eid_ref[...] = jnp.concatenate([e_a, e_b], axis=0)
    wt_ref[...] = jnp.concatenate([gw / den, gw * e2 / den], axis=0)

    @pl.when(pl.program_id(0) == 0)
    def _():
        base_ref[...] = jnp.zeros_like(base_ref)

    row_e = lax.broadcasted_iota(jnp.int32, (N_EXPERTS, tm), 0)
    oh_a = jnp.where(row_e == e_a, 1.0, 0.0)
    oh_b = jnp.where(row_e == e_b, 1.0, 0.0)
    tri = tri_ref[...]
    pre_a = _dot(oh_a.astype(BF16), tri)
    pre_b = _dot(oh_b.astype(BF16), tri)
    cnt_a = jnp.sum(oh_a, axis=1, keepdims=True)
    cnt_b = jnp.sum(oh_b, axis=1, keepdims=True)
    base = base_ref[...]
    rank_a = jnp.sum(oh_a * (base + pre_a), axis=0, keepdims=True)
    rank_b = jnp.sum(oh_b * (base + cnt_a + pre_b), axis=0, keepdims=True)
    rank_ref[...] = jnp.concatenate([rank_a, rank_b], axis=0).astype(jnp.int32)
    base = base + cnt_a + cnt_b
    base_ref[...] = base
    cnt_ref[...] = jnp.broadcast_to(base, cnt_ref.shape)


def _merge(hg, att, x2, wgate, wa, wb, wo, g1, b1, wr, br, tm, row0):
    T = hg.shape[0]
    tile0 = row0 // tm
    row = lambda w: pl.BlockSpec((tm, w), lambda i: (i, 0))
    const = lambda shape: pl.BlockSpec(shape, lambda i: (0,) * len(shape))
    lanes = pl.BlockSpec((2, tm), lambda i: (0, i))
    tri = jnp.asarray(np.triu(np.ones((tm, tm), np.float32), 1), BF16)
    return pl.pallas_call(
        _merge_kernel,
        grid=(T // tm,),
        in_specs=[row(HG_W), row(ATT_QW),
                  pl.BlockSpec((tm, D_MODEL), lambda i: (i + tile0, 0)),
                  const((D_MODEL, 2 * D_MODEL)), const((HG_W, D_MODEL)), const((ATT_QW, D_MODEL)), const((D_MODEL, D_MODEL)),
                  const((1, D_MODEL)), const((1, D_MODEL)), const((40, D_MODEL)), const((40, 1)),
                  const((tm, tm))],
        out_specs=[row(D_MODEL), row(D_MODEL // 2), lanes, lanes, lanes, const((N_EXPERTS, 128))],
        out_shape=[jax.ShapeDtypeStruct((T, D_MODEL), F32),
                   jax.ShapeDtypeStruct((T, D_MODEL // 2), jnp.uint32),
                   jax.ShapeDtypeStruct((2, T), jnp.int32),
                   jax.ShapeDtypeStruct((2, T), F32),
                   jax.ShapeDtypeStruct((2, T), jnp.int32),
                   jax.ShapeDtypeStruct((N_EXPERTS, 128), F32)],
        scratch_shapes=[pltpu.VMEM((N_EXPERTS, 1), F32), pltpu.VMEM((tm, D_MODEL), BF16)],
        compiler_params=_cparams(("arbitrary",)),
        name="merge",
    )(hg, att, x2, wgate, wa, wb, wo, g1, b1, wr, br, tri)


SC_WINDOW = 64
SC_IDX_LANES = 128


def _pad_indices(idx):
    rows = idx.reshape(-1, SC_WINDOW)
    return jnp.pad(rows, ((0, 0), (0, SC_IDX_LANES - SC_WINDOW)))


def _sc_mesh():
    return plsc.VectorSubcoreMesh(core_axis_name="core", subcore_axis_name="subcore")


def _sc_scatter2(x, idx_a, idx_b, n_out):
    T, d = x.shape

    @pl.kernel(out_type=jax.ShapeDtypeStruct((n_out, d), x.dtype), mesh=_sc_mesh())
    def scatter(x_hbm, ia_hbm, ib_hbm, o_hbm):
        def body(x_vmem, ia_vmem, ib_vmem):
            pltpu.sync_copy(x_vmem, o_hbm.at[ia_vmem.at[0, pl.ds(0, SC_WINDOW)]])
            pltpu.sync_copy(x_vmem, o_hbm.at[ib_vmem.at[0, pl.ds(0, SC_WINDOW)]])

        idx_spec = pl.BlockSpec((1, SC_IDX_LANES), lambda i: (i, 0))
        pltpu.emit_pipeline(
            body, grid=(T // SC_WINDOW,),
            in_specs=[pl.BlockSpec((SC_WINDOW, d), lambda i: (i, 0)), idx_spec, idx_spec],
            out_specs=[],
            core_axis_name=("core", "subcore"),
            dimension_semantics=(pltpu.PARALLEL,),
        )(x_hbm, ia_hbm, ib_hbm)

    return scatter(x, _pad_indices(idx_a), _pad_indices(idx_b))


def _sc_gather(x, idx):
    n = idx.shape[0]
    d = x.shape[1]

    @pl.kernel(out_type=jax.ShapeDtypeStruct((n, d), x.dtype), mesh=_sc_mesh())
    def gather(x_hbm, i_hbm, o_hbm):
        def body(i_vmem, o_vmem):
            pltpu.sync_copy(x_hbm.at[i_vmem.at[0, pl.ds(0, SC_WINDOW)]], o_vmem)

        pltpu.emit_pipeline(
            body, grid=(n // SC_WINDOW,),
            in_specs=[pl.BlockSpec((1, SC_IDX_LANES), lambda i: (i, 0))],
            out_specs=[pl.BlockSpec((SC_WINDOW, d), lambda i: (i, 0))],
            core_axis_name=("core", "subcore"),
            dimension_semantics=(pltpu.PARALLEL,),
        )(i_hbm, o_hbm)

    return gather(x, _pad_indices(idx))


def _expert_kernel(be_ref, nv_ref, nxt_ref, par_ref, xb_ref, wg_hbm, wu_hbm, wd_hbm, y_ref,
                   wg_st, wu_st, wd_st, wg_bf, wu_bf, wd_bf, sem):
    i = pl.program_id(0)
    nv = nv_ref[i]

    def fetch(expert, slot):
        return [pltpu.make_async_copy(src.at[expert], dst.at[slot], sem.at[slot, j])
                for j, (src, dst) in enumerate(((wg_hbm, wg_st), (wu_hbm, wu_st), (wd_hbm, wd_st)))]

    @pl.when(i == 0)
    def _():
        for cp in fetch(be_ref[0], par_ref[0]):
            cp.start()

    @pl.when((i == 0) | (be_ref[i] != be_ref[jnp.maximum(i - 1, 0)]))
    def _():
        slot = par_ref[i]
        for cp in fetch(be_ref[i], slot):
            cp.wait()
        wg_bf[...] = wg_st[slot].astype(BF16)
        wu_bf[...] = wu_st[slot].astype(BF16)
        wd_bf[...] = wd_st[slot].astype(BF16)

        @pl.when(nxt_ref[i] >= 0)
        def _():
            for cp in fetch(nxt_ref[i], 1 - slot):
                cp.start()

    @pl.when(nv > 0)
    def _():
        row = lax.broadcasted_iota(jnp.int32, (MOE_BLOCK, 1), 0)
        xb = _unpack_bf16_pairs(jnp.where(row < nv, xb_ref[...], jnp.uint32(0))).astype(BF16)
        g = _dot(xb, wg_bf[...])
        u = _dot(xb, wu_bf[...])
        h = (_silu(g) * u).astype(BF16)
        y_ref[...] = _pack_bf16_pairs(_dot(h, wd_bf[...]).astype(BF16))


def _experts(block_meta, xbuf, w_gate, w_up, w_down):
    P = xbuf.shape[0]
    nb = P // MOE_BLOCK
    rows = pl.BlockSpec((MOE_BLOCK, D_MODEL // 2), lambda i, *_: (i, 0))
    hbm = pl.BlockSpec(memory_space=pl.ANY)
    return pl.pallas_call(
        _expert_kernel,
        grid_spec=pltpu.PrefetchScalarGridSpec(
            num_scalar_prefetch=4,
            grid=(nb,),
            in_specs=[rows, hbm, hbm, hbm],
            out_specs=rows,
            scratch_shapes=[pltpu.VMEM((2, D_MODEL, D_EXPERT), F32), pltpu.VMEM((2, D_MODEL, D_EXPERT), F32),
                            pltpu.VMEM((2, D_EXPERT, D_MODEL), F32),
                            pltpu.VMEM((D_MODEL, D_EXPERT), BF16), pltpu.VMEM((D_MODEL, D_EXPERT), BF16),
                            pltpu.VMEM((D_EXPERT, D_MODEL), BF16),
                            pltpu.SemaphoreType.DMA((2, 3))]),
        out_shape=jax.ShapeDtypeStruct((P, D_MODEL // 2), jnp.uint32),
        compiler_params=_cparams(("arbitrary",)),
        name="experts",
    )(*block_meta, xbuf, w_gate, w_up, w_down)


def _combine_kernel(ra_ref, rb_ref, x1_ref, wt_ref, g2_ref, b2_ref, *rest):
    o_ref = rest[-1]
    w2 = wt_ref[...]
    w = jnp.concatenate([w2, jnp.zeros((6, w2.shape[1]), F32)], axis=0).T
    y = w[:, 0:1] * _unpack_bf16_pairs(ra_ref[...]) + w[:, 1:2] * _unpack_bf16_pairs(rb_ref[...])
    o_ref[...] = _layer_norm(DN_ALPHA * x1_ref[...] + y, g2_ref[...], b2_ref[...])


def _combine(rows2, x1, wt, g2, b2, tm, out_prev, tile0, t_total):
    T = x1.shape[0]
    nt = T // tm
    const = lambda shape: pl.BlockSpec(shape, lambda i: (0,) * len(shape))
    in_specs = [pl.BlockSpec((tm, D_MODEL // 2), lambda i: (i, 0)),
                pl.BlockSpec((tm, D_MODEL // 2), lambda i: (i + nt, 0)),
                pl.BlockSpec((tm, D_MODEL), lambda i: (i, 0)), pl.BlockSpec((2, tm), lambda i: (0, i)),
                const((1, D_MODEL)), const((1, D_MODEL))]
    args = [rows2, rows2, x1, wt, g2, b2]
    aliases = {}
    if out_prev is not None:
        in_specs.append(pl.BlockSpec(memory_space=pl.ANY))
        args.append(out_prev)
        aliases = {len(args) - 1: 0}
    return pl.pallas_call(
        _combine_kernel,
        grid=(nt,),
        in_specs=in_specs,
        out_specs=pl.BlockSpec((tm, D_MODEL), lambda i: (i + tile0, 0)),
        out_shape=jax.ShapeDtypeStruct((t_total, D_MODEL), F32),
        input_output_aliases=aliases,
        compiler_params=_cparams(("arbitrary",)),
        name="combine",
    )(*args)


def _slot_layout(eid, rank, counts, T):
    P = 2 * T + N_EXPERTS * MOE_BLOCK
    nb = P // MOE_BLOCK
    padded = ((counts + MOE_BLOCK - 1) // MOE_BLOCK) * MOE_BLOCK
    pend = jnp.cumsum(padded)
    pstart = pend - padded
    experts = jnp.arange(N_EXPERTS, dtype=jnp.int32)
    pos = rank + jnp.sum(jnp.where(eid[:, :, None] == experts, pstart, 0), axis=-1)

    block_start = jnp.arange(nb, dtype=jnp.int32) * MOE_BLOCK
    block_expert = jnp.minimum(jnp.sum(block_start[:, None] >= pend[None, :], axis=1), N_EXPERTS - 1)
    onehot = block_expert[:, None] == experts[None, :]
    look = lambda table: jnp.sum(jnp.where(onehot, table[None, :], 0), axis=1)
    block_valid = jnp.clip(look(pstart + counts) - block_start, 0, MOE_BLOCK)
    block_valid = jnp.where(block_start < pend[-1], block_valid, 0)
    present = (counts > 0) | (experts == N_EXPERTS - 1)
    later = present[None, :] & (experts[None, :] > experts[:, None])
    next_present = jnp.min(jnp.where(later, experts[None, :], N_EXPERTS), axis=1)
    next_present = jnp.where(next_present < N_EXPERTS, next_present, -1)
    runs_before = jnp.sum(present[None, :] & (experts[None, :] < experts[:, None]), axis=1)
    block_next = look(next_present)
    block_slot = look(runs_before % 2)
    i32 = lambda a: a.astype(jnp.int32)
    return i32(pos), (i32(block_expert), i32(block_valid), i32(block_next), i32(block_slot))


def kernel(x, lb_logits, w_in, hg_norm_w, sinks, w_branch_a, w_branch_b, w_out, ln1_g, ln1_b,
           router_group_w, router_group_b, router_expert_w, router_expert_b,
           w_exp_gate, w_exp_up, w_exp_down, ln2_g, ln2_b):
    B, S, D = x.shape
    assert D == D_MODEL and S % ATT_BLK == 0 and w_in.shape[0] == DEPTH == 1
    lb_all = jnp.cumsum(jax.nn.softmax(lb_logits.astype(F32), axis=0), axis=0)
    lb = lb_all[0].reshape(1, HG_W)
    w0 = w_in[0]
    w_f = jnp.concatenate([w0[:, :2 * HG_W], w0[:, 3 * HG_W:4 * HG_W]], axis=1).astype(BF16)
    w_b = jnp.concatenate([w0[:, 2 * HG_W:3 * HG_W], w0[:, 4 * HG_W:PROJ_W]], axis=1).astype(BF16)
    w_gate = w0[:, PROJ_W:].astype(BF16)
    nw = hg_norm_w[0].reshape(1, HG_DV).astype(F32)
    wa, wb, wo = w_branch_a[0].astype(BF16), w_branch_b[0].astype(BF16), w_out[0].astype(BF16)
    wr = jnp.zeros((40, D), F32).at[0:N_GROUPS].set(router_group_w[0].T).at[8:40].set(router_expert_w[0].T)
    br = jnp.zeros((40, 1), F32).at[0:N_GROUPS, 0].set(router_group_b[0]).at[8:40, 0].set(router_expert_b[0])
    g1, b1 = ln1_g[0].reshape(1, D), ln1_b[0].reshape(1, D)
    g2, b2 = ln2_g[0].reshape(1, D), ln2_b[0].reshape(1, D)

    n_parts = N_PARTS if B % N_PARTS == 0 else 1
    bp = B // n_parts
    tp = bp * S
    tm = 256 if tp % 256 == 0 else ATT_BLK
    ts = 512 if S % 512 == 0 else ATT_BLK
    x2 = x.reshape(B * S, D)
    out = None
    for part in range(n_parts):
        proj_f, proj_b = _proj(x2, w_f, w_b, tm, part * tp, tp)
        hg, att = _mixers(proj_f, proj_b, lb, nw, sinks[0].astype(F32), bp, S)
        x1, xp, eid, wt, rank, cnt = _merge(hg, att, x2, w_gate, wa, wb, wo, g1, b1, wr.astype(BF16), br,
                                            512 if tp % 512 == 0 else tm, part * tp)
        pos, block_meta = _slot_layout(eid, rank, cnt[:, 0].astype(jnp.int32), tp)
        xbuf = _sc_scatter2(xp, pos[0], pos[1], 2 * tp + N_EXPERTS * MOE_BLOCK)
        ybuf = _experts(block_meta, xbuf, w_exp_gate[0], w_exp_up[0], w_exp_down[0])
        rows2 = _sc_gather(ybuf, pos.reshape(-1))
        out = _combine(rows2, x1, wt, g2, b2, tm, out, part * (tp // tm), B * S)
    return out.reshape(B, S, D)
```

```python
import functools

import numpy as np
import jax
import jax.numpy as jnp
from jax import lax
from jax.experimental import pallas as pl
from jax.experimental.pallas import tpu as pltpu
from jax.experimental.pallas import tpu_sc as plsc

F32 = jnp.float32
BF16 = jnp.bfloat16

D_MODEL = 1024
DEPTH = 1
HG_HEADS = 4
HG_DK = 128
HG_DV = 128
HG_W = HG_HEADS * HG_DK
CHUNK = 64
ATT_Q_HEADS = 8
ATT_KV_HEADS = 2
ATT_GROUP = ATT_Q_HEADS // ATT_KV_HEADS
ATT_HD = 64
ATT_QW = ATT_Q_HEADS * ATT_HD
ATT_KVW = ATT_KV_HEADS * ATT_HD
ATT_BLK = 128
N_GROUPS = 4
EPG = 8
N_EXPERTS = N_GROUPS * EPG
D_EXPERT = 512
MOE_BLOCK = 256
DN_ALPHA = (2.0 * DEPTH) ** 0.25
LN_EPS = 1e-5
RMS_EPS = 1e-6
NEG_INF = -1e30

PROJ_W = 4 * HG_W + ATT_QW + 2 * ATT_KVW
PROJ_F = 3 * HG_W
PROJ_B = HG_W + ATT_QW + 2 * ATT_KVW
N_LEVELS = 6
N_ARG_GROUPS = N_LEVELS + 2
HG_SAFE_LB = 0.125

VMEM_LIMIT = 56 * 1024 * 1024
MERGE_SUB = 256
MIX_TILE = 512
N_PARTS = 2


def _cparams(sem):
    return pltpu.CompilerParams(dimension_semantics=sem, vmem_limit_bytes=VMEM_LIMIT)


def _dot(a, b):
    return jnp.dot(a, b, preferred_element_type=F32)


def _dot_nt(a, b):
    return lax.dot_general(a, b, (((1,), (1,)), ((), ())), preferred_element_type=F32)


def _dot_tn(a, b):
    return lax.dot_general(a, b, (((0,), (0,)), ((), ())), preferred_element_type=F32)


def _sigmoid(x):
    return 0.5 * jnp.tanh(0.5 * x) + 0.5


def _silu(x):
    return x * _sigmoid(x)


def _pack_bf16_pairs(xb):
    n = xb.shape[1] // 2
    lo = lax.bitcast_convert_type(xb[:, :n].astype(F32), jnp.uint32)
    hi = lax.bitcast_convert_type(xb[:, n:].astype(F32), jnp.uint32)
    return (lo >> 16) | hi


def _unpack_bf16_pairs(w):
    lo = lax.bitcast_convert_type(w << 16, F32)
    hi = lax.bitcast_convert_type(w & jnp.uint32(0xFFFF0000), F32)
    return jnp.concatenate([lo, hi], axis=1)


def _layer_norm(z, g, b):
    mu = jnp.mean(z, axis=-1, keepdims=True)
    zc = z - mu
    var = jnp.mean(zc * zc, axis=-1, keepdims=True)
    return zc * lax.rsqrt(var + LN_EPS) * g + b


def _proj_kernel(x_ref, wf_ref, wb_ref, of_ref, ob_ref):
    xb = x_ref[...].astype(BF16)
    of_ref[...] = _dot(xb, wf_ref[...])
    ob_ref[...] = _dot(xb, wb_ref[...]).astype(BF16)


def _proj(x2, w_f, w_b, tm, row0, T):
    tile0 = row0 // tm
    return pl.pallas_call(
        _proj_kernel,
        grid=(T // tm,),
        in_specs=[pl.BlockSpec((tm, D_MODEL), lambda i: (i + tile0, 0)),
                  pl.BlockSpec((D_MODEL, PROJ_F), lambda i: (0, 0)),
                  pl.BlockSpec((D_MODEL, PROJ_B), lambda i: (0, 0))],
        out_specs=[pl.BlockSpec((tm, PROJ_F), lambda i: (i, 0)), pl.BlockSpec((tm, PROJ_B), lambda i: (i, 0))],
        out_shape=[jax.ShapeDtypeStruct((T, PROJ_F), F32), jax.ShapeDtypeStruct((T, PROJ_B), BF16)],
        compiler_params=_cparams(("arbitrary",)),
        name="proj",
    )(x2, w_f, w_b)


def _hgrn_tables():
    C = CHUNK
    w = np.zeros((N_ARG_GROUPS, C, C), np.float32)
    masks = np.zeros((N_LEVELS, C, C), np.float32)
    for lvl in range(N_LEVELS):
        h = 1 << lvl
        for t in range(C):
            base = (t // (2 * h)) * 2 * h
            m = base + h - 1
            if (t // h) % 2 == 1:
                w[lvl, t, m + 1:t + 1] = 1.0
                masks[lvl, t, base:base + h] = 1.0
            else:
                w[lvl, t, t + 1:m + 1] = 1.0
    for t in range(C):
        w[N_LEVELS, t, :t + 1] = 1.0
        w[N_LEVELS + 1, t, t + 1:] = 1.0
    w = w.reshape(N_ARG_GROUPS * C, C)
    return np.concatenate([w, w], axis=1), masks


def _hgrn_chunk(q, fz, v, gate, c0, c1, nw, seg, mask_ref, state_ref, bounded):
    h = 0.5 * q
    qf = h + h * jnp.tanh(h)
    t1 = c1 * jnp.tanh(0.5 * fz)
    f = c0 + t1
    k = c1 - t1
    l2 = jnp.log2(f)
    v_bf = v.astype(BF16)
    qf_bf = qf.astype(BF16)
    k_bf = k.astype(BF16)

    l_hi = l2.astype(BF16)
    l_lo = (l2 - l_hi.astype(F32)).astype(BF16)
    l_split = jnp.concatenate([l_hi, l_lo], axis=0)
    row = lax.broadcasted_iota(jnp.int32, (CHUNK, 1), 0)
    if bounded:
        cum = _dot(seg[N_LEVELS * CHUNK:(N_LEVELS + 1) * CHUNK, :], l_split)
        mid = cum[CHUNK // 2 - 1:CHUNK // 2, :]
        last = cum[CHUNK - 1:CHUNK, :]
        e_cum = jnp.exp2(cum)
        e_suf = jnp.exp2(last - cum)
        a_q = qf_bf * jnp.exp2(cum - mid).astype(BF16)
        a_k = k_bf * jnp.exp2(mid - cum).astype(BF16)
        causal = row >= lax.broadcasted_iota(jnp.int32, (1, CHUNK), 1)
        scores = [jnp.where(causal, _dot_nt(a_q[:, hd * HG_DK:(hd + 1) * HG_DK], a_k[:, hd * HG_DK:(hd + 1) * HG_DK]),
                            0.0) for hd in range(HG_HEADS)]
        diag_v = None
    else:
        e = jnp.exp2(_dot(seg, l_split))
        scores = [jnp.zeros((CHUNK, CHUNK), F32) for _ in range(HG_HEADS)]
        for lvl in range(N_LEVELS):
            half = 1 << lvl
            e_l = e[lvl * CHUNK:(lvl + 1) * CHUNK, :].astype(BF16)
            if half >= 16:
                sel = jnp.concatenate([(qf_bf if (r0 // half) % 2 else k_bf)[r0:r0 + half]
                                       for r0 in range(0, CHUNK, half)], axis=0)
            else:
                sel = jnp.where((row // half) % 2 == 1, qf_bf, k_bf)
            a = sel * e_l
            m = mask_ref[lvl]
            for hd in range(HG_HEADS):
                a_h = a[:, hd * HG_DK:(hd + 1) * HG_DK]
                scores[hd] = scores[hd] + m * _dot_nt(a_h, a_h)
        e_cum = e[N_LEVELS * CHUNK:(N_LEVELS + 1) * CHUNK, :]
        e_suf = e[(N_LEVELS + 1) * CHUNK:(N_LEVELS + 2) * CHUNK, :]
        diag_v = qf * k

    q_in = qf_bf * e_cum.astype(BF16)
    k_out = k_bf * e_suf.astype(BF16)
    e_last = e_cum[CHUNK - 1:CHUNK, :]

    outs = []
    for hd in range(HG_HEADS):
        cols = slice(hd * HG_DK, (hd + 1) * HG_DK)
        st = state_ref[hd]
        o = _dot_nt(q_in[:, cols], st.astype(BF16)) + _dot(scores[hd].astype(BF16), v_bf[:, cols])
        if diag_v is not None:
            o = o + jnp.sum(diag_v[:, cols], axis=-1, keepdims=True) * v[:, cols]
        state_ref[hd] = st * e_last[:, cols] + _dot_tn(v_bf[:, cols], k_out[:, cols])
        o = o * lax.rsqrt(jnp.mean(o * o, axis=-1, keepdims=True) + RMS_EPS) * nw
        outs.append(o)
    hg = 0.5 * gate
    return jnp.concatenate(outs, axis=1) * (hg + hg * jnp.tanh(hg))


def _attn_bias():
    r = np.arange(ATT_BLK)[:, None]
    c = np.arange(2 * ATT_BLK)[None, :]
    dist = r + ATT_BLK - c
    window = (dist >= 0) & (dist < ATT_BLK)
    slopes = np.exp2(-8.0 * (np.arange(ATT_Q_HEADS, dtype=np.float32) + 1.0) / ATT_Q_HEADS).astype(np.float32)
    alibi = -slopes[:, None, None] * dist.astype(np.float32)[None]
    later = np.where(window[None], alibi, np.float32(NEG_INF))
    first = np.where((window & (c >= ATT_BLK))[None], alibi, np.float32(NEG_INF))
    return np.stack([later, first]).astype(np.float32)


def _attn_block(q_ref, kv_cur, kv_prev, bias_ref, table, sink_ref, o_ref):
    lane = lax.broadcasted_iota(jnp.int32, (2 * ATT_BLK, 2 * ATT_KVW), 1)
    lo = (lane % ATT_KVW) < ATT_HD

    kv = jnp.concatenate([kv_prev, kv_cur], axis=0).astype(BF16)
    kv_sw = jnp.concatenate([kv[:, ATT_HD:ATT_KVW], kv[:, :ATT_HD],
                             kv[:, ATT_KVW + ATT_HD:], kv[:, ATT_KVW:ATT_KVW + ATT_HD]], axis=1)
    zero = jnp.zeros_like(kv)
    placed = {}
    for h in range(ATT_KV_HEADS):
        for off in range(2):
            src = kv if h == off else kv_sw
            placed[h, off] = jnp.where(lo if off == 0 else jnp.logical_not(lo), src, zero)

    scale = ATT_HD ** -0.5
    for pair in range(ATT_Q_HEADS // 2):
        qp = (q_ref[:, pair * 2 * ATT_HD:(pair + 1) * 2 * ATT_HD] * scale).astype(BF16)
        acc = jnp.zeros((ATT_BLK, 2 * ATT_HD), F32)
        for off in range(2):
            j = 2 * pair + off
            kvh = placed[j // ATT_GROUP, off]
            sink = sink_ref[j]
            logits = _dot_nt(qp, kvh[:, :ATT_KVW]) + bias_ref[table, j]
            m = jnp.maximum(jnp.max(logits, axis=-1, keepdims=True), sink)
            p = jnp.exp(logits - m)
            den = jnp.sum(p, axis=-1, keepdims=True) + jnp.exp(sink - m)
            acc = acc + _dot(p.astype(BF16), kvh[:, ATT_KVW:]) / den
        o_ref[:, pair * 2 * ATT_HD:(pair + 1) * 2 * ATT_HD] = acc.astype(o_ref.dtype)


def _mixers_kernel(sink_ref, bounded_ref, q_ref, f_ref, i_ref, g_ref, aq_ref, kv_ref, kvp_ref, lb_ref, nw_ref,
                   seg_ref, mask_ref, bias_ref, hg_ref, at_ref, state_ref):
    first = pl.program_id(1) == 0

    @pl.when(first)
    def _():
        state_ref[...] = jnp.zeros_like(state_ref)

    def tile_body(bounded):
        lb = lb_ref[...]
        c0 = 0.5 + 0.5 * lb
        c1 = 0.5 - 0.5 * lb
        nw = nw_ref[...]
        seg = seg_ref[...]
        tile = q_ref.shape[0]
        for r0 in range(0, tile, CHUNK):
            rows = slice(r0, r0 + CHUNK)
            o = _hgrn_chunk(q_ref[rows, :], f_ref[rows, :], i_ref[rows, :], g_ref[rows, :], c0, c1, nw, seg,
                            mask_ref, state_ref, bounded)
            hg_ref[rows, :] = o.astype(hg_ref.dtype)
        for r0 in range(0, tile, ATT_BLK):
            rows = slice(r0, r0 + ATT_BLK)
            if r0 == 0:
                prev, table = kvp_ref[...], jnp.where(first, 1, 0)
            else:
                prev, table = kv_ref[r0 - ATT_BLK:r0, :], 0
            _attn_block(aq_ref.at[rows, :], kv_ref[rows, :], prev, bias_ref, table, sink_ref, at_ref.at[rows, :])

    pl.when(bounded_ref[0] == 1)(lambda: tile_body(True))
    pl.when(bounded_ref[0] != 1)(lambda: tile_body(False))


def _mixers(proj_f, proj_b, lb, norm_w, sinks, B, S):
    T = B * S
    tile = MIX_TILE if S % MIX_TILE == 0 else ATT_BLK
    nb = S // tile
    per = tile // ATT_BLK
    seg, masks = _hgrn_tables()
    const = lambda shape: pl.BlockSpec(shape, lambda b, n: (0,) * len(shape))
    hcol = lambda cb: pl.BlockSpec((tile, HG_W), lambda b, n: (b * nb + n, cb))
    kvcol = (HG_W + ATT_QW) // (2 * ATT_KVW)
    out = pl.BlockSpec((tile, HG_W), lambda b, n: (b * nb + n, 0))
    return pl.pallas_call(
        _mixers_kernel,
        grid=(B, nb),
        in_specs=[pl.BlockSpec(memory_space=pltpu.SMEM), pl.BlockSpec(memory_space=pltpu.SMEM),
                  hcol(0), hcol(1), hcol(0), hcol(2),
                  pl.BlockSpec((tile, ATT_QW), lambda b, n: (b * nb + n, HG_W // ATT_QW)),
                  pl.BlockSpec((tile, 2 * ATT_KVW), lambda b, n: (b * nb + n, kvcol)),
                  pl.BlockSpec((ATT_BLK, 2 * ATT_KVW),
                               lambda b, n: ((b * nb + n) * per - jnp.where(n > 0, 1, 0), kvcol)),
                  const((1, HG_W)), const((1, HG_DV)), const((N_ARG_GROUPS * CHUNK, 2 * CHUNK)),
                  const((N_LEVELS, CHUNK, CHUNK)), const((2, ATT_Q_HEADS, ATT_BLK, 2 * ATT_BLK))],
        out_specs=[out, out],
        out_shape=[jax.ShapeDtypeStruct((T, HG_W), BF16), jax.ShapeDtypeStruct((T, ATT_QW), BF16)],
        scratch_shapes=[pltpu.VMEM((HG_HEADS, HG_DV, HG_DK), F32)],
        compiler_params=_cparams(("arbitrary", "arbitrary")),
        name="mixers",
    )(sinks, (jnp.min(lb) >= HG_SAFE_LB).astype(jnp.int32).reshape(1),
      proj_f, proj_f, proj_b, proj_f, proj_b, proj_b, proj_b, lb, norm_w, jnp.asarray(seg, BF16),
      jnp.asarray(masks), jnp.asarray(_attn_bias()))


def _merge_kernel(hg_ref, at_ref, x_ref, wgate_ref, wa_ref, wb_ref, wo_ref, g1_ref, b1_ref,
                  wr_ref, br_ref, tri_ref, x1_ref, xp_ref, eid_ref, wt_ref, rank_ref, cnt_ref, base_ref, x1b_ref):
    tm = x_ref.shape[0]
    sub = min(tm, MERGE_SUB)
    for r0 in range(0, tm, sub):
        rows = slice(r0, r0 + sub)
        xs = x_ref[rows, :]
        gates = _dot(xs.astype(BF16), wgate_ref[...])
        ya = _dot(hg_ref[rows, :], wa_ref[...])
        yb = _dot(at_ref[rows, :], wb_ref[...])
        merged = _sigmoid(gates[:, :D_MODEL]) * ya + _sigmoid(gates[:, D_MODEL:]) * yb
        z = DN_ALPHA * xs + _dot(merged.astype(BF16), wo_ref[...])
        x1 = _layer_norm(z, g1_ref[...], b1_ref[...])
        x1_ref[rows, :] = x1
        x1b = x1.astype(BF16)
        x1b_ref[rows, :] = x1b
        xp_ref[rows, :] = _pack_bf16_pairs(x1b)

    lg = _dot_nt(wr_ref[...], x1b_ref[...]) + br_ref[...]
    g = lg[0:8, :]
    row8 = lax.broadcasted_iota(jnp.int32, (8, tm), 0)
    g = jnp.where(row8 < N_GROUPS, g, -jnp.inf)
    gmax = jnp.max(g, axis=0, keepdims=True)
    gsel = jnp.min(jnp.where(g == gmax, row8, 8), axis=0, keepdims=True)
    gw = 1.0 / jnp.sum(jnp.exp(g - gmax), axis=0, keepdims=True)
    el = jnp.where(gsel == 0, lg[8:16, :],
                   jnp.where(gsel == 1, lg[16:24, :], jnp.where(gsel == 2, lg[24:32, :], lg[32:40, :])))
    v1 = jnp.max(el, axis=0, keepdims=True)
    i1 = jnp.min(jnp.where(el == v1, row8, 8), axis=0, keepdims=True)
    el2 = jnp.where(row8 == i1, -jnp.inf, el)
    v2 = jnp.max(el2, axis=0, keepdims=True)
    i2 = jnp.min(jnp.where(el2 == v2, row8, 8), axis=0, keepdims=True)
    e2 = jnp.exp(v2 - v1)
    den = 1.0 + e2
    e_a = gsel * EPG + i1
    e_b = gsel * EPG + i2
    eid_ref[...] = jnp.concatenate([e_a, e_b], axis=0)
    wt_ref[...] = jnp.concatenate([gw / den, gw * e2 / den], axis=0)

    @pl.when(pl.program_id(0) == 0)
    def _():
        base_ref[...] = jnp.zeros_like(base_ref)

    row_e = lax.broadcasted_iota(jnp.int32, (N_EXPERTS, tm), 0)
    oh_a = jnp.where(row_e == e_a, 1.0, 0.0)
    oh_b = jnp.where(row_e == e_b, 1.0, 0.0)
    tri = tri_ref[...]
    pre_a = _dot(oh_a.astype(BF16), tri)
    pre_b = _dot(oh_b.astype(BF16), tri)
    cnt_a = jnp.sum(oh_a, axis=1, keepdims=True)
    cnt_b = jnp.sum(oh_b, axis=1, keepdims=True)
    base = base_ref[...]
    rank_a = jnp.sum(oh_a * (base + pre_a), axis=0, keepdims=True)
    rank_b = jnp.sum(oh_b * (base + cnt_a + pre_b), axis=0, keepdims=True)
    rank_ref[...] = jnp.concatenate([rank_a, rank_b], axis=0).astype(jnp.int32)
    base = base + cnt_a + cnt_b
    base_ref[...] = base
    cnt_ref[...] = jnp.broadcast_to(base, cnt_ref.shape)


def _merge(hg, att, x2, wgate, wa, wb, wo, g1, b1, wr, br, tm, row0):
    T = hg.shape[0]
    tile0 = row0 // tm
    row = lambda w: pl.BlockSpec((tm, w), lambda i: (i, 0))
    const = lambda shape: pl.BlockSpec(shape, lambda i: (0,) * len(shape))
    lanes = pl.BlockSpec((2, tm), lambda i: (0, i))
    tri = jnp.asarray(np.triu(np.ones((tm, tm), np.float32), 1), BF16)
    return pl.pallas_call(
        _merge_kernel,
        grid=(T // tm,),
        in_specs=[row(HG_W), row(ATT_QW),
                  pl.BlockSpec((tm, D_MODEL), lambda i: (i + tile0, 0)),
                  const((D_MODEL, 2 * D_MODEL)), const((HG_W, D_MODEL)), const((ATT_QW, D_MODEL)), const((D_MODEL, D_MODEL)),
                  const((1, D_MODEL)), const((1, D_MODEL)), const((40, D_MODEL)), const((40, 1)),
                  const((tm, tm))],
        out_specs=[row(D_MODEL), row(D_MODEL // 2), lanes, lanes, lanes, const((N_EXPERTS, 128))],
        out_shape=[jax.ShapeDtypeStruct((T, D_MODEL), F32),
                   jax.ShapeDtypeStruct((T, D_MODEL // 2), jnp.uint32),
                   jax.ShapeDtypeStruct((2, T), jnp.int32),
                   jax.ShapeDtypeStruct((2, T), F32),
                   jax.ShapeDtypeStruct((2, T), jnp.int32),
                   jax.ShapeDtypeStruct((N_EXPERTS, 128), F32)],
        scratch_shapes=[pltpu.VMEM((N_EXPERTS, 1), F32), pltpu.VMEM((tm, D_MODEL), BF16)],
        compiler_params=_cparams(("arbitrary",)),
        name="merge",
    )(hg, att, x2, wgate, wa, wb, wo, g1, b1, wr, br, tri)


SC_WINDOW = 64
SC_IDX_LANES = 128


def _pad_indices(idx):
    rows = idx.reshape(-1, SC_WINDOW)
    return jnp.pad(rows, ((0, 0), (0, SC_IDX_LANES - SC_WINDOW)))


def _sc_mesh():
    return plsc.VectorSubcoreMesh(core_axis_name="core", subcore_axis_name="subcore")


def _sc_scatter2(x, idx_a, idx_b, n_out):
    T, d = x.shape

    @pl.kernel(out_type=jax.ShapeDtypeStruct((n_out, d), x.dtype), mesh=_sc_mesh())
    def scatter(x_hbm, ia_hbm, ib_hbm, o_hbm):
        def body(x_vmem, ia_vmem, ib_vmem):
            pltpu.sync_copy(x_vmem, o_hbm.at[ia_vmem.at[0, pl.ds(0, SC_WINDOW)]])
            pltpu.sync_copy(x_vmem, o_hbm.at[ib_vmem.at[0, pl.ds(0, SC_WINDOW)]])

        idx_spec = pl.BlockSpec((1, SC_IDX_LANES), lambda i: (i, 0))
        pltpu.emit_pipeline(
            body, grid=(T // SC_WINDOW,),
            in_specs=[pl.BlockSpec((SC_WINDOW, d), lambda i: (i, 0)), idx_spec, idx_spec],
            out_specs=[],
            core_axis_name=("core", "subcore"),
            dimension_semantics=(pltpu.PARALLEL,),
        )(x_hbm, ia_hbm, ib_hbm)

    return scatter(x, _pad_indices(idx_a), _pad_indices(idx_b))


def _sc_gather(x, idx):
    n = idx.shape[0]
    d = x.shape[1]

    @pl.kernel(out_type=jax.ShapeDtypeStruct((n, d), x.dtype), mesh=_sc_mesh())
    def gather(x_hbm, i_hbm, o_hbm):
        def body(i_vmem, o_vmem):
            pltpu.sync_copy(x_hbm.at[i_vmem.at[0, pl.ds(0, SC_WINDOW)]], o_vmem)

        pltpu.emit_pipeline(
            body, grid=(n // SC_WINDOW,),
            in_specs=[pl.BlockSpec((1, SC_IDX_LANES), lambda i: (i, 0))],
            out_specs=[pl.BlockSpec((SC_WINDOW, d), lambda i: (i, 0))],
            core_axis_name=("core", "subcore"),
            dimension_semantics=(pltpu.PARALLEL,),
        )(i_hbm, o_hbm)

    return gather(x, _pad_indices(idx))


def _expert_kernel(be_ref, nv_ref, nxt_ref, par_ref, xb_ref, wg_hbm, wu_hbm, wd_hbm, y_ref,
                   wg_st, wu_st, wd_st, wg_bf, wu_bf, wd_bf, sem):
    i = pl.program_id(0)
    nv = nv_ref[i]

    def fetch(expert, slot):
        return [pltpu.make_async_copy(src.at[expert], dst.at[slot], sem.at[slot, j])
                for j, (src, dst) in enumerate(((wg_hbm, wg_st), (wu_hbm, wu_st), (wd_hbm, wd_st)))]

    @pl.when(i == 0)
    def _():
        for cp in fetch(be_ref[0], par_ref[0]):
            cp.start()

    @pl.when((i == 0) | (be_ref[i] != be_ref[jnp.maximum(i - 1, 0)]))
    def _():
        slot = par_ref[i]
        for cp in fetch(be_ref[i], slot):
            cp.wait()
        wg_bf[...] = wg_st[slot].astype(BF16)
        wu_bf[...] = wu_st[slot].astype(BF16)
        wd_bf[...] = wd_st[slot].astype(BF16)

        @pl.when(nxt_ref[i] >= 0)
        def _():
            for cp in fetch(nxt_ref[i], 1 - slot):
                cp.start()

    @pl.when(nv > 0)
    def _():
        row = lax.broadcasted_iota(jnp.int32, (MOE_BLOCK, 1), 0)
        xb = _unpack_bf16_pairs(jnp.where(row < nv, xb_ref[...], jnp.uint32(0))).astype(BF16)
        g = _dot(xb, wg_bf[...])
        u = _dot(xb, wu_bf[...])
        h = (_silu(g) * u).astype(BF16)
        y_ref[...] = _pack_bf16_pairs(_dot(h, wd_bf[...]).astype(BF16))


def _experts(block_meta, xbuf, w_gate, w_up, w_down):
    P = xbuf.shape[0]
    nb = P // MOE_BLOCK
    rows = pl.BlockSpec((MOE_BLOCK, D_MODEL // 2), lambda i, *_: (i, 0))
    hbm = pl.BlockSpec(memory_space=pl.ANY)
    return pl.pallas_call(
        _expert_kernel,
        grid_spec=pltpu.PrefetchScalarGridSpec(
            num_scalar_prefetch=4,
            grid=(nb,),
            in_specs=[rows, hbm, hbm, hbm],
            out_specs=rows,
            scratch_shapes=[pltpu.VMEM((2, D_MODEL, D_EXPERT), F32), pltpu.VMEM((2, D_MODEL, D_EXPERT), F32),
                            pltpu.VMEM((2, D_EXPERT, D_MODEL), F32),
                            pltpu.VMEM((D_MODEL, D_EXPERT), BF16), pltpu.VMEM((D_MODEL, D_EXPERT), BF16),
                            pltpu.VMEM((D_EXPERT, D_MODEL), BF16),
                            pltpu.SemaphoreType.DMA((2, 3))]),
        out_shape=jax.ShapeDtypeStruct((P, D_MODEL // 2), jnp.uint32),
        compiler_params=_cparams(("arbitrary",)),
        name="experts",
    )(*block_meta, xbuf, w_gate, w_up, w_down)


def _combine_kernel(ra_ref, rb_ref, x1_ref, wt_ref, g2_ref, b2_ref, *rest):
    o_ref = rest[-1]
    w2 = wt_ref[...]
    w = jnp.concatenate([w2, jnp.zeros((6, w2.shape[1]), F32)], axis=0).T
    y = w[:, 0:1] * _unpack_bf16_pairs(ra_ref[...]) + w[:, 1:2] * _unpack_bf16_pairs(rb_ref[...])
    o_ref[...] = _layer_norm(DN_ALPHA * x1_ref[...] + y, g2_ref[...], b2_ref[...])


def _combine(rows2, x1, wt, g2, b2, tm, out_prev, tile0, t_total):
    T = x1.shape[0]
    nt = T // tm
    const = lambda shape: pl.BlockSpec(shape, lambda i: (0,) * len(shape))
    in_specs = [pl.BlockSpec((tm, D_MODEL // 2), lambda i: (i, 0)),
                pl.BlockSpec((tm, D_MODEL // 2), lambda i: (i + nt, 0)),
                pl.BlockSpec((tm, D_MODEL), lambda i: (i, 0)), pl.BlockSpec((2, tm), lambda i: (0, i)),
                const((1, D_MODEL)), const((1, D_MODEL))]
    args = [rows2, rows2, x1, wt, g2, b2]
    aliases = {}
    if out_prev is not None:
        in_specs.append(pl.BlockSpec(memory_space=pl.ANY))
        args.append(out_prev)
        aliases = {len(args) - 1: 0}
    return pl.pallas_call(
        _combine_kernel,
        grid=(nt,),
        in_specs=in_specs,
        out_specs=pl.BlockSpec((tm, D_MODEL), lambda i: (i + tile0, 0)),
        out_shape=jax.ShapeDtypeStruct((t_total, D_MODEL), F32),
        input_output_aliases=aliases,
        compiler_params=_cparams(("arbitrary",)),
        name="combine",
    )(*args)


def _slot_layout(eid, rank, counts, T):
    P = 2 * T + N_EXPERTS * MOE_BLOCK
    nb = P // MOE_BLOCK
    padded = ((counts + MOE_BLOCK - 1) // MOE_BLOCK) * MOE_BLOCK
    pend = jnp.cumsum(padded)
    pstart = pend - padded
    experts = jnp.arange(N_EXPERTS, dtype=jnp.int32)
    pos = rank + jnp.sum(jnp.where(eid[:, :, None] == experts, pstart, 0), axis=-1)

    block_start = jnp.arange(nb, dtype=jnp.int32) * MOE_BLOCK
    block_expert = jnp.minimum(jnp.sum(block_start[:, None] >= pend[None, :], axis=1), N_EXPERTS - 1)
    onehot = block_expert[:, None] == experts[None, :]
    look = lambda table: jnp.sum(jnp.where(onehot, table[None, :], 0), axis=1)
    block_valid = jnp.clip(look(pstart + counts) - block_start, 0, MOE_BLOCK)
    block_valid = jnp.where(block_start < pend[-1], block_valid, 0)
    present = (counts > 0) | (experts == N_EXPERTS - 1)
    later = present[None, :] & (experts[None, :] > experts[:, None])
    next_present = jnp.min(jnp.where(later, experts[None, :], N_EXPERTS), axis=1)
    next_present = jnp.where(next_present < N_EXPERTS, next_present, -1)
    runs_before = jnp.sum(present[None, :] & (experts[None, :] < experts[:, None]), axis=1)
    block_next = look(next_present)
    block_slot = look(runs_before % 2)
    i32 = lambda a: a.astype(jnp.int32)
    return i32(pos), (i32(block_expert), i32(block_valid), i32(block_next), i32(block_slot))


def kernel(x, lb_logits, w_in, hg_norm_w, sinks, w_branch_a, w_branch_b, w_out, ln1_g, ln1_b,
           router_group_w, router_group_b, router_expert_w, router_expert_b,
           w_exp_gate, w_exp_up, w_exp_down, ln2_g, ln2_b):
    B, S, D = x.shape
    assert D == D_MODEL and S % ATT_BLK == 0 and w_in.shape[0] == DEPTH == 1
    lb_all = jnp.cumsum(jax.nn.softmax(lb_logits.astype(F32), axis=0), axis=0)
    lb = lb_all[0].reshape(1, HG_W)
    w0 = w_in[0]
    w_f = jnp.concatenate([w0[:, :2 * HG_W], w0[:, 3 * HG_W:4 * HG_W]], axis=1).astype(BF16)
    w_b = jnp.concatenate([w0[:, 2 * HG_W:3 * HG_W], w0[:, 4 * HG_W:PROJ_W]], axis=1).astype(BF16)
    w_gate = w0[:, PROJ_W:].astype(BF16)
    nw = hg_norm_w[0].reshape(1, HG_DV).astype(F32)
    wa, wb, wo = w_branch_a[0].astype(BF16), w_branch_b[0].astype(BF16), w_out[0].astype(BF16)
    wr = jnp.zeros((40, D), F32).at[0:N_GROUPS].set(router_group_w[0].T).at[8:40].set(router_expert_w[0].T)
    br = jnp.zeros((40, 1), F32).at[0:N_GROUPS, 0].set(router_group_b[0]).at[8:40, 0].set(router_expert_b[0])
    g1, b1 = ln1_g[0].reshape(1, D), ln1_b[0].reshape(1, D)
    g2, b2 = ln2_g[0].reshape(1, D), ln2_b[0].reshape(1, D)

    n_parts = N_PARTS if B % N_PARTS == 0 else 1
    bp = B // n_parts
    tp = bp * S
    tm = 256 if tp % 256 == 0 else ATT_BLK
    ts = 512 if S % 512 == 0 else ATT_BLK
    x2 = x.reshape(B * S, D)
    out = None
    for part in range(n_parts):
        proj_f, proj_b = _proj(x2, w_f, w_b, tm, part * tp, tp)
        hg, att = _mixers(proj_f, proj_b, lb, nw, sinks[0].astype(F32), bp, S)
        x1, xp, eid, wt, rank, cnt = _merge(hg, att, x2, w_gate, wa, wb, wo, g1, b1, wr.astype(BF16), br,
                                            512 if tp % 512 == 0 else tm, part * tp)
        pos, block_meta = _slot_layout(eid, rank, cnt[:, 0].astype(jnp.int32), tp)
        xbuf = _sc_scatter2(xp, pos[0], pos[1], 2 * tp + N_EXPERTS * MOE_BLOCK)
        ybuf = _experts(block_meta, xbuf, w_exp_gate[0], w_exp_up[0], w_exp_down[0])
        rows2 = _sc_gather(ybuf, pos.reshape(-1))
        out = _combine(rows2, x1, wt, g2, b2, tm, out, part * (tp // tm), B * S)
    return out.reshape(B, S, D)
```

```python
import functools

import numpy as np
import jax
import jax.numpy as jnp
from jax import lax
from jax.experimental import pallas as pl
from jax.experimental.pallas import tpu as pltpu
from jax.experimental.pallas import tpu_sc as plsc

F32 = jnp.float32
BF16 = jnp.bfloat16

D_MODEL = 1024
DEPTH = 1
HG_HEADS = 4
HG_DK = 128
HG_DV = 128
HG_W = HG_HEADS * HG_DK
CHUNK = 64
ATT_Q_HEADS = 8
ATT_KV_HEADS = 2
ATT_GROUP = ATT_Q_HEADS // ATT_KV_HEADS
ATT_HD = 64
ATT_QW = ATT_Q_HEADS * ATT_HD
ATT_KVW = ATT_KV_HEADS * ATT_HD
ATT_BLK = 128
N_GROUPS = 4
EPG = 8
N_EXPERTS = N_GROUPS * EPG
D_EXPERT = 512
MOE_BLOCK = 256
DN_ALPHA = (2.0 * DEPTH) ** 0.25
LN_EPS = 1e-5
RMS_EPS = 1e-6
NEG_INF = -1e30
LOG2_E = 1.4426950408889634

PROJ_W = 4 * HG_W + ATT_QW + 2 * ATT_KVW
PROJ_F = 3 * HG_W
PROJ_B = HG_W + ATT_QW + 2 * ATT_KVW
N_LEVELS = 6
N_ARG_GROUPS = N_LEVELS + 2
HG_SAFE_LB = 0.125

VMEM_LIMIT = 56 * 1024 * 1024
MERGE_SUB = 256
MIX_TILE = 512
N_PARTS = 2


def _cparams(sem):
    return pltpu.CompilerParams(dimension_semantics=sem, vmem_limit_bytes=VMEM_LIMIT)


def _dot(a, b):
    return jnp.dot(a, b, preferred_element_type=F32)


def _dot_nt(a, b):
    return lax.dot_general(a, b, (((1,), (1,)), ((), ())), preferred_element_type=F32)


def _dot_tn(a, b):
    return lax.dot_general(a, b, (((0,), (0,)), ((), ())), preferred_element_type=F32)


def _sigmoid(x):
    return 0.5 * jnp.tanh(0.5 * x) + 0.5


def _silu(x):
    return x * _sigmoid(x)


def _pack_bf16_pairs(xb):
    n = xb.shape[1] // 2
    lo = lax.bitcast_convert_type(xb[:, :n].astype(F32), jnp.uint32)
    hi = lax.bitcast_convert_type(xb[:, n:].astype(F32), jnp.uint32)
    return (lo >> 16) | hi


def _unpack_bf16_pairs(w):
    lo = lax.bitcast_convert_type(w << 16, F32)
    hi = lax.bitcast_convert_type(w & jnp.uint32(0xFFFF0000), F32)
    return jnp.concatenate([lo, hi], axis=1)


def _layer_norm(z, g, b):
    mu = jnp.mean(z, axis=-1, keepdims=True)
    zc = z - mu
    var = jnp.mean(zc * zc, axis=-1, keepdims=True)
    return zc * lax.rsqrt(var + LN_EPS) * g + b


def _proj_kernel(x_ref, wf_ref, wb_ref, of_ref, ob_ref):
    xb = x_ref[...].astype(BF16)
    of_ref[...] = _dot(xb, wf_ref[...])
    ob_ref[...] = _dot(xb, wb_ref[...]).astype(BF16)


def _proj(x2, w_f, w_b, tm, row0, T):
    tile0 = row0 // tm
    return pl.pallas_call(
        _proj_kernel,
        grid=(T // tm,),
        in_specs=[pl.BlockSpec((tm, D_MODEL), lambda i: (i + tile0, 0)),
                  pl.BlockSpec((D_MODEL, PROJ_F), lambda i: (0, 0)),
                  pl.BlockSpec((D_MODEL, PROJ_B), lambda i: (0, 0))],
        out_specs=[pl.BlockSpec((tm, PROJ_F), lambda i: (i, 0)), pl.BlockSpec((tm, PROJ_B), lambda i: (i, 0))],
        out_shape=[jax.ShapeDtypeStruct((T, PROJ_F), F32), jax.ShapeDtypeStruct((T, PROJ_B), BF16)],
        compiler_params=_cparams(("arbitrary",)),
        name="proj",
    )(x2, w_f, w_b)


def _hgrn_tables():
    C = CHUNK
    w = np.zeros((N_ARG_GROUPS, C, C), np.float32)
    masks = np.zeros((N_LEVELS, C, C), np.float32)
    for lvl in range(N_LEVELS):
        h = 1 << lvl
        for t in range(C):
            base = (t // (2 * h)) * 2 * h
            m = base + h - 1
            if (t // h) % 2 == 1:
                w[lvl, t, m + 1:t + 1] = 1.0
                masks[lvl, t, base:base + h] = 1.0
            else:
                w[lvl, t, t + 1:m + 1] = 1.0
    for t in range(C):
        w[N_LEVELS, t, :t + 1] = 1.0
        w[N_LEVELS + 1, t, t + 1:] = 1.0
    w = w.reshape(N_ARG_GROUPS * C, C)
    return np.concatenate([w, w], axis=1), masks


def _hgrn_chunk(q, fz, v, gate, c0, c1, nw, seg, mask_ref, state_ref, bounded):
    h = 0.5 * q
    qf = h + h * jnp.tanh(h)
    t1 = c1 * jnp.tanh(0.5 * fz)
    f = c0 + t1
    k = c1 - t1
    l2 = jnp.log2(f)
    v_bf = v.astype(BF16)
    qf_bf = qf.astype(BF16)
    k_bf = k.astype(BF16)

    l_hi = l2.astype(BF16)
    l_lo = (l2 - l_hi.astype(F32)).astype(BF16)
    l_split = jnp.concatenate([l_hi, l_lo], axis=0)
    row = lax.broadcasted_iota(jnp.int32, (CHUNK, 1), 0)
    if bounded:
        cum = _dot(seg[N_LEVELS * CHUNK:(N_LEVELS + 1) * CHUNK, :], l_split)
        mid = cum[CHUNK // 2 - 1:CHUNK // 2, :]
        last = cum[CHUNK - 1:CHUNK, :]
        e_cum = jnp.exp2(cum)
        e_suf = jnp.exp2(last - cum)
        a_q = qf_bf * jnp.exp2(cum - mid).astype(BF16)
        a_k = k_bf * jnp.exp2(mid - cum).astype(BF16)
        causal = row >= lax.broadcasted_iota(jnp.int32, (1, CHUNK), 1)
        scores = [jnp.where(causal, _dot_nt(a_q[:, hd * HG_DK:(hd + 1) * HG_DK], a_k[:, hd * HG_DK:(hd + 1) * HG_DK]),
                            0.0) for hd in range(HG_HEADS)]
        diag_v = None
    else:
        e = jnp.exp2(_dot(seg, l_split))
        scores = [jnp.zeros((CHUNK, CHUNK), F32) for _ in range(HG_HEADS)]
        for lvl in range(N_LEVELS):
            half = 1 << lvl
            e_l = e[lvl * CHUNK:(lvl + 1) * CHUNK, :].astype(BF16)
            if half >= 16:
                sel = jnp.concatenate([(qf_bf if (r0 // half) % 2 else k_bf)[r0:r0 + half]
                                       for r0 in range(0, CHUNK, half)], axis=0)
            else:
                sel = jnp.where((row // half) % 2 == 1, qf_bf, k_bf)
            a = sel * e_l
            m = mask_ref[lvl]
            for hd in range(HG_HEADS):
                a_h = a[:, hd * HG_DK:(hd + 1) * HG_DK]
                scores[hd] = scores[hd] + m * _dot_nt(a_h, a_h)
        e_cum = e[N_LEVELS * CHUNK:(N_LEVELS + 1) * CHUNK, :]
        e_suf = e[(N_LEVELS + 1) * CHUNK:(N_LEVELS + 2) * CHUNK, :]
        diag_v = qf * k

    q_in = qf_bf * e_cum.astype(BF16)
    k_out = k_bf * e_suf.astype(BF16)
    e_last = e_cum[CHUNK - 1:CHUNK, :]

    outs = []
    for hd in range(HG_HEADS):
        cols = slice(hd * HG_DK, (hd + 1) * HG_DK)
        st = state_ref[hd]
        o = _dot_nt(q_in[:, cols], st.astype(BF16)) + _dot(scores[hd].astype(BF16), v_bf[:, cols])
        if diag_v is not None:
            o = o + jnp.sum(diag_v[:, cols], axis=-1, keepdims=True) * v[:, cols]
        state_ref[hd] = st * e_last[:, cols] + _dot_tn(v_bf[:, cols], k_out[:, cols])
        o = o * lax.rsqrt(jnp.mean(o * o, axis=-1, keepdims=True) + RMS_EPS) * nw
        outs.append(o)
    hg = 0.5 * gate
    return jnp.concatenate(outs, axis=1) * (hg + hg * jnp.tanh(hg))


def _attn_bias():
    r = np.arange(ATT_BLK)[:, None]
    c = np.arange(2 * ATT_BLK)[None, :]
    dist = r + ATT_BLK - c
    window = (dist >= 0) & (dist < ATT_BLK)
    slopes = np.exp2(-8.0 * (np.arange(ATT_Q_HEADS, dtype=np.float32) + 1.0) / ATT_Q_HEADS).astype(np.float32)
    alibi = -slopes[:, None, None] * dist.astype(np.float32)[None]
    later = np.where(window[None], alibi, np.float32(NEG_INF))
    first = np.where((window & (c >= ATT_BLK))[None], alibi, np.float32(NEG_INF))
    return (np.stack([later, first]) * LOG2_E).astype(np.float32)


def _attn_block(q_ref, kv_cur, kv_prev, bias_ref, table, sink_ref, o_ref):
    lane = lax.broadcasted_iota(jnp.int32, (2 * ATT_BLK, 2 * ATT_KVW), 1)
    lo = (lane % ATT_KVW) < ATT_HD

    kv = jnp.concatenate([kv_prev, kv_cur], axis=0).astype(BF16)
    kv_sw = jnp.concatenate([kv[:, ATT_HD:ATT_KVW], kv[:, :ATT_HD],
                             kv[:, ATT_KVW + ATT_HD:], kv[:, ATT_KVW:ATT_KVW + ATT_HD]], axis=1)
    zero = jnp.zeros_like(kv)
    placed = {}
    for h in range(ATT_KV_HEADS):
        for off in range(2):
            src = kv if h == off else kv_sw
            placed[h, off] = jnp.where(lo if off == 0 else jnp.logical_not(lo), src, zero)

    scale = ATT_HD ** -0.5 * LOG2_E
    for pair in range(ATT_Q_HEADS // 2):
        qp = (q_ref[:, pair * 2 * ATT_HD:(pair + 1) * 2 * ATT_HD] * scale).astype(BF16)
        acc = jnp.zeros((ATT_BLK, 2 * ATT_HD), F32)
        for off in range(2):
            j = 2 * pair + off
            kvh = placed[j // ATT_GROUP, off]
            sink = sink_ref[j] * LOG2_E
            logits = _dot_nt(qp, kvh[:, :ATT_KVW]) + bias_ref[table, j]
            m = jnp.maximum(jnp.max(logits, axis=-1, keepdims=True), sink)
            p = jnp.exp2(logits - m)
            den = jnp.sum(p, axis=-1, keepdims=True) + jnp.exp2(sink - m)
            acc = acc + _dot(p.astype(BF16), kvh[:, ATT_KVW:]) / den
        o_ref[:, pair * 2 * ATT_HD:(pair + 1) * 2 * ATT_HD] = acc.astype(o_ref.dtype)


def _mixers_kernel(sink_ref, bounded_ref, q_ref, f_ref, i_ref, g_ref, aq_ref, kv_ref, kvp_ref, lb_ref, nw_ref,
                   seg_ref, mask_ref, bias_ref, hg_ref, at_ref, state_ref):
    first = pl.program_id(1) == 0

    @pl.when(first)
    def _():
        state_ref[...] = jnp.zeros_like(state_ref)

    def tile_body(bounded):
        lb = lb_ref[...]
        c0 = 0.5 + 0.5 * lb
        c1 = 0.5 - 0.5 * lb
        nw = nw_ref[...]
        seg = seg_ref[...]
        tile = q_ref.shape[0]
        for r0 in range(0, tile, CHUNK):
            rows = slice(r0, r0 + CHUNK)
            o = _hgrn_chunk(q_ref[rows, :], f_ref[rows, :], i_ref[rows, :], g_ref[rows, :], c0, c1, nw, seg,
                            mask_ref, state_ref, bounded)
            hg_ref[rows, :] = o.astype(hg_ref.dtype)
        for r0 in range(0, tile, ATT_BLK):
            rows = slice(r0, r0 + ATT_BLK)
            if r0 == 0:
                prev, table = kvp_ref[...], jnp.where(first, 1, 0)
            else:
                prev, table = kv_ref[r0 - ATT_BLK:r0, :], 0
            _attn_block(aq_ref.at[rows, :], kv_ref[rows, :], prev, bias_ref, table, sink_ref, at_ref.at[rows, :])

    pl.when(bounded_ref[0] == 1)(lambda: tile_body(True))
    pl.when(bounded_ref[0] != 1)(lambda: tile_body(False))


def _mixers(proj_f, proj_b, lb, norm_w, sinks, B, S):
    T = B * S
    tile = MIX_TILE if S % MIX_TILE == 0 else ATT_BLK
    nb = S // tile
    per = tile // ATT_BLK
    seg, masks = _hgrn_tables()
    const = lambda shape: pl.BlockSpec(shape, lambda b, n: (0,) * len(shape))
    hcol = lambda cb: pl.BlockSpec((tile, HG_W), lambda b, n: (b * nb + n, cb))
    kvcol = (HG_W + ATT_QW) // (2 * ATT_KVW)
    out = pl.BlockSpec((tile, HG_W), lambda b, n: (b * nb + n, 0))
    return pl.pallas_call(
        _mixers_kernel,
        grid=(B, nb),
        in_specs=[pl.BlockSpec(memory_space=pltpu.SMEM), pl.BlockSpec(memory_space=pltpu.SMEM),
                  hcol(0), hcol(1), hcol(0), hcol(2),
                  pl.BlockSpec((tile, ATT_QW), lambda b, n: (b * nb + n, HG_W // ATT_QW)),
                  pl.BlockSpec((tile, 2 * ATT_KVW), lambda b, n: (b * nb + n, kvcol)),
                  pl.BlockSpec((ATT_BLK, 2 * ATT_KVW),
                               lambda b, n: ((b * nb + n) * per - jnp.where(n > 0, 1, 0), kvcol)),
                  const((1, HG_W)), const((1, HG_DV)), const((N_ARG_GROUPS * CHUNK, 2 * CHUNK)),
                  const((N_LEVELS, CHUNK, CHUNK)), const((2, ATT_Q_HEADS, ATT_BLK, 2 * ATT_BLK))],
        out_specs=[out, out],
        out_shape=[jax.ShapeDtypeStruct((T, HG_W), BF16), jax.ShapeDtypeStruct((T, ATT_QW), BF16)],
        scratch_shapes=[pltpu.VMEM((HG_HEADS, HG_DV, HG_DK), F32)],
        compiler_params=_cparams(("arbitrary", "arbitrary")),
        name="mixers",
    )(sinks, (jnp.min(lb) >= HG_SAFE_LB).astype(jnp.int32).reshape(1),
      proj_f, proj_f, proj_b, proj_f, proj_b, proj_b, proj_b, lb, norm_w, jnp.asarray(seg, BF16),
      jnp.asarray(masks), jnp.asarray(_attn_bias()))


def _merge_kernel(hg_ref, at_ref, x_ref, wgate_ref, wa_ref, wb_ref, wo_ref, g1_ref, b1_ref,
                  wr_ref, br_ref, tri_ref, x1_ref, xp_ref, eid_ref, wt_ref, rank_ref, cnt_ref, base_ref, x1b_ref):
    tm = x_ref.shape[0]
    sub = min(tm, MERGE_SUB)
    for r0 in range(0, tm, sub):
        rows = slice(r0, r0 + sub)
        xs = x_ref[rows, :]
        gates = _dot(xs.astype(BF16), wgate_ref[...])
        ya = _dot(hg_ref[rows, :], wa_ref[...])
        yb = _dot(at_ref[rows, :], wb_ref[...])
        merged = _sigmoid(gates[:, :D_MODEL]) * ya + _sigmoid(gates[:, D_MODEL:]) * yb
        z = DN_ALPHA * xs + _dot(merged.astype(BF16), wo_ref[...])
        x1 = _layer_norm(z, g1_ref[...], b1_ref[...])
        x1_ref[rows, :] = x1
        x1b = x1.astype(BF16)
        x1b_ref[rows, :] = x1b
        xp_ref[rows, :] = _pack_bf16_pairs(x1b)

    lg = _dot_nt(wr_ref[...], x1b_ref[...]) + br_ref[...]
    g = lg[0:8, :]
    row8 = lax.broadcasted_iota(jnp.int32, (8, tm), 0)
    g = jnp.where(row8 < N_GROUPS, g, -jnp.inf)
    gmax = jnp.max(g, axis=0, keepdims=True)
    gsel = jnp.min(jnp.where(g == gmax, row8, 8), axis=0, keepdims=True)
    gw = 1.0 / jnp.sum(jnp.exp(g - gmax), axis=0, keepdims=True)
    el = jnp.where(gsel == 0, lg[8:16, :],
                   jnp.where(gsel == 1, lg[16:24, :], jnp.where(gsel == 2, lg[24:32, :], lg[32:40, :])))
    v1 = jnp.max(el, axis=0, keepdims=True)
    i1 = jnp.min(jnp.where(el == v1, row8, 8), axis=0, keepdims=True)
    el2 = jnp.where(row8 == i1, -jnp.inf, el)
    v2 = jnp.max(el2, axis=0, keepdims=True)
    i2 = jnp.min(jnp.where(el2 == v2, row8, 8), axis=0, keepdims=True)
    e2 = jnp.exp(v2 - v1)
    den = 1.0 + e2
    e_a = gsel * EPG + i1
    e_b = gsel * EPG + i2
    eid_ref[...] = jnp.concatenate([e_a, e_b], axis=0)
    wt_ref[...] = jnp.concatenate([gw / den, gw * e2 / den], axis=0)

    @pl.when(pl.program_id(0) == 0)
    def _():
        base_ref[...] = jnp.zeros_like(base_ref)

    row_e = lax.broadcasted_iota(jnp.int32, (N_EXPERTS, tm), 0)
    oh_a = jnp.where(row_e == e_a, 1.0, 0.0)
    oh_b = jnp.where(row_e == e_b, 1.0, 0.0)
    tri = tri_ref[...]
    pre_a = _dot(oh_a.astype(BF16), tri)
    pre_b = _dot(oh_b.astype(BF16), tri)
    cnt_a = jnp.sum(oh_a, axis=1, keepdims=True)
    cnt_b = jnp.sum(oh_b, axis=1, keepdims=True)
    base = base_ref[...]
    rank_a = jnp.sum(oh_a * (base + pre_a), axis=0, keepdims=True)
    rank_b = jnp.sum(oh_b * (base + cnt_a + pre_b), axis=0, keepdims=True)
    rank_ref[...] = jnp.concatenate([rank_a, rank_b], axis=0).astype(jnp.int32)
    base = base + cnt_a + cnt_b
    base_ref[...] = base
    cnt_ref[...] = jnp.broadcast_to(base, cnt_ref.shape)


def _merge(hg, att, x2, wgate, wa, wb, wo, g1, b1, wr, br, tm, row0):
    T = hg.shape[0]
    tile0 = row0 // tm
    row = lambda w: pl.BlockSpec((tm, w), lambda i: (i, 0))
    const = lambda shape: pl.BlockSpec(shape, lambda i: (0,) * len(shape))
    lanes = pl.BlockSpec((2, tm), lambda i: (0, i))
    tri = jnp.asarray(np.triu(np.ones((tm, tm), np.float32), 1), BF16)
    return pl.pallas_call(
        _merge_kernel,
        grid=(T // tm,),
        in_specs=[row(HG_W), row(ATT_QW),
                  pl.BlockSpec((tm, D_MODEL), lambda i: (i + tile0, 0)),
                  const((D_MODEL, 2 * D_MODEL)), const((HG_W, D_MODEL)), const((ATT_QW, D_MODEL)), const((D_MODEL, D_MODEL)),
                  const((1, D_MODEL)), const((1, D_MODEL)), const((40, D_MODEL)), const((40, 1)),
                  const((tm, tm))],
        out_specs=[row(D_MODEL), row(D_MODEL // 2), lanes, lanes, lanes, const((N_EXPERTS, 128))],
        out_shape=[jax.ShapeDtypeStruct((T, D_MODEL), F32),
                   jax.ShapeDtypeStruct((T, D_MODEL // 2), jnp.uint32),
                   jax.ShapeDtypeStruct((2, T), jnp.int32),
                   jax.ShapeDtypeStruct((2, T), F32),
                   jax.ShapeDtypeStruct((2, T), jnp.int32),
                   jax.ShapeDtypeStruct((N_EXPERTS, 128), F32)],
        scratch_shapes=[pltpu.VMEM((N_EXPERTS, 1), F32), pltpu.VMEM((tm, D_MODEL), BF16)],
        compiler_params=_cparams(("arbitrary",)),
        name="merge",
    )(hg, att, x2, wgate, wa, wb, wo, g1, b1, wr, br, tri)


SC_WINDOW = 64
SC_IDX_LANES = 128


def _pad_indices(idx):
    rows = idx.reshape(-1, SC_WINDOW)
    return jnp.pad(rows, ((0, 0), (0, SC_IDX_LANES - SC_WINDOW)))


def _sc_mesh():
    return plsc.VectorSubcoreMesh(core_axis_name="core", subcore_axis_name="subcore")


def _sc_scatter2(x, idx_a, idx_b, n_out):
    T, d = x.shape

    @pl.kernel(out_type=jax.ShapeDtypeStruct((n_out, d), x.dtype), mesh=_sc_mesh())
    def scatter(x_hbm, ia_hbm, ib_hbm, o_hbm):
        def body(x_vmem, ia_vmem, ib_vmem):
            pltpu.sync_copy(x_vmem, o_hbm.at[ia_vmem.at[0, pl.ds(0, SC_WINDOW)]])
            pltpu.sync_copy(x_vmem, o_hbm.at[ib_vmem.at[0, pl.ds(0, SC_WINDOW)]])

        idx_spec = pl.BlockSpec((1, SC_IDX_LANES), lambda i: (i, 0))
        pltpu.emit_pipeline(
            body, grid=(T // SC_WINDOW,),
            in_specs=[pl.BlockSpec((SC_WINDOW, d), lambda i: (i, 0)), idx_spec, idx_spec],
            out_specs=[],
            core_axis_name=("core", "subcore"),
            dimension_semantics=(pltpu.PARALLEL,),
        )(x_hbm, ia_hbm, ib_hbm)

    return scatter(x, _pad_indices(idx_a), _pad_indices(idx_b))


def _sc_gather(x, idx):
    n = idx.shape[0]
    d = x.shape[1]

    @pl.kernel(out_type=jax.ShapeDtypeStruct((n, d), x.dtype), mesh=_sc_mesh())
    def gather(x_hbm, i_hbm, o_hbm):
        def body(i_vmem, o_vmem):
            pltpu.sync_copy(x_hbm.at[i_vmem.at[0, pl.ds(0, SC_WINDOW)]], o_vmem)

        pltpu.emit_pipeline(
            body, grid=(n // SC_WINDOW,),
            in_specs=[pl.BlockSpec((1, SC_IDX_LANES), lambda i: (i, 0))],
            out_specs=[pl.BlockSpec((SC_WINDOW, d), lambda i: (i, 0))],
            core_axis_name=("core", "subcore"),
            dimension_semantics=(pltpu.PARALLEL,),
        )(i_hbm, o_hbm)

    return gather(x, _pad_indices(idx))


def _expert_kernel(be_ref, nv_ref, nxt_ref, par_ref, xb_ref, wg_hbm, wu_hbm, wd_hbm, y_ref,
                   wg_st, wu_st, wd_st, wg_bf, wu_bf, wd_bf, sem):
    i = pl.program_id(0)
    nv = nv_ref[i]

    def fetch(expert, slot):
        return [pltpu.make_async_copy(src.at[expert], dst.at[slot], sem.at[slot, j])
                for j, (src, dst) in enumerate(((wg_hbm, wg_st), (wu_hbm, wu_st), (wd_hbm, wd_st)))]

    @pl.when(i == 0)
    def _():
        for cp in fetch(be_ref[0], par_ref[0]):
            cp.start()

    @pl.when((i == 0) | (be_ref[i] != be_ref[jnp.maximum(i - 1, 0)]))
    def _():
        slot = par_ref[i]
        for cp in fetch(be_ref[i], slot):
            cp.wait()
        wg_bf[...] = wg_st[slot].astype(BF16)
        wu_bf[...] = wu_st[slot].astype(BF16)
        wd_bf[...] = wd_st[slot].astype(BF16)

        @pl.when(nxt_ref[i] >= 0)
        def _():
            for cp in fetch(nxt_ref[i], 1 - slot):
                cp.start()

    @pl.when(nv > 0)
    def _():
        row = lax.broadcasted_iota(jnp.int32, (MOE_BLOCK, 1), 0)
        xb = _unpack_bf16_pairs(jnp.where(row < nv, xb_ref[...], jnp.uint32(0))).astype(BF16)
        g = _dot(xb, wg_bf[...])
        u = _dot(xb, wu_bf[...])
        h = (_silu(g) * u).astype(BF16)
        y_ref[...] = _pack_bf16_pairs(_dot(h, wd_bf[...]).astype(BF16))


def _experts(block_meta, xbuf, w_gate, w_up, w_down):
    P = xbuf.shape[0]
    nb = P // MOE_BLOCK
    rows = pl.BlockSpec((MOE_BLOCK, D_MODEL // 2), lambda i, *_: (i, 0))
    hbm = pl.BlockSpec(memory_space=pl.ANY)
    return pl.pallas_call(
        _expert_kernel,
        grid_spec=pltpu.PrefetchScalarGridSpec(
            num_scalar_prefetch=4,
            grid=(nb,),
            in_specs=[rows, hbm, hbm, hbm],
            out_specs=rows,
            scratch_shapes=[pltpu.VMEM((2, D_MODEL, D_EXPERT), F32), pltpu.VMEM((2, D_MODEL, D_EXPERT), F32),
                            pltpu.VMEM((2, D_EXPERT, D_MODEL), F32),
                            pltpu.VMEM((D_MODEL, D_EXPERT), BF16), pltpu.VMEM((D_MODEL, D_EXPERT), BF16),
                            pltpu.VMEM((D_EXPERT, D_MODEL), BF16),
                            pltpu.SemaphoreType.DMA((2, 3))]),
        out_shape=jax.ShapeDtypeStruct((P, D_MODEL // 2), jnp.uint32),
        compiler_params=_cparams(("arbitrary",)),
        name="experts",
    )(*block_meta, xbuf, w_gate, w_up, w_down)


def _combine_kernel(ra_ref, rb_ref, x1_ref, wt_ref, g2_ref, b2_ref, *rest):
    o_ref = rest[-1]
    w2 = wt_ref[...]
    w = jnp.concatenate([w2, jnp.zeros((6, w2.shape[1]), F32)], axis=0).T
    y = w[:, 0:1] * _unpack_bf16_pairs(ra_ref[...]) + w[:, 1:2] * _unpack_bf16_pairs(rb_ref[...])
    o_ref[...] = _layer_norm(DN_ALPHA * x1_ref[...] + y, g2_ref[...], b2_ref[...])


def _combine(rows2, x1, wt, g2, b2, tm, out_prev, tile0, t_total):
    T = x1.shape[0]
    nt = T // tm
    const = lambda shape: pl.BlockSpec(shape, lambda i: (0,) * len(shape))
    in_specs = [pl.BlockSpec((tm, D_MODEL // 2), lambda i: (i, 0)),
                pl.BlockSpec((tm, D_MODEL // 2), lambda i: (i + nt, 0)),
                pl.BlockSpec((tm, D_MODEL), lambda i: (i, 0)), pl.BlockSpec((2, tm), lambda i: (0, i)),
                const((1, D_MODEL)), const((1, D_MODEL))]
    args = [rows2, rows2, x1, wt, g2, b2]
    aliases = {}
    if out_prev is not None:
        in_specs.append(pl.BlockSpec(memory_space=pl.ANY))
        args.append(out_prev)
        aliases = {len(args) - 1: 0}
    return pl.pallas_call(
        _combine_kernel,
        grid=(nt,),
        in_specs=in_specs,
        out_specs=pl.BlockSpec((tm, D_MODEL), lambda i: (i + tile0, 0)),
        out_shape=jax.ShapeDtypeStruct((t_total, D_MODEL), F32),
        input_output_aliases=aliases,
        compiler_params=_cparams(("arbitrary",)),
        name="combine",
    )(*args)


def _slot_layout(eid, rank, counts, T):
    P = 2 * T + N_EXPERTS * MOE_BLOCK
    nb = P // MOE_BLOCK
    padded = ((counts + MOE_BLOCK - 1) // MOE_BLOCK) * MOE_BLOCK
    pend = jnp.cumsum(padded)
    pstart = pend - padded
    experts = jnp.arange(N_EXPERTS, dtype=jnp.int32)
    pos = rank + jnp.sum(jnp.where(eid[:, :, None] == experts, pstart, 0), axis=-1)

    block_start = jnp.arange(nb, dtype=jnp.int32) * MOE_BLOCK
    block_expert = jnp.minimum(jnp.sum(block_start[:, None] >= pend[None, :], axis=1), N_EXPERTS - 1)
    onehot = block_expert[:, None] == experts[None, :]
    look = lambda table: jnp.sum(jnp.where(onehot, table[None, :], 0), axis=1)
    block_valid = jnp.clip(look(pstart + counts) - block_start, 0, MOE_BLOCK)
    block_valid = jnp.where(block_start < pend[-1], block_valid, 0)
    present = (counts > 0) | (experts == N_EXPERTS - 1)
    later = present[None, :] & (experts[None, :] > experts[:, None])
    next_present = jnp.min(jnp.where(later, experts[None, :], N_EXPERTS), axis=1)
    next_present = jnp.where(next_present < N_EXPERTS, next_present, -1)
    runs_before = jnp.sum(present[None, :] & (experts[None, :] < experts[:, None]), axis=1)
    block_next = look(next_present)
    block_slot = look(runs_before % 2)
    i32 = lambda a: a.astype(jnp.int32)
    return i32(pos), (i32(block_expert), i32(block_valid), i32(block_next), i32(block_slot))


def kernel(x, lb_logits, w_in, hg_norm_w, sinks, w_branch_a, w_branch_b, w_out, ln1_g, ln1_b,
           router_group_w, router_group_b, router_expert_w, router_expert_b,
           w_exp_gate, w_exp_up, w_exp_down, ln2_g, ln2_b):
    B, S, D = x.shape
    assert D == D_MODEL and S % ATT_BLK == 0 and w_in.shape[0] == DEPTH == 1
    lb_all = jnp.cumsum(jax.nn.softmax(lb_logits.astype(F32), axis=0), axis=0)
    lb = lb_all[0].reshape(1, HG_W)
    w0 = w_in[0]
    w_f = jnp.concatenate([w0[:, :2 * HG_W], w0[:, 3 * HG_W:4 * HG_W]], axis=1).astype(BF16)
    w_b = jnp.concatenate([w0[:, 2 * HG_W:3 * HG_W], w0[:, 4 * HG_W:PROJ_W]], axis=1).astype(BF16)
    w_gate = w0[:, PROJ_W:].astype(BF16)
    nw = hg_norm_w[0].reshape(1, HG_DV).astype(F32)
    wa, wb, wo = w_branch_a[0].astype(BF16), w_branch_b[0].astype(BF16), w_out[0].astype(BF16)
    wr = jnp.zeros((40, D), F32).at[0:N_GROUPS].set(router_group_w[0].T).at[8:40].set(router_expert_w[0].T)
    br = jnp.zeros((40, 1), F32).at[0:N_GROUPS, 0].set(router_group_b[0]).at[8:40, 0].set(router_expert_b[0])
    g1, b1 = ln1_g[0].reshape(1, D), ln1_b[0].reshape(1, D)
    g2, b2 = ln2_g[0].reshape(1, D), ln2_b[0].reshape(1, D)

    n_parts = N_PARTS if B % N_PARTS == 0 else 1
    bp = B // n_parts
    tp = bp * S
    pick = lambda want: next(t for t in (want, 512, 256, ATT_BLK) if t <= want and tp % t == 0)
    tm_proj, tm_merge, tm_comb = pick(512), pick(512), pick(1024)
    x2 = x.reshape(B * S, D)
    out = None
    for part in range(n_parts):
        proj_f, proj_b = _proj(x2, w_f, w_b, tm_proj, part * tp, tp)
        hg, att = _mixers(proj_f, proj_b, lb, nw, sinks[0].astype(F32), bp, S)
        x1, xp, eid, wt, rank, cnt = _merge(hg, att, x2, w_gate, wa, wb, wo, g1, b1, wr.astype(BF16), br,
                                            tm_merge, part * tp)
        pos, block_meta = _slot_layout(eid, rank, cnt[:, 0].astype(jnp.int32), tp)
        xbuf = _sc_scatter2(xp, pos[0], pos[1], 2 * tp + N_EXPERTS * MOE_BLOCK)
        ybuf = _experts(block_meta, xbuf, w_exp_gate[0], w_exp_up[0], w_exp_down[0])
        rows2 = _sc_gather(ybuf, pos.reshape(-1))
        out = _combine(rows2, x1, wt, g2, b2, tm_comb, out, part * (tp // tm_comb), B * S)
    return out.reshape(B, S, D)
```

```python
import functools

import numpy as np
import jax
import jax.numpy as jnp
from jax import lax
from jax.experimental import pallas as pl
from jax.experimental.pallas import tpu as pltpu
from jax.experimental.pallas import tpu_sc as plsc

F32 = jnp.float32
BF16 = jnp.bfloat16

D_MODEL = 1024
DEPTH = 1
HG_HEADS = 4
HG_DK = 128
HG_DV = 128
HG_W = HG_HEADS * HG_DK
CHUNK = 64
ATT_Q_HEADS = 8
ATT_KV_HEADS = 2
ATT_GROUP = ATT_Q_HEADS // ATT_KV_HEADS
ATT_HD = 64
ATT_QW = ATT_Q_HEADS * ATT_HD
ATT_KVW = ATT_KV_HEADS * ATT_HD
ATT_BLK = 128
N_GROUPS = 4
EPG = 8
N_EXPERTS = N_GROUPS * EPG
D_EXPERT = 512
MOE_BLOCK = 512
DN_ALPHA = (2.0 * DEPTH) ** 0.25
LN_EPS = 1e-5
RMS_EPS = 1e-6
NEG_INF = -1e30
LOG2_E = 1.4426950408889634

PROJ_W = 4 * HG_W + ATT_QW + 2 * ATT_KVW
PROJ_F = 3 * HG_W
PROJ_B = HG_W + ATT_QW + 2 * ATT_KVW
N_LEVELS = 6
N_ARG_GROUPS = N_LEVELS + 2
HG_SAFE_LB = 0.125

VMEM_LIMIT = 56 * 1024 * 1024
MERGE_SUB = 256
MIX_TILE = 512
N_PARTS = 2


def _cparams(sem):
    return pltpu.CompilerParams(dimension_semantics=sem, vmem_limit_bytes=VMEM_LIMIT)


def _dot(a, b):
    return jnp.dot(a, b, preferred_element_type=F32)


def _dot_nt(a, b):
    return lax.dot_general(a, b, (((1,), (1,)), ((), ())), preferred_element_type=F32)


def _dot_tn(a, b):
    return lax.dot_general(a, b, (((0,), (0,)), ((), ())), preferred_element_type=F32)


def _sigmoid(x):
    return 0.5 * jnp.tanh(0.5 * x) + 0.5


def _silu(x):
    return x * _sigmoid(x)


def _pack_bf16_pairs(xb):
    n = xb.shape[1] // 2
    lo = lax.bitcast_convert_type(xb[:, :n].astype(F32), jnp.uint32)
    hi = lax.bitcast_convert_type(xb[:, n:].astype(F32), jnp.uint32)
    return (lo >> 16) | hi


def _unpack_bf16_pairs(w):
    lo = lax.bitcast_convert_type(w << 16, F32)
    hi = lax.bitcast_convert_type(w & jnp.uint32(0xFFFF0000), F32)
    return jnp.concatenate([lo, hi], axis=1)


def _layer_norm(z, g, b):
    mu = jnp.mean(z, axis=-1, keepdims=True)
    zc = z - mu
    var = jnp.mean(zc * zc, axis=-1, keepdims=True)
    return zc * lax.rsqrt(var + LN_EPS) * g + b


def _proj_kernel(x_ref, wf_ref, wb_ref, of_ref, ob_ref):
    xb = x_ref[...].astype(BF16)
    of_ref[...] = _dot(xb, wf_ref[...])
    ob_ref[...] = _dot(xb, wb_ref[...]).astype(BF16)


def _proj(x2, w_f, w_b, tm, row0, T):
    tile0 = row0 // tm
    return pl.pallas_call(
        _proj_kernel,
        grid=(T // tm,),
        in_specs=[pl.BlockSpec((tm, D_MODEL), lambda i: (i + tile0, 0)),
                  pl.BlockSpec((D_MODEL, PROJ_F), lambda i: (0, 0)),
                  pl.BlockSpec((D_MODEL, PROJ_B), lambda i: (0, 0))],
        out_specs=[pl.BlockSpec((tm, PROJ_F), lambda i: (i, 0)), pl.BlockSpec((tm, PROJ_B), lambda i: (i, 0))],
        out_shape=[jax.ShapeDtypeStruct((T, PROJ_F), F32), jax.ShapeDtypeStruct((T, PROJ_B), BF16)],
        compiler_params=_cparams(("arbitrary",)),
        name="proj",
    )(x2, w_f, w_b)


def _hgrn_tables():
    C = CHUNK
    w = np.zeros((N_ARG_GROUPS, C, C), np.float32)
    masks = np.zeros((N_LEVELS, C, C), np.float32)
    for lvl in range(N_LEVELS):
        h = 1 << lvl
        for t in range(C):
            base = (t // (2 * h)) * 2 * h
            m = base + h - 1
            if (t // h) % 2 == 1:
                w[lvl, t, m + 1:t + 1] = 1.0
                masks[lvl, t, base:base + h] = 1.0
            else:
                w[lvl, t, t + 1:m + 1] = 1.0
    for t in range(C):
        w[N_LEVELS, t, :t + 1] = 1.0
        w[N_LEVELS + 1, t, t + 1:] = 1.0
    w = w.reshape(N_ARG_GROUPS * C, C)
    return np.concatenate([w, w], axis=1), masks


def _hgrn_chunk(q, fz, v, gate, c0, c1, nw, seg, mask_ref, state_ref, bounded):
    h = 0.5 * q
    qf = h + h * jnp.tanh(h)
    t1 = c1 * jnp.tanh(0.5 * fz)
    f = c0 + t1
    k = c1 - t1
    l2 = jnp.log2(f)
    v_bf = v.astype(BF16)
    qf_bf = qf.astype(BF16)
    k_bf = k.astype(BF16)

    l_hi = l2.astype(BF16)
    l_lo = (l2 - l_hi.astype(F32)).astype(BF16)
    l_split = jnp.concatenate([l_hi, l_lo], axis=0)
    row = lax.broadcasted_iota(jnp.int32, (CHUNK, 1), 0)
    if bounded:
        cum = _dot(seg[N_LEVELS * CHUNK:(N_LEVELS + 1) * CHUNK, :], l_split)
        mid = cum[CHUNK // 2 - 1:CHUNK // 2, :]
        last = cum[CHUNK - 1:CHUNK, :]
        e_cum = jnp.exp2(cum)
        e_suf = jnp.exp2(last - cum)
        a_q = qf_bf * jnp.exp2(cum - mid).astype(BF16)
        a_k = k_bf * jnp.exp2(mid - cum).astype(BF16)
        causal = row >= lax.broadcasted_iota(jnp.int32, (1, CHUNK), 1)
        scores = [jnp.where(causal, _dot_nt(a_q[:, hd * HG_DK:(hd + 1) * HG_DK], a_k[:, hd * HG_DK:(hd + 1) * HG_DK]),
                            0.0) for hd in range(HG_HEADS)]
        diag_v = None
    else:
        e = jnp.exp2(_dot(seg, l_split))
        scores = [jnp.zeros((CHUNK, CHUNK), F32) for _ in range(HG_HEADS)]
        for lvl in range(N_LEVELS):
            half = 1 << lvl
            e_l = e[lvl * CHUNK:(lvl + 1) * CHUNK, :].astype(BF16)
            if half >= 16:
                sel = jnp.concatenate([(qf_bf if (r0 // half) % 2 else k_bf)[r0:r0 + half]
                                       for r0 in range(0, CHUNK, half)], axis=0)
            else:
                sel = jnp.where((row // half) % 2 == 1, qf_bf, k_bf)
            a = sel * e_l
            m = mask_ref[lvl]
            for hd in range(HG_HEADS):
                a_h = a[:, hd * HG_DK:(hd + 1) * HG_DK]
                scores[hd] = scores[hd] + m * _dot_nt(a_h, a_h)
        e_cum = e[N_LEVELS * CHUNK:(N_LEVELS + 1) * CHUNK, :]
        e_suf = e[(N_LEVELS + 1) * CHUNK:(N_LEVELS + 2) * CHUNK, :]
        diag_v = qf * k

    q_in = qf_bf * e_cum.astype(BF16)
    k_out = k_bf * e_suf.astype(BF16)
    e_last = e_cum[CHUNK - 1:CHUNK, :]

    outs = []
    for hd in range(HG_HEADS):
        cols = slice(hd * HG_DK, (hd + 1) * HG_DK)
        st = state_ref[hd]
        o = _dot_nt(q_in[:, cols], st.astype(BF16)) + _dot(scores[hd].astype(BF16), v_bf[:, cols])
        if diag_v is not None:
            o = o + jnp.sum(diag_v[:, cols], axis=-1, keepdims=True) * v[:, cols]
        state_ref[hd] = st * e_last[:, cols] + _dot_tn(v_bf[:, cols], k_out[:, cols])
        o = o * lax.rsqrt(jnp.mean(o * o, axis=-1, keepdims=True) + RMS_EPS) * nw
        outs.append(o)
    hg = 0.5 * gate
    return jnp.concatenate(outs, axis=1) * (hg + hg * jnp.tanh(hg))


def _attn_bias():
    r = np.arange(ATT_BLK)[:, None]
    c = np.arange(2 * ATT_BLK)[None, :]
    dist = r + ATT_BLK - c
    window = (dist >= 0) & (dist < ATT_BLK)
    slopes = np.exp2(-8.0 * (np.arange(ATT_Q_HEADS, dtype=np.float32) + 1.0) / ATT_Q_HEADS).astype(np.float32)
    alibi = -slopes[:, None, None] * dist.astype(np.float32)[None]
    later = np.where(window[None], alibi, np.float32(NEG_INF))
    first = np.where((window & (c >= ATT_BLK))[None], alibi, np.float32(NEG_INF))
    return (np.stack([later, first]) * LOG2_E).astype(np.float32)


def _attn_block(q_ref, kv_cur, kv_prev, bias_ref, table, sink_ref, o_ref):
    lane = lax.broadcasted_iota(jnp.int32, (2 * ATT_BLK, 2 * ATT_KVW), 1)
    lo = (lane % ATT_KVW) < ATT_HD

    kv = jnp.concatenate([kv_prev, kv_cur], axis=0).astype(BF16)
    kv_sw = jnp.concatenate([kv[:, ATT_HD:ATT_KVW], kv[:, :ATT_HD],
                             kv[:, ATT_KVW + ATT_HD:], kv[:, ATT_KVW:ATT_KVW + ATT_HD]], axis=1)
    zero = jnp.zeros_like(kv)
    placed = {}
    for h in range(ATT_KV_HEADS):
        for off in range(2):
            src = kv if h == off else kv_sw
            placed[h, off] = jnp.where(lo if off == 0 else jnp.logical_not(lo), src, zero)

    scale = ATT_HD ** -0.5 * LOG2_E
    for pair in range(ATT_Q_HEADS // 2):
        qp = (q_ref[:, pair * 2 * ATT_HD:(pair + 1) * 2 * ATT_HD] * scale).astype(BF16)
        acc = jnp.zeros((ATT_BLK, 2 * ATT_HD), F32)
        for off in range(2):
            j = 2 * pair + off
            kvh = placed[j // ATT_GROUP, off]
            sink = sink_ref[j] * LOG2_E
            logits = _dot_nt(qp, kvh[:, :ATT_KVW]) + bias_ref[table, j]
            m = jnp.maximum(jnp.max(logits, axis=-1, keepdims=True), sink)
            p = jnp.exp2(logits - m)
            den = jnp.sum(p, axis=-1, keepdims=True) + jnp.exp2(sink - m)
            acc = acc + _dot(p.astype(BF16), kvh[:, ATT_KVW:]) / den
        o_ref[:, pair * 2 * ATT_HD:(pair + 1) * 2 * ATT_HD] = acc.astype(o_ref.dtype)


def _mixers_kernel(sink_ref, bounded_ref, q_ref, f_ref, i_ref, g_ref, aq_ref, kv_ref, kvp_ref, lb_ref, nw_ref,
                   seg_ref, mask_ref, bias_ref, hg_ref, at_ref, state_ref):
    first = pl.program_id(1) == 0

    @pl.when(first)
    def _():
        state_ref[...] = jnp.zeros_like(state_ref)

    def tile_body(bounded):
        lb = lb_ref[...]
        c0 = 0.5 + 0.5 * lb
        c1 = 0.5 - 0.5 * lb
        nw = nw_ref[...]
        seg = seg_ref[...]
        tile = q_ref.shape[0]
        for r0 in range(0, tile, CHUNK):
            rows = slice(r0, r0 + CHUNK)
            o = _hgrn_chunk(q_ref[rows, :], f_ref[rows, :], i_ref[rows, :], g_ref[rows, :], c0, c1, nw, seg,
                            mask_ref, state_ref, bounded)
            hg_ref[rows, :] = o.astype(hg_ref.dtype)
        for r0 in range(0, tile, ATT_BLK):
            rows = slice(r0, r0 + ATT_BLK)
            if r0 == 0:
                prev, table = kvp_ref[...], jnp.where(first, 1, 0)
            else:
                prev, table = kv_ref[r0 - ATT_BLK:r0, :], 0
            _attn_block(aq_ref.at[rows, :], kv_ref[rows, :], prev, bias_ref, table, sink_ref, at_ref.at[rows, :])

    pl.when(bounded_ref[0] == 1)(lambda: tile_body(True))
    pl.when(bounded_ref[0] != 1)(lambda: tile_body(False))


def _mixers(proj_f, proj_b, lb, norm_w, sinks, B, S):
    T = B * S
    tile = MIX_TILE if S % MIX_TILE == 0 else ATT_BLK
    nb = S // tile
    per = tile // ATT_BLK
    seg, masks = _hgrn_tables()
    const = lambda shape: pl.BlockSpec(shape, lambda b, n: (0,) * len(shape))
    hcol = lambda cb: pl.BlockSpec((tile, HG_W), lambda b, n: (b * nb + n, cb))
    kvcol = (HG_W + ATT_QW) // (2 * ATT_KVW)
    out = pl.BlockSpec((tile, HG_W), lambda b, n: (b * nb + n, 0))
    return pl.pallas_call(
        _mixers_kernel,
        grid=(B, nb),
        in_specs=[pl.BlockSpec(memory_space=pltpu.SMEM), pl.BlockSpec(memory_space=pltpu.SMEM),
                  hcol(0), hcol(1), hcol(0), hcol(2),
                  pl.BlockSpec((tile, ATT_QW), lambda b, n: (b * nb + n, HG_W // ATT_QW)),
                  pl.BlockSpec((tile, 2 * ATT_KVW), lambda b, n: (b * nb + n, kvcol)),
                  pl.BlockSpec((ATT_BLK, 2 * ATT_KVW),
                               lambda b, n: ((b * nb + n) * per - jnp.where(n > 0, 1, 0), kvcol)),
                  const((1, HG_W)), const((1, HG_DV)), const((N_ARG_GROUPS * CHUNK, 2 * CHUNK)),
                  const((N_LEVELS, CHUNK, CHUNK)), const((2, ATT_Q_HEADS, ATT_BLK, 2 * ATT_BLK))],
        out_specs=[out, out],
        out_shape=[jax.ShapeDtypeStruct((T, HG_W), BF16), jax.ShapeDtypeStruct((T, ATT_QW), BF16)],
        scratch_shapes=[pltpu.VMEM((HG_HEADS, HG_DV, HG_DK), F32)],
        compiler_params=_cparams(("arbitrary", "arbitrary")),
        name="mixers",
    )(sinks, (jnp.min(lb) >= HG_SAFE_LB).astype(jnp.int32).reshape(1),
      proj_f, proj_f, proj_b, proj_f, proj_b, proj_b, proj_b, lb, norm_w, jnp.asarray(seg, BF16),
      jnp.asarray(masks), jnp.asarray(_attn_bias()))


def _merge_kernel(hg_ref, at_ref, x_ref, wgate_ref, wa_ref, wb_ref, wo_ref, g1_ref, b1_ref,
                  wr_ref, br_ref, tri_ref, x1_ref, xp_ref, eid_ref, wt_ref, rank_ref, cnt_ref, base_ref, x1b_ref):
    tm = x_ref.shape[0]
    sub = min(tm, MERGE_SUB)
    for r0 in range(0, tm, sub):
        rows = slice(r0, r0 + sub)
        xs = x_ref[rows, :]
        gates = _dot(xs.astype(BF16), wgate_ref[...])
        ya = _dot(hg_ref[rows, :], wa_ref[...])
        yb = _dot(at_ref[rows, :], wb_ref[...])
        merged = _sigmoid(gates[:, :D_MODEL]) * ya + _sigmoid(gates[:, D_MODEL:]) * yb
        z = DN_ALPHA * xs + _dot(merged.astype(BF16), wo_ref[...])
        x1 = _layer_norm(z, g1_ref[...], b1_ref[...])
        x1_ref[rows, :] = x1
        x1b = x1.astype(BF16)
        x1b_ref[rows, :] = x1b
        xp_ref[rows, :] = _pack_bf16_pairs(x1b)

    lg = _dot_nt(wr_ref[...], x1b_ref[...]) + br_ref[...]
    g = lg[0:8, :]
    row8 = lax.broadcasted_iota(jnp.int32, (8, tm), 0)
    g = jnp.where(row8 < N_GROUPS, g, -jnp.inf)
    gmax = jnp.max(g, axis=0, keepdims=True)
    gsel = jnp.min(jnp.where(g == gmax, row8, 8), axis=0, keepdims=True)
    gw = 1.0 / jnp.sum(jnp.exp(g - gmax), axis=0, keepdims=True)
    el = jnp.where(gsel == 0, lg[8:16, :],
                   jnp.where(gsel == 1, lg[16:24, :], jnp.where(gsel == 2, lg[24:32, :], lg[32:40, :])))
    v1 = jnp.max(el, axis=0, keepdims=True)
    i1 = jnp.min(jnp.where(el == v1, row8, 8), axis=0, keepdims=True)
    el2 = jnp.where(row8 == i1, -jnp.inf, el)
    v2 = jnp.max(el2, axis=0, keepdims=True)
    i2 = jnp.min(jnp.where(el2 == v2, row8, 8), axis=0, keepdims=True)
    e2 = jnp.exp(v2 - v1)
    den = 1.0 + e2
    e_a = gsel * EPG + i1
    e_b = gsel * EPG + i2
    eid_ref[...] = jnp.concatenate([e_a, e_b], axis=0)
    wt_ref[...] = jnp.concatenate([gw / den, gw * e2 / den], axis=0)

    @pl.when(pl.program_id(0) == 0)
    def _():
        base_ref[...] = jnp.zeros_like(base_ref)

    row_e = lax.broadcasted_iota(jnp.int32, (N_EXPERTS, tm), 0)
    oh_a = jnp.where(row_e == e_a, 1.0, 0.0)
    oh_b = jnp.where(row_e == e_b, 1.0, 0.0)
    tri = tri_ref[...]
    pre_a = _dot(oh_a.astype(BF16), tri)
    pre_b = _dot(oh_b.astype(BF16), tri)
    cnt_a = jnp.sum(oh_a, axis=1, keepdims=True)
    cnt_b = jnp.sum(oh_b, axis=1, keepdims=True)
    base = base_ref[...]
    rank_a = jnp.sum(oh_a * (base + pre_a), axis=0, keepdims=True)
    rank_b = jnp.sum(oh_b * (base + cnt_a + pre_b), axis=0, keepdims=True)
    rank_ref[...] = jnp.concatenate([rank_a, rank_b], axis=0).astype(jnp.int32)
    base = base + cnt_a + cnt_b
    base_ref[...] = base
    cnt_ref[...] = jnp.broadcast_to(base, cnt_ref.shape)


def _merge(hg, att, x2, wgate, wa, wb, wo, g1, b1, wr, br, tm, row0):
    T = hg.shape[0]
    tile0 = row0 // tm
    row = lambda w: pl.BlockSpec((tm, w), lambda i: (i, 0))
    const = lambda shape: pl.BlockSpec(shape, lambda i: (0,) * len(shape), pipeline_mode=pl.Buffered(1))
    lanes = pl.BlockSpec((2, tm), lambda i: (0, i))
    tri = jnp.asarray(np.triu(np.ones((tm, tm), np.float32), 1), BF16)
    return pl.pallas_call(
        _merge_kernel,
        grid=(T // tm,),
        in_specs=[row(HG_W), row(ATT_QW),
                  pl.BlockSpec((tm, D_MODEL), lambda i: (i + tile0, 0)),
                  const((D_MODEL, 2 * D_MODEL)), const((HG_W, D_MODEL)), const((ATT_QW, D_MODEL)), const((D_MODEL, D_MODEL)),
                  const((1, D_MODEL)), const((1, D_MODEL)), const((40, D_MODEL)), const((40, 1)),
                  const((tm, tm))],
        out_specs=[row(D_MODEL), row(D_MODEL // 2), lanes, lanes, lanes,
                   pl.BlockSpec((N_EXPERTS, 128), lambda i: (0, 0))],
        out_shape=[jax.ShapeDtypeStruct((T, D_MODEL), F32),
                   jax.ShapeDtypeStruct((T, D_MODEL // 2), jnp.uint32),
                   jax.ShapeDtypeStruct((2, T), jnp.int32),
                   jax.ShapeDtypeStruct((2, T), F32),
                   jax.ShapeDtypeStruct((2, T), jnp.int32),
                   jax.ShapeDtypeStruct((N_EXPERTS, 128), F32)],
        scratch_shapes=[pltpu.VMEM((N_EXPERTS, 1), F32), pltpu.VMEM((tm, D_MODEL), BF16)],
        compiler_params=_cparams(("arbitrary",)),
        name="merge",
    )(hg, att, x2, wgate, wa, wb, wo, g1, b1, wr, br, tri)


SC_WINDOW = 64
SC_IDX_LANES = 128


def _pad_indices(idx):
    rows = idx.reshape(-1, SC_WINDOW)
    return jnp.pad(rows, ((0, 0), (0, SC_IDX_LANES - SC_WINDOW)))


def _sc_mesh():
    return plsc.VectorSubcoreMesh(core_axis_name="core", subcore_axis_name="subcore")


def _sc_scatter2(x, idx_a, idx_b, n_out):
    T, d = x.shape

    @pl.kernel(out_type=jax.ShapeDtypeStruct((n_out, d), x.dtype), mesh=_sc_mesh())
    def scatter(x_hbm, ia_hbm, ib_hbm, o_hbm):
        def body(x_vmem, ia_vmem, ib_vmem):
            pltpu.sync_copy(x_vmem, o_hbm.at[ia_vmem.at[0, pl.ds(0, SC_WINDOW)]])
            pltpu.sync_copy(x_vmem, o_hbm.at[ib_vmem.at[0, pl.ds(0, SC_WINDOW)]])

        idx_spec = pl.BlockSpec((1, SC_IDX_LANES), lambda i: (i, 0))
        pltpu.emit_pipeline(
            body, grid=(T // SC_WINDOW,),
            in_specs=[pl.BlockSpec((SC_WINDOW, d), lambda i: (i, 0)), idx_spec, idx_spec],
            out_specs=[],
            core_axis_name=("core", "subcore"),
            dimension_semantics=(pltpu.PARALLEL,),
        )(x_hbm, ia_hbm, ib_hbm)

    return scatter(x, _pad_indices(idx_a), _pad_indices(idx_b))


def _sc_gather(x, idx):
    n = idx.shape[0]
    d = x.shape[1]

    @pl.kernel(out_type=jax.ShapeDtypeStruct((n, d), x.dtype), mesh=_sc_mesh())
    def gather(x_hbm, i_hbm, o_hbm):
        def body(i_vmem, o_vmem):
            pltpu.sync_copy(x_hbm.at[i_vmem.at[0, pl.ds(0, SC_WINDOW)]], o_vmem)

        pltpu.emit_pipeline(
            body, grid=(n // SC_WINDOW,),
            in_specs=[pl.BlockSpec((1, SC_IDX_LANES), lambda i: (i, 0))],
            out_specs=[pl.BlockSpec((SC_WINDOW, d), lambda i: (i, 0))],
            core_axis_name=("core", "subcore"),
            dimension_semantics=(pltpu.PARALLEL,),
        )(i_hbm, o_hbm)

    return gather(x, _pad_indices(idx))


def _expert_kernel(be_ref, nv_ref, nxt_ref, par_ref, xb_ref, wg_hbm, wu_hbm, wd_hbm, y_ref,
                   wg_st, wu_st, wd_st, wg_bf, wu_bf, wd_bf, sem):
    i = pl.program_id(0)
    nv = nv_ref[i]

    def fetch(expert, slot):
        return [pltpu.make_async_copy(src.at[expert], dst.at[slot], sem.at[slot, j])
                for j, (src, dst) in enumerate(((wg_hbm, wg_st), (wu_hbm, wu_st), (wd_hbm, wd_st)))]

    @pl.when(i == 0)
    def _():
        for cp in fetch(be_ref[0], par_ref[0]):
            cp.start()

    @pl.when((i == 0) | (be_ref[i] != be_ref[jnp.maximum(i - 1, 0)]))
    def _():
        slot = par_ref[i]
        for cp in fetch(be_ref[i], slot):
            cp.wait()
        wg_bf[...] = wg_st[slot].astype(BF16)
        wu_bf[...] = wu_st[slot].astype(BF16)
        wd_bf[...] = wd_st[slot].astype(BF16)

        @pl.when(nxt_ref[i] >= 0)
        def _():
            for cp in fetch(nxt_ref[i], 1 - slot):
                cp.start()

    @pl.when(nv > 0)
    def _():
        row = lax.broadcasted_iota(jnp.int32, (MOE_BLOCK, 1), 0)
        xb = _unpack_bf16_pairs(jnp.where(row < nv, xb_ref[...], jnp.uint32(0))).astype(BF16)
        g = _dot(xb, wg_bf[...])
        u = _dot(xb, wu_bf[...])
        h = (_silu(g) * u).astype(BF16)
        y_ref[...] = _pack_bf16_pairs(_dot(h, wd_bf[...]).astype(BF16))


def _experts(block_meta, xbuf, w_gate, w_up, w_down):
    P = xbuf.shape[0]
    nb = P // MOE_BLOCK
    rows = pl.BlockSpec((MOE_BLOCK, D_MODEL // 2), lambda i, *_: (i, 0))
    hbm = pl.BlockSpec(memory_space=pl.ANY)
    return pl.pallas_call(
        _expert_kernel,
        grid_spec=pltpu.PrefetchScalarGridSpec(
            num_scalar_prefetch=4,
            grid=(nb,),
            in_specs=[rows, hbm, hbm, hbm],
            out_specs=rows,
            scratch_shapes=[pltpu.VMEM((2, D_MODEL, D_EXPERT), F32), pltpu.VMEM((2, D_MODEL, D_EXPERT), F32),
                            pltpu.VMEM((2, D_EXPERT, D_MODEL), F32),
                            pltpu.VMEM((D_MODEL, D_EXPERT), BF16), pltpu.VMEM((D_MODEL, D_EXPERT), BF16),
                            pltpu.VMEM((D_EXPERT, D_MODEL), BF16),
                            pltpu.SemaphoreType.DMA((2, 3))]),
        out_shape=jax.ShapeDtypeStruct((P, D_MODEL // 2), jnp.uint32),
        compiler_params=_cparams(("arbitrary",)),
        name="experts",
    )(*block_meta, xbuf, w_gate, w_up, w_down)


def _combine_kernel(ra_ref, rb_ref, x1_ref, wt_ref, g2_ref, b2_ref, *rest):
    o_ref = rest[-1]
    w2 = wt_ref[...]
    w = jnp.concatenate([w2, jnp.zeros((6, w2.shape[1]), F32)], axis=0).T
    y = w[:, 0:1] * _unpack_bf16_pairs(ra_ref[...]) + w[:, 1:2] * _unpack_bf16_pairs(rb_ref[...])
    o_ref[...] = _layer_norm(DN_ALPHA * x1_ref[...] + y, g2_ref[...], b2_ref[...])


def _combine(rows2, x1, wt, g2, b2, tm, out_prev, tile0, t_total):
    T = x1.shape[0]
    nt = T // tm
    const = lambda shape: pl.BlockSpec(shape, lambda i: (0,) * len(shape))
    in_specs = [pl.BlockSpec((tm, D_MODEL // 2), lambda i: (i, 0)),
                pl.BlockSpec((tm, D_MODEL // 2), lambda i: (i + nt, 0)),
                pl.BlockSpec((tm, D_MODEL), lambda i: (i, 0)), pl.BlockSpec((2, tm), lambda i: (0, i)),
                const((1, D_MODEL)), const((1, D_MODEL))]
    args = [rows2, rows2, x1, wt, g2, b2]
    aliases = {}
    if out_prev is not None:
        in_specs.append(pl.BlockSpec(memory_space=pl.ANY))
        args.append(out_prev)
        aliases = {len(args) - 1: 0}
    return pl.pallas_call(
        _combine_kernel,
        grid=(nt,),
        in_specs=in_specs,
        out_specs=pl.BlockSpec((tm, D_MODEL), lambda i: (i + tile0, 0)),
        out_shape=jax.ShapeDtypeStruct((t_total, D_MODEL), F32),
        input_output_aliases=aliases,
        compiler_params=_cparams(("arbitrary",)),
        name="combine",
    )(*args)


def _slot_layout(eid, rank, counts, T):
    P = 2 * T + N_EXPERTS * MOE_BLOCK
    nb = P // MOE_BLOCK
    padded = ((counts + MOE_BLOCK - 1) // MOE_BLOCK) * MOE_BLOCK
    pend = jnp.cumsum(padded)
    pstart = pend - padded
    experts = jnp.arange(N_EXPERTS, dtype=jnp.int32)
    pos = rank + jnp.sum(jnp.where(eid[:, :, None] == experts, pstart, 0), axis=-1)

    block_start = jnp.arange(nb, dtype=jnp.int32) * MOE_BLOCK
    block_expert = jnp.minimum(jnp.sum(block_start[:, None] >= pend[None, :], axis=1), N_EXPERTS - 1)
    onehot = block_expert[:, None] == experts[None, :]
    look = lambda table: jnp.sum(jnp.where(onehot, table[None, :], 0), axis=1)
    block_valid = jnp.clip(look(pstart + counts) - block_start, 0, MOE_BLOCK)
    block_valid = jnp.where(block_start < pend[-1], block_valid, 0)
    present = (counts > 0) | (experts == N_EXPERTS - 1)
    later = present[None, :] & (experts[None, :] > experts[:, None])
    next_present = jnp.min(jnp.where(later, experts[None, :], N_EXPERTS), axis=1)
    next_present = jnp.where(next_present < N_EXPERTS, next_present, -1)
    runs_before = jnp.sum(present[None, :] & (experts[None, :] < experts[:, None]), axis=1)
    block_next = look(next_present)
    block_slot = look(runs_before % 2)
    i32 = lambda a: a.astype(jnp.int32)
    return i32(pos), (i32(block_expert), i32(block_valid), i32(block_next), i32(block_slot))


def kernel(x, lb_logits, w_in, hg_norm_w, sinks, w_branch_a, w_branch_b, w_out, ln1_g, ln1_b,
           router_group_w, router_group_b, router_expert_w, router_expert_b,
           w_exp_gate, w_exp_up, w_exp_down, ln2_g, ln2_b):
    B, S, D = x.shape
    assert D == D_MODEL and S % ATT_BLK == 0 and w_in.shape[0] == DEPTH == 1
    lb_all = jnp.cumsum(jax.nn.softmax(lb_logits.astype(F32), axis=0), axis=0)
    lb = lb_all[0].reshape(1, HG_W)
    w0 = w_in[0]
    w_f = jnp.concatenate([w0[:, :2 * HG_W], w0[:, 3 * HG_W:4 * HG_W]], axis=1).astype(BF16)
    w_b = jnp.concatenate([w0[:, 2 * HG_W:3 * HG_W], w0[:, 4 * HG_W:PROJ_W]], axis=1).astype(BF16)
    w_gate = w0[:, PROJ_W:].astype(BF16)
    nw = hg_norm_w[0].reshape(1, HG_DV).astype(F32)
    wa, wb, wo = w_branch_a[0].astype(BF16), w_branch_b[0].astype(BF16), w_out[0].astype(BF16)
    wr = jnp.zeros((40, D), F32).at[0:N_GROUPS].set(router_group_w[0].T).at[8:40].set(router_expert_w[0].T)
    br = jnp.zeros((40, 1), F32).at[0:N_GROUPS, 0].set(router_group_b[0]).at[8:40, 0].set(router_expert_b[0])
    g1, b1 = ln1_g[0].reshape(1, D), ln1_b[0].reshape(1, D)
    g2, b2 = ln2_g[0].reshape(1, D), ln2_b[0].reshape(1, D)

    n_parts = N_PARTS if B % N_PARTS == 0 else 1
    bp = B // n_parts
    tp = bp * S
    pick = lambda want: next(t for t in (want, 512, 256, ATT_BLK) if t <= want and tp % t == 0)
    tm_proj, tm_merge, tm_comb = pick(512), pick(1024), pick(1024)
    x2 = x.reshape(B * S, D)
    out = None
    for part in range(n_parts):
        proj_f, proj_b = _proj(x2, w_f, w_b, tm_proj, part * tp, tp)
        hg, att = _mixers(proj_f, proj_b, lb, nw, sinks[0].astype(F32), bp, S)
        x1, xp, eid, wt, rank, cnt = _merge(hg, att, x2, w_gate, wa, wb, wo, g1, b1, wr.astype(BF16), br,
                                            tm_merge, part * tp)
        pos, block_meta = _slot_layout(eid, rank, cnt[:, 0].astype(jnp.int32), tp)
        xbuf = _sc_scatter2(xp, pos[0], pos[1], 2 * tp + N_EXPERTS * MOE_BLOCK)
        ybuf = _experts(block_meta, xbuf, w_exp_gate[0], w_exp_up[0], w_exp_down[0])
        rows2 = _sc_gather(ybuf, pos.reshape(-1))
        out = _combine(rows2, x1, wt, g2, b2, tm_comb, out, part * (tp // tm_comb), B * S)
    return out.reshape(B, S, D)
```

```python
import functools

import numpy as np
import jax
import jax.numpy as jnp
from jax import lax
from jax.experimental import pallas as pl
from jax.experimental.pallas import tpu as pltpu
from jax.experimental.pallas import tpu_sc as plsc

F32 = jnp.float32
BF16 = jnp.bfloat16

D_MODEL = 1024
DEPTH = 1
HG_HEADS = 4
HG_DK = 128
HG_DV = 128
HG_W = HG_HEADS * HG_DK
CHUNK = 64
ATT_Q_HEADS = 8
ATT_KV_HEADS = 2
ATT_GROUP = ATT_Q_HEADS // ATT_KV_HEADS
ATT_HD = 64
ATT_QW = ATT_Q_HEADS * ATT_HD
ATT_KVW = ATT_KV_HEADS * ATT_HD
ATT_BLK = 128
N_GROUPS = 4
EPG = 8
N_EXPERTS = N_GROUPS * EPG
D_EXPERT = 512
MOE_BLOCK = 512
DN_ALPHA = (2.0 * DEPTH) ** 0.25
LN_EPS = 1e-5
RMS_EPS = 1e-6
NEG_INF = -1e30
LOG2_E = 1.4426950408889634

PROJ_W = 4 * HG_W + ATT_QW + 2 * ATT_KVW
PROJ_F = 3 * HG_W
PROJ_B = HG_W + ATT_QW + 2 * ATT_KVW
N_LEVELS = 6
N_ARG_GROUPS = N_LEVELS + 2
HG_SAFE_LB = 0.125

VMEM_LIMIT = 56 * 1024 * 1024
MERGE_SUB = 256
MIX_TILE = 512
N_PARTS = 2


def _cparams(sem):
    return pltpu.CompilerParams(dimension_semantics=sem, vmem_limit_bytes=VMEM_LIMIT)


def _dot(a, b):
    return jnp.dot(a, b, preferred_element_type=F32)


def _dot_nt(a, b):
    return lax.dot_general(a, b, (((1,), (1,)), ((), ())), preferred_element_type=F32)


def _dot_tn(a, b):
    return lax.dot_general(a, b, (((0,), (0,)), ((), ())), preferred_element_type=F32)


def _sigmoid(x):
    return 0.5 * jnp.tanh(0.5 * x) + 0.5


def _silu(x):
    return x * _sigmoid(x)


def _pack_bf16_pairs(xb):
    n = xb.shape[1] // 2
    lo = lax.bitcast_convert_type(xb[:, :n].astype(F32), jnp.uint32)
    hi = lax.bitcast_convert_type(xb[:, n:].astype(F32), jnp.uint32)
    return (lo >> 16) | hi


def _unpack_bf16_pairs(w):
    lo = lax.bitcast_convert_type(w << 16, F32)
    hi = lax.bitcast_convert_type(w & jnp.uint32(0xFFFF0000), F32)
    return jnp.concatenate([lo, hi], axis=1)


def _layer_norm(z, g, b):
    mu = jnp.mean(z, axis=-1, keepdims=True)
    zc = z - mu
    var = jnp.mean(zc * zc, axis=-1, keepdims=True)
    return zc * lax.rsqrt(var + LN_EPS) * g + b


def _proj_kernel(x_ref, wf_ref, wb_ref, of_ref, ob_ref):
    xb = x_ref[...].astype(BF16)
    of_ref[...] = _dot(xb, wf_ref[...])
    ob_ref[...] = _dot(xb, wb_ref[...]).astype(BF16)


def _proj(x2, w_f, w_b, tm, row0, T):
    tile0 = row0 // tm
    return pl.pallas_call(
        _proj_kernel,
        grid=(T // tm,),
        in_specs=[pl.BlockSpec((tm, D_MODEL), lambda i: (i + tile0, 0)),
                  pl.BlockSpec((D_MODEL, PROJ_F), lambda i: (0, 0), pipeline_mode=pl.Buffered(1)),
                  pl.BlockSpec((D_MODEL, PROJ_B), lambda i: (0, 0), pipeline_mode=pl.Buffered(1))],
        out_specs=[pl.BlockSpec((tm, PROJ_F), lambda i: (i, 0)), pl.BlockSpec((tm, PROJ_B), lambda i: (i, 0))],
        out_shape=[jax.ShapeDtypeStruct((T, PROJ_F), F32), jax.ShapeDtypeStruct((T, PROJ_B), BF16)],
        compiler_params=_cparams(("arbitrary",)),
        name="proj",
    )(x2, w_f, w_b)


def _hgrn_tables():
    C = CHUNK
    w = np.zeros((N_ARG_GROUPS, C, C), np.float32)
    masks = np.zeros((N_LEVELS, C, C), np.float32)
    for lvl in range(N_LEVELS):
        h = 1 << lvl
        for t in range(C):
            base = (t // (2 * h)) * 2 * h
            m = base + h - 1
            if (t // h) % 2 == 1:
                w[lvl, t, m + 1:t + 1] = 1.0
                masks[lvl, t, base:base + h] = 1.0
            else:
                w[lvl, t, t + 1:m + 1] = 1.0
    for t in range(C):
        w[N_LEVELS, t, :t + 1] = 1.0
        w[N_LEVELS + 1, t, t + 1:] = 1.0
    w = w.reshape(N_ARG_GROUPS * C, C)
    return np.concatenate([w, w], axis=1), masks


def _hgrn_chunk(q, fz, v, gate, c0, c1, nw, seg, mask_ref, state_ref, bounded):
    h = 0.5 * q
    qf = h + h * jnp.tanh(h)
    t1 = c1 * jnp.tanh(0.5 * fz)
    f = c0 + t1
    k = c1 - t1
    l2 = jnp.log2(f)
    v_bf = v.astype(BF16)
    qf_bf = qf.astype(BF16)
    k_bf = k.astype(BF16)

    l_hi = l2.astype(BF16)
    l_lo = (l2 - l_hi.astype(F32)).astype(BF16)
    l_split = jnp.concatenate([l_hi, l_lo], axis=0)
    row = lax.broadcasted_iota(jnp.int32, (CHUNK, 1), 0)
    if bounded:
        cum = _dot(seg[N_LEVELS * CHUNK:(N_LEVELS + 1) * CHUNK, :], l_split)
        mid = cum[CHUNK // 2 - 1:CHUNK // 2, :]
        last = cum[CHUNK - 1:CHUNK, :]
        e_cum = jnp.exp2(cum)
        e_suf = jnp.exp2(last - cum)
        a_q = qf_bf * jnp.exp2(cum - mid).astype(BF16)
        a_k = k_bf * jnp.exp2(mid - cum).astype(BF16)
        causal = row >= lax.broadcasted_iota(jnp.int32, (1, CHUNK), 1)
        scores = [jnp.where(causal, _dot_nt(a_q[:, hd * HG_DK:(hd + 1) * HG_DK], a_k[:, hd * HG_DK:(hd + 1) * HG_DK]),
                            0.0) for hd in range(HG_HEADS)]
        diag_v = None
    else:
        e = jnp.exp2(_dot(seg, l_split))
        scores = [jnp.zeros((CHUNK, CHUNK), F32) for _ in range(HG_HEADS)]
        for lvl in range(N_LEVELS):
            half = 1 << lvl
            e_l = e[lvl * CHUNK:(lvl + 1) * CHUNK, :].astype(BF16)
            if half >= 16:
                sel = jnp.concatenate([(qf_bf if (r0 // half) % 2 else k_bf)[r0:r0 + half]
                                       for r0 in range(0, CHUNK, half)], axis=0)
            else:
                sel = jnp.where((row // half) % 2 == 1, qf_bf, k_bf)
            a = sel * e_l
            m = mask_ref[lvl]
            for hd in range(HG_HEADS):
                a_h = a[:, hd * HG_DK:(hd + 1) * HG_DK]
                scores[hd] = scores[hd] + m * _dot_nt(a_h, a_h)
        e_cum = e[N_LEVELS * CHUNK:(N_LEVELS + 1) * CHUNK, :]
        e_suf = e[(N_LEVELS + 1) * CHUNK:(N_LEVELS + 2) * CHUNK, :]
        diag_v = qf * k

    q_in = qf_bf * e_cum.astype(BF16)
    k_out = k_bf * e_suf.astype(BF16)
    e_last = e_cum[CHUNK - 1:CHUNK, :]

    outs = []
    for hd in range(HG_HEADS):
        cols = slice(hd * HG_DK, (hd + 1) * HG_DK)
        st = state_ref[hd]
        o = _dot_nt(q_in[:, cols], st.astype(BF16)) + _dot(scores[hd].astype(BF16), v_bf[:, cols])
        if diag_v is not None:
            o = o + jnp.sum(diag_v[:, cols], axis=-1, keepdims=True) * v[:, cols]
        state_ref[hd] = st * e_last[:, cols] + _dot_tn(v_bf[:, cols], k_out[:, cols])
        o = o * lax.rsqrt(jnp.mean(o * o, axis=-1, keepdims=True) + RMS_EPS) * nw
        outs.append(o)
    hg = 0.5 * gate
    return jnp.concatenate(outs, axis=1) * (hg + hg * jnp.tanh(hg))


def _attn_bias():
    r = np.arange(ATT_BLK)[:, None]
    c = np.arange(2 * ATT_BLK)[None, :]
    dist = r + ATT_BLK - c
    window = (dist >= 0) & (dist < ATT_BLK)
    slopes = np.exp2(-8.0 * (np.arange(ATT_Q_HEADS, dtype=np.float32) + 1.0) / ATT_Q_HEADS).astype(np.float32)
    alibi = -slopes[:, None, None] * dist.astype(np.float32)[None]
    later = np.where(window[None], alibi, np.float32(NEG_INF))
    first = np.where((window & (c >= ATT_BLK))[None], alibi, np.float32(NEG_INF))
    return (np.stack([later, first]) * LOG2_E).astype(np.float32)


def _attn_block(q_ref, kv_cur, kv_prev, bias_ref, table, sink_ref, o_ref):
    lane = lax.broadcasted_iota(jnp.int32, (2 * ATT_BLK, 2 * ATT_KVW), 1)
    lo = (lane % ATT_KVW) < ATT_HD

    kv = jnp.concatenate([kv_prev, kv_cur], axis=0).astype(BF16)
    kv_sw = jnp.concatenate([kv[:, ATT_HD:ATT_KVW], kv[:, :ATT_HD],
                             kv[:, ATT_KVW + ATT_HD:], kv[:, ATT_KVW:ATT_KVW + ATT_HD]], axis=1)
    zero = jnp.zeros_like(kv)
    placed = {}
    for h in range(ATT_KV_HEADS):
        for off in range(2):
            src = kv if h == off else kv_sw
            placed[h, off] = jnp.where(lo if off == 0 else jnp.logical_not(lo), src, zero)

    scale = ATT_HD ** -0.5 * LOG2_E
    for pair in range(ATT_Q_HEADS // 2):
        qp = (q_ref[:, pair * 2 * ATT_HD:(pair + 1) * 2 * ATT_HD] * scale).astype(BF16)
        acc = jnp.zeros((ATT_BLK, 2 * ATT_HD), F32)
        for off in range(2):
            j = 2 * pair + off
            kvh = placed[j // ATT_GROUP, off]
            sink = sink_ref[j] * LOG2_E
            logits = _dot_nt(qp, kvh[:, :ATT_KVW]) + bias_ref[table, j]
            m = jnp.maximum(jnp.max(logits, axis=-1, keepdims=True), sink)
            p = jnp.exp2(logits - m)
            den = jnp.sum(p, axis=-1, keepdims=True) + jnp.exp2(sink - m)
            acc = acc + _dot(p.astype(BF16), kvh[:, ATT_KVW:]) / den
        o_ref[:, pair * 2 * ATT_HD:(pair + 1) * 2 * ATT_HD] = acc.astype(o_ref.dtype)


def _mixers_kernel(sink_ref, bounded_ref, q_ref, f_ref, i_ref, g_ref, aq_ref, kv_ref, kvp_ref, lb_ref, nw_ref,
                   seg_ref, mask_ref, bias_ref, hg_ref, at_ref, state_ref):
    first = pl.program_id(1) == 0

    @pl.when(first)
    def _():
        state_ref[...] = jnp.zeros_like(state_ref)

    def tile_body(bounded):
        lb = lb_ref[...]
        c0 = 0.5 + 0.5 * lb
        c1 = 0.5 - 0.5 * lb
        nw = nw_ref[...]
        seg = seg_ref[...]
        tile = q_ref.shape[0]
        for r0 in range(0, tile, CHUNK):
            rows = slice(r0, r0 + CHUNK)
            o = _hgrn_chunk(q_ref[rows, :], f_ref[rows, :], i_ref[rows, :], g_ref[rows, :], c0, c1, nw, seg,
                            mask_ref, state_ref, bounded)
            hg_ref[rows, :] = o.astype(hg_ref.dtype)
        for r0 in range(0, tile, ATT_BLK):
            rows = slice(r0, r0 + ATT_BLK)
            if r0 == 0:
                prev, table = kvp_ref[...], jnp.where(first, 1, 0)
            else:
                prev, table = kv_ref[r0 - ATT_BLK:r0, :], 0
            _attn_block(aq_ref.at[rows, :], kv_ref[rows, :], prev, bias_ref, table, sink_ref, at_ref.at[rows, :])

    pl.when(bounded_ref[0] == 1)(lambda: tile_body(True))
    pl.when(bounded_ref[0] != 1)(lambda: tile_body(False))


def _mixers(proj_f, proj_b, lb, norm_w, sinks, B, S):
    T = B * S
    tile = MIX_TILE if S % MIX_TILE == 0 else ATT_BLK
    nb = S // tile
    per = tile // ATT_BLK
    seg, masks = _hgrn_tables()
    const = lambda shape: pl.BlockSpec(shape, lambda b, n: (0,) * len(shape))
    hcol = lambda cb: pl.BlockSpec((tile, HG_W), lambda b, n: (b * nb + n, cb))
    kvcol = (HG_W + ATT_QW) // (2 * ATT_KVW)
    out = pl.BlockSpec((tile, HG_W), lambda b, n: (b * nb + n, 0))
    return pl.pallas_call(
        _mixers_kernel,
        grid=(B, nb),
        in_specs=[pl.BlockSpec(memory_space=pltpu.SMEM), pl.BlockSpec(memory_space=pltpu.SMEM),
                  hcol(0), hcol(1), hcol(0), hcol(2),
                  pl.BlockSpec((tile, ATT_QW), lambda b, n: (b * nb + n, HG_W // ATT_QW)),
                  pl.BlockSpec((tile, 2 * ATT_KVW), lambda b, n: (b * nb + n, kvcol)),
                  pl.BlockSpec((ATT_BLK, 2 * ATT_KVW),
                               lambda b, n: ((b * nb + n) * per - jnp.where(n > 0, 1, 0), kvcol)),
                  const((1, HG_W)), const((1, HG_DV)), const((N_ARG_GROUPS * CHUNK, 2 * CHUNK)),
                  const((N_LEVELS, CHUNK, CHUNK)), const((2, ATT_Q_HEADS, ATT_BLK, 2 * ATT_BLK))],
        out_specs=[out, out],
        out_shape=[jax.ShapeDtypeStruct((T, HG_W), BF16), jax.ShapeDtypeStruct((T, ATT_QW), BF16)],
        scratch_shapes=[pltpu.VMEM((HG_HEADS, HG_DV, HG_DK), F32)],
        compiler_params=_cparams(("arbitrary", "arbitrary")),
        name="mixers",
    )(sinks, (jnp.min(lb) >= HG_SAFE_LB).astype(jnp.int32).reshape(1),
      proj_f, proj_f, proj_b, proj_f, proj_b, proj_b, proj_b, lb, norm_w, jnp.asarray(seg, BF16),
      jnp.asarray(masks), jnp.asarray(_attn_bias()))


def _merge_kernel(hg_ref, at_ref, x_ref, wgate_ref, wa_ref, wb_ref, wo_ref, g1_ref, b1_ref,
                  wr_ref, br_ref, tri_ref, x1_ref, xp_ref, eid_ref, wt_ref, rank_ref, cnt_ref, base_ref, x1b_ref):
    tm = x_ref.shape[0]
    sub = min(tm, MERGE_SUB)
    for r0 in range(0, tm, sub):
        rows = slice(r0, r0 + sub)
        xs = x_ref[rows, :]
        gates = _dot(xs.astype(BF16), wgate_ref[...])
        ya = _dot(hg_ref[rows, :], wa_ref[...])
        yb = _dot(at_ref[rows, :], wb_ref[...])
        merged = _sigmoid(gates[:, :D_MODEL]) * ya + _sigmoid(gates[:, D_MODEL:]) * yb
        z = DN_ALPHA * xs + _dot(merged.astype(BF16), wo_ref[...])
        x1 = _layer_norm(z, g1_ref[...], b1_ref[...])
        x1_ref[rows, :] = x1
        x1b = x1.astype(BF16)
        x1b_ref[rows, :] = x1b
        xp_ref[rows, :] = _pack_bf16_pairs(x1b)

    lg = _dot_nt(wr_ref[...], x1b_ref[...]) + br_ref[...]
    g = lg[0:8, :]
    row8 = lax.broadcasted_iota(jnp.int32, (8, tm), 0)
    g = jnp.where(row8 < N_GROUPS, g, -jnp.inf)
    gmax = jnp.max(g, axis=0, keepdims=True)
    gsel = jnp.min(jnp.where(g == gmax, row8, 8), axis=0, keepdims=True)
    gw = 1.0 / jnp.sum(jnp.exp(g - gmax), axis=0, keepdims=True)
    el = jnp.where(gsel == 0, lg[8:16, :],
                   jnp.where(gsel == 1, lg[16:24, :], jnp.where(gsel == 2, lg[24:32, :], lg[32:40, :])))
    v1 = jnp.max(el, axis=0, keepdims=True)
    i1 = jnp.min(jnp.where(el == v1, row8, 8), axis=0, keepdims=True)
    el2 = jnp.where(row8 == i1, -jnp.inf, el)
    v2 = jnp.max(el2, axis=0, keepdims=True)
    i2 = jnp.min(jnp.where(el2 == v2, row8, 8), axis=0, keepdims=True)
    e2 = jnp.exp(v2 - v1)
    den = 1.0 + e2
    e_a = gsel * EPG + i1
    e_b = gsel * EPG + i2
    eid_ref[...] = jnp.concatenate([e_a, e_b], axis=0)
    wt_ref[...] = jnp.concatenate([gw / den, gw * e2 / den], axis=0)

    @pl.when(pl.program_id(0) == 0)
    def _():
        base_ref[...] = jnp.zeros_like(base_ref)

    row_e = lax.broadcasted_iota(jnp.int32, (N_EXPERTS, tm), 0)
    oh_a = jnp.where(row_e == e_a, 1.0, 0.0)
    oh_b = jnp.where(row_e == e_b, 1.0, 0.0)
    tri = tri_ref[...]
    pre_a = _dot(oh_a.astype(BF16), tri)
    pre_b = _dot(oh_b.astype(BF16), tri)
    cnt_a = jnp.sum(oh_a, axis=1, keepdims=True)
    cnt_b = jnp.sum(oh_b, axis=1, keepdims=True)
    base = base_ref[...]
    rank_a = jnp.sum(oh_a * (base + pre_a), axis=0, keepdims=True)
    rank_b = jnp.sum(oh_b * (base + cnt_a + pre_b), axis=0, keepdims=True)
    rank_ref[...] = jnp.concatenate([rank_a, rank_b], axis=0).astype(jnp.int32)
    base = base + cnt_a + cnt_b
    base_ref[...] = base
    cnt_ref[...] = jnp.broadcast_to(base, cnt_ref.shape)


def _merge(hg, att, x2, wgate, wa, wb, wo, g1, b1, wr, br, tm, row0):
    T = hg.shape[0]
    tile0 = row0 // tm
    row = lambda w: pl.BlockSpec((tm, w), lambda i: (i, 0))
    const = lambda shape: pl.BlockSpec(shape, lambda i: (0,) * len(shape), pipeline_mode=pl.Buffered(1))
    lanes = pl.BlockSpec((2, tm), lambda i: (0, i))
    tri = jnp.asarray(np.triu(np.ones((tm, tm), np.float32), 1), BF16)
    return pl.pallas_call(
        _merge_kernel,
        grid=(T // tm,),
        in_specs=[row(HG_W), row(ATT_QW),
                  pl.BlockSpec((tm, D_MODEL), lambda i: (i + tile0, 0)),
                  const((D_MODEL, 2 * D_MODEL)), const((HG_W, D_MODEL)), const((ATT_QW, D_MODEL)), const((D_MODEL, D_MODEL)),
                  const((1, D_MODEL)), const((1, D_MODEL)), const((40, D_MODEL)), const((40, 1)),
                  const((tm, tm))],
        out_specs=[row(D_MODEL), row(D_MODEL // 2), lanes, lanes, lanes,
                   pl.BlockSpec((N_EXPERTS, 128), lambda i: (0, 0))],
        out_shape=[jax.ShapeDtypeStruct((T, D_MODEL), F32),
                   jax.ShapeDtypeStruct((T, D_MODEL // 2), jnp.uint32),
                   jax.ShapeDtypeStruct((2, T), jnp.int32),
                   jax.ShapeDtypeStruct((2, T), F32),
                   jax.ShapeDtypeStruct((2, T), jnp.int32),
                   jax.ShapeDtypeStruct((N_EXPERTS, 128), F32)],
        scratch_shapes=[pltpu.VMEM((N_EXPERTS, 1), F32), pltpu.VMEM((tm, D_MODEL), BF16)],
        compiler_params=_cparams(("arbitrary",)),
        name="merge",
    )(hg, att, x2, wgate, wa, wb, wo, g1, b1, wr, br, tri)


SC_WINDOW = 64
SC_IDX_LANES = 128


def _pad_indices(idx):
    rows = idx.reshape(-1, SC_WINDOW)
    return jnp.pad(rows, ((0, 0), (0, SC_IDX_LANES - SC_WINDOW)))


def _sc_mesh():
    return plsc.VectorSubcoreMesh(core_axis_name="core", subcore_axis_name="subcore")


def _sc_scatter2(x, idx_a, idx_b, n_out):
    T, d = x.shape

    @pl.kernel(out_type=jax.ShapeDtypeStruct((n_out, d), x.dtype), mesh=_sc_mesh())
    def scatter(x_hbm, ia_hbm, ib_hbm, o_hbm):
        def body(x_vmem, ia_vmem, ib_vmem):
            pltpu.sync_copy(x_vmem, o_hbm.at[ia_vmem.at[0, pl.ds(0, SC_WINDOW)]])
            pltpu.sync_copy(x_vmem, o_hbm.at[ib_vmem.at[0, pl.ds(0, SC_WINDOW)]])

        idx_spec = pl.BlockSpec((1, SC_IDX_LANES), lambda i: (i, 0))
        pltpu.emit_pipeline(
            body, grid=(T // SC_WINDOW,),
            in_specs=[pl.BlockSpec((SC_WINDOW, d), lambda i: (i, 0)), idx_spec, idx_spec],
            out_specs=[],
            core_axis_name=("core", "subcore"),
            dimension_semantics=(pltpu.PARALLEL,),
        )(x_hbm, ia_hbm, ib_hbm)

    return scatter(x, _pad_indices(idx_a), _pad_indices(idx_b))


def _sc_gather(x, idx):
    n = idx.shape[0]
    d = x.shape[1]

    @pl.kernel(out_type=jax.ShapeDtypeStruct((n, d), x.dtype), mesh=_sc_mesh())
    def gather(x_hbm, i_hbm, o_hbm):
        def body(i_vmem, o_vmem):
            pltpu.sync_copy(x_hbm.at[i_vmem.at[0, pl.ds(0, SC_WINDOW)]], o_vmem)

        pltpu.emit_pipeline(
            body, grid=(n // SC_WINDOW,),
            in_specs=[pl.BlockSpec((1, SC_IDX_LANES), lambda i: (i, 0))],
            out_specs=[pl.BlockSpec((SC_WINDOW, d), lambda i: (i, 0))],
            core_axis_name=("core", "subcore"),
            dimension_semantics=(pltpu.PARALLEL,),
        )(i_hbm, o_hbm)

    return gather(x, _pad_indices(idx))


def _expert_kernel(be_ref, nv_ref, nxt_ref, par_ref, xb_ref, wg_hbm, wu_hbm, wd_hbm, y_ref,
                   wg_st, wu_st, wd_st, wg_bf, wu_bf, wd_bf, sem):
    i = pl.program_id(0)
    nv = nv_ref[i]

    def fetch(expert, slot):
        return [pltpu.make_async_copy(src.at[expert], dst.at[slot], sem.at[slot, j])
                for j, (src, dst) in enumerate(((wg_hbm, wg_st), (wu_hbm, wu_st), (wd_hbm, wd_st)))]

    @pl.when(i == 0)
    def _():
        for cp in fetch(be_ref[0], par_ref[0]):
            cp.start()

    @pl.when((i == 0) | (be_ref[i] != be_ref[jnp.maximum(i - 1, 0)]))
    def _():
        slot = par_ref[i]
        for cp in fetch(be_ref[i], slot):
            cp.wait()
        wg_bf[...] = wg_st[slot].astype(BF16)
        wu_bf[...] = wu_st[slot].astype(BF16)
        wd_bf[...] = wd_st[slot].astype(BF16)

        @pl.when(nxt_ref[i] >= 0)
        def _():
            for cp in fetch(nxt_ref[i], 1 - slot):
                cp.start()

    @pl.when(nv > 0)
    def _():
        row = lax.broadcasted_iota(jnp.int32, (MOE_BLOCK, 1), 0)
        xb = _unpack_bf16_pairs(jnp.where(row < nv, xb_ref[...], jnp.uint32(0))).astype(BF16)
        g = _dot(xb, wg_bf[...])
        u = _dot(xb, wu_bf[...])
        h = (_silu(g) * u).astype(BF16)
        y_ref[...] = _pack_bf16_pairs(_dot(h, wd_bf[...]).astype(BF16))


def _experts(block_meta, xbuf, w_gate, w_up, w_down):
    P = xbuf.shape[0]
    nb = P // MOE_BLOCK
    rows = pl.BlockSpec((MOE_BLOCK, D_MODEL // 2), lambda i, *_: (i, 0))
    hbm = pl.BlockSpec(memory_space=pl.ANY)
    return pl.pallas_call(
        _expert_kernel,
        grid_spec=pltpu.PrefetchScalarGridSpec(
            num_scalar_prefetch=4,
            grid=(nb,),
            in_specs=[rows, hbm, hbm, hbm],
            out_specs=rows,
            scratch_shapes=[pltpu.VMEM((2, D_MODEL, D_EXPERT), F32), pltpu.VMEM((2, D_MODEL, D_EXPERT), F32),
                            pltpu.VMEM((2, D_EXPERT, D_MODEL), F32),
                            pltpu.VMEM((D_MODEL, D_EXPERT), BF16), pltpu.VMEM((D_MODEL, D_EXPERT), BF16),
                            pltpu.VMEM((D_EXPERT, D_MODEL), BF16),
                            pltpu.SemaphoreType.DMA((2, 3))]),
        out_shape=jax.ShapeDtypeStruct((P, D_MODEL // 2), jnp.uint32),
        compiler_params=_cparams(("arbitrary",)),
        name="experts",
    )(*block_meta, xbuf, w_gate, w_up, w_down)


def _combine_kernel(ra_ref, rb_ref, x1_ref, wt_ref, g2_ref, b2_ref, *rest):
    o_ref = rest[-1]
    w2 = wt_ref[...]
    w = jnp.concatenate([w2, jnp.zeros((6, w2.shape[1]), F32)], axis=0).T
    y = w[:, 0:1] * _unpack_bf16_pairs(ra_ref[...]) + w[:, 1:2] * _unpack_bf16_pairs(rb_ref[...])
    o_ref[...] = _layer_norm(DN_ALPHA * x1_ref[...] + y, g2_ref[...], b2_ref[...])


def _combine(rows2, x1, wt, g2, b2, tm, out_prev, tile0, t_total):
    T = x1.shape[0]
    nt = T // tm
    const = lambda shape: pl.BlockSpec(shape, lambda i: (0,) * len(shape))
    in_specs = [pl.BlockSpec((tm, D_MODEL // 2), lambda i: (i, 0)),
                pl.BlockSpec((tm, D_MODEL // 2), lambda i: (i + nt, 0)),
                pl.BlockSpec((tm, D_MODEL), lambda i: (i, 0)), pl.BlockSpec((2, tm), lambda i: (0, i)),
                const((1, D_MODEL)), const((1, D_MODEL))]
    args = [rows2, rows2, x1, wt, g2, b2]
    aliases = {}
    if out_prev is not None:
        in_specs.append(pl.BlockSpec(memory_space=pl.ANY))
        args.append(out_prev)
        aliases = {len(args) - 1: 0}
    return pl.pallas_call(
        _combine_kernel,
        grid=(nt,),
        in_specs=in_specs,
        out_specs=pl.BlockSpec((tm, D_MODEL), lambda i: (i + tile0, 0)),
        out_shape=jax.ShapeDtypeStruct((t_total, D_MODEL), F32),
        input_output_aliases=aliases,
        compiler_params=_cparams(("arbitrary",)),
        name="combine",
    )(*args)


def _slot_layout(eid, rank, counts, T):
    P = 2 * T + N_EXPERTS * MOE_BLOCK
    nb = P // MOE_BLOCK
    padded = ((counts + MOE_BLOCK - 1) // MOE_BLOCK) * MOE_BLOCK
    pend = jnp.cumsum(padded)
    pstart = pend - padded
    experts = jnp.arange(N_EXPERTS, dtype=jnp.int32)
    pos = rank + jnp.sum(jnp.where(eid[:, :, None] == experts, pstart, 0), axis=-1)

    block_start = jnp.arange(nb, dtype=jnp.int32) * MOE_BLOCK
    block_expert = jnp.minimum(jnp.sum(block_start[:, None] >= pend[None, :], axis=1), N_EXPERTS - 1)
    onehot = block_expert[:, None] == experts[None, :]
    look = lambda table: jnp.sum(jnp.where(onehot, table[None, :], 0), axis=1)
    block_valid = jnp.clip(look(pstart + counts) - block_start, 0, MOE_BLOCK)
    block_valid = jnp.where(block_start < pend[-1], block_valid, 0)
    present = (counts > 0) | (experts == N_EXPERTS - 1)
    later = present[None, :] & (experts[None, :] > experts[:, None])
    next_present = jnp.min(jnp.where(later, experts[None, :], N_EXPERTS), axis=1)
    next_present = jnp.where(next_present < N_EXPERTS, next_present, -1)
    runs_before = jnp.sum(present[None, :] & (experts[None, :] < experts[:, None]), axis=1)
    block_next = look(next_present)
    block_slot = look(runs_before % 2)
    i32 = lambda a: a.astype(jnp.int32)
    return i32(pos), (i32(block_expert), i32(block_valid), i32(block_next), i32(block_slot))


def kernel(x, lb_logits, w_in, hg_norm_w, sinks, w_branch_a, w_branch_b, w_out, ln1_g, ln1_b,
           router_group_w, router_group_b, router_expert_w, router_expert_b,
           w_exp_gate, w_exp_up, w_exp_down, ln2_g, ln2_b):
    B, S, D = x.shape
    assert D == D_MODEL and S % ATT_BLK == 0 and w_in.shape[0] == DEPTH == 1
    lb_all = jnp.cumsum(jax.nn.softmax(lb_logits.astype(F32), axis=0), axis=0)
    lb = lb_all[0].reshape(1, HG_W)
    w0 = w_in[0]
    w_f = jnp.concatenate([w0[:, :2 * HG_W], w0[:, 3 * HG_W:4 * HG_W]], axis=1).astype(BF16)
    w_b = jnp.concatenate([w0[:, 2 * HG_W:3 * HG_W], w0[:, 4 * HG_W:PROJ_W]], axis=1).astype(BF16)
    w_gate = w0[:, PROJ_W:].astype(BF16)
    nw = hg_norm_w[0].reshape(1, HG_DV).astype(F32)
    wa, wb, wo = w_branch_a[0].astype(BF16), w_branch_b[0].astype(BF16), w_out[0].astype(BF16)
    wr = jnp.zeros((40, D), F32).at[0:N_GROUPS].set(router_group_w[0].T).at[8:40].set(router_expert_w[0].T)
    br = jnp.zeros((40, 1), F32).at[0:N_GROUPS, 0].set(router_group_b[0]).at[8:40, 0].set(router_expert_b[0])
    g1, b1 = ln1_g[0].reshape(1, D), ln1_b[0].reshape(1, D)
    g2, b2 = ln2_g[0].reshape(1, D), ln2_b[0].reshape(1, D)

    n_parts = N_PARTS if B % N_PARTS == 0 else 1
    bp = B // n_parts
    tp = bp * S
    pick = lambda want: next(t for t in (want, 512, 256, ATT_BLK) if t <= want and tp % t == 0)
    tm_proj, tm_merge, tm_comb = pick(1024), pick(1024), pick(1024)
    x2 = x.reshape(B * S, D)
    out = None
    for part in range(n_parts):
        proj_f, proj_b = _proj(x2, w_f, w_b, tm_proj, part * tp, tp)
        hg, att = _mixers(proj_f, proj_b, lb, nw, sinks[0].astype(F32), bp, S)
        x1, xp, eid, wt, rank, cnt = _merge(hg, att, x2, w_gate, wa, wb, wo, g1, b1, wr.astype(BF16), br,
                                            tm_merge, part * tp)
        pos, block_meta = _slot_layout(eid, rank, cnt[:, 0].astype(jnp.int32), tp)
        xbuf = _sc_scatter2(xp, pos[0], pos[1], 2 * tp + N_EXPERTS * MOE_BLOCK)
        ybuf = _experts(block_meta, xbuf, w_exp_gate[0], w_exp_up[0], w_exp_down[0])
        rows2 = _sc_gather(ybuf, pos.reshape(-1))
        out = _combine(rows2, x1, wt, g2, b2, tm_comb, out, part * (tp // tm_comb), B * S)
    return out.reshape(B, S, D)
```

```python
import numpy as np
import jax
import jax.numpy as jnp
from jax import lax
from jax.experimental import pallas as pl
from jax.experimental.pallas import tpu as pltpu
from jax.experimental.pallas import tpu_sc as plsc

F32 = jnp.float32
BF16 = jnp.bfloat16

D_MODEL = 1024
DEPTH = 1
HG_HEADS = 4
HG_DK = 128
HG_DV = 128
HG_W = HG_HEADS * HG_DK
CHUNK = 64
ATT_Q_HEADS = 8
ATT_KV_HEADS = 2
ATT_GROUP = ATT_Q_HEADS // ATT_KV_HEADS
ATT_HD = 64
ATT_QW = ATT_Q_HEADS * ATT_HD
ATT_KVW = ATT_KV_HEADS * ATT_HD
ATT_BLK = 128
N_GROUPS = 4
EPG = 8
N_EXPERTS = N_GROUPS * EPG
D_EXPERT = 512
MOE_BLOCK = 512
DN_ALPHA = (2.0 * DEPTH) ** 0.25
LN_EPS = 1e-5
RMS_EPS = 1e-6
NEG_INF = -1e30
LOG2_E = 1.4426950408889634

PROJ_W = 4 * HG_W + ATT_QW + 2 * ATT_KVW
PROJ_F = 3 * HG_W
PROJ_B = HG_W + ATT_QW + 2 * ATT_KVW
N_LEVELS = 6
N_ARG_GROUPS = N_LEVELS + 2
HG_SAFE_LB = 0.125

VMEM_LIMIT = 56 * 1024 * 1024
MERGE_SUB = 256
MIX_TILE = 512
N_PARTS = 2


def _cparams(sem):
    return pltpu.CompilerParams(dimension_semantics=sem, vmem_limit_bytes=VMEM_LIMIT)


def _dot(a, b):
    return jnp.dot(a, b, preferred_element_type=F32)


def _dot_nt(a, b):
    return lax.dot_general(a, b, (((1,), (1,)), ((), ())), preferred_element_type=F32)


def _dot_tn(a, b):
    return lax.dot_general(a, b, (((0,), (0,)), ((), ())), preferred_element_type=F32)


def _sigmoid(x):
    return 0.5 * jnp.tanh(0.5 * x) + 0.5


def _silu(x):
    return x * _sigmoid(x)


def _pack_bf16_pairs(xb):
    n = xb.shape[1] // 2
    lo = lax.bitcast_convert_type(xb[:, :n].astype(F32), jnp.uint32)
    hi = lax.bitcast_convert_type(xb[:, n:].astype(F32), jnp.uint32)
    return (lo >> 16) | hi


def _unpack_bf16_pairs(w):
    lo = lax.bitcast_convert_type(w << 16, F32)
    hi = lax.bitcast_convert_type(w & jnp.uint32(0xFFFF0000), F32)
    return jnp.concatenate([lo, hi], axis=1)


def _layer_norm(z, g, b):
    mu = jnp.mean(z, axis=-1, keepdims=True)
    zc = z - mu
    var = jnp.mean(zc * zc, axis=-1, keepdims=True)
    return zc * lax.rsqrt(var + LN_EPS) * g + b


def _proj_kernel(x_ref, wf_ref, wb_ref, of_ref, ob_ref):
    xb = x_ref[...].astype(BF16)
    of_ref[...] = _dot(xb, wf_ref[...])
    ob_ref[...] = _dot(xb, wb_ref[...]).astype(BF16)


def _proj(x2, w_f, w_b, tm, row0, T):
    tile0 = row0 // tm
    return pl.pallas_call(
        _proj_kernel,
        grid=(T // tm,),
        in_specs=[pl.BlockSpec((tm, D_MODEL), lambda i: (i + tile0, 0)),
                  pl.BlockSpec((D_MODEL, PROJ_F), lambda i: (0, 0), pipeline_mode=pl.Buffered(1)),
                  pl.BlockSpec((D_MODEL, PROJ_B), lambda i: (0, 0), pipeline_mode=pl.Buffered(1))],
        out_specs=[pl.BlockSpec((tm, PROJ_F), lambda i: (i, 0)), pl.BlockSpec((tm, PROJ_B), lambda i: (i, 0))],
        out_shape=[jax.ShapeDtypeStruct((T, PROJ_F), F32), jax.ShapeDtypeStruct((T, PROJ_B), BF16)],
        compiler_params=_cparams(("arbitrary",)),
        name="proj",
    )(x2, w_f, w_b)


def _hgrn_tables():
    C = CHUNK
    w = np.zeros((N_ARG_GROUPS, C, C), np.float32)
    masks = np.zeros((N_LEVELS, C, C), np.float32)
    for lvl in range(N_LEVELS):
        h = 1 << lvl
        for t in range(C):
            base = (t // (2 * h)) * 2 * h
            m = base + h - 1
            if (t // h) % 2 == 1:
                w[lvl, t, m + 1:t + 1] = 1.0
                masks[lvl, t, base:base + h] = 1.0
            else:
                w[lvl, t, t + 1:m + 1] = 1.0
    for t in range(C):
        w[N_LEVELS, t, :t + 1] = 1.0
        w[N_LEVELS + 1, t, t + 1:] = 1.0
    w = w.reshape(N_ARG_GROUPS * C, C)
    return np.concatenate([w, w], axis=1), masks


def _hgrn_chunk(q, fz, v, gate, c0, c1, nw, seg, mask_ref, state_ref, bounded):
    h = 0.5 * q
    qf = h + h * jnp.tanh(h)
    t1 = c1 * jnp.tanh(0.5 * fz)
    f = c0 + t1
    k = c1 - t1
    l2 = jnp.log2(f)
    v_bf = v.astype(BF16)
    qf_bf = qf.astype(BF16)
    k_bf = k.astype(BF16)

    l_hi = l2.astype(BF16)
    l_lo = (l2 - l_hi.astype(F32)).astype(BF16)
    l_split = jnp.concatenate([l_hi, l_lo], axis=0)
    row = lax.broadcasted_iota(jnp.int32, (CHUNK, 1), 0)
    if bounded:
        cum = _dot(seg[N_LEVELS * CHUNK:(N_LEVELS + 1) * CHUNK, :], l_split)
        mid = cum[CHUNK // 2 - 1:CHUNK // 2, :]
        last = cum[CHUNK - 1:CHUNK, :]
        e_cum = jnp.exp2(cum)
        e_suf = jnp.exp2(last - cum)
        a_q = qf_bf * jnp.exp2(cum - mid).astype(BF16)
        a_k = k_bf * jnp.exp2(mid - cum).astype(BF16)
        causal = row >= lax.broadcasted_iota(jnp.int32, (1, CHUNK), 1)
        scores = [jnp.where(causal, _dot_nt(a_q[:, hd * HG_DK:(hd + 1) * HG_DK], a_k[:, hd * HG_DK:(hd + 1) * HG_DK]),
                            0.0) for hd in range(HG_HEADS)]
        diag_v = None
    else:
        e = jnp.exp2(_dot(seg, l_split))
        scores = [jnp.zeros((CHUNK, CHUNK), F32) for _ in range(HG_HEADS)]
        for lvl in range(N_LEVELS):
            half = 1 << lvl
            e_l = e[lvl * CHUNK:(lvl + 1) * CHUNK, :].astype(BF16)
            if half >= 16:
                sel = jnp.concatenate([(qf_bf if (r0 // half) % 2 else k_bf)[r0:r0 + half]
                                       for r0 in range(0, CHUNK, half)], axis=0)
            else:
                sel = jnp.where((row // half) % 2 == 1, qf_bf, k_bf)
            a = sel * e_l
            m = mask_ref[lvl]
            for hd in range(HG_HEADS):
                a_h = a[:, hd * HG_DK:(hd + 1) * HG_DK]
                scores[hd] = scores[hd] + m * _dot_nt(a_h, a_h)
        e_cum = e[N_LEVELS * CHUNK:(N_LEVELS + 1) * CHUNK, :]
        e_suf = e[(N_LEVELS + 1) * CHUNK:(N_LEVELS + 2) * CHUNK, :]
        diag_v = qf * k

    q_in = qf_bf * e_cum.astype(BF16)
    k_out = k_bf * e_suf.astype(BF16)
    e_last = e_cum[CHUNK - 1:CHUNK, :]

    outs = []
    for hd in range(HG_HEADS):
        cols = slice(hd * HG_DK, (hd + 1) * HG_DK)
        st = state_ref[hd]
        o = _dot_nt(q_in[:, cols], st.astype(BF16)) + _dot(scores[hd].astype(BF16), v_bf[:, cols])
        if diag_v is not None:
            o = o + jnp.sum(diag_v[:, cols], axis=-1, keepdims=True) * v[:, cols]
        state_ref[hd] = st * e_last[:, cols] + _dot_tn(v_bf[:, cols], k_out[:, cols])
        o = o * lax.rsqrt(jnp.mean(o * o, axis=-1, keepdims=True) + RMS_EPS) * nw
        outs.append(o)
    hg = 0.5 * gate
    return jnp.concatenate(outs, axis=1) * (hg + hg * jnp.tanh(hg))


def _attn_bias():
    r = np.arange(ATT_BLK)[:, None]
    c = np.arange(2 * ATT_BLK)[None, :]
    dist = r + ATT_BLK - c
    window = (dist >= 0) & (dist < ATT_BLK)
    slopes = np.exp2(-8.0 * (np.arange(ATT_Q_HEADS, dtype=np.float32) + 1.0) / ATT_Q_HEADS).astype(np.float32)
    alibi = -slopes[:, None, None] * dist.astype(np.float32)[None]
    later = np.where(window[None], alibi, np.float32(NEG_INF))
    first = np.where((window & (c >= ATT_BLK))[None], alibi, np.float32(NEG_INF))
    return (np.stack([later, first]) * LOG2_E).astype(np.float32)


def _attn_block(q_ref, kv_cur, kv_prev, bias_ref, table, sink_ref, o_ref):
    lane = lax.broadcasted_iota(jnp.int32, (2 * ATT_BLK, 2 * ATT_KVW), 1)
    lo = (lane % ATT_KVW) < ATT_HD

    kv = jnp.concatenate([kv_prev, kv_cur], axis=0).astype(BF16)
    kv_sw = jnp.concatenate([kv[:, ATT_HD:ATT_KVW], kv[:, :ATT_HD],
                             kv[:, ATT_KVW + ATT_HD:], kv[:, ATT_KVW:ATT_KVW + ATT_HD]], axis=1)
    zero = jnp.zeros_like(kv)
    placed = {}
    for h in range(ATT_KV_HEADS):
        for off in range(2):
            src = kv if h == off else kv_sw
            placed[h, off] = jnp.where(lo if off == 0 else jnp.logical_not(lo), src, zero)

    scale = ATT_HD ** -0.5 * LOG2_E
    for pair in range(ATT_Q_HEADS // 2):
        qp = (q_ref[:, pair * 2 * ATT_HD:(pair + 1) * 2 * ATT_HD] * scale).astype(BF16)
        acc = jnp.zeros((ATT_BLK, 2 * ATT_HD), F32)
        for off in range(2):
            j = 2 * pair + off
            kvh = placed[j // ATT_GROUP, off]
            sink = sink_ref[j] * LOG2_E
            logits = _dot_nt(qp, kvh[:, :ATT_KVW]) + bias_ref[table, j]
            m = jnp.maximum(jnp.max(logits, axis=-1, keepdims=True), sink)
            p = jnp.exp2(logits - m)
            den = jnp.sum(p, axis=-1, keepdims=True) + jnp.exp2(sink - m)
            acc = acc + _dot(p.astype(BF16), kvh[:, ATT_KVW:]) / den
        o_ref[:, pair * 2 * ATT_HD:(pair + 1) * 2 * ATT_HD] = acc.astype(o_ref.dtype)


def _mixers_kernel(sink_ref, bounded_ref, q_ref, f_ref, i_ref, g_ref, aq_ref, kv_ref, kvp_ref, lb_ref, nw_ref,
                   seg_ref, mask_ref, bias_ref, hg_ref, at_ref, state_ref):
    first = pl.program_id(1) == 0

    @pl.when(first)
    def _():
        state_ref[...] = jnp.zeros_like(state_ref)

    def tile_body(bounded):
        lb = lb_ref[...]
        c0 = 0.5 + 0.5 * lb
        c1 = 0.5 - 0.5 * lb
        nw = nw_ref[...]
        seg = seg_ref[...]
        tile = q_ref.shape[0]
        for r0 in range(0, tile, CHUNK):
            rows = slice(r0, r0 + CHUNK)
            o = _hgrn_chunk(q_ref[rows, :], f_ref[rows, :], i_ref[rows, :], g_ref[rows, :], c0, c1, nw, seg,
                            mask_ref, state_ref, bounded)
            hg_ref[rows, :] = o.astype(hg_ref.dtype)
        for r0 in range(0, tile, ATT_BLK):
            rows = slice(r0, r0 + ATT_BLK)
            if r0 == 0:
                prev, table = kvp_ref[...], jnp.where(first, 1, 0)
            else:
                prev, table = kv_ref[r0 - ATT_BLK:r0, :], 0
            _attn_block(aq_ref.at[rows, :], kv_ref[rows, :], prev, bias_ref, table, sink_ref, at_ref.at[rows, :])

    pl.when(bounded_ref[0] == 1)(lambda: tile_body(True))
    pl.when(bounded_ref[0] != 1)(lambda: tile_body(False))


def _mixers(proj_f, proj_b, lb, norm_w, sinks, B, S):
    T = B * S
    tile = MIX_TILE if S % MIX_TILE == 0 else ATT_BLK
    nb = S // tile
    per = tile // ATT_BLK
    seg, masks = _hgrn_tables()
    const = lambda shape: pl.BlockSpec(shape, lambda b, n: (0,) * len(shape))
    hcol = lambda cb: pl.BlockSpec((tile, HG_W), lambda b, n: (b * nb + n, cb))
    kvcol = (HG_W + ATT_QW) // (2 * ATT_KVW)
    out = pl.BlockSpec((tile, HG_W), lambda b, n: (b * nb + n, 0))
    return pl.pallas_call(
        _mixers_kernel,
        grid=(B, nb),
        in_specs=[pl.BlockSpec(memory_space=pltpu.SMEM), pl.BlockSpec(memory_space=pltpu.SMEM),
                  hcol(0), hcol(1), hcol(0), hcol(2),
                  pl.BlockSpec((tile, ATT_QW), lambda b, n: (b * nb + n, HG_W // ATT_QW)),
                  pl.BlockSpec((tile, 2 * ATT_KVW), lambda b, n: (b * nb + n, kvcol)),
                  pl.BlockSpec((ATT_BLK, 2 * ATT_KVW),
                               lambda b, n: ((b * nb + n) * per - jnp.where(n > 0, 1, 0), kvcol)),
                  const((1, HG_W)), const((1, HG_DV)), const((N_ARG_GROUPS * CHUNK, 2 * CHUNK)),
                  const((N_LEVELS, CHUNK, CHUNK)), const((2, ATT_Q_HEADS, ATT_BLK, 2 * ATT_BLK))],
        out_specs=[out, out],
        out_shape=[jax.ShapeDtypeStruct((T, HG_W), BF16), jax.ShapeDtypeStruct((T, ATT_QW), BF16)],
        scratch_shapes=[pltpu.VMEM((HG_HEADS, HG_DV, HG_DK), F32)],
        compiler_params=_cparams(("arbitrary", "arbitrary")),
        name="mixers",
    )(sinks, (jnp.min(lb) >= HG_SAFE_LB).astype(jnp.int32).reshape(1),
      proj_f, proj_f, proj_b, proj_f, proj_b, proj_b, proj_b, lb, norm_w, jnp.asarray(seg, BF16),
      jnp.asarray(masks), jnp.asarray(_attn_bias()))


def _merge_kernel(hg_ref, at_ref, x_ref, wgate_ref, wa_ref, wb_ref, wo_ref, g1_ref, b1_ref,
                  wr_ref, br_ref, tri_ref, x1_ref, xp_ref, eid_ref, wt_ref, rank_ref, cnt_ref, base_ref, x1b_ref):
    tm = x_ref.shape[0]
    sub = min(tm, MERGE_SUB)
    for r0 in range(0, tm, sub):
        rows = slice(r0, r0 + sub)
        xs = x_ref[rows, :]
        gates = _dot(xs.astype(BF16), wgate_ref[...])
        ya = _dot(hg_ref[rows, :], wa_ref[...])
        yb = _dot(at_ref[rows, :], wb_ref[...])
        merged = _sigmoid(gates[:, :D_MODEL]) * ya + _sigmoid(gates[:, D_MODEL:]) * yb
        z = DN_ALPHA * xs + _dot(merged.astype(BF16), wo_ref[...])
        x1 = _layer_norm(z, g1_ref[...], b1_ref[...])
        x1_ref[rows, :] = x1
        x1b = x1.astype(BF16)
        x1b_ref[rows, :] = x1b
        xp_ref[rows, :] = _pack_bf16_pairs(x1b)

    lg = _dot_nt(wr_ref[...], x1b_ref[...]) + br_ref[...]
    g = lg[0:8, :]
    row8 = lax.broadcasted_iota(jnp.int32, (8, tm), 0)
    g = jnp.where(row8 < N_GROUPS, g, -jnp.inf)
    gmax = jnp.max(g, axis=0, keepdims=True)
    gsel = jnp.min(jnp.where(g == gmax, row8, 8), axis=0, keepdims=True)
    gw = 1.0 / jnp.sum(jnp.exp(g - gmax), axis=0, keepdims=True)
    el = jnp.where(gsel == 0, lg[8:16, :],
                   jnp.where(gsel == 1, lg[16:24, :], jnp.where(gsel == 2, lg[24:32, :], lg[32:40, :])))
    v1 = jnp.max(el, axis=0, keepdims=True)
    i1 = jnp.min(jnp.where(el == v1, row8, 8), axis=0, keepdims=True)
    el2 = jnp.where(row8 == i1, -jnp.inf, el)
    v2 = jnp.max(el2, axis=0, keepdims=True)
    i2 = jnp.min(jnp.where(el2 == v2, row8, 8), axis=0, keepdims=True)
    e2 = jnp.exp(v2 - v1)
    den = 1.0 + e2
    e_a = gsel * EPG + i1
    e_b = gsel * EPG + i2
    eid_ref[...] = jnp.concatenate([e_a, e_b], axis=0)
    wt_ref[...] = jnp.concatenate([gw / den, gw * e2 / den], axis=0)

    @pl.when(pl.program_id(0) == 0)
    def _():
        base_ref[...] = jnp.zeros_like(base_ref)

    row_e = lax.broadcasted_iota(jnp.int32, (N_EXPERTS, tm), 0)
    oh_a = jnp.where(row_e == e_a, 1.0, 0.0)
    oh_b = jnp.where(row_e == e_b, 1.0, 0.0)
    tri = tri_ref[...]
    pre_a = _dot(oh_a.astype(BF16), tri)
    pre_b = _dot(oh_b.astype(BF16), tri)
    cnt_a = jnp.sum(oh_a, axis=1, keepdims=True)
    cnt_b = jnp.sum(oh_b, axis=1, keepdims=True)
    base = base_ref[...]
    rank_a = jnp.sum(oh_a * (base + pre_a), axis=0, keepdims=True)
    rank_b = jnp.sum(oh_b * (base + cnt_a + pre_b), axis=0, keepdims=True)
    rank_ref[...] = jnp.concatenate([rank_a, rank_b], axis=0).astype(jnp.int32)
    base = base + cnt_a + cnt_b
    base_ref[...] = base
    cnt_ref[...] = jnp.broadcast_to(base, cnt_ref.shape)


def _merge(hg, att, x2, wgate, wa, wb, wo, g1, b1, wr, br, tm, row0):
    T = hg.shape[0]
    tile0 = row0 // tm
    row = lambda w: pl.BlockSpec((tm, w), lambda i: (i, 0))
    const = lambda shape: pl.BlockSpec(shape, lambda i: (0,) * len(shape), pipeline_mode=pl.Buffered(1))
    lanes = pl.BlockSpec((2, tm), lambda i: (0, i))
    tri = jnp.asarray(np.triu(np.ones((tm, tm), np.float32), 1), BF16)
    return pl.pallas_call(
        _merge_kernel,
        grid=(T // tm,),
        in_specs=[row(HG_W), row(ATT_QW),
                  pl.BlockSpec((tm, D_MODEL), lambda i: (i + tile0, 0)),
                  const((D_MODEL, 2 * D_MODEL)), const((HG_W, D_MODEL)), const((ATT_QW, D_MODEL)), const((D_MODEL, D_MODEL)),
                  const((1, D_MODEL)), const((1, D_MODEL)), const((40, D_MODEL)), const((40, 1)),
                  const((tm, tm))],
        out_specs=[row(D_MODEL), row(D_MODEL // 2), lanes, lanes, lanes,
                   pl.BlockSpec((N_EXPERTS, 128), lambda i: (0, 0))],
        out_shape=[jax.ShapeDtypeStruct((T, D_MODEL), F32),
                   jax.ShapeDtypeStruct((T, D_MODEL // 2), jnp.uint32),
                   jax.ShapeDtypeStruct((2, T), jnp.int32),
                   jax.ShapeDtypeStruct((2, T), F32),
                   jax.ShapeDtypeStruct((2, T), jnp.int32),
                   jax.ShapeDtypeStruct((N_EXPERTS, 128), F32)],
        scratch_shapes=[pltpu.VMEM((N_EXPERTS, 1), F32), pltpu.VMEM((tm, D_MODEL), BF16)],
        compiler_params=_cparams(("arbitrary",)),
        name="merge",
    )(hg, att, x2, wgate, wa, wb, wo, g1, b1, wr, br, tri)


SC_WINDOW = 64
SC_IDX_LANES = 128


def _pad_indices(idx):
    rows = idx.reshape(-1, SC_WINDOW)
    return jnp.pad(rows, ((0, 0), (0, SC_IDX_LANES - SC_WINDOW)))


def _sc_mesh():
    return plsc.VectorSubcoreMesh(core_axis_name="core", subcore_axis_name="subcore")


def _sc_scatter2(x, idx_a, idx_b, n_out):
    T, d = x.shape

    @pl.kernel(out_type=jax.ShapeDtypeStruct((n_out, d), x.dtype), mesh=_sc_mesh())
    def scatter(x_hbm, ia_hbm, ib_hbm, o_hbm):
        def body(x_vmem, ia_vmem, ib_vmem):
            pltpu.sync_copy(x_vmem, o_hbm.at[ia_vmem.at[0, pl.ds(0, SC_WINDOW)]])
            pltpu.sync_copy(x_vmem, o_hbm.at[ib_vmem.at[0, pl.ds(0, SC_WINDOW)]])

        idx_spec = pl.BlockSpec((1, SC_IDX_LANES), lambda i: (i, 0))
        pltpu.emit_pipeline(
            body, grid=(T // SC_WINDOW,),
            in_specs=[pl.BlockSpec((SC_WINDOW, d), lambda i: (i, 0)), idx_spec, idx_spec],
            out_specs=[],
            core_axis_name=("core", "subcore"),
            dimension_semantics=(pltpu.PARALLEL,),
        )(x_hbm, ia_hbm, ib_hbm)

    return scatter(x, _pad_indices(idx_a), _pad_indices(idx_b))


def _sc_gather(x, idx):
    n = idx.shape[0]
    d = x.shape[1]

    @pl.kernel(out_type=jax.ShapeDtypeStruct((n, d), x.dtype), mesh=_sc_mesh())
    def gather(x_hbm, i_hbm, o_hbm):
        def body(i_vmem, o_vmem):
            pltpu.sync_copy(x_hbm.at[i_vmem.at[0, pl.ds(0, SC_WINDOW)]], o_vmem)

        pltpu.emit_pipeline(
            body, grid=(n // SC_WINDOW,),
            in_specs=[pl.BlockSpec((1, SC_IDX_LANES), lambda i: (i, 0))],
            out_specs=[pl.BlockSpec((SC_WINDOW, d), lambda i: (i, 0))],
            core_axis_name=("core", "subcore"),
            dimension_semantics=(pltpu.PARALLEL,),
        )(i_hbm, o_hbm)

    return gather(x, _pad_indices(idx))


def _expert_kernel(be_ref, nv_ref, nxt_ref, par_ref, xb_ref, wg_hbm, wu_hbm, wd_hbm, y_ref,
                   wg_st, wu_st, wd_st, wg_bf, wu_bf, wd_bf, sem):
    i = pl.program_id(0)
    nv = nv_ref[i]

    def fetch(expert, slot):
        return [pltpu.make_async_copy(src.at[expert], dst.at[slot], sem.at[slot, j])
                for j, (src, dst) in enumerate(((wg_hbm, wg_st), (wu_hbm, wu_st), (wd_hbm, wd_st)))]

    @pl.when(i == 0)
    def _():
        for cp in fetch(be_ref[0], par_ref[0]):
            cp.start()

    @pl.when((i == 0) | (be_ref[i] != be_ref[jnp.maximum(i - 1, 0)]))
    def _():
        slot = par_ref[i]
        for cp in fetch(be_ref[i], slot):
            cp.wait()
        wg_bf[...] = wg_st[slot].astype(BF16)
        wu_bf[...] = wu_st[slot].astype(BF16)
        wd_bf[...] = wd_st[slot].astype(BF16)

        @pl.when(nxt_ref[i] >= 0)
        def _():
            for cp in fetch(nxt_ref[i], 1 - slot):
                cp.start()

    @pl.when(nv > 0)
    def _():
        row = lax.broadcasted_iota(jnp.int32, (MOE_BLOCK, 1), 0)
        xb = _unpack_bf16_pairs(jnp.where(row < nv, xb_ref[...], jnp.uint32(0))).astype(BF16)
        g = _dot(xb, wg_bf[...])
        u = _dot(xb, wu_bf[...])
        h = (_silu(g) * u).astype(BF16)
        y_ref[...] = _pack_bf16_pairs(_dot(h, wd_bf[...]).astype(BF16))


def _experts(block_meta, xbuf, w_gate, w_up, w_down):
    P = xbuf.shape[0]
    nb = P // MOE_BLOCK
    rows = pl.BlockSpec((MOE_BLOCK, D_MODEL // 2), lambda i, *_: (i, 0))
    hbm = pl.BlockSpec(memory_space=pl.ANY)
    return pl.pallas_call(
        _expert_kernel,
        grid_spec=pltpu.PrefetchScalarGridSpec(
            num_scalar_prefetch=4,
            grid=(nb,),
            in_specs=[rows, hbm, hbm, hbm],
            out_specs=rows,
            scratch_shapes=[pltpu.VMEM((2, D_MODEL, D_EXPERT), F32), pltpu.VMEM((2, D_MODEL, D_EXPERT), F32),
                            pltpu.VMEM((2, D_EXPERT, D_MODEL), F32),
                            pltpu.VMEM((D_MODEL, D_EXPERT), BF16), pltpu.VMEM((D_MODEL, D_EXPERT), BF16),
                            pltpu.VMEM((D_EXPERT, D_MODEL), BF16),
                            pltpu.SemaphoreType.DMA((2, 3))]),
        out_shape=jax.ShapeDtypeStruct((P, D_MODEL // 2), jnp.uint32),
        compiler_params=_cparams(("arbitrary",)),
        name="experts",
    )(*block_meta, xbuf, w_gate, w_up, w_down)


def _combine_kernel(ra_ref, rb_ref, x1_ref, wt_ref, g2_ref, b2_ref, *rest):
    o_ref = rest[-1]
    w2 = wt_ref[...]
    w = jnp.concatenate([w2, jnp.zeros((6, w2.shape[1]), F32)], axis=0).T
    y = w[:, 0:1] * _unpack_bf16_pairs(ra_ref[...]) + w[:, 1:2] * _unpack_bf16_pairs(rb_ref[...])
    o_ref[...] = _layer_norm(DN_ALPHA * x1_ref[...] + y, g2_ref[...], b2_ref[...])


def _combine(rows2, x1, wt, g2, b2, tm, out_prev, tile0, t_total):
    T = x1.shape[0]
    nt = T // tm
    const = lambda shape: pl.BlockSpec(shape, lambda i: (0,) * len(shape))
    in_specs = [pl.BlockSpec((tm, D_MODEL // 2), lambda i: (i, 0)),
                pl.BlockSpec((tm, D_MODEL // 2), lambda i: (i + nt, 0)),
                pl.BlockSpec((tm, D_MODEL), lambda i: (i, 0)), pl.BlockSpec((2, tm), lambda i: (0, i)),
                const((1, D_MODEL)), const((1, D_MODEL))]
    args = [rows2, rows2, x1, wt, g2, b2]
    aliases = {}
    if out_prev is not None:
        in_specs.append(pl.BlockSpec(memory_space=pl.ANY))
        args.append(out_prev)
        aliases = {len(args) - 1: 0}
    return pl.pallas_call(
        _combine_kernel,
        grid=(nt,),
        in_specs=in_specs,
        out_specs=pl.BlockSpec((tm, D_MODEL), lambda i: (i + tile0, 0)),
        out_shape=jax.ShapeDtypeStruct((t_total, D_MODEL), F32),
        input_output_aliases=aliases,
        compiler_params=_cparams(("arbitrary",)),
        name="combine",
    )(*args)


def _slot_layout(eid, rank, counts, T):
    P = 2 * T + N_EXPERTS * MOE_BLOCK
    nb = P // MOE_BLOCK
    padded = ((counts + MOE_BLOCK - 1) // MOE_BLOCK) * MOE_BLOCK
    pend = jnp.cumsum(padded)
    pstart = pend - padded
    experts = jnp.arange(N_EXPERTS, dtype=jnp.int32)
    pos = rank + jnp.sum(jnp.where(eid[:, :, None] == experts, pstart, 0), axis=-1)

    block_start = jnp.arange(nb, dtype=jnp.int32) * MOE_BLOCK
    block_expert = jnp.minimum(jnp.sum(block_start[:, None] >= pend[None, :], axis=1), N_EXPERTS - 1)
    onehot = block_expert[:, None] == experts[None, :]
    look = lambda table: jnp.sum(jnp.where(onehot, table[None, :], 0), axis=1)
    block_valid = jnp.clip(look(pstart + counts) - block_start, 0, MOE_BLOCK)
    block_valid = jnp.where(block_start < pend[-1], block_valid, 0)
    present = (counts > 0) | (experts == N_EXPERTS - 1)
    later = present[None, :] & (experts[None, :] > experts[:, None])
    next_present = jnp.min(jnp.where(later, experts[None, :], N_EXPERTS), axis=1)
    next_present = jnp.where(next_present < N_EXPERTS, next_present, -1)
    runs_before = jnp.sum(present[None, :] & (experts[None, :] < experts[:, None]), axis=1)
    block_next = look(next_present)
    block_slot = look(runs_before % 2)
    i32 = lambda a: a.astype(jnp.int32)
    return i32(pos), (i32(block_expert), i32(block_valid), i32(block_next), i32(block_slot))


def kernel(x, lb_logits, w_in, hg_norm_w, sinks, w_branch_a, w_branch_b, w_out, ln1_g, ln1_b,
           router_group_w, router_group_b, router_expert_w, router_expert_b,
           w_exp_gate, w_exp_up, w_exp_down, ln2_g, ln2_b):
    B, S, D = x.shape
    assert D == D_MODEL and S % ATT_BLK == 0 and w_in.shape[0] == DEPTH == 1
    lb_all = jnp.cumsum(jax.nn.softmax(lb_logits.astype(F32), axis=0), axis=0)
    lb = lb_all[0].reshape(1, HG_W)
    w0 = w_in[0]
    w_f = jnp.concatenate([w0[:, :2 * HG_W], w0[:, 3 * HG_W:4 * HG_W]], axis=1).astype(BF16)
    w_b = jnp.concatenate([w0[:, 2 * HG_W:3 * HG_W], w0[:, 4 * HG_W:PROJ_W]], axis=1).astype(BF16)
    w_gate = w0[:, PROJ_W:].astype(BF16)
    nw = hg_norm_w[0].reshape(1, HG_DV).astype(F32)
    wa, wb, wo = w_branch_a[0].astype(BF16), w_branch_b[0].astype(BF16), w_out[0].astype(BF16)
    wr = jnp.zeros((40, D), F32).at[0:N_GROUPS].set(router_group_w[0].T).at[8:40].set(router_expert_w[0].T)
    br = jnp.zeros((40, 1), F32).at[0:N_GROUPS, 0].set(router_group_b[0]).at[8:40, 0].set(router_expert_b[0])
    g1, b1 = ln1_g[0].reshape(1, D), ln1_b[0].reshape(1, D)
    g2, b2 = ln2_g[0].reshape(1, D), ln2_b[0].reshape(1, D)

    n_parts = N_PARTS if B % N_PARTS == 0 else 1
    bp = B // n_parts
    tp = bp * S
    pick = lambda want: next(t for t in (want, 512, 256, ATT_BLK) if t <= want and tp % t == 0)
    tm_proj, tm_merge, tm_comb = pick(1024), pick(1024), pick(1024)
    x2 = x.reshape(B * S, D)
    out = None
    for part in range(n_parts):
        proj_f, proj_b = _proj(x2, w_f, w_b, tm_proj, part * tp, tp)
        hg, att = _mixers(proj_f, proj_b, lb, nw, sinks[0].astype(F32), bp, S)
        x1, xp, eid, wt, rank, cnt = _merge(hg, att, x2, w_gate, wa, wb, wo, g1, b1, wr.astype(BF16), br,
                                            tm_merge, part * tp)
        pos, block_meta = _slot_layout(eid, rank, cnt[:, 0].astype(jnp.int32), tp)
        xbuf = _sc_scatter2(xp, pos[0], pos[1], 2 * tp + N_EXPERTS * MOE_BLOCK)
        ybuf = _experts(block_meta, xbuf, w_exp_gate[0], w_exp_up[0], w_exp_down[0])
        rows2 = _sc_gather(ybuf, pos.reshape(-1))
        out = _combine(rows2, x1, wt, g2, b2, tm_comb, out, part * (tp // tm_comb), B * S)
    return out.reshape(B, S, D)
```

```python
import numpy as np
import jax
import jax.numpy as jnp
from jax import lax
from jax.experimental import pallas as pl
from jax.experimental.pallas import tpu as pltpu
from jax.experimental.pallas import tpu_sc as plsc

F32 = jnp.float32
BF16 = jnp.bfloat16

D_MODEL = 1024
DEPTH = 1
HG_HEADS = 4
HG_DK = 128
HG_DV = 128
HG_W = HG_HEADS * HG_DK
CHUNK = 64
ATT_Q_HEADS = 8
ATT_KV_HEADS = 2
ATT_GROUP = ATT_Q_HEADS // ATT_KV_HEADS
ATT_HD = 64
ATT_QW = ATT_Q_HEADS * ATT_HD
ATT_KVW = ATT_KV_HEADS * ATT_HD
ATT_BLK = 128
N_GROUPS = 4
EPG = 8
N_EXPERTS = N_GROUPS * EPG
D_EXPERT = 512
MOE_BLOCK = 512
DN_ALPHA = (2.0 * DEPTH) ** 0.25
LN_EPS = 1e-5
RMS_EPS = 1e-6
NEG_INF = -1e30
LOG2_E = 1.4426950408889634

PROJ_W = 4 * HG_W + ATT_QW + 2 * ATT_KVW
PROJ_F = 3 * HG_W
PROJ_B = HG_W + ATT_QW + 2 * ATT_KVW
N_LEVELS = 6
N_ARG_GROUPS = N_LEVELS + 2
HG_SAFE_LB = 0.125

VMEM_LIMIT = 56 * 1024 * 1024
MERGE_SUB = 256
MIX_TILE = 512
N_PARTS = 4


def _cparams(sem):
    return pltpu.CompilerParams(dimension_semantics=sem, vmem_limit_bytes=VMEM_LIMIT)


def _dot(a, b):
    return jnp.dot(a, b, preferred_element_type=F32)


def _dot_nt(a, b):
    return lax.dot_general(a, b, (((1,), (1,)), ((), ())), preferred_element_type=F32)


def _dot_tn(a, b):
    return lax.dot_general(a, b, (((0,), (0,)), ((), ())), preferred_element_type=F32)


def _sigmoid(x):
    return 0.5 * jnp.tanh(0.5 * x) + 0.5


def _silu(x):
    return x * _sigmoid(x)


def _pack_bf16_pairs(xb):
    n = xb.shape[1] // 2
    lo = lax.bitcast_convert_type(xb[:, :n].astype(F32), jnp.uint32)
    hi = lax.bitcast_convert_type(xb[:, n:].astype(F32), jnp.uint32)
    return (lo >> 16) | hi


def _unpack_bf16_pairs(w):
    lo = lax.bitcast_convert_type(w << 16, F32)
    hi = lax.bitcast_convert_type(w & jnp.uint32(0xFFFF0000), F32)
    return jnp.concatenate([lo, hi], axis=1)


def _layer_norm(z, g, b):
    mu = jnp.mean(z, axis=-1, keepdims=True)
    zc = z - mu
    var = jnp.mean(zc * zc, axis=-1, keepdims=True)
    return zc * lax.rsqrt(var + LN_EPS) * g + b


def _proj_kernel(x_ref, wf_ref, wb_ref, of_ref, ob_ref):
    xb = x_ref[...].astype(BF16)
    of_ref[...] = _dot(xb, wf_ref[...])
    ob_ref[...] = _dot(xb, wb_ref[...]).astype(BF16)


def _proj(x2, w_f, w_b, tm, row0, T):
    tile0 = row0 // tm
    return pl.pallas_call(
        _proj_kernel,
        grid=(T // tm,),
        in_specs=[pl.BlockSpec((tm, D_MODEL), lambda i: (i + tile0, 0)),
                  pl.BlockSpec((D_MODEL, PROJ_F), lambda i: (0, 0), pipeline_mode=pl.Buffered(1)),
                  pl.BlockSpec((D_MODEL, PROJ_B), lambda i: (0, 0), pipeline_mode=pl.Buffered(1))],
        out_specs=[pl.BlockSpec((tm, PROJ_F), lambda i: (i, 0)), pl.BlockSpec((tm, PROJ_B), lambda i: (i, 0))],
        out_shape=[jax.ShapeDtypeStruct((T, PROJ_F), F32), jax.ShapeDtypeStruct((T, PROJ_B), BF16)],
        compiler_params=_cparams(("arbitrary",)),
        name="proj",
    )(x2, w_f, w_b)


def _hgrn_tables():
    C = CHUNK
    w = np.zeros((N_ARG_GROUPS, C, C), np.float32)
    masks = np.zeros((N_LEVELS, C, C), np.float32)
    for lvl in range(N_LEVELS):
        h = 1 << lvl
        for t in range(C):
            base = (t // (2 * h)) * 2 * h
            m = base + h - 1
            if (t // h) % 2 == 1:
                w[lvl, t, m + 1:t + 1] = 1.0
                masks[lvl, t, base:base + h] = 1.0
            else:
                w[lvl, t, t + 1:m + 1] = 1.0
    for t in range(C):
        w[N_LEVELS, t, :t + 1] = 1.0
        w[N_LEVELS + 1, t, t + 1:] = 1.0
    w = w.reshape(N_ARG_GROUPS * C, C)
    return np.concatenate([w, w], axis=1), masks


def _hgrn_chunk(q, fz, v, gate, c0, c1, nw, seg, mask_ref, state_ref, bounded):
    h = 0.5 * q
    qf = h + h * jnp.tanh(h)
    t1 = c1 * jnp.tanh(0.5 * fz)
    f = c0 + t1
    k = c1 - t1
    l2 = jnp.log2(f)
    v_bf = v.astype(BF16)
    qf_bf = qf.astype(BF16)
    k_bf = k.astype(BF16)

    l_hi = l2.astype(BF16)
    l_lo = (l2 - l_hi.astype(F32)).astype(BF16)
    l_split = jnp.concatenate([l_hi, l_lo], axis=0)
    row = lax.broadcasted_iota(jnp.int32, (CHUNK, 1), 0)
    if bounded:
        cum = _dot(seg[N_LEVELS * CHUNK:(N_LEVELS + 1) * CHUNK, :], l_split)
        mid = cum[CHUNK // 2 - 1:CHUNK // 2, :]
        last = cum[CHUNK - 1:CHUNK, :]
        e_cum = jnp.exp2(cum)
        e_suf = jnp.exp2(last - cum)
        a_q = qf_bf * jnp.exp2(cum - mid).astype(BF16)
        a_k = k_bf * jnp.exp2(mid - cum).astype(BF16)
        causal = row >= lax.broadcasted_iota(jnp.int32, (1, CHUNK), 1)
        scores = [jnp.where(causal, _dot_nt(a_q[:, hd * HG_DK:(hd + 1) * HG_DK], a_k[:, hd * HG_DK:(hd + 1) * HG_DK]),
                            0.0) for hd in range(HG_HEADS)]
        diag_v = None
    else:
        e = jnp.exp2(_dot(seg, l_split))
        scores = [jnp.zeros((CHUNK, CHUNK), F32) for _ in range(HG_HEADS)]
        for lvl in range(N_LEVELS):
            half = 1 << lvl
            e_l = e[lvl * CHUNK:(lvl + 1) * CHUNK, :].astype(BF16)
            if half >= 16:
                sel = jnp.concatenate([(qf_bf if (r0 // half) % 2 else k_bf)[r0:r0 + half]
                                       for r0 in range(0, CHUNK, half)], axis=0)
            else:
                sel = jnp.where((row // half) % 2 == 1, qf_bf, k_bf)
            a = sel * e_l
            m = mask_ref[lvl]
            for hd in range(HG_HEADS):
                a_h = a[:, hd * HG_DK:(hd + 1) * HG_DK]
                scores[hd] = scores[hd] + m * _dot_nt(a_h, a_h)
        e_cum = e[N_LEVELS * CHUNK:(N_LEVELS + 1) * CHUNK, :]
        e_suf = e[(N_LEVELS + 1) * CHUNK:(N_LEVELS + 2) * CHUNK, :]
        diag_v = qf * k

    q_in = qf_bf * e_cum.astype(BF16)
    k_out = k_bf * e_suf.astype(BF16)
    e_last = e_cum[CHUNK - 1:CHUNK, :]

    outs = []
    for hd in range(HG_HEADS):
        cols = slice(hd * HG_DK, (hd + 1) * HG_DK)
        st = state_ref[hd]
        o = _dot_nt(q_in[:, cols], st.astype(BF16)) + _dot(scores[hd].astype(BF16), v_bf[:, cols])
        if diag_v is not None:
            o = o + jnp.sum(diag_v[:, cols], axis=-1, keepdims=True) * v[:, cols]
        state_ref[hd] = st * e_last[:, cols] + _dot_tn(v_bf[:, cols], k_out[:, cols])
        o = o * lax.rsqrt(jnp.mean(o * o, axis=-1, keepdims=True) + RMS_EPS) * nw
        outs.append(o)
    hg = 0.5 * gate
    return jnp.concatenate(outs, axis=1) * (hg + hg * jnp.tanh(hg))


def _attn_bias():
    r = np.arange(ATT_BLK)[:, None]
    c = np.arange(2 * ATT_BLK)[None, :]
    dist = r + ATT_BLK - c
    window = (dist >= 0) & (dist < ATT_BLK)
    slopes = np.exp2(-8.0 * (np.arange(ATT_Q_HEADS, dtype=np.float32) + 1.0) / ATT_Q_HEADS).astype(np.float32)
    alibi = -slopes[:, None, None] * dist.astype(np.float32)[None]
    later = np.where(window[None], alibi, np.float32(NEG_INF))
    first = np.where((window & (c >= ATT_BLK))[None], alibi, np.float32(NEG_INF))
    return (np.stack([later, first]) * LOG2_E).astype(np.float32)


def _attn_block(q_ref, kv_cur, kv_prev, bias_ref, table, sink_ref, o_ref):
    lane = lax.broadcasted_iota(jnp.int32, (2 * ATT_BLK, 2 * ATT_KVW), 1)
    lo = (lane % ATT_KVW) < ATT_HD

    kv = jnp.concatenate([kv_prev, kv_cur], axis=0).astype(BF16)
    kv_sw = jnp.concatenate([kv[:, ATT_HD:ATT_KVW], kv[:, :ATT_HD],
                             kv[:, ATT_KVW + ATT_HD:], kv[:, ATT_KVW:ATT_KVW + ATT_HD]], axis=1)
    zero = jnp.zeros_like(kv)
    placed = {}
    for h in range(ATT_KV_HEADS):
        for off in range(2):
            src = kv if h == off else kv_sw
            placed[h, off] = jnp.where(lo if off == 0 else jnp.logical_not(lo), src, zero)

    scale = ATT_HD ** -0.5 * LOG2_E
    for pair in range(ATT_Q_HEADS // 2):
        qp = (q_ref[:, pair * 2 * ATT_HD:(pair + 1) * 2 * ATT_HD] * scale).astype(BF16)
        acc = jnp.zeros((ATT_BLK, 2 * ATT_HD), F32)
        for off in range(2):
            j = 2 * pair + off
            kvh = placed[j // ATT_GROUP, off]
            sink = sink_ref[j] * LOG2_E
            logits = _dot_nt(qp, kvh[:, :ATT_KVW]) + bias_ref[table, j]
            m = jnp.maximum(jnp.max(logits, axis=-1, keepdims=True), sink)
            p = jnp.exp2(logits - m)
            den = jnp.sum(p, axis=-1, keepdims=True) + jnp.exp2(sink - m)
            acc = acc + _dot(p.astype(BF16), kvh[:, ATT_KVW:]) / den
        o_ref[:, pair * 2 * ATT_HD:(pair + 1) * 2 * ATT_HD] = acc.astype(o_ref.dtype)


def _mixers_kernel(sink_ref, bounded_ref, q_ref, f_ref, i_ref, g_ref, aq_ref, kv_ref, kvp_ref, lb_ref, nw_ref,
                   seg_ref, mask_ref, bias_ref, hg_ref, at_ref, state_ref):
    first = pl.program_id(1) == 0

    @pl.when(first)
    def _():
        state_ref[...] = jnp.zeros_like(state_ref)

    def tile_body(bounded):
        lb = lb_ref[...]
        c0 = 0.5 + 0.5 * lb
        c1 = 0.5 - 0.5 * lb
        nw = nw_ref[...]
        seg = seg_ref[...]
        tile = q_ref.shape[0]
        for r0 in range(0, tile, CHUNK):
            rows = slice(r0, r0 + CHUNK)
            o = _hgrn_chunk(q_ref[rows, :], f_ref[rows, :], i_ref[rows, :], g_ref[rows, :], c0, c1, nw, seg,
                            mask_ref, state_ref, bounded)
            hg_ref[rows, :] = o.astype(hg_ref.dtype)
        for r0 in range(0, tile, ATT_BLK):
            rows = slice(r0, r0 + ATT_BLK)
            if r0 == 0:
                prev, table = kvp_ref[...], jnp.where(first, 1, 0)
            else:
                prev, table = kv_ref[r0 - ATT_BLK:r0, :], 0
            _attn_block(aq_ref.at[rows, :], kv_ref[rows, :], prev, bias_ref, table, sink_ref, at_ref.at[rows, :])

    pl.when(bounded_ref[0] == 1)(lambda: tile_body(True))
    pl.when(bounded_ref[0] != 1)(lambda: tile_body(False))


def _mixers(proj_f, proj_b, lb, norm_w, sinks, B, S):
    T = B * S
    tile = MIX_TILE if S % MIX_TILE == 0 else ATT_BLK
    nb = S // tile
    per = tile // ATT_BLK
    seg, masks = _hgrn_tables()
    const = lambda shape: pl.BlockSpec(shape, lambda b, n: (0,) * len(shape))
    hcol = lambda cb: pl.BlockSpec((tile, HG_W), lambda b, n: (b * nb + n, cb))
    kvcol = (HG_W + ATT_QW) // (2 * ATT_KVW)
    out = pl.BlockSpec((tile, HG_W), lambda b, n: (b * nb + n, 0))
    return pl.pallas_call(
        _mixers_kernel,
        grid=(B, nb),
        in_specs=[pl.BlockSpec(memory_space=pltpu.SMEM), pl.BlockSpec(memory_space=pltpu.SMEM),
                  hcol(0), hcol(1), hcol(0), hcol(2),
                  pl.BlockSpec((tile, ATT_QW), lambda b, n: (b * nb + n, HG_W // ATT_QW)),
                  pl.BlockSpec((tile, 2 * ATT_KVW), lambda b, n: (b * nb + n, kvcol)),
                  pl.BlockSpec((ATT_BLK, 2 * ATT_KVW),
                               lambda b, n: ((b * nb + n) * per - jnp.where(n > 0, 1, 0), kvcol)),
                  const((1, HG_W)), const((1, HG_DV)), const((N_ARG_GROUPS * CHUNK, 2 * CHUNK)),
                  const((N_LEVELS, CHUNK, CHUNK)), const((2, ATT_Q_HEADS, ATT_BLK, 2 * ATT_BLK))],
        out_specs=[out, out],
        out_shape=[jax.ShapeDtypeStruct((T, HG_W), BF16), jax.ShapeDtypeStruct((T, ATT_QW), BF16)],
        scratch_shapes=[pltpu.VMEM((HG_HEADS, HG_DV, HG_DK), F32)],
        compiler_params=_cparams(("arbitrary", "arbitrary")),
        name="mixers",
    )(sinks, (jnp.min(lb) >= HG_SAFE_LB).astype(jnp.int32).reshape(1),
      proj_f, proj_f, proj_b, proj_f, proj_b, proj_b, proj_b, lb, norm_w, jnp.asarray(seg, BF16),
      jnp.asarray(masks), jnp.asarray(_attn_bias()))


def _merge_kernel(hg_ref, at_ref, x_ref, wgate_ref, wa_ref, wb_ref, wo_ref, g1_ref, b1_ref,
                  wr_ref, br_ref, tri_ref, x1_ref, xp_ref, eid_ref, wt_ref, rank_ref, cnt_ref, base_ref, x1b_ref):
    tm = x_ref.shape[0]
    sub = min(tm, MERGE_SUB)
    for r0 in range(0, tm, sub):
        rows = slice(r0, r0 + sub)
        xs = x_ref[rows, :]
        gates = _dot(xs.astype(BF16), wgate_ref[...])
        ya = _dot(hg_ref[rows, :], wa_ref[...])
        yb = _dot(at_ref[rows, :], wb_ref[...])
        merged = _sigmoid(gates[:, :D_MODEL]) * ya + _sigmoid(gates[:, D_MODEL:]) * yb
        z = DN_ALPHA * xs + _dot(merged.astype(BF16), wo_ref[...])
        x1 = _layer_norm(z, g1_ref[...], b1_ref[...])
        x1_ref[rows, :] = x1
        x1b = x1.astype(BF16)
        x1b_ref[rows, :] = x1b
        xp_ref[rows, :] = _pack_bf16_pairs(x1b)

    lg = _dot_nt(wr_ref[...], x1b_ref[...]) + br_ref[...]
    g = lg[0:8, :]
    row8 = lax.broadcasted_iota(jnp.int32, (8, tm), 0)
    g = jnp.where(row8 < N_GROUPS, g, -jnp.inf)
    gmax = jnp.max(g, axis=0, keepdims=True)
    gsel = jnp.min(jnp.where(g == gmax, row8, 8), axis=0, keepdims=True)
    gw = 1.0 / jnp.sum(jnp.exp(g - gmax), axis=0, keepdims=True)
    el = jnp.where(gsel == 0, lg[8:16, :],
                   jnp.where(gsel == 1, lg[16:24, :], jnp.where(gsel == 2, lg[24:32, :], lg[32:40, :])))
    v1 = jnp.max(el, axis=0, keepdims=True)
    i1 = jnp.min(jnp.where(el == v1, row8, 8), axis=0, keepdims=True)
    el2 = jnp.where(row8 == i1, -jnp.inf, el)
    v2 = jnp.max(el2, axis=0, keepdims=True)
    i2 = jnp.min(jnp.where(el2 == v2, row8, 8), axis=0, keepdims=True)
    e2 = jnp.exp(v2 - v1)
    den = 1.0 + e2
    e_a = gsel * EPG + i1
    e_b = gsel * EPG + i2
    eid_ref[...] = jnp.concatenate([e_a, e_b], axis=0)
    wt_ref[...] = jnp.concatenate([gw / den, gw * e2 / den], axis=0)

    @pl.when(pl.program_id(0) == 0)
    def _():
        base_ref[...] = jnp.zeros_like(base_ref)

    row_e = lax.broadcasted_iota(jnp.int32, (N_EXPERTS, tm), 0)
    oh_a = jnp.where(row_e == e_a, 1.0, 0.0)
    oh_b = jnp.where(row_e == e_b, 1.0, 0.0)
    tri = tri_ref[...]
    pre_a = _dot(oh_a.astype(BF16), tri)
    pre_b = _dot(oh_b.astype(BF16), tri)
    cnt_a = jnp.sum(oh_a, axis=1, keepdims=True)
    cnt_b = jnp.sum(oh_b, axis=1, keepdims=True)
    base = base_ref[...]
    rank_a = jnp.sum(oh_a * (base + pre_a), axis=0, keepdims=True)
    rank_b = jnp.sum(oh_b * (base + cnt_a + pre_b), axis=0, keepdims=True)
    rank_ref[...] = jnp.concatenate([rank_a, rank_b], axis=0).astype(jnp.int32)
    base = base + cnt_a + cnt_b
    base_ref[...] = base
    cnt_ref[...] = jnp.broadcast_to(base, cnt_ref.shape)


def _merge(hg, att, x2, wgate, wa, wb, wo, g1, b1, wr, br, tm, row0):
    T = hg.shape[0]
    tile0 = row0 // tm
    row = lambda w: pl.BlockSpec((tm, w), lambda i: (i, 0))
    const = lambda shape: pl.BlockSpec(shape, lambda i: (0,) * len(shape), pipeline_mode=pl.Buffered(1))
    lanes = pl.BlockSpec((2, tm), lambda i: (0, i))
    tri = jnp.asarray(np.triu(np.ones((tm, tm), np.float32), 1), BF16)
    return pl.pallas_call(
        _merge_kernel,
        grid=(T // tm,),
        in_specs=[row(HG_W), row(ATT_QW),
                  pl.BlockSpec((tm, D_MODEL), lambda i: (i + tile0, 0)),
                  const((D_MODEL, 2 * D_MODEL)), const((HG_W, D_MODEL)), const((ATT_QW, D_MODEL)), const((D_MODEL, D_MODEL)),
                  const((1, D_MODEL)), const((1, D_MODEL)), const((40, D_MODEL)), const((40, 1)),
                  const((tm, tm))],
        out_specs=[row(D_MODEL), row(D_MODEL // 2), lanes, lanes, lanes,
                   pl.BlockSpec((N_EXPERTS, 128), lambda i: (0, 0))],
        out_shape=[jax.ShapeDtypeStruct((T, D_MODEL), F32),
                   jax.ShapeDtypeStruct((T, D_MODEL // 2), jnp.uint32),
                   jax.ShapeDtypeStruct((2, T), jnp.int32),
                   jax.ShapeDtypeStruct((2, T), F32),
                   jax.ShapeDtypeStruct((2, T), jnp.int32),
                   jax.ShapeDtypeStruct((N_EXPERTS, 128), F32)],
        scratch_shapes=[pltpu.VMEM((N_EXPERTS, 1), F32), pltpu.VMEM((tm, D_MODEL), BF16)],
        compiler_params=_cparams(("arbitrary",)),
        name="merge",
    )(hg, att, x2, wgate, wa, wb, wo, g1, b1, wr, br, tri)


SC_WINDOW = 64
SC_IDX_LANES = 128


def _pad_indices(idx):
    rows = idx.reshape(-1, SC_WINDOW)
    return jnp.pad(rows, ((0, 0), (0, SC_IDX_LANES - SC_WINDOW)))


def _sc_mesh():
    return plsc.VectorSubcoreMesh(core_axis_name="core", subcore_axis_name="subcore")


def _sc_scatter2(x, idx_a, idx_b, n_out):
    T, d = x.shape

    @pl.kernel(out_type=jax.ShapeDtypeStruct((n_out, d), x.dtype), mesh=_sc_mesh())
    def scatter(x_hbm, ia_hbm, ib_hbm, o_hbm):
        def body(x_vmem, ia_vmem, ib_vmem):
            pltpu.sync_copy(x_vmem, o_hbm.at[ia_vmem.at[0, pl.ds(0, SC_WINDOW)]])
            pltpu.sync_copy(x_vmem, o_hbm.at[ib_vmem.at[0, pl.ds(0, SC_WINDOW)]])

        idx_spec = pl.BlockSpec((1, SC_IDX_LANES), lambda i: (i, 0))
        pltpu.emit_pipeline(
            body, grid=(T // SC_WINDOW,),
            in_specs=[pl.BlockSpec((SC_WINDOW, d), lambda i: (i, 0)), idx_spec, idx_spec],
            out_specs=[],
            core_axis_name=("core", "subcore"),
            dimension_semantics=(pltpu.PARALLEL,),
        )(x_hbm, ia_hbm, ib_hbm)

    return scatter(x, _pad_indices(idx_a), _pad_indices(idx_b))


def _sc_gather(x, idx):
    n = idx.shape[0]
    d = x.shape[1]

    @pl.kernel(out_type=jax.ShapeDtypeStruct((n, d), x.dtype), mesh=_sc_mesh())
    def gather(x_hbm, i_hbm, o_hbm):
        def body(i_vmem, o_vmem):
            pltpu.sync_copy(x_hbm.at[i_vmem.at[0, pl.ds(0, SC_WINDOW)]], o_vmem)

        pltpu.emit_pipeline(
            body, grid=(n // SC_WINDOW,),
            in_specs=[pl.BlockSpec((1, SC_IDX_LANES), lambda i: (i, 0))],
            out_specs=[pl.BlockSpec((SC_WINDOW, d), lambda i: (i, 0))],
            core_axis_name=("core", "subcore"),
            dimension_semantics=(pltpu.PARALLEL,),
        )(i_hbm, o_hbm)

    return gather(x, _pad_indices(idx))


def _expert_kernel(be_ref, nv_ref, nxt_ref, par_ref, xb_ref, wg_hbm, wu_hbm, wd_hbm, y_ref,
                   wg_st, wu_st, wd_st, wg_bf, wu_bf, wd_bf, sem):
    i = pl.program_id(0)
    nv = nv_ref[i]

    def fetch(expert, slot):
        return [pltpu.make_async_copy(src.at[expert], dst.at[slot], sem.at[slot, j])
                for j, (src, dst) in enumerate(((wg_hbm, wg_st), (wu_hbm, wu_st), (wd_hbm, wd_st)))]

    @pl.when(i == 0)
    def _():
        for cp in fetch(be_ref[0], par_ref[0]):
            cp.start()

    @pl.when((i == 0) | (be_ref[i] != be_ref[jnp.maximum(i - 1, 0)]))
    def _():
        slot = par_ref[i]
        for cp in fetch(be_ref[i], slot):
            cp.wait()
        wg_bf[...] = wg_st[slot].astype(BF16)
        wu_bf[...] = wu_st[slot].astype(BF16)
        wd_bf[...] = wd_st[slot].astype(BF16)

        @pl.when(nxt_ref[i] >= 0)
        def _():
            for cp in fetch(nxt_ref[i], 1 - slot):
                cp.start()

    @pl.when(nv > 0)
    def _():
        row = lax.broadcasted_iota(jnp.int32, (MOE_BLOCK, 1), 0)
        xb = _unpack_bf16_pairs(jnp.where(row < nv, xb_ref[...], jnp.uint32(0))).astype(BF16)
        g = _dot(xb, wg_bf[...])
        u = _dot(xb, wu_bf[...])
        h = (_silu(g) * u).astype(BF16)
        y_ref[...] = _pack_bf16_pairs(_dot(h, wd_bf[...]).astype(BF16))


def _experts(block_meta, xbuf, w_gate, w_up, w_down):
    P = xbuf.shape[0]
    nb = P // MOE_BLOCK
    rows = pl.BlockSpec((MOE_BLOCK, D_MODEL // 2), lambda i, *_: (i, 0))
    hbm = pl.BlockSpec(memory_space=pl.ANY)
    return pl.pallas_call(
        _expert_kernel,
        grid_spec=pltpu.PrefetchScalarGridSpec(
            num_scalar_prefetch=4,
            grid=(nb,),
            in_specs=[rows, hbm, hbm, hbm],
            out_specs=rows,
            scratch_shapes=[pltpu.VMEM((2, D_MODEL, D_EXPERT), F32), pltpu.VMEM((2, D_MODEL, D_EXPERT), F32),
                            pltpu.VMEM((2, D_EXPERT, D_MODEL), F32),
                            pltpu.VMEM((D_MODEL, D_EXPERT), BF16), pltpu.VMEM((D_MODEL, D_EXPERT), BF16),
                            pltpu.VMEM((D_EXPERT, D_MODEL), BF16),
                            pltpu.SemaphoreType.DMA((2, 3))]),
        out_shape=jax.ShapeDtypeStruct((P, D_MODEL // 2), jnp.uint32),
        compiler_params=_cparams(("arbitrary",)),
        name="experts",
    )(*block_meta, xbuf, w_gate, w_up, w_down)


def _combine_kernel(ra_ref, rb_ref, x1_ref, wt_ref, g2_ref, b2_ref, *rest):
    o_ref = rest[-1]
    w2 = wt_ref[...]
    w = jnp.concatenate([w2, jnp.zeros((6, w2.shape[1]), F32)], axis=0).T
    y = w[:, 0:1] * _unpack_bf16_pairs(ra_ref[...]) + w[:, 1:2] * _unpack_bf16_pairs(rb_ref[...])
    o_ref[...] = _layer_norm(DN_ALPHA * x1_ref[...] + y, g2_ref[...], b2_ref[...])


def _combine(rows2, x1, wt, g2, b2, tm, out_prev, tile0, t_total):
    T = x1.shape[0]
    nt = T // tm
    const = lambda shape: pl.BlockSpec(shape, lambda i: (0,) * len(shape))
    in_specs = [pl.BlockSpec((tm, D_MODEL // 2), lambda i: (i, 0)),
                pl.BlockSpec((tm, D_MODEL // 2), lambda i: (i + nt, 0)),
                pl.BlockSpec((tm, D_MODEL), lambda i: (i, 0)), pl.BlockSpec((2, tm), lambda i: (0, i)),
                const((1, D_MODEL)), const((1, D_MODEL))]
    args = [rows2, rows2, x1, wt, g2, b2]
    aliases = {}
    if out_prev is not None:
        in_specs.append(pl.BlockSpec(memory_space=pl.ANY))
        args.append(out_prev)
        aliases = {len(args) - 1: 0}
    return pl.pallas_call(
        _combine_kernel,
        grid=(nt,),
        in_specs=in_specs,
        out_specs=pl.BlockSpec((tm, D_MODEL), lambda i: (i + tile0, 0)),
        out_shape=jax.ShapeDtypeStruct((t_total, D_MODEL), F32),
        input_output_aliases=aliases,
        compiler_params=_cparams(("arbitrary",)),
        name="combine",
    )(*args)


def _slot_layout(eid, rank, counts, T):
    P = 2 * T + N_EXPERTS * MOE_BLOCK
    nb = P // MOE_BLOCK
    padded = ((counts + MOE_BLOCK - 1) // MOE_BLOCK) * MOE_BLOCK
    pend = jnp.cumsum(padded)
    pstart = pend - padded
    experts = jnp.arange(N_EXPERTS, dtype=jnp.int32)
    pos = rank + jnp.sum(jnp.where(eid[:, :, None] == experts, pstart, 0), axis=-1)

    block_start = jnp.arange(nb, dtype=jnp.int32) * MOE_BLOCK
    block_expert = jnp.minimum(jnp.sum(block_start[:, None] >= pend[None, :], axis=1), N_EXPERTS - 1)
    onehot = block_expert[:, None] == experts[None, :]
    look = lambda table: jnp.sum(jnp.where(onehot, table[None, :], 0), axis=1)
    block_valid = jnp.clip(look(pstart + counts) - block_start, 0, MOE_BLOCK)
    block_valid = jnp.where(block_start < pend[-1], block_valid, 0)
    present = (counts > 0) | (experts == N_EXPERTS - 1)
    later = present[None, :] & (experts[None, :] > experts[:, None])
    next_present = jnp.min(jnp.where(later, experts[None, :], N_EXPERTS), axis=1)
    next_present = jnp.where(next_present < N_EXPERTS, next_present, -1)
    runs_before = jnp.sum(present[None, :] & (experts[None, :] < experts[:, None]), axis=1)
    block_next = look(next_present)
    block_slot = look(runs_before % 2)
    i32 = lambda a: a.astype(jnp.int32)
    return i32(pos), (i32(block_expert), i32(block_valid), i32(block_next), i32(block_slot))


def kernel(x, lb_logits, w_in, hg_norm_w, sinks, w_branch_a, w_branch_b, w_out, ln1_g, ln1_b,
           router_group_w, router_group_b, router_expert_w, router_expert_b,
           w_exp_gate, w_exp_up, w_exp_down, ln2_g, ln2_b):
    B, S, D = x.shape
    assert D == D_MODEL and S % ATT_BLK == 0 and w_in.shape[0] == DEPTH == 1
    lb_all = jnp.cumsum(jax.nn.softmax(lb_logits.astype(F32), axis=0), axis=0)
    lb = lb_all[0].reshape(1, HG_W)
    w0 = w_in[0]
    w_f = jnp.concatenate([w0[:, :2 * HG_W], w0[:, 3 * HG_W:4 * HG_W]], axis=1).astype(BF16)
    w_b = jnp.concatenate([w0[:, 2 * HG_W:3 * HG_W], w0[:, 4 * HG_W:PROJ_W]], axis=1).astype(BF16)
    w_gate = w0[:, PROJ_W:].astype(BF16)
    nw = hg_norm_w[0].reshape(1, HG_DV).astype(F32)
    wa, wb, wo = w_branch_a[0].astype(BF16), w_branch_b[0].astype(BF16), w_out[0].astype(BF16)
    wr = jnp.zeros((40, D), F32).at[0:N_GROUPS].set(router_group_w[0].T).at[8:40].set(router_expert_w[0].T)
    br = jnp.zeros((40, 1), F32).at[0:N_GROUPS, 0].set(router_group_b[0]).at[8:40, 0].set(router_expert_b[0])
    g1, b1 = ln1_g[0].reshape(1, D), ln1_b[0].reshape(1, D)
    g2, b2 = ln2_g[0].reshape(1, D), ln2_b[0].reshape(1, D)

    n_parts = N_PARTS if B % N_PARTS == 0 else 1
    bp = B // n_parts
    tp = bp * S
    pick = lambda want: next(t for t in (want, 512, 256, ATT_BLK) if t <= want and tp % t == 0)
    tm_proj, tm_merge, tm_comb = pick(1024), pick(1024), pick(1024)
    x2 = x.reshape(B * S, D)
    out = None
    for part in range(n_parts):
        proj_f, proj_b = _proj(x2, w_f, w_b, tm_proj, part * tp, tp)
        hg, att = _mixers(proj_f, proj_b, lb, nw, sinks[0].astype(F32), bp, S)
        x1, xp, eid, wt, rank, cnt = _merge(hg, att, x2, w_gate, wa, wb, wo, g1, b1, wr.astype(BF16), br,
                                            tm_merge, part * tp)
        pos, block_meta = _slot_layout(eid, rank, cnt[:, 0].astype(jnp.int32), tp)
        xbuf = _sc_scatter2(xp, pos[0], pos[1], 2 * tp + N_EXPERTS * MOE_BLOCK)
        ybuf = _experts(block_meta, xbuf, w_exp_gate[0], w_exp_up[0], w_exp_down[0])
        rows2 = _sc_gather(ybuf, pos.reshape(-1))
        out = _combine(rows2, x1, wt, g2, b2, tm_comb, out, part * (tp // tm_comb), B * S)
    return out.reshape(B, S, D)
```

```python
import numpy as np
import jax
import jax.numpy as jnp
from jax import lax
from jax.experimental import pallas as pl
from jax.experimental.pallas import tpu as pltpu
from jax.experimental.pallas import tpu_sc as plsc

F32 = jnp.float32
BF16 = jnp.bfloat16

D_MODEL = 1024
DEPTH = 1
HG_HEADS = 4
HG_DK = 128
HG_DV = 128
HG_W = HG_HEADS * HG_DK
CHUNK = 64
ATT_Q_HEADS = 8
ATT_KV_HEADS = 2
ATT_GROUP = ATT_Q_HEADS // ATT_KV_HEADS
ATT_HD = 64
ATT_QW = ATT_Q_HEADS * ATT_HD
ATT_KVW = ATT_KV_HEADS * ATT_HD
ATT_BLK = 128
N_GROUPS = 4
EPG = 8
N_EXPERTS = N_GROUPS * EPG
D_EXPERT = 512
MOE_BLOCK = 512
DN_ALPHA = (2.0 * DEPTH) ** 0.25
LN_EPS = 1e-5
RMS_EPS = 1e-6
NEG_INF = -1e30
LOG2_E = 1.4426950408889634

PROJ_W = 4 * HG_W + ATT_QW + 2 * ATT_KVW
PROJ_F = 3 * HG_W
PROJ_B = HG_W + ATT_QW + 2 * ATT_KVW
N_LEVELS = 6
N_ARG_GROUPS = N_LEVELS + 2
HG_SAFE_LB = 0.125

VMEM_LIMIT = 56 * 1024 * 1024
MERGE_SUB = 256
MIX_TILE = 1024
N_PARTS = 2


def _cparams(sem):
    return pltpu.CompilerParams(dimension_semantics=sem, vmem_limit_bytes=VMEM_LIMIT)


def _dot(a, b):
    return jnp.dot(a, b, preferred_element_type=F32)


def _dot_nt(a, b):
    return lax.dot_general(a, b, (((1,), (1,)), ((), ())), preferred_element_type=F32)


def _dot_tn(a, b):
    return lax.dot_general(a, b, (((0,), (0,)), ((), ())), preferred_element_type=F32)


def _sigmoid(x):
    return 0.5 * jnp.tanh(0.5 * x) + 0.5


def _silu(x):
    return x * _sigmoid(x)


def _pack_bf16_pairs(xb):
    n = xb.shape[1] // 2
    lo = lax.bitcast_convert_type(xb[:, :n].astype(F32), jnp.uint32)
    hi = lax.bitcast_convert_type(xb[:, n:].astype(F32), jnp.uint32)
    return (lo >> 16) | hi


def _unpack_bf16_pairs(w):
    lo = lax.bitcast_convert_type(w << 16, F32)
    hi = lax.bitcast_convert_type(w & jnp.uint32(0xFFFF0000), F32)
    return jnp.concatenate([lo, hi], axis=1)


def _layer_norm(z, g, b):
    mu = jnp.mean(z, axis=-1, keepdims=True)
    zc = z - mu
    var = jnp.mean(zc * zc, axis=-1, keepdims=True)
    return zc * lax.rsqrt(var + LN_EPS) * g + b


def _proj_kernel(x_ref, wf_ref, wb_ref, of_ref, ob_ref):
    xb = x_ref[...].astype(BF16)
    of_ref[...] = _dot(xb, wf_ref[...])
    ob_ref[...] = _dot(xb, wb_ref[...]).astype(BF16)


def _proj(x2, w_f, w_b, tm, row0, T):
    tile0 = row0 // tm
    return pl.pallas_call(
        _proj_kernel,
        grid=(T // tm,),
        in_specs=[pl.BlockSpec((tm, D_MODEL), lambda i: (i + tile0, 0)),
                  pl.BlockSpec((D_MODEL, PROJ_F), lambda i: (0, 0), pipeline_mode=pl.Buffered(1)),
                  pl.BlockSpec((D_MODEL, PROJ_B), lambda i: (0, 0), pipeline_mode=pl.Buffered(1))],
        out_specs=[pl.BlockSpec((tm, PROJ_F), lambda i: (i, 0)), pl.BlockSpec((tm, PROJ_B), lambda i: (i, 0))],
        out_shape=[jax.ShapeDtypeStruct((T, PROJ_F), F32), jax.ShapeDtypeStruct((T, PROJ_B), BF16)],
        compiler_params=_cparams(("arbitrary",)),
        name="proj",
    )(x2, w_f, w_b)


def _hgrn_tables():
    C = CHUNK
    w = np.zeros((N_ARG_GROUPS, C, C), np.float32)
    masks = np.zeros((N_LEVELS, C, C), np.float32)
    for lvl in range(N_LEVELS):
        h = 1 << lvl
        for t in range(C):
            base = (t // (2 * h)) * 2 * h
            m = base + h - 1
            if (t // h) % 2 == 1:
                w[lvl, t, m + 1:t + 1] = 1.0
                masks[lvl, t, base:base + h] = 1.0
            else:
                w[lvl, t, t + 1:m + 1] = 1.0
    for t in range(C):
        w[N_LEVELS, t, :t + 1] = 1.0
        w[N_LEVELS + 1, t, t + 1:] = 1.0
    w = w.reshape(N_ARG_GROUPS * C, C)
    return np.concatenate([w, w], axis=1), masks


def _hgrn_chunk(q, fz, v, gate, c0, c1, nw, seg, mask_ref, state_ref, bounded):
    h = 0.5 * q
    qf = h + h * jnp.tanh(h)
    t1 = c1 * jnp.tanh(0.5 * fz)
    f = c0 + t1
    k = c1 - t1
    l2 = jnp.log2(f)
    v_bf = v.astype(BF16)
    qf_bf = qf.astype(BF16)
    k_bf = k.astype(BF16)

    l_hi = l2.astype(BF16)
    l_lo = (l2 - l_hi.astype(F32)).astype(BF16)
    l_split = jnp.concatenate([l_hi, l_lo], axis=0)
    row = lax.broadcasted_iota(jnp.int32, (CHUNK, 1), 0)
    if bounded:
        cum = _dot(seg[N_LEVELS * CHUNK:(N_LEVELS + 1) * CHUNK, :], l_split)
        mid = cum[CHUNK // 2 - 1:CHUNK // 2, :]
        last = cum[CHUNK - 1:CHUNK, :]
        e_cum = jnp.exp2(cum)
        e_suf = jnp.exp2(last - cum)
        a_q = qf_bf * jnp.exp2(cum - mid).astype(BF16)
        a_k = k_bf * jnp.exp2(mid - cum).astype(BF16)
        causal = row >= lax.broadcasted_iota(jnp.int32, (1, CHUNK), 1)
        scores = [jnp.where(causal, _dot_nt(a_q[:, hd * HG_DK:(hd + 1) * HG_DK], a_k[:, hd * HG_DK:(hd + 1) * HG_DK]),
                            0.0) for hd in range(HG_HEADS)]
        diag_v = None
    else:
        e = jnp.exp2(_dot(seg, l_split))
        scores = [jnp.zeros((CHUNK, CHUNK), F32) for _ in range(HG_HEADS)]
        for lvl in range(N_LEVELS):
            half = 1 << lvl
            e_l = e[lvl * CHUNK:(lvl + 1) * CHUNK, :].astype(BF16)
            if half >= 16:
                sel = jnp.concatenate([(qf_bf if (r0 // half) % 2 else k_bf)[r0:r0 + half]
                                       for r0 in range(0, CHUNK, half)], axis=0)
            else:
                sel = jnp.where((row // half) % 2 == 1, qf_bf, k_bf)
            a = sel * e_l
            m = mask_ref[lvl]
            for hd in range(HG_HEADS):
                a_h = a[:, hd * HG_DK:(hd + 1) * HG_DK]
                scores[hd] = scores[hd] + m * _dot_nt(a_h, a_h)
        e_cum = e[N_LEVELS * CHUNK:(N_LEVELS + 1) * CHUNK, :]
        e_suf = e[(N_LEVELS + 1) * CHUNK:(N_LEVELS + 2) * CHUNK, :]
        diag_v = qf * k

    q_in = qf_bf * e_cum.astype(BF16)
    k_out = k_bf * e_suf.astype(BF16)
    e_last = e_cum[CHUNK - 1:CHUNK, :]

    outs = []
    for hd in range(HG_HEADS):
        cols = slice(hd * HG_DK, (hd + 1) * HG_DK)
        st = state_ref[hd]
        o = _dot_nt(q_in[:, cols], st.astype(BF16)) + _dot(scores[hd].astype(BF16), v_bf[:, cols])
        if diag_v is not None:
            o = o + jnp.sum(diag_v[:, cols], axis=-1, keepdims=True) * v[:, cols]
        state_ref[hd] = st * e_last[:, cols] + _dot_tn(v_bf[:, cols], k_out[:, cols])
        o = o * lax.rsqrt(jnp.mean(o * o, axis=-1, keepdims=True) + RMS_EPS) * nw
        outs.append(o)
    hg = 0.5 * gate
    return jnp.concatenate(outs, axis=1) * (hg + hg * jnp.tanh(hg))


def _attn_bias():
    r = np.arange(ATT_BLK)[:, None]
    c = np.arange(2 * ATT_BLK)[None, :]
    dist = r + ATT_BLK - c
    window = (dist >= 0) & (dist < ATT_BLK)
    slopes = np.exp2(-8.0 * (np.arange(ATT_Q_HEADS, dtype=np.float32) + 1.0) / ATT_Q_HEADS).astype(np.float32)
    alibi = -slopes[:, None, None] * dist.astype(np.float32)[None]
    later = np.where(window[None], alibi, np.float32(NEG_INF))
    first = np.where((window & (c >= ATT_BLK))[None], alibi, np.float32(NEG_INF))
    return (np.stack([later, first]) * LOG2_E).astype(np.float32)


def _attn_block(q_ref, kv_cur, kv_prev, bias_ref, table, sink_ref, o_ref):
    lane = lax.broadcasted_iota(jnp.int32, (2 * ATT_BLK, 2 * ATT_KVW), 1)
    lo = (lane % ATT_KVW) < ATT_HD

    kv = jnp.concatenate([kv_prev, kv_cur], axis=0).astype(BF16)
    kv_sw = jnp.concatenate([kv[:, ATT_HD:ATT_KVW], kv[:, :ATT_HD],
                             kv[:, ATT_KVW + ATT_HD:], kv[:, ATT_KVW:ATT_KVW + ATT_HD]], axis=1)
    zero = jnp.zeros_like(kv)
    placed = {}
    for h in range(ATT_KV_HEADS):
        for off in range(2):
            src = kv if h == off else kv_sw
            placed[h, off] = jnp.where(lo if off == 0 else jnp.logical_not(lo), src, zero)

    scale = ATT_HD ** -0.5 * LOG2_E
    for pair in range(ATT_Q_HEADS // 2):
        qp = (q_ref[:, pair * 2 * ATT_HD:(pair + 1) * 2 * ATT_HD] * scale).astype(BF16)
        acc = jnp.zeros((ATT_BLK, 2 * ATT_HD), F32)
        for off in range(2):
            j = 2 * pair + off
            kvh = placed[j // ATT_GROUP, off]
            sink = sink_ref[j] * LOG2_E
            logits = _dot_nt(qp, kvh[:, :ATT_KVW]) + bias_ref[table, j]
            m = jnp.maximum(jnp.max(logits, axis=-1, keepdims=True), sink)
            p = jnp.exp2(logits - m)
            den = jnp.sum(p, axis=-1, keepdims=True) + jnp.exp2(sink - m)
            acc = acc + _dot(p.astype(BF16), kvh[:, ATT_KVW:]) / den
        o_ref[:, pair * 2 * ATT_HD:(pair + 1) * 2 * ATT_HD] = acc.astype(o_ref.dtype)


def _mixers_kernel(sink_ref, bounded_ref, q_ref, f_ref, i_ref, g_ref, aq_ref, kv_ref, kvp_ref, lb_ref, nw_ref,
                   seg_ref, mask_ref, bias_ref, hg_ref, at_ref, state_ref):
    first = pl.program_id(1) == 0

    @pl.when(first)
    def _():
        state_ref[...] = jnp.zeros_like(state_ref)

    def tile_body(bounded):
        lb = lb_ref[...]
        c0 = 0.5 + 0.5 * lb
        c1 = 0.5 - 0.5 * lb
        nw = nw_ref[...]
        seg = seg_ref[...]
        tile = q_ref.shape[0]
        for r0 in range(0, tile, CHUNK):
            rows = slice(r0, r0 + CHUNK)
            o = _hgrn_chunk(q_ref[rows, :], f_ref[rows, :], i_ref[rows, :], g_ref[rows, :], c0, c1, nw, seg,
                            mask_ref, state_ref, bounded)
            hg_ref[rows, :] = o.astype(hg_ref.dtype)
        for r0 in range(0, tile, ATT_BLK):
            rows = slice(r0, r0 + ATT_BLK)
            if r0 == 0:
                prev, table = kvp_ref[...], jnp.where(first, 1, 0)
            else:
                prev, table = kv_ref[r0 - ATT_BLK:r0, :], 0
            _attn_block(aq_ref.at[rows, :], kv_ref[rows, :], prev, bias_ref, table, sink_ref, at_ref.at[rows, :])

    pl.when(bounded_ref[0] == 1)(lambda: tile_body(True))
    pl.when(bounded_ref[0] != 1)(lambda: tile_body(False))


def _mixers(proj_f, proj_b, lb, norm_w, sinks, B, S):
    T = B * S
    tile = MIX_TILE if S % MIX_TILE == 0 else ATT_BLK
    nb = S // tile
    per = tile // ATT_BLK
    seg, masks = _hgrn_tables()
    const = lambda shape: pl.BlockSpec(shape, lambda b, n: (0,) * len(shape))
    hcol = lambda cb: pl.BlockSpec((tile, HG_W), lambda b, n: (b * nb + n, cb))
    kvcol = (HG_W + ATT_QW) // (2 * ATT_KVW)
    out = pl.BlockSpec((tile, HG_W), lambda b, n: (b * nb + n, 0))
    return pl.pallas_call(
        _mixers_kernel,
        grid=(B, nb),
        in_specs=[pl.BlockSpec(memory_space=pltpu.SMEM), pl.BlockSpec(memory_space=pltpu.SMEM),
                  hcol(0), hcol(1), hcol(0), hcol(2),
                  pl.BlockSpec((tile, ATT_QW), lambda b, n: (b * nb + n, HG_W // ATT_QW)),
                  pl.BlockSpec((tile, 2 * ATT_KVW), lambda b, n: (b * nb + n, kvcol)),
                  pl.BlockSpec((ATT_BLK, 2 * ATT_KVW),
                               lambda b, n: ((b * nb + n) * per - jnp.where(n > 0, 1, 0), kvcol)),
                  const((1, HG_W)), const((1, HG_DV)), const((N_ARG_GROUPS * CHUNK, 2 * CHUNK)),
                  const((N_LEVELS, CHUNK, CHUNK)), const((2, ATT_Q_HEADS, ATT_BLK, 2 * ATT_BLK))],
        out_specs=[out, out],
        out_shape=[jax.ShapeDtypeStruct((T, HG_W), BF16), jax.ShapeDtypeStruct((T, ATT_QW), BF16)],
        scratch_shapes=[pltpu.VMEM((HG_HEADS, HG_DV, HG_DK), F32)],
        compiler_params=_cparams(("arbitrary", "arbitrary")),
        name="mixers",
    )(sinks, (jnp.min(lb) >= HG_SAFE_LB).astype(jnp.int32).reshape(1),
      proj_f, proj_f, proj_b, proj_f, proj_b, proj_b, proj_b, lb, norm_w, jnp.asarray(seg, BF16),
      jnp.asarray(masks), jnp.asarray(_attn_bias()))


def _merge_kernel(hg_ref, at_ref, x_ref, wgate_ref, wa_ref, wb_ref, wo_ref, g1_ref, b1_ref,
                  wr_ref, br_ref, tri_ref, x1_ref, xp_ref, eid_ref, wt_ref, rank_ref, cnt_ref, base_ref, x1b_ref):
    tm = x_ref.shape[0]
    sub = min(tm, MERGE_SUB)
    for r0 in range(0, tm, sub):
        rows = slice(r0, r0 + sub)
        xs = x_ref[rows, :]
        gates = _dot(xs.astype(BF16), wgate_ref[...])
        ya = _dot(hg_ref[rows, :], wa_ref[...])
        yb = _dot(at_ref[rows, :], wb_ref[...])
        merged = _sigmoid(gates[:, :D_MODEL]) * ya + _sigmoid(gates[:, D_MODEL:]) * yb
        z = DN_ALPHA * xs + _dot(merged.astype(BF16), wo_ref[...])
        x1 = _layer_norm(z, g1_ref[...], b1_ref[...])
        x1_ref[rows, :] = x1
        x1b = x1.astype(BF16)
        x1b_ref[rows, :] = x1b
        xp_ref[rows, :] = _pack_bf16_pairs(x1b)

    lg = _dot_nt(wr_ref[...], x1b_ref[...]) + br_ref[...]
    g = lg[0:8, :]
    row8 = lax.broadcasted_iota(jnp.int32, (8, tm), 0)
    g = jnp.where(row8 < N_GROUPS, g, -jnp.inf)
    gmax = jnp.max(g, axis=0, keepdims=True)
    gsel = jnp.min(jnp.where(g == gmax, row8, 8), axis=0, keepdims=True)
    gw = 1.0 / jnp.sum(jnp.exp(g - gmax), axis=0, keepdims=True)
    el = jnp.where(gsel == 0, lg[8:16, :],
                   jnp.where(gsel == 1, lg[16:24, :], jnp.where(gsel == 2, lg[24:32, :], lg[32:40, :])))
    v1 = jnp.max(el, axis=0, keepdims=True)
    i1 = jnp.min(jnp.where(el == v1, row8, 8), axis=0, keepdims=True)
    el2 = jnp.where(row8 == i1, -jnp.inf, el)
    v2 = jnp.max(el2, axis=0, keepdims=True)
    i2 = jnp.min(jnp.where(el2 == v2, row8, 8), axis=0, keepdims=True)
    e2 = jnp.exp(v2 - v1)
    den = 1.0 + e2
    e_a = gsel * EPG + i1
    e_b = gsel * EPG + i2
    eid_ref[...] = jnp.concatenate([e_a, e_b], axis=0)
    wt_ref[...] = jnp.concatenate([gw / den, gw * e2 / den], axis=0)

    @pl.when(pl.program_id(0) == 0)
    def _():
        base_ref[...] = jnp.zeros_like(base_ref)

    row_e = lax.broadcasted_iota(jnp.int32, (N_EXPERTS, tm), 0)
    oh_a = jnp.where(row_e == e_a, 1.0, 0.0)
    oh_b = jnp.where(row_e == e_b, 1.0, 0.0)
    tri = tri_ref[...]
    pre_a = _dot(oh_a.astype(BF16), tri)
    pre_b = _dot(oh_b.astype(BF16), tri)
    cnt_a = jnp.sum(oh_a, axis=1, keepdims=True)
    cnt_b = jnp.sum(oh_b, axis=1, keepdims=True)
    base = base_ref[...]
    rank_a = jnp.sum(oh_a * (base + pre_a), axis=0, keepdims=True)
    rank_b = jnp.sum(oh_b * (base + cnt_a + pre_b), axis=0, keepdims=True)
    rank_ref[...] = jnp.concatenate([rank_a, rank_b], axis=0).astype(jnp.int32)
    base = base + cnt_a + cnt_b
    base_ref[...] = base
    cnt_ref[...] = jnp.broadcast_to(base, cnt_ref.shape)


def _merge(hg, att, x2, wgate, wa, wb, wo, g1, b1, wr, br, tm, row0):
    T = hg.shape[0]
    tile0 = row0 // tm
    row = lambda w: pl.BlockSpec((tm, w), lambda i: (i, 0))
    const = lambda shape: pl.BlockSpec(shape, lambda i: (0,) * len(shape), pipeline_mode=pl.Buffered(1))
    lanes = pl.BlockSpec((2, tm), lambda i: (0, i))
    tri = jnp.asarray(np.triu(np.ones((tm, tm), np.float32), 1), BF16)
    return pl.pallas_call(
        _merge_kernel,
        grid=(T // tm,),
        in_specs=[row(HG_W), row(ATT_QW),
                  pl.BlockSpec((tm, D_MODEL), lambda i: (i + tile0, 0)),
                  const((D_MODEL, 2 * D_MODEL)), const((HG_W, D_MODEL)), const((ATT_QW, D_MODEL)), const((D_MODEL, D_MODEL)),
                  const((1, D_MODEL)), const((1, D_MODEL)), const((40, D_MODEL)), const((40, 1)),
                  const((tm, tm))],
        out_specs=[row(D_MODEL), row(D_MODEL // 2), lanes, lanes, lanes,
                   pl.BlockSpec((N_EXPERTS, 128), lambda i: (0, 0))],
        out_shape=[jax.ShapeDtypeStruct((T, D_MODEL), F32),
                   jax.ShapeDtypeStruct((T, D_MODEL // 2), jnp.uint32),
                   jax.ShapeDtypeStruct((2, T), jnp.int32),
                   jax.ShapeDtypeStruct((2, T), F32),
                   jax.ShapeDtypeStruct((2, T), jnp.int32),
                   jax.ShapeDtypeStruct((N_EXPERTS, 128), F32)],
        scratch_shapes=[pltpu.VMEM((N_EXPERTS, 1), F32), pltpu.VMEM((tm, D_MODEL), BF16)],
        compiler_params=_cparams(("arbitrary",)),
        name="merge",
    )(hg, att, x2, wgate, wa, wb, wo, g1, b1, wr, br, tri)


SC_WINDOW = 64
SC_IDX_LANES = 128


def _pad_indices(idx):
    rows = idx.reshape(-1, SC_WINDOW)
    return jnp.pad(rows, ((0, 0), (0, SC_IDX_LANES - SC_WINDOW)))


def _sc_mesh():
    return plsc.VectorSubcoreMesh(core_axis_name="core", subcore_axis_name="subcore")


def _sc_scatter2(x, idx_a, idx_b, n_out):
    T, d = x.shape

    @pl.kernel(out_type=jax.ShapeDtypeStruct((n_out, d), x.dtype), mesh=_sc_mesh())
    def scatter(x_hbm, ia_hbm, ib_hbm, o_hbm):
        def body(x_vmem, ia_vmem, ib_vmem):
            pltpu.sync_copy(x_vmem, o_hbm.at[ia_vmem.at[0, pl.ds(0, SC_WINDOW)]])
            pltpu.sync_copy(x_vmem, o_hbm.at[ib_vmem.at[0, pl.ds(0, SC_WINDOW)]])

        idx_spec = pl.BlockSpec((1, SC_IDX_LANES), lambda i: (i, 0))
        pltpu.emit_pipeline(
            body, grid=(T // SC_WINDOW,),
            in_specs=[pl.BlockSpec((SC_WINDOW, d), lambda i: (i, 0)), idx_spec, idx_spec],
            out_specs=[],
            core_axis_name=("core", "subcore"),
            dimension_semantics=(pltpu.PARALLEL,),
        )(x_hbm, ia_hbm, ib_hbm)

    return scatter(x, _pad_indices(idx_a), _pad_indices(idx_b))


def _sc_gather(x, idx):
    n = idx.shape[0]
    d = x.shape[1]

    @pl.kernel(out_type=jax.ShapeDtypeStruct((n, d), x.dtype), mesh=_sc_mesh())
    def gather(x_hbm, i_hbm, o_hbm):
        def body(i_vmem, o_vmem):
            pltpu.sync_copy(x_hbm.at[i_vmem.at[0, pl.ds(0, SC_WINDOW)]], o_vmem)

        pltpu.emit_pipeline(
            body, grid=(n // SC_WINDOW,),
            in_specs=[pl.BlockSpec((1, SC_IDX_LANES), lambda i: (i, 0))],
            out_specs=[pl.BlockSpec((SC_WINDOW, d), lambda i: (i, 0))],
            core_axis_name=("core", "subcore"),
            dimension_semantics=(pltpu.PARALLEL,),
        )(i_hbm, o_hbm)

    return gather(x, _pad_indices(idx))


def _expert_kernel(be_ref, nv_ref, nxt_ref, par_ref, xb_ref, wg_hbm, wu_hbm, wd_hbm, y_ref,
                   wg_st, wu_st, wd_st, wg_bf, wu_bf, wd_bf, sem):
    i = pl.program_id(0)
    nv = nv_ref[i]

    def fetch(expert, slot):
        return [pltpu.make_async_copy(src.at[expert], dst.at[slot], sem.at[slot, j])
                for j, (src, dst) in enumerate(((wg_hbm, wg_st), (wu_hbm, wu_st), (wd_hbm, wd_st)))]

    @pl.when(i == 0)
    def _():
        for cp in fetch(be_ref[0], par_ref[0]):
            cp.start()

    @pl.when((i == 0) | (be_ref[i] != be_ref[jnp.maximum(i - 1, 0)]))
    def _():
        slot = par_ref[i]
        for cp in fetch(be_ref[i], slot):
            cp.wait()
        wg_bf[...] = wg_st[slot].astype(BF16)
        wu_bf[...] = wu_st[slot].astype(BF16)
        wd_bf[...] = wd_st[slot].astype(BF16)

        @pl.when(nxt_ref[i] >= 0)
        def _():
            for cp in fetch(nxt_ref[i], 1 - slot):
                cp.start()

    @pl.when(nv > 0)
    def _():
        row = lax.broadcasted_iota(jnp.int32, (MOE_BLOCK, 1), 0)
        xb = _unpack_bf16_pairs(jnp.where(row < nv, xb_ref[...], jnp.uint32(0))).astype(BF16)
        g = _dot(xb, wg_bf[...])
        u = _dot(xb, wu_bf[...])
        h = (_silu(g) * u).astype(BF16)
        y_ref[...] = _pack_bf16_pairs(_dot(h, wd_bf[...]).astype(BF16))


def _experts(block_meta, xbuf, w_gate, w_up, w_down):
    P = xbuf.shape[0]
    nb = P // MOE_BLOCK
    rows = pl.BlockSpec((MOE_BLOCK, D_MODEL // 2), lambda i, *_: (i, 0))
    hbm = pl.BlockSpec(memory_space=pl.ANY)
    return pl.pallas_call(
        _expert_kernel,
        grid_spec=pltpu.PrefetchScalarGridSpec(
            num_scalar_prefetch=4,
            grid=(nb,),
            in_specs=[rows, hbm, hbm, hbm],
            out_specs=rows,
            scratch_shapes=[pltpu.VMEM((2, D_MODEL, D_EXPERT), F32), pltpu.VMEM((2, D_MODEL, D_EXPERT), F32),
                            pltpu.VMEM((2, D_EXPERT, D_MODEL), F32),
                            pltpu.VMEM((D_MODEL, D_EXPERT), BF16), pltpu.VMEM((D_MODEL, D_EXPERT), BF16),
                            pltpu.VMEM((D_EXPERT, D_MODEL), BF16),
                            pltpu.SemaphoreType.DMA((2, 3))]),
        out_shape=jax.ShapeDtypeStruct((P, D_MODEL // 2), jnp.uint32),
        compiler_params=_cparams(("arbitrary",)),
        name="experts",
    )(*block_meta, xbuf, w_gate, w_up, w_down)


def _combine_kernel(ra_ref, rb_ref, x1_ref, wt_ref, g2_ref, b2_ref, *rest):
    o_ref = rest[-1]
    w2 = wt_ref[...]
    w = jnp.concatenate([w2, jnp.zeros((6, w2.shape[1]), F32)], axis=0).T
    y = w[:, 0:1] * _unpack_bf16_pairs(ra_ref[...]) + w[:, 1:2] * _unpack_bf16_pairs(rb_ref[...])
    o_ref[...] = _layer_norm(DN_ALPHA * x1_ref[...] + y, g2_ref[...], b2_ref[...])


def _combine(rows2, x1, wt, g2, b2, tm, out_prev, tile0, t_total):
    T = x1.shape[0]
    nt = T // tm
    const = lambda shape: pl.BlockSpec(shape, lambda i: (0,) * len(shape))
    in_specs = [pl.BlockSpec((tm, D_MODEL // 2), lambda i: (i, 0)),
                pl.BlockSpec((tm, D_MODEL // 2), lambda i: (i + nt, 0)),
                pl.BlockSpec((tm, D_MODEL), lambda i: (i, 0)), pl.BlockSpec((2, tm), lambda i: (0, i)),
                const((1, D_MODEL)), const((1, D_MODEL))]
    args = [rows2, rows2, x1, wt, g2, b2]
    aliases = {}
    if out_prev is not None:
        in_specs.append(pl.BlockSpec(memory_space=pl.ANY))
        args.append(out_prev)
        aliases = {len(args) - 1: 0}
    return pl.pallas_call(
        _combine_kernel,
        grid=(nt,),
        in_specs=in_specs,
        out_specs=pl.BlockSpec((tm, D_MODEL), lambda i: (i + tile0, 0)),
        out_shape=jax.ShapeDtypeStruct((t_total, D_MODEL), F32),
        input_output_aliases=aliases,
        compiler_params=_cparams(("arbitrary",)),
        name="combine",
    )(*args)


def _slot_layout(eid, rank, counts, T):
    P = 2 * T + N_EXPERTS * MOE_BLOCK
    nb = P // MOE_BLOCK
    padded = ((counts + MOE_BLOCK - 1) // MOE_BLOCK) * MOE_BLOCK
    pend = jnp.cumsum(padded)
    pstart = pend - padded
    experts = jnp.arange(N_EXPERTS, dtype=jnp.int32)
    pos = rank + jnp.sum(jnp.where(eid[:, :, None] == experts, pstart, 0), axis=-1)

    block_start = jnp.arange(nb, dtype=jnp.int32) * MOE_BLOCK
    block_expert = jnp.minimum(jnp.sum(block_start[:, None] >= pend[None, :], axis=1), N_EXPERTS - 1)
    onehot = block_expert[:, None] == experts[None, :]
    look = lambda table: jnp.sum(jnp.where(onehot, table[None, :], 0), axis=1)
    block_valid = jnp.clip(look(pstart + counts) - block_start, 0, MOE_BLOCK)
    block_valid = jnp.where(block_start < pend[-1], block_valid, 0)
    present = (counts > 0) | (experts == N_EXPERTS - 1)
    later = present[None, :] & (experts[None, :] > experts[:, None])
    next_present = jnp.min(jnp.where(later, experts[None, :], N_EXPERTS), axis=1)
    next_present = jnp.where(next_present < N_EXPERTS, next_present, -1)
    runs_before = jnp.sum(present[None, :] & (experts[None, :] < experts[:, None]), axis=1)
    block_next = look(next_present)
    block_slot = look(runs_before % 2)
    i32 = lambda a: a.astype(jnp.int32)
    return i32(pos), (i32(block_expert), i32(block_valid), i32(block_next), i32(block_slot))


def kernel(x, lb_logits, w_in, hg_norm_w, sinks, w_branch_a, w_branch_b, w_out, ln1_g, ln1_b,
           router_group_w, router_group_b, router_expert_w, router_expert_b,
           w_exp_gate, w_exp_up, w_exp_down, ln2_g, ln2_b):
    B, S, D = x.shape
    assert D == D_MODEL and S % ATT_BLK == 0 and w_in.shape[0] == DEPTH == 1
    lb_all = jnp.cumsum(jax.nn.softmax(lb_logits.astype(F32), axis=0), axis=0)
    lb = lb_all[0].reshape(1, HG_W)
    w0 = w_in[0]
    w_f = jnp.concatenate([w0[:, :2 * HG_W], w0[:, 3 * HG_W:4 * HG_W]], axis=1).astype(BF16)
    w_b = jnp.concatenate([w0[:, 2 * HG_W:3 * HG_W], w0[:, 4 * HG_W:PROJ_W]], axis=1).astype(BF16)
    w_gate = w0[:, PROJ_W:].astype(BF16)
    nw = hg_norm_w[0].reshape(1, HG_DV).astype(F32)
    wa, wb, wo = w_branch_a[0].astype(BF16), w_branch_b[0].astype(BF16), w_out[0].astype(BF16)
    wr = jnp.zeros((40, D), F32).at[0:N_GROUPS].set(router_group_w[0].T).at[8:40].set(router_expert_w[0].T)
    br = jnp.zeros((40, 1), F32).at[0:N_GROUPS, 0].set(router_group_b[0]).at[8:40, 0].set(router_expert_b[0])
    g1, b1 = ln1_g[0].reshape(1, D), ln1_b[0].reshape(1, D)
    g2, b2 = ln2_g[0].reshape(1, D), ln2_b[0].reshape(1, D)

    n_parts = N_PARTS if B % N_PARTS == 0 else 1
    bp = B // n_parts
    tp = bp * S
    pick = lambda want: next(t for t in (want, 512, 256, ATT_BLK) if t <= want and tp % t == 0)
    tm_proj, tm_merge, tm_comb = pick(1024), pick(1024), pick(1024)
    x2 = x.reshape(B * S, D)
    out = None
    for part in range(n_parts):
        proj_f, proj_b = _proj(x2, w_f, w_b, tm_proj, part * tp, tp)
        hg, att = _mixers(proj_f, proj_b, lb, nw, sinks[0].astype(F32), bp, S)
        x1, xp, eid, wt, rank, cnt = _merge(hg, att, x2, w_gate, wa, wb, wo, g1, b1, wr.astype(BF16), br,
                                            tm_merge, part * tp)
        pos, block_meta = _slot_layout(eid, rank, cnt[:, 0].astype(jnp.int32), tp)
        xbuf = _sc_scatter2(xp, pos[0], pos[1], 2 * tp + N_EXPERTS * MOE_BLOCK)
        ybuf = _experts(block_meta, xbuf, w_exp_gate[0], w_exp_up[0], w_exp_down[0])
        rows2 = _sc_gather(ybuf, pos.reshape(-1))
        out = _combine(rows2, x1, wt, g2, b2, tm_comb, out, part * (tp // tm_comb), B * S)
    return out.reshape(B, S, D)
```

```python
import functools

import numpy as np
import jax
import jax.numpy as jnp
from jax import lax
from jax.experimental import pallas as pl
from jax.experimental.pallas import tpu as pltpu
from jax.experimental.pallas import tpu_sc as plsc

F32 = jnp.float32
BF16 = jnp.bfloat16

D_MODEL = 1024
DEPTH = 1
HG_HEADS = 4
HG_DK = 128
HG_DV = 128
HG_W = HG_HEADS * HG_DK
CHUNK = 64
ATT_Q_HEADS = 8
ATT_KV_HEADS = 2
ATT_GROUP = ATT_Q_HEADS // ATT_KV_HEADS
ATT_HD = 64
ATT_QW = ATT_Q_HEADS * ATT_HD
ATT_KVW = ATT_KV_HEADS * ATT_HD
ATT_BLK = 128
N_GROUPS = 4
EPG = 8
N_EXPERTS = N_GROUPS * EPG
D_EXPERT = 512
MOE_BLOCK = 512
DN_ALPHA = (2.0 * DEPTH) ** 0.25
LN_EPS = 1e-5
RMS_EPS = 1e-6
NEG_INF = -1e30
LOG2_E = 1.4426950408889634

PROJ_W = 4 * HG_W + ATT_QW + 2 * ATT_KVW
PROJ_F = 3 * HG_W
PROJ_B = HG_W + ATT_QW + 2 * ATT_KVW
N_LEVELS = 6
N_ARG_GROUPS = N_LEVELS + 2
HG_SAFE_LB = 0.125

VMEM_LIMIT = 56 * 1024 * 1024
MERGE_SUB = 256
MIX_TILE = 512
N_PARTS = 2


def _cparams(sem):
    return pltpu.CompilerParams(dimension_semantics=sem, vmem_limit_bytes=VMEM_LIMIT)


def _dot(a, b):
    return jnp.dot(a, b, preferred_element_type=F32)


def _dot_nt(a, b):
    return lax.dot_general(a, b, (((1,), (1,)), ((), ())), preferred_element_type=F32)


def _dot_tn(a, b):
    return lax.dot_general(a, b, (((0,), (0,)), ((), ())), preferred_element_type=F32)


def _sigmoid(x):
    return 0.5 * jnp.tanh(0.5 * x) + 0.5


def _silu(x):
    return x * _sigmoid(x)


def _pack_bf16_pairs(xb):
    n = xb.shape[1] // 2
    lo = lax.bitcast_convert_type(xb[:, :n].astype(F32), jnp.uint32)
    hi = lax.bitcast_convert_type(xb[:, n:].astype(F32), jnp.uint32)
    return (lo >> 16) | hi


def _unpack_bf16_pairs(w):
    lo = lax.bitcast_convert_type(w << 16, F32)
    hi = lax.bitcast_convert_type(w & jnp.uint32(0xFFFF0000), F32)
    return jnp.concatenate([lo, hi], axis=1)


def _layer_norm(z, g, b):
    mu = jnp.mean(z, axis=-1, keepdims=True)
    zc = z - mu
    var = jnp.mean(zc * zc, axis=-1, keepdims=True)
    return zc * lax.rsqrt(var + LN_EPS) * g + b


def _proj_kernel(x_ref, wf_ref, wb_ref, of_ref, ob_ref):
    xb = x_ref[...].astype(BF16)
    of_ref[...] = _dot(xb, wf_ref[...])
    ob_ref[...] = _dot(xb, wb_ref[...]).astype(BF16)


def _proj(x2, w_f, w_b, tm, row0, T):
    tile0 = row0 // tm
    return pl.pallas_call(
        _proj_kernel,
        grid=(T // tm,),
        in_specs=[pl.BlockSpec((tm, D_MODEL), lambda i: (i + tile0, 0)),
                  pl.BlockSpec((D_MODEL, PROJ_F), lambda i: (0, 0), pipeline_mode=pl.Buffered(1)),
                  pl.BlockSpec((D_MODEL, PROJ_B), lambda i: (0, 0), pipeline_mode=pl.Buffered(1))],
        out_specs=[pl.BlockSpec((tm, PROJ_F), lambda i: (i, 0)), pl.BlockSpec((tm, PROJ_B), lambda i: (i, 0))],
        out_shape=[jax.ShapeDtypeStruct((T, PROJ_F), F32), jax.ShapeDtypeStruct((T, PROJ_B), BF16)],
        compiler_params=_cparams(("arbitrary",)),
        name="proj",
    )(x2, w_f, w_b)


def _hgrn_tables():
    C = CHUNK
    w = np.zeros((N_ARG_GROUPS, C, C), np.float32)
    masks = np.zeros((N_LEVELS, C, C), np.float32)
    for lvl in range(N_LEVELS):
        h = 1 << lvl
        for t in range(C):
            base = (t // (2 * h)) * 2 * h
            m = base + h - 1
            if (t // h) % 2 == 1:
                w[lvl, t, m + 1:t + 1] = 1.0
                masks[lvl, t, base:base + h] = 1.0
            else:
                w[lvl, t, t + 1:m + 1] = 1.0
    for t in range(C):
        w[N_LEVELS, t, :t + 1] = 1.0
        w[N_LEVELS + 1, t, t + 1:] = 1.0
    w = w.reshape(N_ARG_GROUPS * C, C)
    return np.concatenate([w, w], axis=1), masks


def _hgrn_chunk(q, fz, v, gate, c0, c1, nw, seg, mask_ref, state_ref, bounded):
    h = 0.5 * q
    qf = h + h * jnp.tanh(h)
    t1 = c1 * jnp.tanh(0.5 * fz)
    f = c0 + t1
    k = c1 - t1
    l2 = jnp.log2(f)
    v_bf = v.astype(BF16)
    qf_bf = qf.astype(BF16)
    k_bf = k.astype(BF16)

    l_hi = l2.astype(BF16)
    l_lo = (l2 - l_hi.astype(F32)).astype(BF16)
    l_split = jnp.concatenate([l_hi, l_lo], axis=0)
    row = lax.broadcasted_iota(jnp.int32, (CHUNK, 1), 0)
    if bounded:
        cum = _dot(seg[N_LEVELS * CHUNK:(N_LEVELS + 1) * CHUNK, :], l_split)
        mid = cum[CHUNK // 2 - 1:CHUNK // 2, :]
        last = cum[CHUNK - 1:CHUNK, :]
        e_cum = jnp.exp2(cum)
        e_suf = jnp.exp2(last - cum)
        a_q = qf_bf * jnp.exp2(cum - mid).astype(BF16)
        a_k = k_bf * jnp.exp2(mid - cum).astype(BF16)
        causal = row >= lax.broadcasted_iota(jnp.int32, (1, CHUNK), 1)
        scores = [jnp.where(causal, _dot_nt(a_q[:, hd * HG_DK:(hd + 1) * HG_DK], a_k[:, hd * HG_DK:(hd + 1) * HG_DK]),
                            0.0) for hd in range(HG_HEADS)]
        diag_v = None
    else:
        e = jnp.exp2(_dot(seg, l_split))
        scores = [jnp.zeros((CHUNK, CHUNK), F32) for _ in range(HG_HEADS)]
        for lvl in range(N_LEVELS):
            half = 1 << lvl
            e_l = e[lvl * CHUNK:(lvl + 1) * CHUNK, :].astype(BF16)
            if half >= 16:
                sel = jnp.concatenate([(qf_bf if (r0 // half) % 2 else k_bf)[r0:r0 + half]
                                       for r0 in range(0, CHUNK, half)], axis=0)
            else:
                sel = jnp.where((row // half) % 2 == 1, qf_bf, k_bf)
            a = sel * e_l
            m = mask_ref[lvl]
            for hd in range(HG_HEADS):
                a_h = a[:, hd * HG_DK:(hd + 1) * HG_DK]
                scores[hd] = scores[hd] + m * _dot_nt(a_h, a_h)
        e_cum = e[N_LEVELS * CHUNK:(N_LEVELS + 1) * CHUNK, :]
        e_suf = e[(N_LEVELS + 1) * CHUNK:(N_LEVELS + 2) * CHUNK, :]
        diag_v = qf * k

    q_in = qf_bf * e_cum.astype(BF16)
    k_out = k_bf * e_suf.astype(BF16)
    e_last = e_cum[CHUNK - 1:CHUNK, :]

    outs = []
    for hd in range(HG_HEADS):
        cols = slice(hd * HG_DK, (hd + 1) * HG_DK)
        st = state_ref[hd]
        o = _dot_nt(q_in[:, cols], st.astype(BF16)) + _dot(scores[hd].astype(BF16), v_bf[:, cols])
        if diag_v is not None:
            o = o + jnp.sum(diag_v[:, cols], axis=-1, keepdims=True) * v[:, cols]
        state_ref[hd] = st * e_last[:, cols] + _dot_tn(v_bf[:, cols], k_out[:, cols])
        o = o * lax.rsqrt(jnp.mean(o * o, axis=-1, keepdims=True) + RMS_EPS) * nw
        outs.append(o)
    hg = 0.5 * gate
    return jnp.concatenate(outs, axis=1) * (hg + hg * jnp.tanh(hg))


def _attn_bias():
    r = np.arange(ATT_BLK)[:, None]
    c = np.arange(2 * ATT_BLK)[None, :]
    dist = r + ATT_BLK - c
    window = (dist >= 0) & (dist < ATT_BLK)
    slopes = np.exp2(-8.0 * (np.arange(ATT_Q_HEADS, dtype=np.float32) + 1.0) / ATT_Q_HEADS).astype(np.float32)
    alibi = -slopes[:, None, None] * dist.astype(np.float32)[None]
    later = np.where(window[None], alibi, np.float32(NEG_INF))
    first = np.where((window & (c >= ATT_BLK))[None], alibi, np.float32(NEG_INF))
    return (np.stack([later, first]) * LOG2_E).astype(np.float32)


def _attn_block(q_ref, kv_cur, kv_prev, bias_ref, table, sink_ref, o_ref):
    lane = lax.broadcasted_iota(jnp.int32, (2 * ATT_BLK, 2 * ATT_KVW), 1)
    lo = (lane % ATT_KVW) < ATT_HD

    kv = jnp.concatenate([kv_prev, kv_cur], axis=0).astype(BF16)
    kv_sw = jnp.concatenate([kv[:, ATT_HD:ATT_KVW], kv[:, :ATT_HD],
                             kv[:, ATT_KVW + ATT_HD:], kv[:, ATT_KVW:ATT_KVW + ATT_HD]], axis=1)
    zero = jnp.zeros_like(kv)
    placed = {}
    for h in range(ATT_KV_HEADS):
        for off in range(2):
            src = kv if h == off else kv_sw
            placed[h, off] = jnp.where(lo if off == 0 else jnp.logical_not(lo), src, zero)

    scale = ATT_HD ** -0.5 * LOG2_E
    for pair in range(ATT_Q_HEADS // 2):
        qp = (q_ref[:, pair * 2 * ATT_HD:(pair + 1) * 2 * ATT_HD] * scale).astype(BF16)
        acc = jnp.zeros((ATT_BLK, 2 * ATT_HD), F32)
        for off in range(2):
            j = 2 * pair + off
            kvh = placed[j // ATT_GROUP, off]
            sink = sink_ref[j] * LOG2_E
            logits = _dot_nt(qp, kvh[:, :ATT_KVW]) + bias_ref[table, j]
            m = jnp.maximum(jnp.max(logits, axis=-1, keepdims=True), sink)
            p = jnp.exp2(logits - m)
            den = jnp.sum(p, axis=-1, keepdims=True) + jnp.exp2(sink - m)
            acc = acc + _dot(p.astype(BF16), kvh[:, ATT_KVW:]) / den
        o_ref[:, pair * 2 * ATT_HD:(pair + 1) * 2 * ATT_HD] = acc.astype(o_ref.dtype)


def _mixers_kernel(sink_ref, bounded_ref, q_ref, f_ref, i_ref, g_ref, aq_ref, kv_ref, kvp_ref, lb_ref, nw_ref,
                   seg_ref, mask_ref, bias_ref, hg_ref, at_ref, state_ref):
    first = pl.program_id(1) == 0

    @pl.when(first)
    def _():
        state_ref[...] = jnp.zeros_like(state_ref)

    def tile_body(bounded):
        lb = lb_ref[...]
        c0 = 0.5 + 0.5 * lb
        c1 = 0.5 - 0.5 * lb
        nw = nw_ref[...]
        seg = seg_ref[...]
        tile = q_ref.shape[0]
        for r0 in range(0, tile, CHUNK):
            rows = slice(r0, r0 + CHUNK)
            o = _hgrn_chunk(q_ref[rows, :], f_ref[rows, :], i_ref[rows, :], g_ref[rows, :], c0, c1, nw, seg,
                            mask_ref, state_ref, bounded)
            hg_ref[rows, :] = o.astype(hg_ref.dtype)
        for r0 in range(0, tile, ATT_BLK):
            rows = slice(r0, r0 + ATT_BLK)
            if r0 == 0:
                prev, table = kvp_ref[...], jnp.where(first, 1, 0)
            else:
                prev, table = kv_ref[r0 - ATT_BLK:r0, :], 0
            _attn_block(aq_ref.at[rows, :], kv_ref[rows, :], prev, bias_ref, table, sink_ref, at_ref.at[rows, :])

    pl.when(bounded_ref[0] == 1)(lambda: tile_body(True))
    pl.when(bounded_ref[0] != 1)(lambda: tile_body(False))


def _mixers(proj_f, proj_b, lb, norm_w, sinks, B, S):
    T = B * S
    tile = MIX_TILE if S % MIX_TILE == 0 else ATT_BLK
    nb = S // tile
    per = tile // ATT_BLK
    seg, masks = _hgrn_tables()
    const = lambda shape: pl.BlockSpec(shape, lambda b, n: (0,) * len(shape))
    hcol = lambda cb: pl.BlockSpec((tile, HG_W), lambda b, n: (b * nb + n, cb))
    kvcol = (HG_W + ATT_QW) // (2 * ATT_KVW)
    out = pl.BlockSpec((tile, HG_W), lambda b, n: (b * nb + n, 0))
    return pl.pallas_call(
        _mixers_kernel,
        grid=(B, nb),
        in_specs=[pl.BlockSpec(memory_space=pltpu.SMEM), pl.BlockSpec(memory_space=pltpu.SMEM),
                  hcol(0), hcol(1), hcol(0), hcol(2),
                  pl.BlockSpec((tile, ATT_QW), lambda b, n: (b * nb + n, HG_W // ATT_QW)),
                  pl.BlockSpec((tile, 2 * ATT_KVW), lambda b, n: (b * nb + n, kvcol)),
                  pl.BlockSpec((ATT_BLK, 2 * ATT_KVW),
                               lambda b, n: ((b * nb + n) * per - jnp.where(n > 0, 1, 0), kvcol)),
                  const((1, HG_W)), const((1, HG_DV)), const((N_ARG_GROUPS * CHUNK, 2 * CHUNK)),
                  const((N_LEVELS, CHUNK, CHUNK)), const((2, ATT_Q_HEADS, ATT_BLK, 2 * ATT_BLK))],
        out_specs=[out, out],
        out_shape=[jax.ShapeDtypeStruct((T, HG_W), BF16), jax.ShapeDtypeStruct((T, ATT_QW), BF16)],
        scratch_shapes=[pltpu.VMEM((HG_HEADS, HG_DV, HG_DK), F32)],
        compiler_params=_cparams(("arbitrary", "arbitrary")),
        name="mixers",
    )(sinks, (jnp.min(lb) >= HG_SAFE_LB).astype(jnp.int32).reshape(1),
      proj_f, proj_f, proj_b, proj_f, proj_b, proj_b, proj_b, lb, norm_w, jnp.asarray(seg, BF16),
      jnp.asarray(masks), jnp.asarray(_attn_bias()))


def _merge_kernel(hg_ref, at_ref, x_ref, wgate_ref, wa_ref, wb_ref, wo_ref, g1_ref, b1_ref,
                  wr_ref, br_ref, tri_ref, x1_ref, xp_ref, eid_ref, wt_ref, rank_ref, cnt_ref, base_ref, x1b_ref):
    tm = x_ref.shape[0]
    sub = min(tm, MERGE_SUB)
    for r0 in range(0, tm, sub):
        rows = slice(r0, r0 + sub)
        xs = x_ref[rows, :]
        gates = _dot(xs.astype(BF16), wgate_ref[...])
        ya = _dot(hg_ref[rows, :], wa_ref[...])
        yb = _dot(at_ref[rows, :], wb_ref[...])
        merged = _sigmoid(gates[:, :D_MODEL]) * ya + _sigmoid(gates[:, D_MODEL:]) * yb
        z = DN_ALPHA * xs + _dot(merged.astype(BF16), wo_ref[...])
        x1 = _layer_norm(z, g1_ref[...], b1_ref[...])
        x1_ref[rows, :] = x1
        x1b = x1.astype(BF16)
        x1b_ref[rows, :] = x1b
        xp_ref[rows, :] = _pack_bf16_pairs(x1b)

    lg = _dot_nt(wr_ref[...], x1b_ref[...]) + br_ref[...]
    g = lg[0:8, :]
    row8 = lax.broadcasted_iota(jnp.int32, (8, tm), 0)
    g = jnp.where(row8 < N_GROUPS, g, -jnp.inf)
    gmax = jnp.max(g, axis=0, keepdims=True)
    gsel = jnp.min(jnp.where(g == gmax, row8, 8), axis=0, keepdims=True)
    gw = 1.0 / jnp.sum(jnp.exp(g - gmax), axis=0, keepdims=True)
    el = jnp.where(gsel == 0, lg[8:16, :],
                   jnp.where(gsel == 1, lg[16:24, :], jnp.where(gsel == 2, lg[24:32, :], lg[32:40, :])))
    v1 = jnp.max(el, axis=0, keepdims=True)
    i1 = jnp.min(jnp.where(el == v1, row8, 8), axis=0, keepdims=True)
    el2 = jnp.where(row8 == i1, -jnp.inf, el)
    v2 = jnp.max(el2, axis=0, keepdims=True)
    i2 = jnp.min(jnp.where(el2 == v2, row8, 8), axis=0, keepdims=True)
    e2 = jnp.exp(v2 - v1)
    den = 1.0 + e2
    e_a = gsel * EPG + i1
    e_b = gsel * EPG + i2
    eid_ref[...] = jnp.concatenate([e_a, e_b], axis=0)
    wt_ref[...] = jnp.concatenate([gw / den, gw * e2 / den], axis=0)

    @pl.when(pl.program_id(0) == 0)
    def _():
        base_ref[...] = jnp.zeros_like(base_ref)

    row_e = lax.broadcasted_iota(jnp.int32, (N_EXPERTS, tm), 0)
    oh_a = jnp.where(row_e == e_a, 1.0, 0.0)
    oh_b = jnp.where(row_e == e_b, 1.0, 0.0)
    tri = tri_ref[...]
    pre_a = _dot(oh_a.astype(BF16), tri)
    pre_b = _dot(oh_b.astype(BF16), tri)
    cnt_a = jnp.sum(oh_a, axis=1, keepdims=True)
    cnt_b = jnp.sum(oh_b, axis=1, keepdims=True)
    base = base_ref[...]
    rank_a = jnp.sum(oh_a * (base + pre_a), axis=0, keepdims=True)
    rank_b = jnp.sum(oh_b * (base + cnt_a + pre_b), axis=0, keepdims=True)
    rank_ref[...] = jnp.concatenate([rank_a, rank_b], axis=0).astype(jnp.int32)
    base = base + cnt_a + cnt_b
    base_ref[...] = base
    cnt_ref[...] = jnp.broadcast_to(base, cnt_ref.shape)


def _merge(hg, att, x2, wgate, wa, wb, wo, g1, b1, wr, br, tm, row0):
    T = hg.shape[0]
    tile0 = row0 // tm
    row = lambda w: pl.BlockSpec((tm, w), lambda i: (i, 0))
    const = lambda shape: pl.BlockSpec(shape, lambda i: (0,) * len(shape), pipeline_mode=pl.Buffered(1))
    lanes = pl.BlockSpec((2, tm), lambda i: (0, i))
    tri = jnp.asarray(np.triu(np.ones((tm, tm), np.float32), 1), BF16)
    return pl.pallas_call(
        _merge_kernel,
        grid=(T // tm,),
        in_specs=[row(HG_W), row(ATT_QW),
                  pl.BlockSpec((tm, D_MODEL), lambda i: (i + tile0, 0)),
                  const((D_MODEL, 2 * D_MODEL)), const((HG_W, D_MODEL)), const((ATT_QW, D_MODEL)), const((D_MODEL, D_MODEL)),
                  const((1, D_MODEL)), const((1, D_MODEL)), const((40, D_MODEL)), const((40, 1)),
                  const((tm, tm))],
        out_specs=[row(D_MODEL), row(D_MODEL // 2), lanes, lanes, lanes,
                   pl.BlockSpec((N_EXPERTS, 128), lambda i: (0, 0))],
        out_shape=[jax.ShapeDtypeStruct((T, D_MODEL), F32),
                   jax.ShapeDtypeStruct((T, D_MODEL // 2), jnp.uint32),
                   jax.ShapeDtypeStruct((2, T), jnp.int32),
                   jax.ShapeDtypeStruct((2, T), F32),
                   jax.ShapeDtypeStruct((2, T), jnp.int32),
                   jax.ShapeDtypeStruct((N_EXPERTS, 128), F32)],
        scratch_shapes=[pltpu.VMEM((N_EXPERTS, 1), F32), pltpu.VMEM((tm, D_MODEL), BF16)],
        compiler_params=_cparams(("arbitrary",)),
        name="merge",
    )(hg, att, x2, wgate, wa, wb, wo, g1, b1, wr, br, tri)


SC_WINDOW = 64
SC_IDX_LANES = 128


def _pad_indices(idx):
    rows = idx.reshape(-1, SC_WINDOW)
    return jnp.pad(rows, ((0, 0), (0, SC_IDX_LANES - SC_WINDOW)))


def _sc_mesh():
    return plsc.VectorSubcoreMesh(core_axis_name="core", subcore_axis_name="subcore")


def _sc_scatter2(x, idx_a, idx_b, n_out):
    T, d = x.shape

    @pl.kernel(out_type=jax.ShapeDtypeStruct((n_out, d), x.dtype), mesh=_sc_mesh())
    def scatter(x_hbm, ia_hbm, ib_hbm, o_hbm):
        def body(x_vmem, ia_vmem, ib_vmem):
            pltpu.sync_copy(x_vmem, o_hbm.at[ia_vmem.at[0, pl.ds(0, SC_WINDOW)]])
            pltpu.sync_copy(x_vmem, o_hbm.at[ib_vmem.at[0, pl.ds(0, SC_WINDOW)]])

        idx_spec = pl.BlockSpec((1, SC_IDX_LANES), lambda i: (i, 0))
        pltpu.emit_pipeline(
            body, grid=(T // SC_WINDOW,),
            in_specs=[pl.BlockSpec((SC_WINDOW, d), lambda i: (i, 0)), idx_spec, idx_spec],
            out_specs=[],
            core_axis_name=("core", "subcore"),
            dimension_semantics=(pltpu.PARALLEL,),
        )(x_hbm, ia_hbm, ib_hbm)

    return scatter(x, _pad_indices(idx_a), _pad_indices(idx_b))


def _sc_gather(x, idx):
    n = idx.shape[0]
    d = x.shape[1]

    @pl.kernel(out_type=jax.ShapeDtypeStruct((n, d), x.dtype), mesh=_sc_mesh())
    def gather(x_hbm, i_hbm, o_hbm):
        def body(i_vmem, o_vmem):
            pltpu.sync_copy(x_hbm.at[i_vmem.at[0, pl.ds(0, SC_WINDOW)]], o_vmem)

        pltpu.emit_pipeline(
            body, grid=(n // SC_WINDOW,),
            in_specs=[pl.BlockSpec((1, SC_IDX_LANES), lambda i: (i, 0))],
            out_specs=[pl.BlockSpec((SC_WINDOW, d), lambda i: (i, 0))],
            core_axis_name=("core", "subcore"),
            dimension_semantics=(pltpu.PARALLEL,),
        )(i_hbm, o_hbm)

    return gather(x, _pad_indices(idx))


def _expert_kernel(be_ref, nv_ref, nxt_ref, par_ref, xb_ref, wg_hbm, wu_hbm, wd_hbm, y_ref,
                   wg_st, wu_st, wd_st, wg_bf, wu_bf, wd_bf, sem):
    i = pl.program_id(0)
    nv = nv_ref[i]

    def fetch(expert, slot):
        return [pltpu.make_async_copy(src.at[expert], dst.at[slot], sem.at[slot, j])
                for j, (src, dst) in enumerate(((wg_hbm, wg_st), (wu_hbm, wu_st), (wd_hbm, wd_st)))]

    @pl.when(i == 0)
    def _():
        for cp in fetch(be_ref[0], par_ref[0]):
            cp.start()

    @pl.when((i == 0) | (be_ref[i] != be_ref[jnp.maximum(i - 1, 0)]))
    def _():
        slot = par_ref[i]
        for cp in fetch(be_ref[i], slot):
            cp.wait()
        wg_bf[...] = wg_st[slot].astype(BF16)
        wu_bf[...] = wu_st[slot].astype(BF16)
        wd_bf[...] = wd_st[slot].astype(BF16)

        @pl.when(nxt_ref[i] >= 0)
        def _():
            for cp in fetch(nxt_ref[i], 1 - slot):
                cp.start()

    @pl.when(nv > 0)
    def _():
        row = lax.broadcasted_iota(jnp.int32, (MOE_BLOCK, 1), 0)
        xb = _unpack_bf16_pairs(jnp.where(row < nv, xb_ref[...], jnp.uint32(0))).astype(BF16)
        g = _dot(xb, wg_bf[...])
        u = _dot(xb, wu_bf[...])
        h = (_silu(g) * u).astype(BF16)
        y_ref[...] = _pack_bf16_pairs(_dot(h, wd_bf[...]).astype(BF16))


def _experts(block_meta, xbuf, w_gate, w_up, w_down):
    P = xbuf.shape[0]
    nb = P // MOE_BLOCK
    rows = pl.BlockSpec((MOE_BLOCK, D_MODEL // 2), lambda i, *_: (i, 0))
    hbm = pl.BlockSpec(memory_space=pl.ANY)
    return pl.pallas_call(
        _expert_kernel,
        grid_spec=pltpu.PrefetchScalarGridSpec(
            num_scalar_prefetch=4,
            grid=(nb,),
            in_specs=[rows, hbm, hbm, hbm],
            out_specs=rows,
            scratch_shapes=[pltpu.VMEM((2, D_MODEL, D_EXPERT), F32), pltpu.VMEM((2, D_MODEL, D_EXPERT), F32),
                            pltpu.VMEM((2, D_EXPERT, D_MODEL), F32),
                            pltpu.VMEM((D_MODEL, D_EXPERT), BF16), pltpu.VMEM((D_MODEL, D_EXPERT), BF16),
                            pltpu.VMEM((D_EXPERT, D_MODEL), BF16),
                            pltpu.SemaphoreType.DMA((2, 3))]),
        out_shape=jax.ShapeDtypeStruct((P, D_MODEL // 2), jnp.uint32),
        compiler_params=_cparams(("arbitrary",)),
        name="experts",
    )(*block_meta, xbuf, w_gate, w_up, w_down)


COMBINE_SLOTS = 3


def _combine_kernel(rows_hbm, x1_hbm, wt_ref, g2_ref, b2_ref, *rest, nt):
    o_ref, ra_buf, rb_buf, x1_buf, sem = rest[-5:]
    i = pl.program_id(0)
    tm = o_ref.shape[0]

    def fetch(step):
        slot = step % COMBINE_SLOTS
        aligned = lambda r: r if isinstance(r, int) else pl.multiple_of(r, tm)
        first = aligned(step * tm)
        second = aligned((step + nt) * tm)
        return [pltpu.make_async_copy(rows_hbm.at[pl.ds(first, tm), :], ra_buf.at[slot], sem.at[slot, 0]),
                pltpu.make_async_copy(rows_hbm.at[pl.ds(second, tm), :], rb_buf.at[slot], sem.at[slot, 1]),
                pltpu.make_async_copy(x1_hbm.at[pl.ds(first, tm), :], x1_buf.at[slot], sem.at[slot, 2])]

    @pl.when(i == 0)
    def _():
        for step in range(min(COMBINE_SLOTS - 1, nt)):
            for cp in fetch(step):
                cp.start()

    @pl.when(i + COMBINE_SLOTS - 1 < nt)
    def _():
        for cp in fetch(i + COMBINE_SLOTS - 1):
            cp.start()

    for cp in fetch(i):
        cp.wait()
    slot = i % COMBINE_SLOTS
    w2 = wt_ref[...]
    w = jnp.concatenate([w2, jnp.zeros((6, w2.shape[1]), F32)], axis=0).T
    y = w[:, 0:1] * _unpack_bf16_pairs(ra_buf[slot]) + w[:, 1:2] * _unpack_bf16_pairs(rb_buf[slot])
    o_ref[...] = _layer_norm(DN_ALPHA * x1_buf[slot] + y, g2_ref[...], b2_ref[...])


def _combine(rows2, x1, wt, g2, b2, tm, out_prev, tile0, t_total):
    T = x1.shape[0]
    nt = T // tm
    const = lambda shape: pl.BlockSpec(shape, lambda i: (0,) * len(shape))
    hbm = pl.BlockSpec(memory_space=pl.ANY)
    in_specs = [hbm, hbm, pl.BlockSpec((2, tm), lambda i: (0, i)), const((1, D_MODEL)), const((1, D_MODEL))]
    args = [rows2, x1, wt, g2, b2]
    aliases = {}
    if out_prev is not None:
        in_specs.append(hbm)
        args.append(out_prev)
        aliases = {len(args) - 1: 0}
    return pl.pallas_call(
        functools.partial(_combine_kernel, nt=nt),
        grid=(nt,),
        in_specs=in_specs,
        out_specs=pl.BlockSpec((tm, D_MODEL), lambda i: (i + tile0, 0)),
        out_shape=jax.ShapeDtypeStruct((t_total, D_MODEL), F32),
        scratch_shapes=[pltpu.VMEM((COMBINE_SLOTS, tm, D_MODEL // 2), jnp.uint32),
                        pltpu.VMEM((COMBINE_SLOTS, tm, D_MODEL // 2), jnp.uint32),
                        pltpu.VMEM((COMBINE_SLOTS, tm, D_MODEL), F32),
                        pltpu.SemaphoreType.DMA((COMBINE_SLOTS, 3))],
        input_output_aliases=aliases,
        compiler_params=_cparams(("arbitrary",)),
        name="combine",
    )(*args)


def _slot_layout(eid, rank, counts, T):
    P = 2 * T + N_EXPERTS * MOE_BLOCK
    nb = P // MOE_BLOCK
    padded = ((counts + MOE_BLOCK - 1) // MOE_BLOCK) * MOE_BLOCK
    pend = jnp.cumsum(padded)
    pstart = pend - padded
    experts = jnp.arange(N_EXPERTS, dtype=jnp.int32)
    pos = rank + jnp.sum(jnp.where(eid[:, :, None] == experts, pstart, 0), axis=-1)

    block_start = jnp.arange(nb, dtype=jnp.int32) * MOE_BLOCK
    block_expert = jnp.minimum(jnp.sum(block_start[:, None] >= pend[None, :], axis=1), N_EXPERTS - 1)
    onehot = block_expert[:, None] == experts[None, :]
    look = lambda table: jnp.sum(jnp.where(onehot, table[None, :], 0), axis=1)
    block_valid = jnp.clip(look(pstart + counts) - block_start, 0, MOE_BLOCK)
    block_valid = jnp.where(block_start < pend[-1], block_valid, 0)
    present = (counts > 0) | (experts == N_EXPERTS - 1)
    later = present[None, :] & (experts[None, :] > experts[:, None])
    next_present = jnp.min(jnp.where(later, experts[None, :], N_EXPERTS), axis=1)
    next_present = jnp.where(next_present < N_EXPERTS, next_present, -1)
    runs_before = jnp.sum(present[None, :] & (experts[None, :] < experts[:, None]), axis=1)
    block_next = look(next_present)
    block_slot = look(runs_before % 2)
    i32 = lambda a: a.astype(jnp.int32)
    return i32(pos), (i32(block_expert), i32(block_valid), i32(block_next), i32(block_slot))


def kernel(x, lb_logits, w_in, hg_norm_w, sinks, w_branch_a, w_branch_b, w_out, ln1_g, ln1_b,
           router_group_w, router_group_b, router_expert_w, router_expert_b,
           w_exp_gate, w_exp_up, w_exp_down, ln2_g, ln2_b):
    B, S, D = x.shape
    assert D == D_MODEL and S % ATT_BLK == 0 and w_in.shape[0] == DEPTH == 1
    lb_all = jnp.cumsum(jax.nn.softmax(lb_logits.astype(F32), axis=0), axis=0)
    lb = lb_all[0].reshape(1, HG_W)
    w0 = w_in[0]
    w_f = jnp.concatenate([w0[:, :2 * HG_W], w0[:, 3 * HG_W:4 * HG_W]], axis=1).astype(BF16)
    w_b = jnp.concatenate([w0[:, 2 * HG_W:3 * HG_W], w0[:, 4 * HG_W:PROJ_W]], axis=1).astype(BF16)
    w_gate = w0[:, PROJ_W:].astype(BF16)
    nw = hg_norm_w[0].reshape(1, HG_DV).astype(F32)
    wa, wb, wo = w_branch_a[0].astype(BF16), w_branch_b[0].astype(BF16), w_out[0].astype(BF16)
    wr = jnp.zeros((40, D), F32).at[0:N_GROUPS].set(router_group_w[0].T).at[8:40].set(router_expert_w[0].T)
    br = jnp.zeros((40, 1), F32).at[0:N_GROUPS, 0].set(router_group_b[0]).at[8:40, 0].set(router_expert_b[0])
    g1, b1 = ln1_g[0].reshape(1, D), ln1_b[0].reshape(1, D)
    g2, b2 = ln2_g[0].reshape(1, D), ln2_b[0].reshape(1, D)

    n_parts = N_PARTS if B % N_PARTS == 0 else 1
    bp = B // n_parts
    tp = bp * S
    pick = lambda want: next(t for t in (want, 512, 256, ATT_BLK) if t <= want and tp % t == 0)
    tm_proj, tm_merge, tm_comb = pick(1024), pick(1024), pick(1024)
    x2 = x.reshape(B * S, D)
    out = None
    for part in range(n_parts):
        proj_f, proj_b = _proj(x2, w_f, w_b, tm_proj, part * tp, tp)
        hg, att = _mixers(proj_f, proj_b, lb, nw, sinks[0].astype(F32), bp, S)
        x1, xp, eid, wt, rank, cnt = _merge(hg, att, x2, w_gate, wa, wb, wo, g1, b1, wr.astype(BF16), br,
                                            tm_merge, part * tp)
        pos, block_meta = _slot_layout(eid, rank, cnt[:, 0].astype(jnp.int32), tp)
        xbuf = _sc_scatter2(xp, pos[0], pos[1], 2 * tp + N_EXPERTS * MOE_BLOCK)
        ybuf = _experts(block_meta, xbuf, w_exp_gate[0], w_exp_up[0], w_exp_down[0])
        rows2 = _sc_gather(ybuf, pos.reshape(-1))
        out = _combine(rows2, x1, wt, g2, b2, tm_comb, out, part * (tp // tm_comb), B * S)
    return out.reshape(B, S, D)
```

```python
import numpy as np
import jax
import jax.numpy as jnp
from jax import lax
from jax.experimental import pallas as pl
from jax.experimental.pallas import tpu as pltpu
from jax.experimental.pallas import tpu_sc as plsc

F32 = jnp.float32
BF16 = jnp.bfloat16

D_MODEL = 1024
DEPTH = 1
HG_HEADS = 4
HG_DK = 128
HG_DV = 128
HG_W = HG_HEADS * HG_DK
CHUNK = 64
ATT_Q_HEADS = 8
ATT_KV_HEADS = 2
ATT_GROUP = ATT_Q_HEADS // ATT_KV_HEADS
ATT_HD = 64
ATT_QW = ATT_Q_HEADS * ATT_HD
ATT_KVW = ATT_KV_HEADS * ATT_HD
ATT_BLK = 128
N_GROUPS = 4
EPG = 8
N_EXPERTS = N_GROUPS * EPG
D_EXPERT = 512
MOE_BLOCK = 512
DN_ALPHA = (2.0 * DEPTH) ** 0.25
LN_EPS = 1e-5
RMS_EPS = 1e-6
NEG_INF = -1e30
LOG2_E = 1.4426950408889634

PROJ_W = 4 * HG_W + ATT_QW + 2 * ATT_KVW
PROJ_F = 3 * HG_W
PROJ_B = HG_W + ATT_QW + 2 * ATT_KVW
N_LEVELS = 6
N_ARG_GROUPS = N_LEVELS + 2
HG_SAFE_LB = 0.125

VMEM_LIMIT = 56 * 1024 * 1024
MERGE_SUB = 256
MIX_TILE = 512
N_PARTS = 2


def _cparams(sem):
    return pltpu.CompilerParams(dimension_semantics=sem, vmem_limit_bytes=VMEM_LIMIT)


def _dot(a, b):
    return jnp.dot(a, b, preferred_element_type=F32)


def _dot_nt(a, b):
    return lax.dot_general(a, b, (((1,), (1,)), ((), ())), preferred_element_type=F32)


def _dot_tn(a, b):
    return lax.dot_general(a, b, (((0,), (0,)), ((), ())), preferred_element_type=F32)


def _sigmoid(x):
    return 0.5 * jnp.tanh(0.5 * x) + 0.5


def _silu(x):
    return x * _sigmoid(x)


def _pack_bf16_pairs(xb):
    n = xb.shape[1] // 2
    lo = lax.bitcast_convert_type(xb[:, :n].astype(F32), jnp.uint32)
    hi = lax.bitcast_convert_type(xb[:, n:].astype(F32), jnp.uint32)
    return (lo >> 16) | hi


def _unpack_bf16_pairs(w):
    lo = lax.bitcast_convert_type(w << 16, F32)
    hi = lax.bitcast_convert_type(w & jnp.uint32(0xFFFF0000), F32)
    return jnp.concatenate([lo, hi], axis=1)


def _layer_norm(z, g, b):
    mu = jnp.mean(z, axis=-1, keepdims=True)
    zc = z - mu
    var = jnp.mean(zc * zc, axis=-1, keepdims=True)
    return zc * lax.rsqrt(var + LN_EPS) * g + b


def _proj_kernel(x_ref, wf_ref, wb_ref, of_ref, ob_ref):
    xb = x_ref[...].astype(BF16)
    of_ref[...] = _dot(xb, wf_ref[...])
    ob_ref[...] = _dot(xb, wb_ref[...]).astype(BF16)


def _proj(x2, w_f, w_b, tm, row0, T):
    tile0 = row0 // tm
    return pl.pallas_call(
        _proj_kernel,
        grid=(T // tm,),
        in_specs=[pl.BlockSpec((tm, D_MODEL), lambda i: (i + tile0, 0)),
                  pl.BlockSpec((D_MODEL, PROJ_F), lambda i: (0, 0), pipeline_mode=pl.Buffered(1)),
                  pl.BlockSpec((D_MODEL, PROJ_B), lambda i: (0, 0), pipeline_mode=pl.Buffered(1))],
        out_specs=[pl.BlockSpec((tm, PROJ_F), lambda i: (i, 0)), pl.BlockSpec((tm, PROJ_B), lambda i: (i, 0))],
        out_shape=[jax.ShapeDtypeStruct((T, PROJ_F), F32), jax.ShapeDtypeStruct((T, PROJ_B), BF16)],
        compiler_params=_cparams(("arbitrary",)),
        name="proj",
    )(x2, w_f, w_b)


def _hgrn_tables():
    C = CHUNK
    w = np.zeros((N_ARG_GROUPS, C, C), np.float32)
    masks = np.zeros((N_LEVELS, C, C), np.float32)
    for lvl in range(N_LEVELS):
        h = 1 << lvl
        for t in range(C):
            base = (t // (2 * h)) * 2 * h
            m = base + h - 1
            if (t // h) % 2 == 1:
                w[lvl, t, m + 1:t + 1] = 1.0
                masks[lvl, t, base:base + h] = 1.0
            else:
                w[lvl, t, t + 1:m + 1] = 1.0
    for t in range(C):
        w[N_LEVELS, t, :t + 1] = 1.0
        w[N_LEVELS + 1, t, t + 1:] = 1.0
    w = w.reshape(N_ARG_GROUPS * C, C)
    return np.concatenate([w, w], axis=1), masks


def _hgrn_chunk(q, fz, v, gate, c0, c1, nw, seg, mask_ref, state_ref, bounded):
    h = 0.5 * q
    qf = h + h * jnp.tanh(h)
    t1 = c1 * jnp.tanh(0.5 * fz)
    f = c0 + t1
    k = c1 - t1
    l2 = jnp.log2(f)
    v_bf = v.astype(BF16)
    qf_bf = qf.astype(BF16)
    k_bf = k.astype(BF16)

    l_hi = l2.astype(BF16)
    l_lo = (l2 - l_hi.astype(F32)).astype(BF16)
    l_split = jnp.concatenate([l_hi, l_lo], axis=0)
    row = lax.broadcasted_iota(jnp.int32, (CHUNK, 1), 0)
    if bounded:
        cum = _dot(seg[N_LEVELS * CHUNK:(N_LEVELS + 1) * CHUNK, :], l_split)
        mid = cum[CHUNK // 2 - 1:CHUNK // 2, :]
        last = cum[CHUNK - 1:CHUNK, :]
        e_cum = jnp.exp2(cum)
        e_suf = jnp.exp2(last - cum)
        a_q = qf_bf * jnp.exp2(cum - mid).astype(BF16)
        a_k = k_bf * jnp.exp2(mid - cum).astype(BF16)
        causal = row >= lax.broadcasted_iota(jnp.int32, (1, CHUNK), 1)
        scores = [jnp.where(causal, _dot_nt(a_q[:, hd * HG_DK:(hd + 1) * HG_DK], a_k[:, hd * HG_DK:(hd + 1) * HG_DK]),
                            0.0) for hd in range(HG_HEADS)]
        diag_v = None
    else:
        e = jnp.exp2(_dot(seg, l_split))
        scores = [jnp.zeros((CHUNK, CHUNK), F32) for _ in range(HG_HEADS)]
        for lvl in range(N_LEVELS):
            half = 1 << lvl
            e_l = e[lvl * CHUNK:(lvl + 1) * CHUNK, :].astype(BF16)
            if half >= 16:
                sel = jnp.concatenate([(qf_bf if (r0 // half) % 2 else k_bf)[r0:r0 + half]
                                       for r0 in range(0, CHUNK, half)], axis=0)
            else:
                sel = jnp.where((row // half) % 2 == 1, qf_bf, k_bf)
            a = sel * e_l
            m = mask_ref[lvl]
            for hd in range(HG_HEADS):
                a_h = a[:, hd * HG_DK:(hd + 1) * HG_DK]
                scores[hd] = scores[hd] + m * _dot_nt(a_h, a_h)
        e_cum = e[N_LEVELS * CHUNK:(N_LEVELS + 1) * CHUNK, :]
        e_suf = e[(N_LEVELS + 1) * CHUNK:(N_LEVELS + 2) * CHUNK, :]
        diag_v = qf * k

    q_in = qf_bf * e_cum.astype(BF16)
    k_out = k_bf * e_suf.astype(BF16)
    e_last = e_cum[CHUNK - 1:CHUNK, :]

    outs = []
    for hd in range(HG_HEADS):
        cols = slice(hd * HG_DK, (hd + 1) * HG_DK)
        st = state_ref[hd]
        o = _dot_nt(q_in[:, cols], st.astype(BF16)) + _dot(scores[hd].astype(BF16), v_bf[:, cols])
        if diag_v is not None:
            o = o + jnp.sum(diag_v[:, cols], axis=-1, keepdims=True) * v[:, cols]
        state_ref[hd] = st * e_last[:, cols] + _dot_tn(v_bf[:, cols], k_out[:, cols])
        o = o * lax.rsqrt(jnp.mean(o * o, axis=-1, keepdims=True) + RMS_EPS) * nw
        outs.append(o)
    hg = 0.5 * gate
    return jnp.concatenate(outs, axis=1) * (hg + hg * jnp.tanh(hg))


def _attn_bias():
    r = np.arange(ATT_BLK)[:, None]
    c = np.arange(2 * ATT_BLK)[None, :]
    dist = r + ATT_BLK - c
    window = (dist >= 0) & (dist < ATT_BLK)
    slopes = np.exp2(-8.0 * (np.arange(ATT_Q_HEADS, dtype=np.float32) + 1.0) / ATT_Q_HEADS).astype(np.float32)
    alibi = -slopes[:, None, None] * dist.astype(np.float32)[None]
    later = np.where(window[None], alibi, np.float32(NEG_INF))
    first = np.where((window & (c >= ATT_BLK))[None], alibi, np.float32(NEG_INF))
    return (np.stack([later, first]) * LOG2_E).astype(np.float32)


def _attn_block(q_ref, kv_cur, kv_prev, bias_ref, table, sink_ref, o_ref):
    lane = lax.broadcasted_iota(jnp.int32, (2 * ATT_BLK, 2 * ATT_KVW), 1)
    lo = (lane % ATT_KVW) < ATT_HD

    kv = jnp.concatenate([kv_prev, kv_cur], axis=0).astype(BF16)
    kv_sw = jnp.concatenate([kv[:, ATT_HD:ATT_KVW], kv[:, :ATT_HD],
                             kv[:, ATT_KVW + ATT_HD:], kv[:, ATT_KVW:ATT_KVW + ATT_HD]], axis=1)
    zero = jnp.zeros_like(kv)
    placed = {}
    for h in range(ATT_KV_HEADS):
        for off in range(2):
            src = kv if h == off else kv_sw
            placed[h, off] = jnp.where(lo if off == 0 else jnp.logical_not(lo), src, zero)

    scale = ATT_HD ** -0.5 * LOG2_E
    for pair in range(ATT_Q_HEADS // 2):
        qp = (q_ref[:, pair * 2 * ATT_HD:(pair + 1) * 2 * ATT_HD] * scale).astype(BF16)
        acc = jnp.zeros((ATT_BLK, 2 * ATT_HD), F32)
        for off in range(2):
            j = 2 * pair + off
            kvh = placed[j // ATT_GROUP, off]
            sink = sink_ref[j] * LOG2_E
            logits = _dot_nt(qp, kvh[:, :ATT_KVW]) + bias_ref[table, j]
            m = jnp.maximum(jnp.max(logits, axis=-1, keepdims=True), sink)
            p = jnp.exp2(logits - m)
            den = jnp.sum(p, axis=-1, keepdims=True) + jnp.exp2(sink - m)
            acc = acc + _dot(p.astype(BF16), kvh[:, ATT_KVW:]) / den
        o_ref[:, pair * 2 * ATT_HD:(pair + 1) * 2 * ATT_HD] = acc.astype(o_ref.dtype)


def _mixers_kernel(sink_ref, bounded_ref, q_ref, f_ref, i_ref, g_ref, aq_ref, kv_ref, kvp_ref, lb_ref, nw_ref,
                   seg_ref, mask_ref, bias_ref, hg_ref, at_ref, state_ref):
    first = pl.program_id(1) == 0

    @pl.when(first)
    def _():
        state_ref[...] = jnp.zeros_like(state_ref)

    def tile_body(bounded):
        lb = lb_ref[...]
        c0 = 0.5 + 0.5 * lb
        c1 = 0.5 - 0.5 * lb
        nw = nw_ref[...]
        seg = seg_ref[...]
        tile = q_ref.shape[0]
        for r0 in range(0, tile, CHUNK):
            rows = slice(r0, r0 + CHUNK)
            o = _hgrn_chunk(q_ref[rows, :], f_ref[rows, :], i_ref[rows, :], g_ref[rows, :], c0, c1, nw, seg,
                            mask_ref, state_ref, bounded)
            hg_ref[rows, :] = o.astype(hg_ref.dtype)
        for r0 in range(0, tile, ATT_BLK):
            rows = slice(r0, r0 + ATT_BLK)
            if r0 == 0:
                prev, table = kvp_ref[...], jnp.where(first, 1, 0)
            else:
                prev, table = kv_ref[r0 - ATT_BLK:r0, :], 0
            _attn_block(aq_ref.at[rows, :], kv_ref[rows, :], prev, bias_ref, table, sink_ref, at_ref.at[rows, :])

    pl.when(bounded_ref[0] == 1)(lambda: tile_body(True))
    pl.when(bounded_ref[0] != 1)(lambda: tile_body(False))


def _mixers(proj_f, proj_b, lb, norm_w, sinks, B, S):
    T = B * S
    tile = MIX_TILE if S % MIX_TILE == 0 else ATT_BLK
    nb = S // tile
    per = tile // ATT_BLK
    seg, masks = _hgrn_tables()
    const = lambda shape: pl.BlockSpec(shape, lambda b, n: (0,) * len(shape))
    hcol = lambda cb: pl.BlockSpec((tile, HG_W), lambda b, n: (b * nb + n, cb))
    kvcol = (HG_W + ATT_QW) // (2 * ATT_KVW)
    out = pl.BlockSpec((tile, HG_W), lambda b, n: (b * nb + n, 0))
    return pl.pallas_call(
        _mixers_kernel,
        grid=(B, nb),
        in_specs=[pl.BlockSpec(memory_space=pltpu.SMEM), pl.BlockSpec(memory_space=pltpu.SMEM),
                  hcol(0), hcol(1), hcol(0), hcol(2),
                  pl.BlockSpec((tile, ATT_QW), lambda b, n: (b * nb + n, HG_W // ATT_QW)),
                  pl.BlockSpec((tile, 2 * ATT_KVW), lambda b, n: (b * nb + n, kvcol)),
                  pl.BlockSpec((ATT_BLK, 2 * ATT_KVW),
                               lambda b, n: ((b * nb + n) * per - jnp.where(n > 0, 1, 0), kvcol)),
                  const((1, HG_W)), const((1, HG_DV)), const((N_ARG_GROUPS * CHUNK, 2 * CHUNK)),
                  const((N_LEVELS, CHUNK, CHUNK)), const((2, ATT_Q_HEADS, ATT_BLK, 2 * ATT_BLK))],
        out_specs=[out, out],
        out_shape=[jax.ShapeDtypeStruct((T, HG_W), BF16), jax.ShapeDtypeStruct((T, ATT_QW), BF16)],
        scratch_shapes=[pltpu.VMEM((HG_HEADS, HG_DV, HG_DK), F32)],
        compiler_params=_cparams(("arbitrary", "arbitrary")),
        name="mixers",
    )(sinks, (jnp.min(lb) >= HG_SAFE_LB).astype(jnp.int32).reshape(1),
      proj_f, proj_f, proj_b, proj_f, proj_b, proj_b, proj_b, lb, norm_w, jnp.asarray(seg, BF16),
      jnp.asarray(masks), jnp.asarray(_attn_bias()))


def _merge_kernel(hg_ref, at_ref, x_ref, wgate_ref, wa_ref, wb_ref, wo_ref, g1_ref, b1_ref,
                  wr_ref, br_ref, tri_ref, x1_ref, xp_ref, eid_ref, wt_ref, rank_ref, cnt_ref, base_ref, x1b_ref):
    tm = x_ref.shape[0]
    sub = min(tm, MERGE_SUB)
    for r0 in range(0, tm, sub):
        rows = slice(r0, r0 + sub)
        xs = x_ref[rows, :]
        gates = _dot(xs.astype(BF16), wgate_ref[...])
        ya = _dot(hg_ref[rows, :], wa_ref[...])
        yb = _dot(at_ref[rows, :], wb_ref[...])
        merged = _sigmoid(gates[:, :D_MODEL]) * ya + _sigmoid(gates[:, D_MODEL:]) * yb
        z = DN_ALPHA * xs + _dot(merged.astype(BF16), wo_ref[...])
        x1 = _layer_norm(z, g1_ref[...], b1_ref[...])
        x1_ref[rows, :] = x1
        x1b = x1.astype(BF16)
        x1b_ref[rows, :] = x1b
        xp_ref[rows, :] = _pack_bf16_pairs(x1b)

    lg = _dot_nt(wr_ref[...], x1b_ref[...]) + br_ref[...]
    g = lg[0:8, :]
    row8 = lax.broadcasted_iota(jnp.int32, (8, tm), 0)
    g = jnp.where(row8 < N_GROUPS, g, -jnp.inf)
    gmax = jnp.max(g, axis=0, keepdims=True)
    gsel = jnp.min(jnp.where(g == gmax, row8, 8), axis=0, keepdims=True)
    gw = 1.0 / jnp.sum(jnp.exp(g - gmax), axis=0, keepdims=True)
    el = jnp.where(gsel == 0, lg[8:16, :],
                   jnp.where(gsel == 1, lg[16:24, :], jnp.where(gsel == 2, lg[24:32, :], lg[32:40, :])))
    v1 = jnp.max(el, axis=0, keepdims=True)
    i1 = jnp.min(jnp.where(el == v1, row8, 8), axis=0, keepdims=True)
    el2 = jnp.where(row8 == i1, -jnp.inf, el)
    v2 = jnp.max(el2, axis=0, keepdims=True)
    i2 = jnp.min(jnp.where(el2 == v2, row8, 8), axis=0, keepdims=True)
    e2 = jnp.exp(v2 - v1)
    den = 1.0 + e2
    e_a = gsel * EPG + i1
    e_b = gsel * EPG + i2
    eid_ref[...] = jnp.concatenate([e_a, e_b], axis=0)
    wt_ref[...] = jnp.concatenate([gw / den, gw * e2 / den], axis=0)

    @pl.when(pl.program_id(0) == 0)
    def _():
        base_ref[...] = jnp.zeros_like(base_ref)

    row_e = lax.broadcasted_iota(jnp.int32, (N_EXPERTS, tm), 0)
    oh_a = jnp.where(row_e == e_a, 1.0, 0.0)
    oh_b = jnp.where(row_e == e_b, 1.0, 0.0)
    tri = tri_ref[...]
    pre_a = _dot(oh_a.astype(BF16), tri)
    pre_b = _dot(oh_b.astype(BF16), tri)
    cnt_a = jnp.sum(oh_a, axis=1, keepdims=True)
    cnt_b = jnp.sum(oh_b, axis=1, keepdims=True)
    base = base_ref[...]
    rank_a = jnp.sum(oh_a * (base + pre_a), axis=0, keepdims=True)
    rank_b = jnp.sum(oh_b * (base + cnt_a + pre_b), axis=0, keepdims=True)
    rank_ref[...] = jnp.concatenate([rank_a, rank_b], axis=0).astype(jnp.int32)
    base = base + cnt_a + cnt_b
    base_ref[...] = base
    cnt_ref[...] = jnp.broadcast_to(base, cnt_ref.shape)


def _merge(hg, att, x2, wgate, wa, wb, wo, g1, b1, wr, br, tm, row0):
    T = hg.shape[0]
    tile0 = row0 // tm
    row = lambda w: pl.BlockSpec((tm, w), lambda i: (i, 0))
    const = lambda shape: pl.BlockSpec(shape, lambda i: (0,) * len(shape), pipeline_mode=pl.Buffered(1))
    lanes = pl.BlockSpec((2, tm), lambda i: (0, i))
    tri = jnp.asarray(np.triu(np.ones((tm, tm), np.float32), 1), BF16)
    return pl.pallas_call(
        _merge_kernel,
        grid=(T // tm,),
        in_specs=[row(HG_W), row(ATT_QW),
                  pl.BlockSpec((tm, D_MODEL), lambda i: (i + tile0, 0)),
                  const((D_MODEL, 2 * D_MODEL)), const((HG_W, D_MODEL)), const((ATT_QW, D_MODEL)), const((D_MODEL, D_MODEL)),
                  const((1, D_MODEL)), const((1, D_MODEL)), const((40, D_MODEL)), const((40, 1)),
                  const((tm, tm))],
        out_specs=[row(D_MODEL), row(D_MODEL // 2), lanes, lanes, lanes,
                   pl.BlockSpec((N_EXPERTS, 128), lambda i: (0, 0))],
        out_shape=[jax.ShapeDtypeStruct((T, D_MODEL), F32),
                   jax.ShapeDtypeStruct((T, D_MODEL // 2), jnp.uint32),
                   jax.ShapeDtypeStruct((2, T), jnp.int32),
                   jax.ShapeDtypeStruct((2, T), F32),
                   jax.ShapeDtypeStruct((2, T), jnp.int32),
                   jax.ShapeDtypeStruct((N_EXPERTS, 128), F32)],
        scratch_shapes=[pltpu.VMEM((N_EXPERTS, 1), F32), pltpu.VMEM((tm, D_MODEL), BF16)],
        compiler_params=_cparams(("arbitrary",)),
        name="merge",
    )(hg, att, x2, wgate, wa, wb, wo, g1, b1, wr, br, tri)


SC_WINDOW = 64
SC_IDX_LANES = 128


def _pad_indices(idx):
    rows = idx.reshape(-1, SC_WINDOW)
    return jnp.pad(rows, ((0, 0), (0, SC_IDX_LANES - SC_WINDOW)))


def _sc_mesh():
    return plsc.VectorSubcoreMesh(core_axis_name="core", subcore_axis_name="subcore")


def _sc_scatter2(x, idx_a, idx_b, n_out):
    T, d = x.shape

    @pl.kernel(out_type=jax.ShapeDtypeStruct((n_out, d), x.dtype), mesh=_sc_mesh())
    def scatter(x_hbm, ia_hbm, ib_hbm, o_hbm):
        def body(x_vmem, ia_vmem, ib_vmem):
            pltpu.sync_copy(x_vmem, o_hbm.at[ia_vmem.at[0, pl.ds(0, SC_WINDOW)]])
            pltpu.sync_copy(x_vmem, o_hbm.at[ib_vmem.at[0, pl.ds(0, SC_WINDOW)]])

        idx_spec = pl.BlockSpec((1, SC_IDX_LANES), lambda i: (i, 0))
        pltpu.emit_pipeline(
            body, grid=(T // SC_WINDOW,),
            in_specs=[pl.BlockSpec((SC_WINDOW, d), lambda i: (i, 0)), idx_spec, idx_spec],
            out_specs=[],
            core_axis_name=("core", "subcore"),
            dimension_semantics=(pltpu.PARALLEL,),
        )(x_hbm, ia_hbm, ib_hbm)

    return scatter(x, _pad_indices(idx_a), _pad_indices(idx_b))


def _sc_gather(x, idx):
    n = idx.shape[0]
    d = x.shape[1]

    @pl.kernel(out_type=jax.ShapeDtypeStruct((n, d), x.dtype), mesh=_sc_mesh())
    def gather(x_hbm, i_hbm, o_hbm):
        def body(i_vmem, o_vmem):
            pltpu.sync_copy(x_hbm.at[i_vmem.at[0, pl.ds(0, SC_WINDOW)]], o_vmem)

        pltpu.emit_pipeline(
            body, grid=(n // SC_WINDOW,),
            in_specs=[pl.BlockSpec((1, SC_IDX_LANES), lambda i: (i, 0))],
            out_specs=[pl.BlockSpec((SC_WINDOW, d), lambda i: (i, 0))],
            core_axis_name=("core", "subcore"),
            dimension_semantics=(pltpu.PARALLEL,),
        )(i_hbm, o_hbm)

    return gather(x, _pad_indices(idx))


def _expert_kernel(be_ref, nv_ref, nxt_ref, par_ref, xb_ref, wg_hbm, wu_hbm, wd_hbm, y_ref,
                   wg_st, wu_st, wd_st, wg_bf, wu_bf, wd_bf, sem):
    i = pl.program_id(0)
    nv = nv_ref[i]

    def fetch(expert, slot):
        return [pltpu.make_async_copy(src.at[expert], dst.at[slot], sem.at[slot, j])
                for j, (src, dst) in enumerate(((wg_hbm, wg_st), (wu_hbm, wu_st), (wd_hbm, wd_st)))]

    @pl.when(i == 0)
    def _():
        for cp in fetch(be_ref[0], par_ref[0]):
            cp.start()

    @pl.when((i == 0) | (be_ref[i] != be_ref[jnp.maximum(i - 1, 0)]))
    def _():
        slot = par_ref[i]
        for cp in fetch(be_ref[i], slot):
            cp.wait()
        wg_bf[...] = wg_st[slot].astype(BF16)
        wu_bf[...] = wu_st[slot].astype(BF16)
        wd_bf[...] = wd_st[slot].astype(BF16)

        @pl.when(nxt_ref[i] >= 0)
        def _():
            for cp in fetch(nxt_ref[i], 1 - slot):
                cp.start()

    @pl.when(nv > 0)
    def _():
        row = lax.broadcasted_iota(jnp.int32, (MOE_BLOCK, 1), 0)
        xb = _unpack_bf16_pairs(jnp.where(row < nv, xb_ref[...], jnp.uint32(0))).astype(BF16)
        g = _dot(xb, wg_bf[...])
        u = _dot(xb, wu_bf[...])
        h = (_silu(g) * u).astype(BF16)
        y_ref[...] = _pack_bf16_pairs(_dot(h, wd_bf[...]).astype(BF16))


def _experts(block_meta, xbuf, w_gate, w_up, w_down):
    P = xbuf.shape[0]
    nb = P // MOE_BLOCK
    rows = pl.BlockSpec((MOE_BLOCK, D_MODEL // 2), lambda i, *_: (i, 0))
    hbm = pl.BlockSpec(memory_space=pl.ANY)
    return pl.pallas_call(
        _expert_kernel,
        grid_spec=pltpu.PrefetchScalarGridSpec(
            num_scalar_prefetch=4,
            grid=(nb,),
            in_specs=[rows, hbm, hbm, hbm],
            out_specs=rows,
            scratch_shapes=[pltpu.VMEM((2, D_MODEL, D_EXPERT), F32), pltpu.VMEM((2, D_MODEL, D_EXPERT), F32),
                            pltpu.VMEM((2, D_EXPERT, D_MODEL), F32),
                            pltpu.VMEM((D_MODEL, D_EXPERT), BF16), pltpu.VMEM((D_MODEL, D_EXPERT), BF16),
                            pltpu.VMEM((D_EXPERT, D_MODEL), BF16),
                            pltpu.SemaphoreType.DMA((2, 3))]),
        out_shape=jax.ShapeDtypeStruct((P, D_MODEL // 2), jnp.uint32),
        compiler_params=_cparams(("arbitrary",)),
        name="experts",
    )(*block_meta, xbuf, w_gate, w_up, w_down)


def _combine_kernel(ra_ref, rb_ref, x1_ref, wt_ref, g2_ref, b2_ref, *rest):
    o_ref = rest[-1]
    w2 = wt_ref[...]
    w = jnp.concatenate([w2, jnp.zeros((6, w2.shape[1]), F32)], axis=0).T
    y = w[:, 0:1] * _unpack_bf16_pairs(ra_ref[...]) + w[:, 1:2] * _unpack_bf16_pairs(rb_ref[...])
    o_ref[...] = _layer_norm(DN_ALPHA * x1_ref[...] + y, g2_ref[...], b2_ref[...])


def _combine(rows2, x1, wt, g2, b2, tm, out_prev, tile0, t_total):
    T = x1.shape[0]
    nt = T // tm
    const = lambda shape: pl.BlockSpec(shape, lambda i: (0,) * len(shape))
    in_specs = [pl.BlockSpec((tm, D_MODEL // 2), lambda i: (i, 0)),
                pl.BlockSpec((tm, D_MODEL // 2), lambda i: (i + nt, 0)),
                pl.BlockSpec((tm, D_MODEL), lambda i: (i, 0)), pl.BlockSpec((2, tm), lambda i: (0, i)),
                const((1, D_MODEL)), const((1, D_MODEL))]
    args = [rows2, rows2, x1, wt, g2, b2]
    aliases = {}
    if out_prev is not None:
        in_specs.append(pl.BlockSpec(memory_space=pl.ANY))
        args.append(out_prev)
        aliases = {len(args) - 1: 0}
    return pl.pallas_call(
        _combine_kernel,
        grid=(nt,),
        in_specs=in_specs,
        out_specs=pl.BlockSpec((tm, D_MODEL), lambda i: (i + tile0, 0)),
        out_shape=jax.ShapeDtypeStruct((t_total, D_MODEL), F32),
        input_output_aliases=aliases,
        compiler_params=_cparams(("arbitrary",)),
        name="combine",
    )(*args)


def _slot_layout(eid, rank, counts, T):
    P = 2 * T + N_EXPERTS * MOE_BLOCK
    nb = P // MOE_BLOCK
    padded = ((counts + MOE_BLOCK - 1) // MOE_BLOCK) * MOE_BLOCK
    pend = jnp.cumsum(padded)
    pstart = pend - padded
    experts = jnp.arange(N_EXPERTS, dtype=jnp.int32)
    pos = rank + jnp.sum(jnp.where(eid[:, :, None] == experts, pstart, 0), axis=-1)

    block_start = jnp.arange(nb, dtype=jnp.int32) * MOE_BLOCK
    block_expert = jnp.minimum(jnp.sum(block_start[:, None] >= pend[None, :], axis=1), N_EXPERTS - 1)
    onehot = block_expert[:, None] == experts[None, :]
    look = lambda table: jnp.sum(jnp.where(onehot, table[None, :], 0), axis=1)
    block_valid = jnp.clip(look(pstart + counts) - block_start, 0, MOE_BLOCK)
    block_valid = jnp.where(block_start < pend[-1], block_valid, 0)
    present = (counts > 0) | (experts == N_EXPERTS - 1)
    later = present[None, :] & (experts[None, :] > experts[:, None])
    next_present = jnp.min(jnp.where(later, experts[None, :], N_EXPERTS), axis=1)
    next_present = jnp.where(next_present < N_EXPERTS, next_present, -1)
    runs_before = jnp.sum(present[None, :] & (experts[None, :] < experts[:, None]), axis=1)
    block_next = look(next_present)
    block_slot = look(runs_before % 2)
    i32 = lambda a: a.astype(jnp.int32)
    return i32(pos), (i32(block_expert), i32(block_valid), i32(block_next), i32(block_slot))


def kernel(x, lb_logits, w_in, hg_norm_w, sinks, w_branch_a, w_branch_b, w_out, ln1_g, ln1_b,
           router_group_w, router_group_b, router_expert_w, router_expert_b,
           w_exp_gate, w_exp_up, w_exp_down, ln2_g, ln2_b):
    B, S, D = x.shape
    assert D == D_MODEL and S % ATT_BLK == 0 and w_in.shape[0] == DEPTH == 1
    lb_all = jnp.cumsum(jax.nn.softmax(lb_logits.astype(F32), axis=0), axis=0)
    lb = lb_all[0].reshape(1, HG_W)
    w0 = w_in[0]
    w_f = jnp.concatenate([w0[:, :2 * HG_W], w0[:, 3 * HG_W:4 * HG_W]], axis=1).astype(BF16)
    w_b = jnp.concatenate([w0[:, 2 * HG_W:3 * HG_W], w0[:, 4 * HG_W:PROJ_W]], axis=1).astype(BF16)
    w_gate = w0[:, PROJ_W:].astype(BF16)
    nw = hg_norm_w[0].reshape(1, HG_DV).astype(F32)
    wa, wb, wo = w_branch_a[0].astype(BF16), w_branch_b[0].astype(BF16), w_out[0].astype(BF16)
    wr = jnp.zeros((40, D), F32).at[0:N_GROUPS].set(router_group_w[0].T).at[8:40].set(router_expert_w[0].T)
    br = jnp.zeros((40, 1), F32).at[0:N_GROUPS, 0].set(router_group_b[0]).at[8:40, 0].set(router_expert_b[0])
    g1, b1 = ln1_g[0].reshape(1, D), ln1_b[0].reshape(1, D)
    g2, b2 = ln2_g[0].reshape(1, D), ln2_b[0].reshape(1, D)

    n_parts = N_PARTS if B % N_PARTS == 0 else 1
    bp = B // n_parts
    tp = bp * S
    pick = lambda want: next(t for t in (want, 512, 256, ATT_BLK) if t <= want and tp % t == 0)
    tm_proj, tm_merge, tm_comb = pick(1024), pick(1024), pick(1024)
    x2 = x.reshape(B * S, D)
    parts = range(n_parts)
    fronts = []
    for part in parts:
        proj_f, proj_b = _proj(x2, w_f, w_b, tm_proj, part * tp, tp)
        hg, att = _mixers(proj_f, proj_b, lb, nw, sinks[0].astype(F32), bp, S)
        fronts.append(_merge(hg, att, x2, w_gate, wa, wb, wo, g1, b1, wr.astype(BF16), br, tm_merge, part * tp))
    layouts = [_slot_layout(eid, rank, cnt[:, 0].astype(jnp.int32), tp) for _, _, eid, _, rank, cnt in fronts]
    xbufs = [_sc_scatter2(fronts[p][1], layouts[p][0][0], layouts[p][0][1], 2 * tp + N_EXPERTS * MOE_BLOCK)
             for p in parts]
    ybufs = [_experts(layouts[p][1], xbufs[p], w_exp_gate[0], w_exp_up[0], w_exp_down[0]) for p in parts]
    rows = [_sc_gather(ybufs[p], layouts[p][0].reshape(-1)) for p in parts]
    out = None
    for p in parts:
        out = _combine(rows[p], fronts[p][0], fronts[p][3], g2, b2, tm_comb, out, p * (tp // tm_comb), B * S)
    return out.reshape(B, S, D)
```
